```python
import math
import jax, jax.numpy as jnp
from jax import lax
import numpy as np

D_MODEL = 1024
BATCH = 32
SEQ = 256
DEPTH = 2
DEC_BATCH = 4
DEC_SEQ = 4096
PAST_LEN = 256

GRID_W = 64
EPS = 1e-6
N_MOD = 6
HEAD_DIM = 64
N_HEADS = 8
N_KV_HEADS = 2
ATTN_WIDTH = N_HEADS * HEAD_DIM
KV_WIDTH = N_KV_HEADS * HEAD_DIM
Q_BLOCK = 128
ROPE_BASE = 10000.0
ROPE_PAIRS_PER_AXIS = HEAD_DIM // 4
POOL_WINDOWS = (2, 4, 8, 16)
POOL_GROUP = 64
POOL_WIDTH = len(POOL_WINDOWS) * POOL_GROUP
SSM_H = 16
SSM_GROUPS = 16
SSM_WIDTH = SSM_H * SSM_GROUPS
SSM_STATE = 64
N_BRANCHES = 3
SPLIT_POINTS = (ATTN_WIDTH,
                ATTN_WIDTH + KV_WIDTH,
                ATTN_WIDTH + 2 * KV_WIDTH,
                ATTN_WIDTH + 2 * KV_WIDTH + POOL_WIDTH,
                ATTN_WIDTH + 2 * KV_WIDTH + POOL_WIDTH + SSM_WIDTH)
IN_WIDTH = ATTN_WIDTH + 2 * KV_WIDTH + POOL_WIDTH + SSM_WIDTH + N_BRANCHES * D_MODEL
N_EXPERTS = 64
TOP_K = 8
EXPERT_DIM = 256
SHARED_DIM = 256
ROUTE_SCALE = 2.5
MOE_BLOCK = 256

kernel_name = 'hybrid_diffusion_prefix_step'


def rmsnorm(x, g):
    xf = x.astype(jnp.float32)
    y = xf * lax.rsqrt(jnp.mean(xf * xf, axis=-1, keepdims=True) + EPS)
    return (y * g.astype(jnp.float32)).astype(x.dtype)


def ada_params(cond, w_mod, b_mod):
    m = (jax.nn.silu(cond) @ w_mod + b_mod)[..., None, :]
    return jnp.split(m, N_MOD, axis=-1)


def modulate(h, shift, scale):
    return h * (1.0 + scale) + shift


def grid_rope(n_tokens):
    rows = n_tokens // GRID_W
    row = jnp.repeat(jnp.arange(rows, dtype=jnp.float32), GRID_W)
    col = jnp.tile(jnp.arange(GRID_W, dtype=jnp.float32), rows)
    inv_freq = ROPE_BASE ** (-jnp.arange(ROPE_PAIRS_PER_AXIS, dtype=jnp.float32) / ROPE_PAIRS_PER_AXIS)
    ang = jnp.concatenate([row[:, None] * inv_freq, col[:, None] * inv_freq], axis=-1)
    return jnp.cos(ang), jnp.sin(ang)


def apply_rope(x, cos, sin):
    xf = x.astype(jnp.float32).reshape(x.shape[:-1] + (HEAD_DIM // 2, 2))
    x0, x1 = xf[..., 0], xf[..., 1]
    c = cos[None, :, None, :]
    s = sin[None, :, None, :]
    out = jnp.stack([x0 * c - x1 * s, x0 * s + x1 * c], axis=-1).reshape(x.shape)
    return out.astype(x.dtype)


def project_in(h, w_in, q_norm_g, k_norm_g):
    b, s, _ = h.shape
    z = h @ w_in
    q, k, v, pool_u, ssm_u, gates = jnp.split(z, SPLIT_POINTS, axis=-1)
    q = rmsnorm(q.reshape(b, s, N_HEADS, HEAD_DIM), q_norm_g)
    k = rmsnorm(k.reshape(b, s, N_KV_HEADS, HEAD_DIM), k_norm_g)
    v = v.reshape(b, s, N_KV_HEADS, HEAD_DIM)
    return q, k, v, pool_u, ssm_u, gates


def block_attention(q, k, v):
    b, s, h, dh = q.shape
    g = h // N_KV_HEADS
    nb = s // Q_BLOCK
    qb = q.reshape(b, nb, Q_BLOCK, N_KV_HEADS, g, dh).transpose(1, 0, 2, 3, 4, 5)
    kf = k.astype(jnp.float32)
    vf = v.astype(jnp.float32)
    scale = dh ** -0.5

    def one_block(qblk):
        sc = jnp.einsum('bqkgd,blkd->bkgql', qblk.astype(jnp.float32), kf) * scale
        p = jax.nn.softmax(sc, axis=-1)
        return jnp.einsum('bkgql,blkd->bqkgd', p, vf).astype(q.dtype)

    o = lax.map(one_block, qb)
    return o.transpose(1, 0, 2, 3, 4, 5).reshape(b, s, h * dh)


def multiscale_pool(u, pool_w, pool_scale):
    b, s, _ = u.shape
    ng = len(POOL_WINDOWS)
    uf = u.astype(jnp.float32).reshape(b, s, ng, POOL_GROUP)
    cs = jnp.concatenate([jnp.zeros((b, 1, ng, POOL_GROUP), jnp.float32), jnp.cumsum(uf, axis=1)], axis=1)
    win = np.array(POOL_WINDOWS, dtype=np.int32)
    t = jnp.arange(s, dtype=jnp.int32)[:, None]
    lo = jnp.clip(t - win // 2, 0, s)
    hi = jnp.clip(t - win // 2 + win, 0, s)
    gsel = jnp.arange(ng)[None, :]
    tot = cs[:, hi, gsel] - cs[:, lo, gsel]
    pooled = tot / (hi - lo).astype(jnp.float32)[None, :, :, None] - uf
    mixed = jnp.einsum('bsgc,gcd->bsgd', pooled, pool_w.astype(jnp.float32))
    return (mixed.reshape(b, s, POOL_WIDTH) * pool_scale.astype(jnp.float32)).astype(u.dtype)


def linear_scan(a_bar, bu, h0):
    bu = bu.at[:, 0].add(a_bar * h0)
    a = jnp.broadcast_to(a_bar, bu.shape)

    def combine(left, right):
        return (left[0] * right[0], right[0] * left[1] + right[1])

    _, h = lax.associative_scan(combine, (a, bu), axis=1)
    return h


def bidir_ssm(u, h0, a_re, a_im, log_dt, b_re, b_im, c_re, c_im, d_skip, w_glu, b_glu, want_state):
    b, s, _ = u.shape
    f32 = jnp.float32
    uf = u.astype(f32).reshape(b, s, SSM_GROUPS, SSM_H)
    lam = lax.complex(a_re.astype(f32), a_im.astype(f32))
    dt = jnp.exp(log_dt.astype(f32))[..., None]
    a_bar = jnp.exp(lam * dt)
    b_bar = ((a_bar - 1.0) / lam)[..., None] * lax.complex(b_re.astype(f32), b_im.astype(f32))
    cmat = lax.complex(c_re.astype(f32), c_im.astype(f32))
    h0c = lax.complex(h0[..., 0].astype(f32), h0[..., 1].astype(f32))
    uc = uf.astype(jnp.complex64)
    y = d_skip.astype(f32) * uf
    finals = []
    for direction in range(2):
        bu = jnp.einsum('bsgh,gph->bsgp', uc, b_bar[direction])
        if direction == 1:
            bu = jnp.flip(bu, axis=1)
        h = linear_scan(a_bar[direction], bu, h0c[:, direction])
        finals.append(h[:, -1])
        if direction == 1:
            h = jnp.flip(h, axis=1)
        y = y + jnp.real(jnp.einsum('bsgp,ghp->bsgh', h, cmat[direction]))
    y = jax.nn.gelu(y.reshape(b, s, SSM_WIDTH))
    y = (y * jax.nn.sigmoid(y @ w_glu.astype(f32) + b_glu.astype(f32))).astype(u.dtype)
    if want_state:
        hf = jnp.stack(finals, axis=1)
        return y, jnp.stack([jnp.real(hf), jnp.imag(hf)], axis=-1)
    return y


def merge_branches(gates, o_attn, o_pool, o_ssm, w_up_attn, w_up_pool, w_up_ssm, w_out):
    g_attn, g_pool, g_ssm = jnp.split(jax.nn.sigmoid(gates), N_BRANCHES, axis=-1)
    m = g_attn * (o_attn @ w_up_attn) + g_pool * (o_pool @ w_up_pool) + g_ssm * (o_ssm @ w_up_ssm)
    return m @ w_out


def swiglu(x, wg, wu, wd):
    return (jax.nn.silu(x @ wg) * (x @ wu)) @ wd


def routed_experts(xt, idx, w, w_gate, w_up, w_down):
    n = xt.shape[0]
    m = n * TOP_K
    e_flat = idx.reshape(m)
    tok_flat = jnp.repeat(jnp.arange(n, dtype=jnp.int32), TOP_K)
    w_flat = w.reshape(m)
    order = jnp.argsort(e_flat)
    e_sorted, tok_sorted, w_sorted = e_flat[order], tok_flat[order], w_flat[order]
    counts = jnp.zeros((N_EXPERTS,), jnp.int32).at[e_flat].add(1)
    starts = jnp.cumsum(counts) - counts
    padded = (counts + MOE_BLOCK - 1) // MOE_BLOCK * MOE_BLOCK
    pad_end = jnp.cumsum(padded)
    pad_start = pad_end - padded
    dest = pad_start[e_sorted] + (jnp.arange(m, dtype=jnp.int32) - starts[e_sorted])
    n_blocks = -(-(m + N_EXPERTS * (MOE_BLOCK - 1)) // MOE_BLOCK)
    cap = n_blocks * MOE_BLOCK
    buf_tok = jnp.full((cap,), n, jnp.int32).at[dest].set(tok_sorted)
    buf_w = jnp.zeros((cap,), jnp.float32).at[dest].set(w_sorted)
    block_start = jnp.arange(n_blocks, dtype=jnp.int32) * MOE_BLOCK
    block_expert = jnp.minimum(jnp.searchsorted(pad_end, block_start, side='right'), N_EXPERTS - 1)
    x_pad = jnp.concatenate([xt, jnp.zeros((1, D_MODEL), xt.dtype)], axis=0)

    def step(acc, blk):
        toks, wts, e = blk
        yb = swiglu(x_pad[toks], w_gate[e], w_up[e], w_down[e]).astype(jnp.float32)
        return acc.at[toks].add(yb * wts[:, None]), None

    acc0 = jnp.zeros((n + 1, D_MODEL), jnp.float32)
    acc, _ = lax.scan(step, acc0, (buf_tok.reshape(n_blocks, MOE_BLOCK),
                                   buf_w.reshape(n_blocks, MOE_BLOCK), block_expert))
    return acc[:n].astype(xt.dtype)


def moe(h, w_router, router_bias, w_gate, w_up, w_down, ws_gate, ws_up, ws_down):
    shape = h.shape
    xt = h.reshape(-1, D_MODEL)
    scores = jax.nn.sigmoid(xt.astype(jnp.float32) @ w_router.astype(jnp.float32))
    _, idx = lax.top_k(scores + router_bias.astype(jnp.float32), TOP_K)
    w = jnp.take_along_axis(scores, idx, axis=-1)
    w = w / jnp.sum(w, axis=-1, keepdims=True) * ROUTE_SCALE
    routed = routed_experts(xt, idx, w, w_gate, w_up, w_down)
    return (swiglu(xt, ws_gate, ws_up, ws_down) + routed).reshape(shape)


def setup_inputs(seed: int = 0) -> dict:
    key = jax.random.key(seed)
    ks = iter(jax.random.split(key, 64))
    f32 = jnp.float32
    L = DEPTH

    def nrm(shape, scale):
        return jax.random.normal(next(ks), shape, f32) * scale

    def gain(shape):
        return 1.0 + nrm(shape, 0.02)

    n_idx = jnp.arange(SSM_STATE, dtype=f32)
    inp = {}
    inp['x_prompt'] = nrm((BATCH, SEQ, D_MODEL), 1.0)
    inp['x_sample'] = nrm((DEC_BATCH, DEC_SEQ, D_MODEL), 1.0)
    inp['c'] = nrm((DEC_BATCH, D_MODEL), 1.0)
    inp['cache_k'] = nrm((DEC_BATCH, DEPTH, PAST_LEN, N_KV_HEADS, HEAD_DIM), 1.0)
    inp['cache_v'] = nrm((DEC_BATCH, DEPTH, PAST_LEN, N_KV_HEADS, HEAD_DIM), 1.0)
    inp['state_ssm'] = nrm((DEC_BATCH, DEPTH, 2, SSM_GROUPS, SSM_STATE, 2), 0.5)
    inp['c_ctx'] = nrm((D_MODEL,), 1.0)
    inp['w_mod'] = nrm((L, D_MODEL, N_MOD * D_MODEL), 0.5 * D_MODEL ** -0.5)
    inp['b_mod'] = nrm((L, N_MOD * D_MODEL), 0.02)
    inp['norm1_g'] = gain((L, D_MODEL))
    inp['norm2_g'] = gain((L, D_MODEL))
    inp['w_in'] = nrm((L, D_MODEL, IN_WIDTH), D_MODEL ** -0.5)
    inp['q_norm_g'] = gain((L, HEAD_DIM))
    inp['k_norm_g'] = gain((L, HEAD_DIM))
    inp['w_up_attn'] = nrm((L, ATTN_WIDTH, D_MODEL), ATTN_WIDTH ** -0.5)
    inp['pool_w'] = nrm((L, len(POOL_WINDOWS), POOL_GROUP, POOL_GROUP), POOL_GROUP ** -0.5)
    inp['pool_scale'] = gain((L, POOL_WIDTH))
    inp['w_up_pool'] = nrm((L, POOL_WIDTH, D_MODEL), POOL_WIDTH ** -0.5)
    inp['ssm_a_re'] = -0.5 + nrm((L, 2, SSM_GROUPS, SSM_STATE), 0.01)
    inp['ssm_a_im'] = jnp.broadcast_to(math.pi * n_idx, (L, 2, SSM_GROUPS, SSM_STATE)) + nrm((L, 2, SSM_GROUPS, SSM_STATE), 0.01)
    inp['ssm_log_dt'] = jax.random.uniform(next(ks), (L, 2, SSM_GROUPS), f32, math.log(1e-3), math.log(1e-1))
    inp['ssm_b_re'] = nrm((L, 2, SSM_GROUPS, SSM_STATE, SSM_H), (2.0 * SSM_H) ** -0.5)
    inp['ssm_b_im'] = nrm((L, 2, SSM_GROUPS, SSM_STATE, SSM_H), (2.0 * SSM_H) ** -0.5)
    inp['ssm_c_re'] = nrm((L, 2, SSM_GROUPS, SSM_H, SSM_STATE), (2.0 * SSM_STATE) ** -0.5)
    inp['ssm_c_im'] = nrm((L, 2, SSM_GROUPS, SSM_H, SSM_STATE), (2.0 * SSM_STATE) ** -0.5)
    inp['ssm_d'] = nrm((L, SSM_GROUPS, SSM_H), 1.0)
    inp['w_glu'] = nrm((L, SSM_WIDTH, SSM_WIDTH), SSM_WIDTH ** -0.5)
    inp['b_glu'] = nrm((L, SSM_WIDTH), 0.02)
    inp['w_up_ssm'] = nrm((L, SSM_WIDTH, D_MODEL), SSM_WIDTH ** -0.5)
    inp['w_out'] = nrm((L, D_MODEL, D_MODEL), D_MODEL ** -0.5)
    inp['w_router'] = nrm((L, D_MODEL, N_EXPERTS), D_MODEL ** -0.5)
    inp['router_bias'] = nrm((L, N_EXPERTS), 0.01)
    inp['w_gate'] = nrm((L, N_EXPERTS, D_MODEL, EXPERT_DIM), D_MODEL ** -0.5)
    inp['w_up'] = nrm((L, N_EXPERTS, D_MODEL, EXPERT_DIM), D_MODEL ** -0.5)
    inp['w_down'] = nrm((L, N_EXPERTS, EXPERT_DIM, D_MODEL), EXPERT_DIM ** -0.5)
    inp['ws_gate'] = nrm((L, D_MODEL, SHARED_DIM), D_MODEL ** -0.5)
    inp['ws_up'] = nrm((L, D_MODEL, SHARED_DIM), D_MODEL ** -0.5)
    inp['ws_down'] = nrm((L, SHARED_DIM, D_MODEL), SHARED_DIM ** -0.5)
    inp['final_norm_g'] = gain((D_MODEL,))
    return inp


def reference(x_prompt, x_sample, c, cache_k, cache_v, state_ssm, c_ctx,
              w_mod, b_mod, norm1_g, norm2_g, w_in, q_norm_g, k_norm_g,
              w_up_attn, pool_w, pool_scale, w_up_pool,
              ssm_a_re, ssm_a_im, ssm_log_dt, ssm_b_re, ssm_b_im, ssm_c_re, ssm_c_im, ssm_d,
              w_glu, b_glu, w_up_ssm, w_out,
              w_router, router_bias, w_gate, w_up, w_down, ws_gate, ws_up, ws_down,
              final_norm_g):
    xp = x_prompt
    xs = x_sample
    cos, sin = grid_rope(xs.shape[1])
    h0_ctx = jnp.zeros((xp.shape[0], 2, SSM_GROUPS, SSM_STATE, 2), jnp.float32)
    new_k, new_v, new_s = [], [], []
    for l in range(DEPTH):
        ssm_p = (ssm_a_re[l], ssm_a_im[l], ssm_log_dt[l], ssm_b_re[l], ssm_b_im[l],
                 ssm_c_re[l], ssm_c_im[l], ssm_d[l], w_glu[l], b_glu[l])
        merge_p = (w_up_attn[l], w_up_pool[l], w_up_ssm[l], w_out[l])
        moe_p = (w_router[l], router_bias[l], w_gate[l], w_up[l], w_down[l],
                 ws_gate[l], ws_up[l], ws_down[l])

        sh1, sc1, g1, sh2, sc2, g2 = ada_params(c_ctx, w_mod[l], b_mod[l])
        h = modulate(rmsnorm(xp, norm1_g[l]), sh1, sc1)
        q, k, v, pool_u, ssm_u, gates = project_in(h, w_in[l], q_norm_g[l], k_norm_g[l])
        o_attn = block_attention(q, k, v)
        o_pool = multiscale_pool(pool_u, pool_w[l], pool_scale[l])
        o_ssm, st = bidir_ssm(ssm_u, h0_ctx, *ssm_p, want_state=True)
        xp = xp + g1 * merge_branches(gates, o_attn, o_pool, o_ssm, *merge_p)
        xp = xp + g2 * moe(modulate(rmsnorm(xp, norm2_g[l]), sh2, sc2), *moe_p)
        new_k.append(k)
        new_v.append(v)
        new_s.append(st)

        sh1, sc1, g1, sh2, sc2, g2 = ada_params(c, w_mod[l], b_mod[l])
        h = modulate(rmsnorm(xs, norm1_g[l]), sh1, sc1)
        q, k, v, pool_u, ssm_u, gates = project_in(h, w_in[l], q_norm_g[l], k_norm_g[l])
        q = apply_rope(q, cos, sin)
        k = apply_rope(k, cos, sin)
        keys = jnp.concatenate([cache_k[:, l].astype(k.dtype), k], axis=1)
        vals = jnp.concatenate([cache_v[:, l].astype(v.dtype), v], axis=1)
        o_attn = block_attention(q, keys, vals)
        o_pool = multiscale_pool(pool_u, pool_w[l], pool_scale[l])
        o_ssm = bidir_ssm(ssm_u, state_ssm[:, l], *ssm_p, want_state=False)
        xs = xs + g1 * merge_branches(gates, o_attn, o_pool, o_ssm, *merge_p)
        xs = xs + g2 * moe(modulate(rmsnorm(xs, norm2_g[l]), sh2, sc2), *moe_p)

    y_prompt = rmsnorm(xp, final_norm_g)
    y_sample = rmsnorm(xs, final_norm_g)
    return (y_prompt, y_sample, jnp.stack(new_k, axis=1), jnp.stack(new_v, axis=1), jnp.stack(new_s, axis=1))
```

```python
import functools
import math

import jax
import jax.numpy as jnp
import numpy as np
from jax import lax
from jax.experimental import pallas as pl
from jax.experimental.pallas import tpu as pltpu

D_MODEL = 1024
BATCH = 32
SEQ = 256
DEPTH = 2
DEC_BATCH = 4
DEC_SEQ = 4096
PAST_LEN = 256
GRID_W = 64
EPS = 1e-6
N_MOD = 6
HEAD_DIM = 64
N_HEADS = 8
N_KV_HEADS = 2
ATTN_WIDTH = N_HEADS * HEAD_DIM
KV_WIDTH = N_KV_HEADS * HEAD_DIM
ROPE_BASE = 10000.0
ROPE_PAIRS_PER_AXIS = HEAD_DIM // 4
POOL_WINDOWS = (2, 4, 8, 16)
POOL_GROUP = 64
POOL_WIDTH = len(POOL_WINDOWS) * POOL_GROUP
SSM_H = 16
SSM_GROUPS = 16
SSM_WIDTH = SSM_H * SSM_GROUPS
SSM_STATE = 64
N_EXPERTS = 64
TOP_K = 8
EXPERT_DIM = 256
SHARED_DIM = 256
ROUTE_SCALE = 2.5

N_PROMPT = BATCH * SEQ
N_SAMPLE = DEC_BATCH * DEC_SEQ
N_TOK = N_PROMPT + N_SAMPLE
N_COND = 1 + DEC_BATCH
COND_ROWS = 8
QK_WIDTH = ATTN_WIDTH + KV_WIDTH
MIX_WIDTH = QK_WIDTH + KV_WIDTH + POOL_WIDTH + SSM_WIDTH
GATE_WIDTH = 3 * D_MODEL
SSM_COLS = SSM_GROUPS * SSM_STATE
SSM_LANES = 2 * SSM_COLS
SSM_SEQS = 4
SSM_CHUNK = 128
ROUTER_LANES = 128
EXPERT_CHUNK = 5
N_EXPERT_ALL = N_EXPERTS + 1

LANE = 128
VMEM_LIMIT = 56 * 1024 * 1024

F32 = jnp.float32
BF16 = jnp.bfloat16
HIGHEST = lax.Precision.HIGHEST


def _sigmoid(x):
    return 1.0 / (1.0 + jnp.exp(-x))


def _params(dims, vmem=VMEM_LIMIT):
    return pltpu.CompilerParams(dimension_semantics=dims, vmem_limit_bytes=vmem)


def _mod_index(i, tm):
    p = N_PROMPT // tm
    t = DEC_SEQ // tm
    return jnp.where(i < p, 0, 1 + (i - p) // t)


def _norm_mod(x, g, shift, scale):
    ms = jnp.mean(x * x, axis=-1, keepdims=True)
    return (x * lax.rsqrt(ms + EPS) * g) * (1.0 + scale) + shift


def _mod_kernel(cond_ref, w_ref, b_ref, o_ref):
    c = cond_ref[...]
    s = c * _sigmoid(c)
    o_ref[0] = jnp.dot(s, w_ref[0], preferred_element_type=F32, precision=HIGHEST) + b_ref[0]


def _ada_all(cond, w_mod, b_mod):
    tn = 1536
    width = N_MOD * D_MODEL
    out = pl.pallas_call(
        _mod_kernel,
        grid=(DEPTH, width // tn),
        in_specs=[
            pl.BlockSpec((COND_ROWS, D_MODEL), lambda l, j: (0, 0)),
            pl.BlockSpec((1, D_MODEL, tn), lambda l, j: (l, 0, j)),
            pl.BlockSpec((1, 1, tn), lambda l, j: (l, 0, j)),
        ],
        out_specs=pl.BlockSpec((1, COND_ROWS, tn), lambda l, j: (l, 0, j)),
        out_shape=jax.ShapeDtypeStruct((DEPTH, COND_ROWS, width), F32),
        compiler_params=_params(("parallel", "parallel")),
        name="ada_mod",
    )(cond, w_mod, b_mod.reshape(DEPTH, 1, width))
    return out.reshape(DEPTH, COND_ROWS, N_MOD, D_MODEL)


def _inproj_kernel(x_ref, mod_ref, g_ref, w_ref, bd_ref, qkg_ref, cos_ref, sin_ref,
                   q_ref, k_ref, v_ref, pu_ref, su_ref):
    h = _norm_mod(x_ref[...], g_ref[...], mod_ref[0, 0:1, :], mod_ref[0, 1:2, :])
    z = jnp.dot(h.astype(BF16), w_ref[...], preferred_element_type=F32)
    qk = z[:, :QK_WIDTH]
    qq = qk * qk
    hi = qq.astype(BF16)
    lo = (qq - hi.astype(F32)).astype(BF16)
    ms = (jnp.dot(hi, bd_ref[...], preferred_element_type=F32)
          + jnp.dot(lo, bd_ref[...], preferred_element_type=F32))
    qkn = qk * lax.rsqrt(ms + EPS) * qkg_ref[...]
    parts = []
    for c in range(QK_WIDTH // LANE):
        blk = qkn[:, c * LANE:(c + 1) * LANE]
        nxt = pltpu.roll(blk, LANE - 1, axis=1)
        prv = pltpu.roll(blk, 1, axis=1)
        lane = lax.broadcasted_iota(jnp.int32, blk.shape, 1)
        parts.append(jnp.where((lane & 1) == 0, nxt, prv))
    partner = jnp.concatenate(parts, axis=1)
    qkr = qkn * cos_ref[...] + partner * sin_ref[...]
    q_ref[...] = (qkr[:, :ATTN_WIDTH] * (HEAD_DIM ** -0.5)).astype(BF16)
    k_ref[...] = qkr[:, ATTN_WIDTH:QK_WIDTH]
    v_ref[...] = z[:, QK_WIDTH:QK_WIDTH + KV_WIDTH]
    pu_ref[...] = z[:, QK_WIDTH + KV_WIDTH:QK_WIDTH + KV_WIDTH + POOL_WIDTH]
    su_ref[...] = z[:, QK_WIDTH + KV_WIDTH + POOL_WIDTH:MIX_WIDTH]


def _in_project(x, mod, norm_g, w_mix, bd, qk_gain, cos_t, sin_t):
    tm = 512
    p_tiles = N_PROMPT // tm
    s_tiles = DEC_SEQ // tm

    def rope_idx(i):
        return (jnp.where(i < p_tiles, 0, 1 + (i - p_tiles) % s_tiles), 0)

    row = lambda i: (i, 0)
    const = lambda i: (0, 0)
    return pl.pallas_call(
        _inproj_kernel,
        grid=(N_TOK // tm,),
        in_specs=[
            pl.BlockSpec((tm, D_MODEL), row),
            pl.BlockSpec((1, N_MOD, D_MODEL), lambda i: (_mod_index(i, tm), 0, 0)),
            pl.BlockSpec((1, D_MODEL), const),
            pl.BlockSpec((D_MODEL, MIX_WIDTH), const),
            pl.BlockSpec((QK_WIDTH, QK_WIDTH), const),
            pl.BlockSpec((1, QK_WIDTH), const),
            pl.BlockSpec((tm, QK_WIDTH), rope_idx),
            pl.BlockSpec((tm, QK_WIDTH), rope_idx),
        ],
        out_specs=[
            pl.BlockSpec((tm, ATTN_WIDTH), row),
            pl.BlockSpec((tm, KV_WIDTH), row),
            pl.BlockSpec((tm, KV_WIDTH), row),
            pl.BlockSpec((tm, POOL_WIDTH), row),
            pl.BlockSpec((tm, SSM_WIDTH), row),
        ],
        out_shape=[
            jax.ShapeDtypeStruct((N_TOK, ATTN_WIDTH), BF16),
            jax.ShapeDtypeStruct((N_TOK, KV_WIDTH), F32),
            jax.ShapeDtypeStruct((N_TOK, KV_WIDTH), F32),
            jax.ShapeDtypeStruct((N_TOK, POOL_WIDTH), F32),
            jax.ShapeDtypeStruct((N_TOK, SSM_WIDTH), F32),
        ],
        compiler_params=_params(("parallel",)),
        name="in_project",
    )(x, mod, norm_g, w_mix, bd, qk_gain, cos_t, sin_t)


def _rope_tables(tm):
    rows = DEC_SEQ // GRID_W
    row = jnp.repeat(jnp.arange(rows, dtype=F32), GRID_W)
    col = jnp.tile(jnp.arange(GRID_W, dtype=F32), rows)
    inv_freq = ROPE_BASE ** (-jnp.arange(ROPE_PAIRS_PER_AXIS, dtype=F32) / ROPE_PAIRS_PER_AXIS)
    ang = jnp.concatenate([row[:, None] * inv_freq, col[:, None] * inv_freq], axis=-1)
    cos = jnp.repeat(jnp.cos(ang), 2, axis=-1)
    sin = jnp.repeat(jnp.sin(ang), 2, axis=-1) * jnp.tile(jnp.array([-1.0, 1.0], F32), HEAD_DIM // 2)
    n_rep = QK_WIDTH // HEAD_DIM
    cos = jnp.concatenate([jnp.ones((tm, HEAD_DIM), F32), cos], axis=0)
    sin = jnp.concatenate([jnp.zeros((tm, HEAD_DIM), F32), sin], axis=0)
    return jnp.tile(cos, (1, n_rep)), jnp.tile(sin, (1, n_rep))


def _attn_kernel(q_ref, kt_ref, v_ref, o_ref):
    q = q_ref[...]
    kt = kt_ref[0]
    v = v_ref[0]
    lane = lax.broadcasted_iota(jnp.int32, q.shape, 1)
    first = lane < HEAD_DIM
    outs = []
    for keep in (first, lane >= HEAD_DIM):
        qh = jnp.where(keep, q, jnp.zeros_like(q))
        s = jnp.dot(qh, kt, preferred_element_type=F32)
        m = jnp.max(s, axis=-1, keepdims=True)
        p = jnp.exp(s - m)
        denom = jnp.sum(p, axis=-1, keepdims=True)
        o = jnp.dot(p.astype(BF16), v, preferred_element_type=F32)
        outs.append(o / denom)
    o_ref[...] = jnp.where(first, outs[0], outs[1]).astype(o_ref.dtype)


def _attention(q, keys, vals, n_batch, n_q, row0):
    n_keys = keys.shape[1]
    tq = 256
    kt = jnp.transpose(keys, (0, 2, 3, 1)).astype(BF16)
    kt = jnp.concatenate([kt, kt], axis=2).reshape(n_batch * N_KV_HEADS, 2 * HEAD_DIM, n_keys)
    vv = jnp.transpose(vals, (0, 2, 1, 3)).astype(BF16)
    vv = jnp.concatenate([vv, vv], axis=3).reshape(n_batch * N_KV_HEADS, n_keys, 2 * HEAD_DIM)
    q_tiles = n_q // tq
    base = row0 // tq
    pairs = N_HEADS // 2
    pairs_per_kv = pairs // N_KV_HEADS
    return pl.pallas_call(
        _attn_kernel,
        grid=(n_batch, pairs, q_tiles),
        in_specs=[
            pl.BlockSpec((tq, LANE), lambda b, p, i: (base + b * q_tiles + i, p)),
            pl.BlockSpec((1, LANE, n_keys), lambda b, p, i: (b * N_KV_HEADS + p // pairs_per_kv, 0, 0)),
            pl.BlockSpec((1, n_keys, LANE), lambda b, p, i: (b * N_KV_HEADS + p // pairs_per_kv, 0, 0)),
        ],
        out_specs=pl.BlockSpec((tq, LANE), lambda b, p, i: (b * q_tiles + i, p)),
        out_shape=jax.ShapeDtypeStruct((n_batch * n_q, ATTN_WIDTH), BF16),
        compiler_params=_params(("parallel", "parallel", "parallel")),
        name="attention",
    )(q, kt, vv)


def _pool_kernel(u_ref, w_ref, sc_ref, o_ref, pad_ref, *, n_seq):
    halo = 8
    u = u_ref[...]
    zeros = jnp.zeros((halo, POOL_WIDTH), F32)
    pad_ref[0:halo, :] = zeros
    pad_ref[halo + n_seq:2 * halo + n_seq, :] = zeros
    pad_ref[halo:halo + n_seq, :] = u

    def sh(j):
        return pad_ref[halo + j:halo + j + n_seq, :]

    t2 = sh(-1) + u
    t4 = t2 + sh(-2) + sh(1)
    t8 = t4 + sh(-4) + sh(-3) + sh(2) + sh(3)
    t16 = t8 + sh(-8) + sh(-7) + sh(-6) + sh(-5) + sh(4) + sh(5) + sh(6) + sh(7)
    grp = lax.broadcasted_iota(jnp.int32, u.shape, 1) >> 6
    t = lax.broadcasted_iota(jnp.int32, u.shape, 0)
    tot = jnp.where(grp == 0, t2, jnp.where(grp == 1, t4, jnp.where(grp == 2, t8, t16)))
    half = jnp.where(grp == 0, 1, jnp.where(grp == 1, 2, jnp.where(grp == 2, 4, 8)))
    lo = jnp.maximum(t - half, 0)
    hi = jnp.minimum(t + half, n_seq)
    pooled = tot / (hi - lo).astype(F32) - u
    mixed = jnp.dot(pooled.astype(BF16), w_ref[...], preferred_element_type=F32)
    o_ref[...] = (mixed * sc_ref[...]).astype(o_ref.dtype)


def _pool(u, w_bd, scale, n_batch, n_seq, row0):
    base = row0 // n_seq
    return pl.pallas_call(
        functools.partial(_pool_kernel, n_seq=n_seq),
        grid=(n_batch,),
        in_specs=[
            pl.BlockSpec((n_seq, POOL_WIDTH), lambda b: (base + b, 0)),
            pl.BlockSpec((POOL_WIDTH, POOL_WIDTH), lambda b: (0, 0)),
            pl.BlockSpec((1, POOL_WIDTH), lambda b: (0, 0)),
        ],
        out_specs=pl.BlockSpec((n_seq, POOL_WIDTH), lambda b: (b, 0)),
        out_shape=jax.ShapeDtypeStruct((n_batch * n_seq, POOL_WIDTH), BF16),
        scratch_shapes=[pltpu.VMEM((n_seq + 16, POOL_WIDTH), F32)],
        compiler_params=_params(("parallel",)),
        name="pool_mixer",
    )(u, w_bd, scale)


def _ssm_kernel(u_ref, a_ref, b_ref, c_ref, h0_ref, y_ref, hfin_ref, bu_ref, st_ref):
    rows = 2 * SSM_SEQS

    @pl.when(pl.program_id(1) == 0)
    def _():
        st_ref[...] = h0_ref[0]

    bu_ref[...] = jnp.dot(u_ref[...], b_ref[...], preferred_element_type=F32)
    a_re = a_ref[:, :SSM_COLS]
    a_im = a_ref[:, SSM_COLS:]

    def step(t, carry):
        h_re, h_im = carry
        r = pl.multiple_of(t * rows, rows)
        n_re = a_re * h_re - a_im * h_im + bu_ref[pl.ds(r, rows), :SSM_COLS]
        n_im = a_re * h_im + a_im * h_re + bu_ref[pl.ds(r, rows), SSM_COLS:]
        bu_ref[pl.ds(r, rows), :SSM_COLS] = n_re
        bu_ref[pl.ds(r, rows), SSM_COLS:] = n_im
        return n_re, n_im

    h_re, h_im = lax.fori_loop(0, SSM_CHUNK, step, (st_ref[:, :SSM_COLS], st_ref[:, SSM_COLS:]))
    st_ref[:, :SSM_COLS] = h_re
    st_ref[:, SSM_COLS:] = h_im
    hfin_ref[0] = st_ref[...]
    yy = jnp.dot(bu_ref[...].astype(BF16), c_ref[...], preferred_element_type=F32)
    r_idx = lax.broadcasted_iota(jnp.int32, (SSM_CHUNK * rows, SSM_WIDTH), 0)
    y_ref[...] = jnp.where((r_idx & SSM_SEQS) == 0, yy[:, :SSM_WIDTH], yy[:, SSM_WIDTH:])


def _ssm_scan(u, h0, a_rows, b_mat, c_mat):
    n_batch, n_seq, _ = u.shape
    groups = n_batch // SSM_SEQS
    chunks = n_seq // SSM_CHUNK
    rows = 2 * SSM_SEQS
    ug = u.reshape(groups, SSM_SEQS, n_seq, SSM_WIDTH).transpose(0, 2, 1, 3)
    zero = jnp.zeros_like(ug)
    u_all = jnp.concatenate([jnp.concatenate([ug, zero], axis=-1),
                             jnp.concatenate([zero, ug[:, ::-1]], axis=-1)], axis=2)
    u_all = u_all.astype(BF16).reshape(groups * n_seq * rows, 2 * SSM_WIDTH)
    hh = h0.astype(F32).reshape(groups, SSM_SEQS, 2, SSM_COLS, 2).transpose(0, 2, 1, 4, 3)
    hh = hh.reshape(groups, rows, SSM_LANES)
    blk = SSM_CHUNK * rows
    y_all, hfin = pl.pallas_call(
        _ssm_kernel,
        grid=(groups, chunks),
        in_specs=[
            pl.BlockSpec((blk, 2 * SSM_WIDTH), lambda g, c: (g * chunks + c, 0)),
            pl.BlockSpec((rows, SSM_LANES), lambda g, c: (0, 0)),
            pl.BlockSpec((2 * SSM_WIDTH, SSM_LANES), lambda g, c: (0, 0)),
            pl.BlockSpec((SSM_LANES, 2 * SSM_WIDTH), lambda g, c: (0, 0)),
            pl.BlockSpec((1, rows, SSM_LANES), lambda g, c: (g, 0, 0)),
        ],
        out_specs=[
            pl.BlockSpec((blk, SSM_WIDTH), lambda g, c: (g * chunks + c, 0)),
            pl.BlockSpec((1, rows, SSM_LANES), lambda g, c: (g, 0, 0)),
        ],
        out_shape=[
            jax.ShapeDtypeStruct((groups * n_seq * rows, SSM_WIDTH), F32),
            jax.ShapeDtypeStruct((groups, rows, SSM_LANES), F32),
        ],
        scratch_shapes=[pltpu.VMEM((blk, SSM_LANES), F32), pltpu.VMEM((rows, SSM_LANES), F32)],
        compiler_params=_params(("parallel", "arbitrary")),
        name="ssm_scan",
    )(u_all, a_rows, b_mat, c_mat, hh)
    y_all = y_all.reshape(groups, n_seq, rows, SSM_WIDTH)
    y_f = y_all[:, :, :SSM_SEQS].transpose(0, 2, 1, 3).reshape(n_batch * n_seq, SSM_WIDTH)
    y_b = y_all[:, ::-1, SSM_SEQS:].transpose(0, 2, 1, 3).reshape(n_batch * n_seq, SSM_WIDTH)
    fin = hfin.reshape(groups, 2, SSM_SEQS, 2, SSM_GROUPS, SSM_STATE).transpose(0, 2, 1, 4, 5, 3)
    return y_f, y_b, fin.reshape(n_batch, 2, SSM_GROUPS, SSM_STATE, 2)


def _ssm_matrices(a_re, a_im, log_dt, b_re, b_im, c_re, c_im):
    lam = lax.complex(a_re.astype(F32), a_im.astype(F32))
    dt = jnp.exp(log_dt.astype(F32))[..., None]
    a_bar = jnp.exp(lam * dt)
    b_bar = ((a_bar - 1.0) / lam)[..., None] * lax.complex(b_re.astype(F32), b_im.astype(F32))
    a_dir = jnp.concatenate([jnp.real(a_bar).reshape(2, SSM_COLS), jnp.imag(a_bar).reshape(2, SSM_COLS)], axis=-1)
    a_rows = jnp.repeat(a_dir, SSM_SEQS, axis=0)
    eye = jnp.eye(SSM_GROUPS, dtype=F32)
    bt = jnp.transpose(b_bar, (0, 1, 3, 2))
    b_real = jnp.einsum('dghp,ge->dghep', jnp.real(bt), eye).reshape(2 * SSM_WIDTH, SSM_COLS)
    b_imag = jnp.einsum('dghp,ge->dghep', jnp.imag(bt), eye).reshape(2 * SSM_WIDTH, SSM_COLS)
    b_mat = jnp.concatenate([b_real, b_imag], axis=-1).astype(BF16)
    cr = jnp.transpose(c_re.astype(F32), (0, 1, 3, 2))
    ci = jnp.transpose(c_im.astype(F32), (0, 1, 3, 2))
    c_real = jnp.einsum('dgph,ge->gpdeh', cr, eye).reshape(SSM_COLS, 2 * SSM_WIDTH)
    c_imag = jnp.einsum('dgph,ge->gpdeh', -ci, eye).reshape(SSM_COLS, 2 * SSM_WIDTH)
    c_mat = jnp.concatenate([c_real, c_imag], axis=0).astype(BF16)
    return a_rows, b_mat, c_mat


def _merge_kernel(x_ref, mod_ref, g_ref, wg_ref, oa_ref, op_ref, su_ref, yf_ref, yb_ref, d_ref,
                  wglu_ref, bglu_ref, wua_ref, wup_ref, wus_ref, wo_ref, o_ref):
    x = x_ref[...]
    h = _norm_mod(x, g_ref[...], mod_ref[0, 0:1, :], mod_ref[0, 1:2, :])
    gates = _sigmoid(jnp.dot(h.astype(BF16), wg_ref[...], preferred_element_type=F32))
    y = d_ref[...] * su_ref[...] + yf_ref[...] + yb_ref[...]
    y = 0.5 * y * (1.0 + jnp.tanh(math.sqrt(2.0 / math.pi) * (y + 0.044715 * (y * y * y))))
    glu = jnp.dot(y.astype(BF16), wglu_ref[...], preferred_element_type=F32) + bglu_ref[...]
    o_ssm = y * _sigmoid(glu)
    m = (gates[:, :D_MODEL] * jnp.dot(oa_ref[...], wua_ref[...], preferred_element_type=F32)
         + gates[:, D_MODEL:2 * D_MODEL] * jnp.dot(op_ref[...], wup_ref[...], preferred_element_type=F32)
         + gates[:, 2 * D_MODEL:] * jnp.dot(o_ssm.astype(BF16), wus_ref[...], preferred_element_type=F32))
    o_ref[...] = x + mod_ref[0, 2:3, :] * jnp.dot(m.astype(BF16), wo_ref[...], preferred_element_type=F32)


def _merge(x, mod, norm_g, w_gates, o_attn, o_pool, ssm_u, y_f, y_b, d_skip, w_glu, b_glu,
           w_up_attn, w_up_pool, w_up_ssm, w_out):
    tm = 512
    row = lambda i: (i, 0)
    const = lambda i: (0, 0)
    return pl.pallas_call(
        _merge_kernel,
        grid=(N_TOK // tm,),
        in_specs=[
            pl.BlockSpec((tm, D_MODEL), row),
            pl.BlockSpec((1, N_MOD, D_MODEL), lambda i: (_mod_index(i, tm), 0, 0)),
            pl.BlockSpec((1, D_MODEL), const),
            pl.BlockSpec((D_MODEL, GATE_WIDTH), const),
            pl.BlockSpec((tm, ATTN_WIDTH), row),
            pl.BlockSpec((tm, POOL_WIDTH), row),
            pl.BlockSpec((tm, SSM_WIDTH), row),
            pl.BlockSpec((tm, SSM_WIDTH), row),
            pl.BlockSpec((tm, SSM_WIDTH), row),
            pl.BlockSpec((1, SSM_WIDTH), const),
            pl.BlockSpec((SSM_WIDTH, SSM_WIDTH), const),
            pl.BlockSpec((1, SSM_WIDTH), const),
            pl.BlockSpec((ATTN_WIDTH, D_MODEL), const),
            pl.BlockSpec((POOL_WIDTH, D_MODEL), const),
            pl.BlockSpec((SSM_WIDTH, D_MODEL), const),
            pl.BlockSpec((D_MODEL, D_MODEL), const),
        ],
        out_specs=pl.BlockSpec((tm, D_MODEL), row),
        out_shape=jax.ShapeDtypeStruct((N_TOK, D_MODEL), F32),
        compiler_params=_params(("parallel",)),
        name="merge_branches",
    )(x, mod, norm_g, w_gates, o_attn, o_pool, ssm_u, y_f, y_b, d_skip, w_glu, b_glu,
      w_up_attn, w_up_pool, w_up_ssm, w_out)


def _router_kernel(x_ref, mod_ref, g_ref, wr_ref, rb_ref, h_ref, gw_ref):
    h = _norm_mod(x_ref[...], g_ref[...], mod_ref[0, 3:4, :], mod_ref[0, 4:5, :])
    h_ref[...] = h.astype(BF16)
    logits = jnp.dot(h, wr_ref[...], preferred_element_type=F32, precision=HIGHEST)
    scores = _sigmoid(logits)
    work = scores + rb_ref[...]
    lane = lax.broadcasted_iota(jnp.int32, work.shape, 1).astype(F32)
    w = jnp.zeros_like(scores)
    for _ in range(TOP_K):
        mx = jnp.max(work, axis=-1, keepdims=True)
        first = jnp.min(jnp.where(work == mx, lane, float(ROUTER_LANES)), axis=-1, keepdims=True)
        pick = lane == first
        w = jnp.where(pick, scores, w)
        work = jnp.where(pick, -jnp.inf, work)
    gw_ref[...] = w / jnp.sum(w, axis=-1, keepdims=True) * ROUTE_SCALE


def _route(x, mod, norm_g, w_router, router_bias):
    tm = 512
    row = lambda i: (i, 0)
    const = lambda i: (0, 0)
    return pl.pallas_call(
        _router_kernel,
        grid=(N_TOK // tm,),
        in_specs=[
            pl.BlockSpec((tm, D_MODEL), row),
            pl.BlockSpec((1, N_MOD, D_MODEL), lambda i: (_mod_index(i, tm), 0, 0)),
            pl.BlockSpec((1, D_MODEL), const),
            pl.BlockSpec((D_MODEL, ROUTER_LANES), const),
            pl.BlockSpec((1, ROUTER_LANES), const),
        ],
        out_specs=[pl.BlockSpec((tm, D_MODEL), row), pl.BlockSpec((tm, ROUTER_LANES), row)],
        out_shape=[jax.ShapeDtypeStruct((N_TOK, D_MODEL), BF16),
                   jax.ShapeDtypeStruct((N_TOK, ROUTER_LANES), F32)],
        compiler_params=_params(("parallel",)),
        name="moe_router",
    )(x, mod, norm_g, w_router, router_bias)


def _experts_kernel(h_ref, gw_ref, wg_ref, wu_ref, wd_ref, o_ref):
    @pl.when(pl.program_id(1) == 0)
    def _():
        o_ref[...] = jnp.zeros_like(o_ref)

    h = h_ref[...]
    gw = gw_ref[0]
    mids = []
    for e in range(EXPERT_CHUNK):
        a = jnp.dot(h, wg_ref[e], preferred_element_type=F32)
        b = jnp.dot(h, wu_ref[e], preferred_element_type=F32)
        mids.append(((a * _sigmoid(a)) * b * gw[:, e:e + 1]).astype(BF16))
    mid = jnp.concatenate(mids, axis=1)
    wd = wd_ref[...].reshape(EXPERT_CHUNK * EXPERT_DIM, D_MODEL)
    o_ref[...] += jnp.dot(mid, wd, preferred_element_type=F32)


def _experts(h, gate_w, wg_all, wu_all, wd_all):
    tm = 1024
    n_chunks = N_EXPERT_ALL // EXPERT_CHUNK
    gw = gate_w.reshape(N_TOK, n_chunks, EXPERT_CHUNK).transpose(1, 0, 2)
    return pl.pallas_call(
        _experts_kernel,
        grid=(N_TOK // tm, n_chunks),
        in_specs=[
            pl.BlockSpec((tm, D_MODEL), lambda i, j: (i, 0)),
            pl.BlockSpec((1, tm, EXPERT_CHUNK), lambda i, j: (j, i, 0)),
            pl.BlockSpec((EXPERT_CHUNK, D_MODEL, EXPERT_DIM), lambda i, j: (j, 0, 0)),
            pl.BlockSpec((EXPERT_CHUNK, D_MODEL, EXPERT_DIM), lambda i, j: (j, 0, 0)),
            pl.BlockSpec((EXPERT_CHUNK, EXPERT_DIM, D_MODEL), lambda i, j: (j, 0, 0)),
        ],
        out_specs=pl.BlockSpec((tm, D_MODEL), lambda i, j: (i, 0)),
        out_shape=jax.ShapeDtypeStruct((N_TOK, D_MODEL), F32),
        compiler_params=_params(("parallel", "arbitrary")),
        name="moe_experts",
    )(h, gw, wg_all, wu_all, wd_all)


def _residual_kernel(x_ref, y_ref, mod_ref, fg_ref, o_ref, *, final):
    x = x_ref[...] + mod_ref[0, 5:6, :] * y_ref[...]
    if final:
        ms = jnp.mean(x * x, axis=-1, keepdims=True)
        x = x * lax.rsqrt(ms + EPS) * fg_ref[...]
    o_ref[...] = x


def _residual(x, y, mod, final_g, final):
    tm = 512
    row = lambda i: (i, 0)
    return pl.pallas_call(
        functools.partial(_residual_kernel, final=final),
        grid=(N_TOK // tm,),
        in_specs=[
            pl.BlockSpec((tm, D_MODEL), row),
            pl.BlockSpec((tm, D_MODEL), row),
            pl.BlockSpec((1, N_MOD, D_MODEL), lambda i: (_mod_index(i, tm), 0, 0)),
            pl.BlockSpec((1, D_MODEL), lambda i: (0, 0)),
        ],
        out_specs=pl.BlockSpec((tm, D_MODEL), row),
        out_shape=jax.ShapeDtypeStruct((N_TOK, D_MODEL), F32),
        compiler_params=_params(("parallel",)),
        name="moe_residual",
    )(x, y, mod, final_g)


def kernel(x_prompt, x_sample, c, cache_k, cache_v, state_ssm, c_ctx, w_mod, b_mod, norm1_g, norm2_g, w_in, q_norm_g, k_norm_g, w_up_attn, pool_w, pool_scale, w_up_pool, ssm_a_re, ssm_a_im, ssm_log_dt, ssm_b_re, ssm_b_im, ssm_c_re, ssm_c_im, ssm_d, w_glu, b_glu, w_up_ssm, w_out, w_router, router_bias, w_gate, w_up, w_down, ws_gate, ws_up, ws_down, final_norm_g):
    x = jnp.concatenate([x_prompt.reshape(N_PROMPT, D_MODEL), x_sample.reshape(N_SAMPLE, D_MODEL)], axis=0)
    cond = jnp.concatenate([c_ctx[None, :], c, jnp.zeros((COND_ROWS - N_COND, D_MODEL), F32)], axis=0)
    mod_all = _ada_all(cond, w_mod, b_mod)

    tm_in = 512
    cos_t, sin_t = _rope_tables(tm_in)
    head_avg = jnp.kron(jnp.eye(QK_WIDTH // HEAD_DIM, dtype=F32),
                        jnp.full((HEAD_DIM, HEAD_DIM), 1.0 / HEAD_DIM, F32)).astype(BF16)
    h0_ctx = jnp.zeros((BATCH, 2, SSM_GROUPS, SSM_STATE, 2), F32)
    final_g = final_norm_g.reshape(1, D_MODEL)

    new_k, new_v, new_s = [], [], []
    for l in range(DEPTH):
        mod = mod_all[l]
        qk_gain = jnp.concatenate([jnp.tile(q_norm_g[l], N_HEADS), jnp.tile(k_norm_g[l], N_KV_HEADS)])[None, :]
        q, k, v, pool_u, ssm_u = _in_project(
            x, mod, norm1_g[l][None, :], w_in[l][:, :MIX_WIDTH].astype(BF16), head_avg, qk_gain, cos_t, sin_t)

        k_p = k[:N_PROMPT].reshape(BATCH, SEQ, N_KV_HEADS, HEAD_DIM)
        v_p = v[:N_PROMPT].reshape(BATCH, SEQ, N_KV_HEADS, HEAD_DIM)
        k_s = k[N_PROMPT:].reshape(DEC_BATCH, DEC_SEQ, N_KV_HEADS, HEAD_DIM)
        v_s = v[N_PROMPT:].reshape(DEC_BATCH, DEC_SEQ, N_KV_HEADS, HEAD_DIM)
        new_k.append(k_p)
        new_v.append(v_p)
        keys = jnp.concatenate([cache_k[:, l], k_s], axis=1)
        vals = jnp.concatenate([cache_v[:, l], v_s], axis=1)
        o_attn = jnp.concatenate([_attention(q, k_p, v_p, BATCH, SEQ, 0),
                                  _attention(q, keys, vals, DEC_BATCH, DEC_SEQ, N_PROMPT)], axis=0)

        pool_bd = jax.scipy.linalg.block_diag(*[pool_w[l, g] for g in range(len(POOL_WINDOWS))]).astype(BF16)
        p_scale = pool_scale[l][None, :]
        o_pool = jnp.concatenate([_pool(pool_u, pool_bd, p_scale, BATCH, SEQ, 0),
                                  _pool(pool_u, pool_bd, p_scale, DEC_BATCH, DEC_SEQ, N_PROMPT)], axis=0)

        a_rows, b_mat, c_mat = _ssm_matrices(ssm_a_re[l], ssm_a_im[l], ssm_log_dt[l], ssm_b_re[l], ssm_b_im[l],
                                             ssm_c_re[l], ssm_c_im[l])
        yf_p, yb_p, st = _ssm_scan(ssm_u[:N_PROMPT].reshape(BATCH, SEQ, SSM_WIDTH), h0_ctx, a_rows, b_mat, c_mat)
        yf_s, yb_s, _ = _ssm_scan(ssm_u[N_PROMPT:].reshape(DEC_BATCH, DEC_SEQ, SSM_WIDTH), state_ssm[:, l],
                                  a_rows, b_mat, c_mat)
        new_s.append(st)
        y_f = jnp.concatenate([yf_p, yf_s], axis=0)
        y_b = jnp.concatenate([yb_p, yb_s], axis=0)

        x = _merge(x, mod, norm1_g[l][None, :], w_in[l][:, MIX_WIDTH:].astype(BF16), o_attn, o_pool, ssm_u,
                   y_f, y_b, ssm_d[l].reshape(1, SSM_WIDTH), w_glu[l].astype(BF16), b_glu[l][None, :],
                   w_up_attn[l].astype(BF16), w_up_pool[l].astype(BF16), w_up_ssm[l].astype(BF16),
                   w_out[l].astype(BF16))

        w_r = jnp.pad(w_router[l], ((0, 0), (0, ROUTER_LANES - N_EXPERTS)))
        r_b = jnp.concatenate([router_bias[l], jnp.full((ROUTER_LANES - N_EXPERTS,), -jnp.inf, F32)])[None, :]
        h2, gate_w = _route(x, mod, norm2_g[l][None, :], w_r, r_b)
        gate_all = jnp.concatenate([gate_w[:, :N_EXPERTS], jnp.ones((N_TOK, 1), F32)], axis=1)
        wg_all = jnp.concatenate([w_gate[l], ws_gate[l][None]], axis=0).astype(BF16)
        wu_all = jnp.concatenate([w_up[l], ws_up[l][None]], axis=0).astype(BF16)
        wd_all = jnp.concatenate([w_down[l], ws_down[l][None]], axis=0).astype(BF16)
        y = _experts(h2, gate_all, wg_all, wu_all, wd_all)
        x = _residual(x, y, mod, final_g, final=(l == DEPTH - 1))

    y_prompt = x[:N_PROMPT].reshape(BATCH, SEQ, D_MODEL)
    y_sample = x[N_PROMPT:].reshape(DEC_BATCH, DEC_SEQ, D_MODEL)
    return (y_prompt, y_sample, jnp.stack(new_k, axis=1), jnp.stack(new_v, axis=1), jnp.stack(new_s, axis=1))
```

```python
import functools
import math

import jax
import jax.numpy as jnp
import numpy as np
from jax import lax
from jax.experimental import pallas as pl
from jax.experimental.pallas import tpu as pltpu

D_MODEL = 1024
BATCH = 32
SEQ = 256
DEPTH = 2
DEC_BATCH = 4
DEC_SEQ = 4096
PAST_LEN = 256
GRID_W = 64
EPS = 1e-6
N_MOD = 6
HEAD_DIM = 64
N_HEADS = 8
N_KV_HEADS = 2
ATTN_WIDTH = N_HEADS * HEAD_DIM
KV_WIDTH = N_KV_HEADS * HEAD_DIM
ROPE_BASE = 10000.0
ROPE_PAIRS_PER_AXIS = HEAD_DIM // 4
POOL_WINDOWS = (2, 4, 8, 16)
POOL_GROUP = 64
POOL_WIDTH = len(POOL_WINDOWS) * POOL_GROUP
SSM_H = 16
SSM_GROUPS = 16
SSM_WIDTH = SSM_H * SSM_GROUPS
SSM_STATE = 64
N_EXPERTS = 64
TOP_K = 8
EXPERT_DIM = 256
SHARED_DIM = 256
ROUTE_SCALE = 2.5

N_PROMPT = BATCH * SEQ
N_SAMPLE = DEC_BATCH * DEC_SEQ
N_TOK = N_PROMPT + N_SAMPLE
N_COND = 1 + DEC_BATCH
COND_ROWS = 8
QK_WIDTH = ATTN_WIDTH + KV_WIDTH
MIX_WIDTH = QK_WIDTH + KV_WIDTH + POOL_WIDTH + SSM_WIDTH
GATE_WIDTH = 3 * D_MODEL
SSM_COLS = SSM_GROUPS * SSM_STATE
SSM_LANES = 2 * SSM_COLS
SSM_SEQS = 4
SSM_CHUNK = 128
ROUTER_LANES = 128
EXPERT_CHUNK = 5
N_EXPERT_ALL = N_EXPERTS + 1

SEQ_TILE = SEQ
SSM_ROWS = 2 * SSM_SEQS
SSM_GROUPS_PROMPT = BATCH // SSM_SEQS
SCAN_T = SSM_GROUPS_PROMPT * SEQ + DEC_SEQ
SCAN_CHUNKS = SCAN_T // SSM_CHUNK
PROMPT_CHUNKS = SSM_GROUPS_PROMPT * SEQ // SSM_CHUNK

LANE = 128
VMEM_LIMIT = 56 * 1024 * 1024

F32 = jnp.float32
BF16 = jnp.bfloat16
HIGHEST = lax.Precision.HIGHEST


def _sigmoid(x):
    return 1.0 / (1.0 + jnp.exp(-x))


def _params(dims, vmem=VMEM_LIMIT):
    return pltpu.CompilerParams(dimension_semantics=dims, vmem_limit_bytes=vmem)


def _mod_index(i, tm):
    p = N_PROMPT // tm
    t = DEC_SEQ // tm
    return jnp.where(i < p, 0, 1 + (i - p) // t)


def _scan_block(i):
    k = i - BATCH
    tiles = DEC_SEQ // SEQ_TILE
    rb = jnp.where(i < BATCH, i // SSM_SEQS, SSM_GROUPS_PROMPT + k % tiles)
    slot = jnp.where(i < BATCH, i % SSM_SEQS, k // tiles)
    return rb, slot


def _norm_mod(x, g, shift, scale):
    ms = jnp.mean(x * x, axis=-1, keepdims=True)
    return (x * lax.rsqrt(ms + EPS) * g) * (1.0 + scale) + shift


def _mod_kernel(cond_ref, w_ref, b_ref, o_ref):
    c = cond_ref[...]
    s = c * _sigmoid(c)
    o_ref[0] = jnp.dot(s, w_ref[0], preferred_element_type=F32, precision=HIGHEST) + b_ref[0]


def _ada_all(cond, w_mod, b_mod):
    tn = 1536
    width = N_MOD * D_MODEL
    out = pl.pallas_call(
        _mod_kernel,
        grid=(DEPTH, width // tn),
        in_specs=[
            pl.BlockSpec((COND_ROWS, D_MODEL), lambda l, j: (0, 0)),
            pl.BlockSpec((1, D_MODEL, tn), lambda l, j: (l, 0, j)),
            pl.BlockSpec((1, 1, tn), lambda l, j: (l, 0, j)),
        ],
        out_specs=pl.BlockSpec((1, COND_ROWS, tn), lambda l, j: (l, 0, j)),
        out_shape=jax.ShapeDtypeStruct((DEPTH, COND_ROWS, width), F32),
        compiler_params=_params(("parallel", "parallel")),
        name="ada_mod",
    )(cond, w_mod, b_mod.reshape(DEPTH, 1, width))
    return out.reshape(DEPTH, COND_ROWS, N_MOD, D_MODEL)


def _inproj_kernel(x_ref, mod_ref, g_ref, w_ref, bd_ref, qkg_ref, cos_ref, sin_ref,
                   q_ref, k_ref, v_ref, pu_ref, su_ref):
    h = _norm_mod(x_ref[...], g_ref[...], mod_ref[0, 0:1, :], mod_ref[0, 1:2, :])
    z = jnp.dot(h.astype(BF16), w_ref[...], preferred_element_type=F32)
    qk = z[:, :QK_WIDTH]
    qq = qk * qk
    hi = qq.astype(BF16)
    lo = (qq - hi.astype(F32)).astype(BF16)
    ms = (jnp.dot(hi, bd_ref[...], preferred_element_type=F32)
          + jnp.dot(lo, bd_ref[...], preferred_element_type=F32))
    qkn = qk * lax.rsqrt(ms + EPS) * qkg_ref[...]
    parts = []
    for c in range(QK_WIDTH // LANE):
        blk = qkn[:, c * LANE:(c + 1) * LANE]
        nxt = pltpu.roll(blk, LANE - 1, axis=1)
        prv = pltpu.roll(blk, 1, axis=1)
        lane = lax.broadcasted_iota(jnp.int32, blk.shape, 1)
        parts.append(jnp.where((lane & 1) == 0, nxt, prv))
    partner = jnp.concatenate(parts, axis=1)
    qkr = qkn * cos_ref[...] + partner * sin_ref[...]
    q_ref[...] = (qkr[:, :ATTN_WIDTH] * (HEAD_DIM ** -0.5)).astype(BF16)
    k_ref[...] = qkr[:, ATTN_WIDTH:QK_WIDTH]
    v_ref[...] = z[:, QK_WIDTH:QK_WIDTH + KV_WIDTH]
    pu_ref[...] = z[:, QK_WIDTH + KV_WIDTH:QK_WIDTH + KV_WIDTH + POOL_WIDTH]
    su = z[:, QK_WIDTH + KV_WIDTH + POOL_WIDTH:MIX_WIDTH]
    su_ref[...] = jnp.concatenate([su, su], axis=1)


def _in_project(x, mod, norm_g, w_mix, bd, qk_gain, cos_t, sin_t):
    tm = SEQ_TILE
    p_tiles = N_PROMPT // tm
    s_tiles = DEC_SEQ // tm

    def rope_idx(i):
        return (jnp.where(i < p_tiles, 0, 1 + (i - p_tiles) % s_tiles), 0)

    row = lambda i: (i, 0)
    const = lambda i: (0, 0)
    return pl.pallas_call(
        _inproj_kernel,
        grid=(N_TOK // tm,),
        in_specs=[
            pl.BlockSpec((tm, D_MODEL), row),
            pl.BlockSpec((1, N_MOD, D_MODEL), lambda i: (_mod_index(i, tm), 0, 0)),
            pl.BlockSpec((1, D_MODEL), const),
            pl.BlockSpec((D_MODEL, MIX_WIDTH), const),
            pl.BlockSpec((QK_WIDTH, QK_WIDTH), const),
            pl.BlockSpec((1, QK_WIDTH), const),
            pl.BlockSpec((tm, QK_WIDTH), rope_idx),
            pl.BlockSpec((tm, QK_WIDTH), rope_idx),
        ],
        out_specs=[
            pl.BlockSpec((tm, ATTN_WIDTH), row),
            pl.BlockSpec((tm, KV_WIDTH), row),
            pl.BlockSpec((tm, KV_WIDTH), row),
            pl.BlockSpec((tm, POOL_WIDTH), row),
            pl.BlockSpec((tm, 2 * SSM_WIDTH), _scan_block),
        ],
        out_shape=[
            jax.ShapeDtypeStruct((N_TOK, ATTN_WIDTH), BF16),
            jax.ShapeDtypeStruct((N_TOK, KV_WIDTH), F32),
            jax.ShapeDtypeStruct((N_TOK, KV_WIDTH), F32),
            jax.ShapeDtypeStruct((N_TOK, POOL_WIDTH), F32),
            jax.ShapeDtypeStruct((SCAN_T, SSM_ROWS * SSM_WIDTH), F32),
        ],
        compiler_params=_params(("parallel",)),
        name="in_project",
    )(x, mod, norm_g, w_mix, bd, qk_gain, cos_t, sin_t)


def _rope_tables(tm):
    rows = DEC_SEQ // GRID_W
    row = jnp.repeat(jnp.arange(rows, dtype=F32), GRID_W)
    col = jnp.tile(jnp.arange(GRID_W, dtype=F32), rows)
    inv_freq = ROPE_BASE ** (-jnp.arange(ROPE_PAIRS_PER_AXIS, dtype=F32) / ROPE_PAIRS_PER_AXIS)
    ang = jnp.concatenate([row[:, None] * inv_freq, col[:, None] * inv_freq], axis=-1)
    cos = jnp.repeat(jnp.cos(ang), 2, axis=-1)
    sin = jnp.repeat(jnp.sin(ang), 2, axis=-1) * jnp.tile(jnp.array([-1.0, 1.0], F32), HEAD_DIM // 2)
    n_rep = QK_WIDTH // HEAD_DIM
    cos = jnp.concatenate([jnp.ones((tm, HEAD_DIM), F32), cos], axis=0)
    sin = jnp.concatenate([jnp.zeros((tm, HEAD_DIM), F32), sin], axis=0)
    return jnp.tile(cos, (1, n_rep)), jnp.tile(sin, (1, n_rep))


def _attn_kernel(q_ref, kt_ref, v_ref, *rest):
    o_ref = rest[-1]
    q = q_ref[...]
    kt = kt_ref[0]
    v = v_ref[0]
    lane = lax.broadcasted_iota(jnp.int32, q.shape, 1)
    first = lane < HEAD_DIM
    outs = []
    for keep in (first, lane >= HEAD_DIM):
        qh = jnp.where(keep, q, jnp.zeros_like(q))
        s = jnp.dot(qh, kt, preferred_element_type=F32)
        m = jnp.max(s, axis=-1, keepdims=True)
        p = jnp.exp(s - m)
        denom = jnp.sum(p, axis=-1, keepdims=True)
        o = jnp.dot(p.astype(BF16), v, preferred_element_type=F32)
        outs.append(o / denom)
    o_ref[...] = jnp.where(first, outs[0], outs[1]).astype(o_ref.dtype)


def _attention(q, keys, vals, n_batch, n_q, row0, prev=None):
    n_keys = keys.shape[1]
    tq = 256
    kt = jnp.transpose(keys, (0, 2, 3, 1)).astype(BF16)
    kt = jnp.concatenate([kt, kt], axis=2).reshape(n_batch * N_KV_HEADS, 2 * HEAD_DIM, n_keys)
    vv = jnp.transpose(vals, (0, 2, 1, 3)).astype(BF16)
    vv = jnp.concatenate([vv, vv], axis=3).reshape(n_batch * N_KV_HEADS, n_keys, 2 * HEAD_DIM)
    q_tiles = n_q // tq
    base = row0 // tq
    pairs = N_HEADS // 2
    pairs_per_kv = pairs // N_KV_HEADS
    q_spec = pl.BlockSpec((tq, LANE), lambda b, p, i: (base + b * q_tiles + i, p))
    in_specs = [
        q_spec,
        pl.BlockSpec((1, LANE, n_keys), lambda b, p, i: (b * N_KV_HEADS + p // pairs_per_kv, 0, 0)),
        pl.BlockSpec((1, n_keys, LANE), lambda b, p, i: (b * N_KV_HEADS + p // pairs_per_kv, 0, 0)),
    ]
    args = [q, kt, vv]
    aliases = {}
    if prev is not None:
        in_specs.append(pl.BlockSpec(memory_space=pl.ANY))
        args.append(prev)
        aliases = {3: 0}
    return pl.pallas_call(
        _attn_kernel,
        grid=(n_batch, pairs, q_tiles),
        in_specs=in_specs,
        out_specs=q_spec,
        out_shape=jax.ShapeDtypeStruct((N_TOK, ATTN_WIDTH), BF16),
        input_output_aliases=aliases,
        compiler_params=_params(("parallel", "parallel", "parallel")),
        name="attention",
    )(*args)


def _pool_kernel(u_ref, w_ref, sc_ref, *rest, n_seq):
    o_ref, pad_ref = rest[-2:]
    halo = 8
    u = u_ref[...]
    zeros = jnp.zeros((halo, POOL_WIDTH), F32)
    pad_ref[0:halo, :] = zeros
    pad_ref[halo + n_seq:2 * halo + n_seq, :] = zeros
    pad_ref[halo:halo + n_seq, :] = u

    def sh(j):
        return pad_ref[halo + j:halo + j + n_seq, :]

    t2 = sh(-1) + u
    t4 = t2 + sh(-2) + sh(1)
    t8 = t4 + sh(-4) + sh(-3) + sh(2) + sh(3)
    t16 = t8 + sh(-8) + sh(-7) + sh(-6) + sh(-5) + sh(4) + sh(5) + sh(6) + sh(7)
    grp = lax.broadcasted_iota(jnp.int32, u.shape, 1) >> 6
    t = lax.broadcasted_iota(jnp.int32, u.shape, 0)
    tot = jnp.where(grp == 0, t2, jnp.where(grp == 1, t4, jnp.where(grp == 2, t8, t16)))
    half = jnp.where(grp == 0, 1, jnp.where(grp == 1, 2, jnp.where(grp == 2, 4, 8)))
    lo = jnp.maximum(t - half, 0)
    hi = jnp.minimum(t + half, n_seq)
    pooled = tot / (hi - lo).astype(F32) - u
    mixed = jnp.dot(pooled.astype(BF16), w_ref[...], preferred_element_type=F32)
    o_ref[...] = (mixed * sc_ref[...]).astype(o_ref.dtype)


def _pool(u, w_bd, scale, n_batch, n_seq, row0, prev=None):
    base = row0 // n_seq
    seq_spec = pl.BlockSpec((n_seq, POOL_WIDTH), lambda b: (base + b, 0))
    in_specs = [
        seq_spec,
        pl.BlockSpec((POOL_WIDTH, POOL_WIDTH), lambda b: (0, 0)),
        pl.BlockSpec((1, POOL_WIDTH), lambda b: (0, 0)),
    ]
    args = [u, w_bd, scale]
    aliases = {}
    if prev is not None:
        in_specs.append(pl.BlockSpec(memory_space=pl.ANY))
        args.append(prev)
        aliases = {3: 0}
    return pl.pallas_call(
        functools.partial(_pool_kernel, n_seq=n_seq),
        grid=(n_batch,),
        in_specs=in_specs,
        out_specs=seq_spec,
        out_shape=jax.ShapeDtypeStruct((N_TOK, POOL_WIDTH), BF16),
        input_output_aliases=aliases,
        scratch_shapes=[pltpu.VMEM((n_seq + 16, POOL_WIDTH), F32)],
        compiler_params=_params(("parallel",)),
        name="pool_mixer",
    )(*args)


def _scan_chunk_first(i):
    per_seq = SEQ // SSM_CHUNK
    return jnp.where(i < PROMPT_CHUNKS, i % per_seq == 0, i == PROMPT_CHUNKS)


def _scan_bwd_chunk(i):
    per_seq = SEQ // SSM_CHUNK
    return jnp.where(i < PROMPT_CHUNKS, (i // per_seq) * per_seq + (per_seq - 1 - i % per_seq),
                     PROMPT_CHUNKS + SCAN_CHUNKS - 1 - i)


def _scan_group(i):
    return jnp.where(i < PROMPT_CHUNKS, i // (SEQ // SSM_CHUNK), SSM_GROUPS_PROMPT)


def _ssm_kernel(uf_ref, ub_ref, a_ref, b_ref, c_ref, h0_ref, yf_ref, yb_ref, hfin_ref, bu_ref, st_ref, tmp_ref):
    rows = SSM_ROWS

    @pl.when(_scan_chunk_first(pl.program_id(0)))
    def _():
        st_ref[...] = h0_ref[0]

    for t in range(SSM_CHUNK):
        tmp_ref[t * rows:(t + 1) * rows, :] = ub_ref[(SSM_CHUNK - 1 - t) * rows:(SSM_CHUNK - t) * rows, :]
    r_idx = lax.broadcasted_iota(jnp.int32, (SSM_CHUNK * rows, SSM_WIDTH), 0)
    bwd = (r_idx & 1) == 1
    zero = jnp.zeros((SSM_CHUNK * rows, SSM_WIDTH), F32)
    lhs = jnp.concatenate([jnp.where(bwd, zero, uf_ref[...]), jnp.where(bwd, tmp_ref[...], zero)], axis=1)
    bu_ref[...] = jnp.dot(lhs.astype(BF16), b_ref[...], preferred_element_type=F32)
    a_re = a_ref[:, :SSM_COLS]
    a_im = a_ref[:, SSM_COLS:]

    def step(t, carry):
        h_re, h_im = carry
        r = pl.multiple_of(t * rows, rows)
        n_re = a_re * h_re - a_im * h_im + bu_ref[pl.ds(r, rows), :SSM_COLS]
        n_im = a_re * h_im + a_im * h_re + bu_ref[pl.ds(r, rows), SSM_COLS:]
        bu_ref[pl.ds(r, rows), :SSM_COLS] = n_re
        bu_ref[pl.ds(r, rows), SSM_COLS:] = n_im
        return n_re, n_im

    h_re, h_im = lax.fori_loop(0, SSM_CHUNK, step, (st_ref[:, :SSM_COLS], st_ref[:, SSM_COLS:]))
    st_ref[:, :SSM_COLS] = h_re
    st_ref[:, SSM_COLS:] = h_im
    hfin_ref[0] = st_ref[...]
    yy = jnp.dot(bu_ref[...].astype(BF16), c_ref[...], preferred_element_type=F32)
    yf_ref[...] = yy[:, :SSM_WIDTH]
    tmp_ref[...] = yy[:, SSM_WIDTH:]
    for t in range(SSM_CHUNK):
        yb_ref[t * rows:(t + 1) * rows, :] = tmp_ref[(SSM_CHUNK - 1 - t) * rows:(SSM_CHUNK - t) * rows, :]


def _ssm_scan(su8, h0_sample, a_rows, b_mat, c_mat):
    rows = SSM_ROWS
    blk = SSM_CHUNK * rows
    n_groups = SSM_GROUPS_PROMPT + 1
    u8 = su8.reshape(SCAN_T * rows, SSM_WIDTH)
    hs = h0_sample.astype(F32).reshape(SSM_SEQS, 2, SSM_COLS, 2).transpose(0, 1, 3, 2).reshape(1, rows, SSM_LANES)
    hh = jnp.concatenate([jnp.zeros((SSM_GROUPS_PROMPT, rows, SSM_LANES), F32), hs], axis=0)
    fwd = lambda i: (i, 0)
    bwd = lambda i: (_scan_bwd_chunk(i), 0)
    grp = lambda i: (_scan_group(i), 0, 0)
    const = lambda i: (0, 0)
    y_f, y_b, hfin = pl.pallas_call(
        _ssm_kernel,
        grid=(SCAN_CHUNKS,),
        in_specs=[
            pl.BlockSpec((blk, SSM_WIDTH), fwd),
            pl.BlockSpec((blk, SSM_WIDTH), bwd),
            pl.BlockSpec((rows, SSM_LANES), const),
            pl.BlockSpec((2 * SSM_WIDTH, SSM_LANES), const),
            pl.BlockSpec((SSM_LANES, 2 * SSM_WIDTH), const),
            pl.BlockSpec((1, rows, SSM_LANES), grp),
        ],
        out_specs=[
            pl.BlockSpec((blk, SSM_WIDTH), fwd),
            pl.BlockSpec((blk, SSM_WIDTH), bwd),
            pl.BlockSpec((1, rows, SSM_LANES), grp),
        ],
        out_shape=[
            jax.ShapeDtypeStruct((SCAN_T * rows, SSM_WIDTH), F32),
            jax.ShapeDtypeStruct((SCAN_T * rows, SSM_WIDTH), F32),
            jax.ShapeDtypeStruct((n_groups, rows, SSM_LANES), F32),
        ],
        scratch_shapes=[pltpu.VMEM((blk, SSM_LANES), F32), pltpu.VMEM((rows, SSM_LANES), F32),
                        pltpu.VMEM((blk, SSM_WIDTH), F32)],
        compiler_params=_params(("arbitrary",)),
        name="ssm_scan",
    )(u8, u8, a_rows, b_mat, c_mat, hh)
    fin = hfin[:SSM_GROUPS_PROMPT].reshape(SSM_GROUPS_PROMPT, SSM_SEQS, 2, 2, SSM_GROUPS, SSM_STATE)
    fin = fin.transpose(0, 1, 2, 4, 5, 3).reshape(BATCH, 2, SSM_GROUPS, SSM_STATE, 2)
    width = SSM_ROWS * SSM_WIDTH
    return y_f.reshape(SCAN_T, width), y_b.reshape(SCAN_T, width), fin


def _ssm_matrices(a_re, a_im, log_dt, b_re, b_im, c_re, c_im):
    lam = lax.complex(a_re.astype(F32), a_im.astype(F32))
    dt = jnp.exp(log_dt.astype(F32))[..., None]
    a_bar = jnp.exp(lam * dt)
    b_bar = ((a_bar - 1.0) / lam)[..., None] * lax.complex(b_re.astype(F32), b_im.astype(F32))
    a_dir = jnp.concatenate([jnp.real(a_bar).reshape(2, SSM_COLS), jnp.imag(a_bar).reshape(2, SSM_COLS)], axis=-1)
    a_rows = jnp.tile(a_dir, (SSM_SEQS, 1))
    eye = jnp.eye(SSM_GROUPS, dtype=F32)
    bt = jnp.transpose(b_bar, (0, 1, 3, 2))
    b_real = jnp.einsum('dghp,ge->dghep', jnp.real(bt), eye).reshape(2 * SSM_WIDTH, SSM_COLS)
    b_imag = jnp.einsum('dghp,ge->dghep', jnp.imag(bt), eye).reshape(2 * SSM_WIDTH, SSM_COLS)
    b_mat = jnp.concatenate([b_real, b_imag], axis=-1).astype(BF16)
    cr = jnp.transpose(c_re.astype(F32), (0, 1, 3, 2))
    ci = jnp.transpose(c_im.astype(F32), (0, 1, 3, 2))
    c_real = jnp.einsum('dgph,ge->gpdeh', cr, eye).reshape(SSM_COLS, 2 * SSM_WIDTH)
    c_imag = jnp.einsum('dgph,ge->gpdeh', -ci, eye).reshape(SSM_COLS, 2 * SSM_WIDTH)
    c_mat = jnp.concatenate([c_real, c_imag], axis=0).astype(BF16)
    return a_rows, b_mat, c_mat


def _merge_kernel(x_ref, mod_ref, g_ref, wg_ref, oa_ref, op_ref, su_ref, yf_ref, yb_ref, d_ref,
                  wglu_ref, bglu_ref, wua_ref, wup_ref, wus_ref, wo_ref, o_ref):
    x = x_ref[...]
    h = _norm_mod(x, g_ref[...], mod_ref[0, 0:1, :], mod_ref[0, 1:2, :])
    gates = _sigmoid(jnp.dot(h.astype(BF16), wg_ref[...], preferred_element_type=F32))
    y = d_ref[...] * su_ref[...] + yf_ref[...] + yb_ref[...]
    y = 0.5 * y * (1.0 + jnp.tanh(math.sqrt(2.0 / math.pi) * (y + 0.044715 * (y * y * y))))
    glu = jnp.dot(y.astype(BF16), wglu_ref[...], preferred_element_type=F32) + bglu_ref[...]
    o_ssm = y * _sigmoid(glu)
    m = (gates[:, :D_MODEL] * jnp.dot(oa_ref[...], wua_ref[...], preferred_element_type=F32)
         + gates[:, D_MODEL:2 * D_MODEL] * jnp.dot(op_ref[...], wup_ref[...], preferred_element_type=F32)
         + gates[:, 2 * D_MODEL:] * jnp.dot(o_ssm.astype(BF16), wus_ref[...], preferred_element_type=F32))
    o_ref[...] = x + mod_ref[0, 2:3, :] * jnp.dot(m.astype(BF16), wo_ref[...], preferred_element_type=F32)


def _merge(x, mod, norm_g, w_gates, o_attn, o_pool, ssm_u, y_f, y_b, d_skip, w_glu, b_glu,
           w_up_attn, w_up_pool, w_up_ssm, w_out):
    tm = SEQ_TILE
    row = lambda i: (i, 0)
    const = lambda i: (0, 0)

    def scan_fwd(i):
        rb, slot = _scan_block(i)
        return rb, 2 * slot

    def scan_bwd(i):
        rb, slot = _scan_block(i)
        return rb, 2 * slot + 1

    return pl.pallas_call(
        _merge_kernel,
        grid=(N_TOK // tm,),
        in_specs=[
            pl.BlockSpec((tm, D_MODEL), row),
            pl.BlockSpec((1, N_MOD, D_MODEL), lambda i: (_mod_index(i, tm), 0, 0)),
            pl.BlockSpec((1, D_MODEL), const),
            pl.BlockSpec((D_MODEL, GATE_WIDTH), const),
            pl.BlockSpec((tm, ATTN_WIDTH), row),
            pl.BlockSpec((tm, POOL_WIDTH), row),
            pl.BlockSpec((tm, SSM_WIDTH), scan_fwd),
            pl.BlockSpec((tm, SSM_WIDTH), scan_fwd),
            pl.BlockSpec((tm, SSM_WIDTH), scan_bwd),
            pl.BlockSpec((1, SSM_WIDTH), const),
            pl.BlockSpec((SSM_WIDTH, SSM_WIDTH), const),
            pl.BlockSpec((1, SSM_WIDTH), const),
            pl.BlockSpec((ATTN_WIDTH, D_MODEL), const),
            pl.BlockSpec((POOL_WIDTH, D_MODEL), const),
            pl.BlockSpec((SSM_WIDTH, D_MODEL), const),
            pl.BlockSpec((D_MODEL, D_MODEL), const),
        ],
        out_specs=pl.BlockSpec((tm, D_MODEL), row),
        out_shape=jax.ShapeDtypeStruct((N_TOK, D_MODEL), F32),
        compiler_params=_params(("parallel",)),
        name="merge_branches",
    )(x, mod, norm_g, w_gates, o_attn, o_pool, ssm_u, y_f, y_b, d_skip, w_glu, b_glu,
      w_up_attn, w_up_pool, w_up_ssm, w_out)


def _router_kernel(x_ref, mod_ref, g_ref, wr_ref, rb_ref, h_ref, gw_ref):
    h = _norm_mod(x_ref[...], g_ref[...], mod_ref[0, 3:4, :], mod_ref[0, 4:5, :])
    h_ref[...] = h.astype(BF16)
    logits = jnp.dot(h, wr_ref[...], preferred_element_type=F32, precision=HIGHEST)
    scores = _sigmoid(logits)
    work = scores + rb_ref[...]
    lane = lax.broadcasted_iota(jnp.int32, work.shape, 1).astype(F32)
    w = jnp.zeros_like(scores)
    for _ in range(TOP_K):
        mx = jnp.max(work, axis=-1, keepdims=True)
        first = jnp.min(jnp.where(work == mx, lane, float(ROUTER_LANES)), axis=-1, keepdims=True)
        pick = lane == first
        w = jnp.where(pick, scores, w)
        work = jnp.where(pick, -jnp.inf, work)
    gw_ref[...] = w / jnp.sum(w, axis=-1, keepdims=True) * ROUTE_SCALE


def _route(x, mod, norm_g, w_router, router_bias):
    tm = 512
    row = lambda i: (i, 0)
    const = lambda i: (0, 0)
    return pl.pallas_call(
        _router_kernel,
        grid=(N_TOK // tm,),
        in_specs=[
            pl.BlockSpec((tm, D_MODEL), row),
            pl.BlockSpec((1, N_MOD, D_MODEL), lambda i: (_mod_index(i, tm), 0, 0)),
            pl.BlockSpec((1, D_MODEL), const),
            pl.BlockSpec((D_MODEL, ROUTER_LANES), const),
            pl.BlockSpec((1, ROUTER_LANES), const),
        ],
        out_specs=[pl.BlockSpec((tm, D_MODEL), row), pl.BlockSpec((tm, ROUTER_LANES), row)],
        out_shape=[jax.ShapeDtypeStruct((N_TOK, D_MODEL), BF16),
                   jax.ShapeDtypeStruct((N_TOK, ROUTER_LANES), F32)],
        compiler_params=_params(("parallel",)),
        name="moe_router",
    )(x, mod, norm_g, w_router, router_bias)


def _experts_kernel(h_ref, gw_ref, wg_ref, wu_ref, wd_ref, o_ref):
    @pl.when(pl.program_id(1) == 0)
    def _():
        o_ref[...] = jnp.zeros_like(o_ref)

    h = h_ref[...]
    gw = gw_ref[0]
    mids = []
    for e in range(EXPERT_CHUNK):
        a = jnp.dot(h, wg_ref[e], preferred_element_type=F32)
        b = jnp.dot(h, wu_ref[e], preferred_element_type=F32)
        mids.append(((a * _sigmoid(a)) * b * gw[:, e:e + 1]).astype(BF16))
    mid = jnp.concatenate(mids, axis=1)
    wd = wd_ref[...].reshape(EXPERT_CHUNK * EXPERT_DIM, D_MODEL)
    o_ref[...] += jnp.dot(mid, wd, preferred_element_type=F32)


def _experts(h, gate_w, wg_all, wu_all, wd_all):
    tm = 1024
    n_chunks = N_EXPERT_ALL // EXPERT_CHUNK
    gw = gate_w.reshape(N_TOK, n_chunks, EXPERT_CHUNK).transpose(1, 0, 2)
    return pl.pallas_call(
        _experts_kernel,
        grid=(N_TOK // tm, n_chunks),
        in_specs=[
            pl.BlockSpec((tm, D_MODEL), lambda i, j: (i, 0)),
            pl.BlockSpec((1, tm, EXPERT_CHUNK), lambda i, j: (j, i, 0)),
            pl.BlockSpec((EXPERT_CHUNK, D_MODEL, EXPERT_DIM), lambda i, j: (j, 0, 0)),
            pl.BlockSpec((EXPERT_CHUNK, D_MODEL, EXPERT_DIM), lambda i, j: (j, 0, 0)),
            pl.BlockSpec((EXPERT_CHUNK, EXPERT_DIM, D_MODEL), lambda i, j: (j, 0, 0)),
        ],
        out_specs=pl.BlockSpec((tm, D_MODEL), lambda i, j: (i, 0)),
        out_shape=jax.ShapeDtypeStruct((N_TOK, D_MODEL), F32),
        compiler_params=_params(("parallel", "arbitrary")),
        name="moe_experts",
    )(h, gw, wg_all, wu_all, wd_all)


def _residual_kernel(x_ref, y_ref, mod_ref, fg_ref, o_ref, *, final):
    x = x_ref[...] + mod_ref[0, 5:6, :] * y_ref[...]
    if final:
        ms = jnp.mean(x * x, axis=-1, keepdims=True)
        x = x * lax.rsqrt(ms + EPS) * fg_ref[...]
    o_ref[...] = x


def _residual(x, y, mod, final_g, final, row0=0, n_rows=N_TOK):
    tm = 512
    base = row0 // tm
    row = lambda i: (base + i, 0)
    return pl.pallas_call(
        functools.partial(_residual_kernel, final=final),
        grid=(n_rows // tm,),
        in_specs=[
            pl.BlockSpec((tm, D_MODEL), row),
            pl.BlockSpec((tm, D_MODEL), row),
            pl.BlockSpec((1, N_MOD, D_MODEL), lambda i: (_mod_index(base + i, tm), 0, 0)),
            pl.BlockSpec((1, D_MODEL), lambda i: (0, 0)),
        ],
        out_specs=pl.BlockSpec((tm, D_MODEL), lambda i: (i, 0)),
        out_shape=jax.ShapeDtypeStruct((n_rows, D_MODEL), F32),
        compiler_params=_params(("parallel",)),
        name="moe_residual",
    )(x, y, mod, final_g)


def kernel(x_prompt, x_sample, c, cache_k, cache_v, state_ssm, c_ctx, w_mod, b_mod, norm1_g, norm2_g, w_in, q_norm_g, k_norm_g, w_up_attn, pool_w, pool_scale, w_up_pool, ssm_a_re, ssm_a_im, ssm_log_dt, ssm_b_re, ssm_b_im, ssm_c_re, ssm_c_im, ssm_d, w_glu, b_glu, w_up_ssm, w_out, w_router, router_bias, w_gate, w_up, w_down, ws_gate, ws_up, ws_down, final_norm_g):
    x = jnp.concatenate([x_prompt.reshape(N_PROMPT, D_MODEL), x_sample.reshape(N_SAMPLE, D_MODEL)], axis=0)
    cond = jnp.concatenate([c_ctx[None, :], c, jnp.zeros((COND_ROWS - N_COND, D_MODEL), F32)], axis=0)
    mod_all = _ada_all(cond, w_mod, b_mod)

    cos_t, sin_t = _rope_tables(SEQ_TILE)
    head_avg = jnp.kron(jnp.eye(QK_WIDTH // HEAD_DIM, dtype=F32),
                        jnp.full((HEAD_DIM, HEAD_DIM), 1.0 / HEAD_DIM, F32)).astype(BF16)
    final_g = final_norm_g.reshape(1, D_MODEL)

    new_k, new_v, new_s = [], [], []
    for l in range(DEPTH):
        mod = mod_all[l]
        qk_gain = jnp.concatenate([jnp.tile(q_norm_g[l], N_HEADS), jnp.tile(k_norm_g[l], N_KV_HEADS)])[None, :]
        q, k, v, pool_u, ssm_u = _in_project(
            x, mod, norm1_g[l][None, :], w_in[l][:, :MIX_WIDTH].astype(BF16), head_avg, qk_gain, cos_t, sin_t)

        k_p = k[:N_PROMPT].reshape(BATCH, SEQ, N_KV_HEADS, HEAD_DIM)
        v_p = v[:N_PROMPT].reshape(BATCH, SEQ, N_KV_HEADS, HEAD_DIM)
        k_s = k[N_PROMPT:].reshape(DEC_BATCH, DEC_SEQ, N_KV_HEADS, HEAD_DIM)
        v_s = v[N_PROMPT:].reshape(DEC_BATCH, DEC_SEQ, N_KV_HEADS, HEAD_DIM)
        new_k.append(k_p)
        new_v.append(v_p)
        keys = jnp.concatenate([cache_k[:, l], k_s], axis=1)
        vals = jnp.concatenate([cache_v[:, l], v_s], axis=1)
        o_attn = _attention(q, k_p, v_p, BATCH, SEQ, 0)
        o_attn = _attention(q, keys, vals, DEC_BATCH, DEC_SEQ, N_PROMPT, prev=o_attn)

        pool_bd = jax.scipy.linalg.block_diag(*[pool_w[l, g] for g in range(len(POOL_WINDOWS))]).astype(BF16)
        p_scale = pool_scale[l][None, :]
        o_pool = _pool(pool_u, pool_bd, p_scale, BATCH, SEQ, 0)
        o_pool = _pool(pool_u, pool_bd, p_scale, DEC_BATCH, DEC_SEQ, N_PROMPT, prev=o_pool)

        a_rows, b_mat, c_mat = _ssm_matrices(ssm_a_re[l], ssm_a_im[l], ssm_log_dt[l], ssm_b_re[l], ssm_b_im[l],
                                             ssm_c_re[l], ssm_c_im[l])
        y_f, y_b, st = _ssm_scan(ssm_u, state_ssm[:, l], a_rows, b_mat, c_mat)
        new_s.append(st)

        x = _merge(x, mod, norm1_g[l][None, :], w_in[l][:, MIX_WIDTH:].astype(BF16), o_attn, o_pool, ssm_u,
                   y_f, y_b, ssm_d[l].reshape(1, SSM_WIDTH), w_glu[l].astype(BF16), b_glu[l][None, :],
                   w_up_attn[l].astype(BF16), w_up_pool[l].astype(BF16), w_up_ssm[l].astype(BF16),
                   w_out[l].astype(BF16))

        w_r = jnp.pad(w_router[l], ((0, 0), (0, ROUTER_LANES - N_EXPERTS)))
        r_b = jnp.concatenate([router_bias[l], jnp.full((ROUTER_LANES - N_EXPERTS,), -jnp.inf, F32)])[None, :]
        h2, gate_w = _route(x, mod, norm2_g[l][None, :], w_r, r_b)
        gate_all = jnp.concatenate([gate_w[:, :N_EXPERTS], jnp.ones((N_TOK, 1), F32)], axis=1)
        wg_all = jnp.concatenate([w_gate[l], ws_gate[l][None]], axis=0).astype(BF16)
        wu_all = jnp.concatenate([w_up[l], ws_up[l][None]], axis=0).astype(BF16)
        wd_all = jnp.concatenate([w_down[l], ws_down[l][None]], axis=0).astype(BF16)
        y = _experts(h2, gate_all, wg_all, wu_all, wd_all)
        if l < DEPTH - 1:
            x = _residual(x, y, mod, final_g, final=False)
        else:
            y_prompt = _residual(x, y, mod, final_g, True, 0, N_PROMPT).reshape(BATCH, SEQ, D_MODEL)
            y_sample = _residual(x, y, mod, final_g, True, N_PROMPT, N_SAMPLE).reshape(DEC_BATCH, DEC_SEQ, D_MODEL)

    return (y_prompt, y_sample, jnp.stack(new_k, axis=1), jnp.stack(new_v, axis=1), jnp.stack(new_s, axis=1))
```

```python
import functools
import math

import jax
import jax.numpy as jnp
import numpy as np
from jax import lax
from jax.experimental import pallas as pl
from jax.experimental.pallas import tpu as pltpu

D_MODEL = 1024
BATCH = 32
SEQ = 256
DEPTH = 2
DEC_BATCH = 4
DEC_SEQ = 4096
PAST_LEN = 256
GRID_W = 64
EPS = 1e-6
N_MOD = 6
HEAD_DIM = 64
N_HEADS = 8
N_KV_HEADS = 2
ATTN_WIDTH = N_HEADS * HEAD_DIM
KV_WIDTH = N_KV_HEADS * HEAD_DIM
ROPE_BASE = 10000.0
ROPE_PAIRS_PER_AXIS = HEAD_DIM // 4
POOL_WINDOWS = (2, 4, 8, 16)
POOL_GROUP = 64
POOL_WIDTH = len(POOL_WINDOWS) * POOL_GROUP
SSM_H = 16
SSM_GROUPS = 16
SSM_WIDTH = SSM_H * SSM_GROUPS
SSM_STATE = 64
N_EXPERTS = 64
TOP_K = 8
EXPERT_DIM = 256
SHARED_DIM = 256
ROUTE_SCALE = 2.5

N_PROMPT = BATCH * SEQ
N_SAMPLE = DEC_BATCH * DEC_SEQ
N_TOK = N_PROMPT + N_SAMPLE
N_COND = 1 + DEC_BATCH
COND_ROWS = 8
QK_WIDTH = ATTN_WIDTH + KV_WIDTH
MIX_WIDTH = QK_WIDTH + KV_WIDTH + POOL_WIDTH + SSM_WIDTH
GATE_WIDTH = 3 * D_MODEL
SSM_COLS = SSM_GROUPS * SSM_STATE
SSM_LANES = 2 * SSM_COLS
SSM_SEQS = 4
SSM_CHUNK = 128
ROUTER_LANES = 128
EXPERT_CHUNK = 5
N_EXPERT_ALL = N_EXPERTS + 1

SEQ_TILE = SEQ
SSM_ROWS = 2 * SSM_SEQS
SSM_GROUPS_PROMPT = BATCH // SSM_SEQS
SCAN_T = SSM_GROUPS_PROMPT * SEQ + DEC_SEQ
SCAN_CHUNKS = SCAN_T // SSM_CHUNK
PROMPT_CHUNKS = SSM_GROUPS_PROMPT * SEQ // SSM_CHUNK
SSM_PITCH = SSM_CHUNK + 4
SSM_HALF = SSM_SEQS * SSM_PITCH

LANE = 128
SSM_SLABS = SSM_LANES // LANE
VMEM_LIMIT = 56 * 1024 * 1024

F32 = jnp.float32
BF16 = jnp.bfloat16
HIGHEST = lax.Precision.HIGHEST


def _sigmoid(x):
    return 1.0 / (1.0 + jnp.exp(-x))


def _params(dims, vmem=VMEM_LIMIT):
    return pltpu.CompilerParams(dimension_semantics=dims, vmem_limit_bytes=vmem)


def _mod_index(i, tm):
    p = N_PROMPT // tm
    t = DEC_SEQ // tm
    return jnp.where(i < p, 0, 1 + (i - p) // t)


def _scan_block(i):
    k = i - BATCH
    tiles = DEC_SEQ // SEQ_TILE
    rb = jnp.where(i < BATCH, i // SSM_SEQS, SSM_GROUPS_PROMPT + k % tiles)
    slot = jnp.where(i < BATCH, i % SSM_SEQS, k // tiles)
    return rb, slot


def _norm_mod(x, g, shift, scale):
    ms = jnp.mean(x * x, axis=-1, keepdims=True)
    return (x * lax.rsqrt(ms + EPS) * g) * (1.0 + scale) + shift


def _mod_kernel(cond_ref, w_ref, b_ref, o_ref):
    c = cond_ref[...]
    s = c * _sigmoid(c)
    o_ref[0] = jnp.dot(s, w_ref[0], preferred_element_type=F32, precision=HIGHEST) + b_ref[0]


def _ada_all(cond, w_mod, b_mod):
    tn = 1536
    width = N_MOD * D_MODEL
    out = pl.pallas_call(
        _mod_kernel,
        grid=(DEPTH, width // tn),
        in_specs=[
            pl.BlockSpec((COND_ROWS, D_MODEL), lambda l, j: (0, 0)),
            pl.BlockSpec((1, D_MODEL, tn), lambda l, j: (l, 0, j)),
            pl.BlockSpec((1, 1, tn), lambda l, j: (l, 0, j)),
        ],
        out_specs=pl.BlockSpec((1, COND_ROWS, tn), lambda l, j: (l, 0, j)),
        out_shape=jax.ShapeDtypeStruct((DEPTH, COND_ROWS, width), F32),
        compiler_params=_params(("parallel", "parallel")),
        name="ada_mod",
    )(cond, w_mod, b_mod.reshape(DEPTH, 1, width))
    return out.reshape(DEPTH, COND_ROWS, N_MOD, D_MODEL)


def _inproj_kernel(x_ref, mod_ref, g_ref, w_ref, bd_ref, qkg_ref, cos_ref, sin_ref,
                   q_ref, k_ref, v_ref, pu_ref, su_ref):
    h = _norm_mod(x_ref[...], g_ref[...], mod_ref[0, 0:1, :], mod_ref[0, 1:2, :])
    z = jnp.dot(h.astype(BF16), w_ref[...], preferred_element_type=F32)
    qk = z[:, :QK_WIDTH]
    qq = qk * qk
    hi = qq.astype(BF16)
    lo = (qq - hi.astype(F32)).astype(BF16)
    ms = (jnp.dot(hi, bd_ref[...], preferred_element_type=F32)
          + jnp.dot(lo, bd_ref[...], preferred_element_type=F32))
    qkn = qk * lax.rsqrt(ms + EPS) * qkg_ref[...]
    parts = []
    for c in range(QK_WIDTH // LANE):
        blk = qkn[:, c * LANE:(c + 1) * LANE]
        nxt = pltpu.roll(blk, LANE - 1, axis=1)
        prv = pltpu.roll(blk, 1, axis=1)
        lane = lax.broadcasted_iota(jnp.int32, blk.shape, 1)
        parts.append(jnp.where((lane & 1) == 0, nxt, prv))
    partner = jnp.concatenate(parts, axis=1)
    qkr = qkn * cos_ref[...] + partner * sin_ref[...]
    q_ref[...] = (qkr[:, :ATTN_WIDTH] * (HEAD_DIM ** -0.5 * math.log2(math.e))).astype(BF16)
    k_ref[...] = qkr[:, ATTN_WIDTH:QK_WIDTH]
    v_ref[...] = z[:, QK_WIDTH:QK_WIDTH + KV_WIDTH]
    pu_ref[...] = z[:, QK_WIDTH + KV_WIDTH:QK_WIDTH + KV_WIDTH + POOL_WIDTH]
    su_ref[...] = z[:, QK_WIDTH + KV_WIDTH + POOL_WIDTH:MIX_WIDTH]


def _in_project(x, mod, norm_g, w_mix, bd, qk_gain, cos_t, sin_t):
    tm = SEQ_TILE
    p_tiles = N_PROMPT // tm
    s_tiles = DEC_SEQ // tm

    def rope_idx(i):
        return (jnp.where(i < p_tiles, 0, 1 + (i - p_tiles) % s_tiles), 0)

    row = lambda i: (i, 0)
    const = lambda i: (0, 0)
    return pl.pallas_call(
        _inproj_kernel,
        grid=(N_TOK // tm,),
        in_specs=[
            pl.BlockSpec((tm, D_MODEL), row),
            pl.BlockSpec((1, N_MOD, D_MODEL), lambda i: (_mod_index(i, tm), 0, 0)),
            pl.BlockSpec((1, D_MODEL), const),
            pl.BlockSpec((D_MODEL, MIX_WIDTH), const),
            pl.BlockSpec((QK_WIDTH, QK_WIDTH), const),
            pl.BlockSpec((1, QK_WIDTH), const),
            pl.BlockSpec((tm, QK_WIDTH), rope_idx),
            pl.BlockSpec((tm, QK_WIDTH), rope_idx),
        ],
        out_specs=[
            pl.BlockSpec((tm, ATTN_WIDTH), row),
            pl.BlockSpec((tm, KV_WIDTH), row),
            pl.BlockSpec((tm, KV_WIDTH), row),
            pl.BlockSpec((tm, POOL_WIDTH), row),
            pl.BlockSpec((tm, SSM_WIDTH), _scan_block),
        ],
        out_shape=[
            jax.ShapeDtypeStruct((N_TOK, ATTN_WIDTH), BF16),
            jax.ShapeDtypeStruct((N_TOK, KV_WIDTH), F32),
            jax.ShapeDtypeStruct((N_TOK, KV_WIDTH), F32),
            jax.ShapeDtypeStruct((N_TOK, POOL_WIDTH), F32),
            jax.ShapeDtypeStruct((SCAN_T, SSM_SEQS * SSM_WIDTH), F32),
        ],
        compiler_params=_params(("parallel",)),
        name="in_project",
    )(x, mod, norm_g, w_mix, bd, qk_gain, cos_t, sin_t)


def _rope_tables(tm):
    rows = DEC_SEQ // GRID_W
    row = jnp.repeat(jnp.arange(rows, dtype=F32), GRID_W)
    col = jnp.tile(jnp.arange(GRID_W, dtype=F32), rows)
    inv_freq = ROPE_BASE ** (-jnp.arange(ROPE_PAIRS_PER_AXIS, dtype=F32) / ROPE_PAIRS_PER_AXIS)
    ang = jnp.concatenate([row[:, None] * inv_freq, col[:, None] * inv_freq], axis=-1)
    cos = jnp.repeat(jnp.cos(ang), 2, axis=-1)
    sin = jnp.repeat(jnp.sin(ang), 2, axis=-1) * jnp.tile(jnp.array([-1.0, 1.0], F32), HEAD_DIM // 2)
    n_rep = QK_WIDTH // HEAD_DIM
    cos = jnp.concatenate([jnp.ones((tm, HEAD_DIM), F32), cos], axis=0)
    sin = jnp.concatenate([jnp.zeros((tm, HEAD_DIM), F32), sin], axis=0)
    return jnp.tile(cos, (1, n_rep)), jnp.tile(sin, (1, n_rep))


def _attn_kernel(q_ref, kt_ref, ve_ref, vo_ref, *rest):
    o_ref = rest[-1]
    kt = kt_ref[0]
    lane = lax.broadcasted_iota(jnp.int32, (q_ref.shape[0], LANE), 1)
    first = lane < HEAD_DIM
    slabs = []
    for pair in range(q_ref.shape[1] // LANE):
        q = q_ref[:, pair * LANE:(pair + 1) * LANE]
        halves = []
        for keep, v_ref in ((first, ve_ref), (lane >= HEAD_DIM, vo_ref)):
            qh = jnp.where(keep, q, jnp.zeros_like(q))
            s = jnp.dot(qh, kt, preferred_element_type=F32)
            m = jnp.max(s, axis=-1, keepdims=True)
            p = jnp.exp2(s - m).astype(BF16)
            a = jnp.dot(p, v_ref[0], preferred_element_type=F32)
            halves.append(a / pltpu.roll(a, HEAD_DIM, axis=1))
        slabs.append(jnp.where(first, halves[0], halves[1]))
    o_ref[...] = jnp.concatenate(slabs, axis=1).astype(o_ref.dtype)


def _attention(q, keys, vals, n_batch, n_q, row0, prev=None):
    n_keys = keys.shape[1]
    tq = 256
    kt = jnp.transpose(keys, (0, 2, 3, 1)).astype(BF16)
    kt = jnp.concatenate([kt, kt], axis=2).reshape(n_batch * N_KV_HEADS, 2 * HEAD_DIM, n_keys)
    vv = jnp.transpose(vals, (0, 2, 1, 3)).astype(BF16)
    ones = jnp.ones_like(vv)
    v_e = jnp.concatenate([vv, ones], axis=3).reshape(n_batch * N_KV_HEADS, n_keys, 2 * HEAD_DIM)
    v_o = jnp.concatenate([ones, vv], axis=3).reshape(n_batch * N_KV_HEADS, n_keys, 2 * HEAD_DIM)
    q_tiles = n_q // tq
    base = row0 // tq
    q_spec = pl.BlockSpec((tq, ATTN_WIDTH // N_KV_HEADS), lambda b, g, i: (base + b * q_tiles + i, g))
    kv_idx = lambda b, g, i: (b * N_KV_HEADS + g, 0, 0)
    in_specs = [
        q_spec,
        pl.BlockSpec((1, LANE, n_keys), kv_idx),
        pl.BlockSpec((1, n_keys, LANE), kv_idx),
        pl.BlockSpec((1, n_keys, LANE), kv_idx),
    ]
    args = [q, kt, v_e, v_o]
    aliases = {}
    if prev is not None:
        in_specs.append(pl.BlockSpec(memory_space=pl.ANY))
        args.append(prev)
        aliases = {len(args) - 1: 0}
    return pl.pallas_call(
        _attn_kernel,
        grid=(n_batch, N_KV_HEADS, q_tiles),
        in_specs=in_specs,
        out_specs=q_spec,
        out_shape=jax.ShapeDtypeStruct((N_TOK, ATTN_WIDTH), BF16),
        input_output_aliases=aliases,
        compiler_params=_params(("parallel", "parallel", "parallel")),
        name="attention",
    )(*args)


def _pool_kernel(u_ref, w_ref, sc_ref, *rest, n_seq):
    o_ref, pad_ref = rest[-2:]
    halo = 8
    u = u_ref[...]
    zeros = jnp.zeros((halo, POOL_WIDTH), F32)
    pad_ref[0:halo, :] = zeros
    pad_ref[halo + n_seq:2 * halo + n_seq, :] = zeros
    pad_ref[halo:halo + n_seq, :] = u

    def sh(j):
        return pad_ref[halo + j:halo + j + n_seq, :]

    t2 = sh(-1) + u
    t4 = t2 + sh(-2) + sh(1)
    t8 = t4 + sh(-4) + sh(-3) + sh(2) + sh(3)
    t16 = t8 + sh(-8) + sh(-7) + sh(-6) + sh(-5) + sh(4) + sh(5) + sh(6) + sh(7)
    grp = lax.broadcasted_iota(jnp.int32, u.shape, 1) >> 6
    t = lax.broadcasted_iota(jnp.int32, u.shape, 0)
    tot = jnp.where(grp == 0, t2, jnp.where(grp == 1, t4, jnp.where(grp == 2, t8, t16)))
    half = jnp.where(grp == 0, 1, jnp.where(grp == 1, 2, jnp.where(grp == 2, 4, 8)))
    lo = jnp.maximum(t - half, 0)
    hi = jnp.minimum(t + half, n_seq)
    pooled = tot / (hi - lo).astype(F32) - u
    mixed = jnp.dot(pooled.astype(BF16), w_ref[...], preferred_element_type=F32)
    o_ref[...] = (mixed * sc_ref[...]).astype(o_ref.dtype)


def _pool(u, w_bd, scale, n_batch, n_seq, row0, prev=None):
    base = row0 // n_seq
    seq_spec = pl.BlockSpec((n_seq, POOL_WIDTH), lambda b: (base + b, 0))
    in_specs = [
        seq_spec,
        pl.BlockSpec((POOL_WIDTH, POOL_WIDTH), lambda b: (0, 0)),
        pl.BlockSpec((1, POOL_WIDTH), lambda b: (0, 0)),
    ]
    args = [u, w_bd, scale]
    aliases = {}
    if prev is not None:
        in_specs.append(pl.BlockSpec(memory_space=pl.ANY))
        args.append(prev)
        aliases = {3: 0}
    return pl.pallas_call(
        functools.partial(_pool_kernel, n_seq=n_seq),
        grid=(n_batch,),
        in_specs=in_specs,
        out_specs=seq_spec,
        out_shape=jax.ShapeDtypeStruct((N_TOK, POOL_WIDTH), BF16),
        input_output_aliases=aliases,
        scratch_shapes=[pltpu.VMEM((n_seq + 16, POOL_WIDTH), F32)],
        compiler_params=_params(("parallel",)),
        name="pool_mixer",
    )(*args)


def _scan_chunk_first(i):
    per_seq = SEQ // SSM_CHUNK
    return jnp.where(i < PROMPT_CHUNKS, i % per_seq == 0, i == PROMPT_CHUNKS)


def _scan_bwd_chunk(i):
    per_seq = SEQ // SSM_CHUNK
    return jnp.where(i < PROMPT_CHUNKS, (i // per_seq) * per_seq + (per_seq - 1 - i % per_seq),
                     PROMPT_CHUNKS + SCAN_CHUNKS - 1 - i)


def _scan_group(i):
    return jnp.where(i < PROMPT_CHUNKS, i // (SEQ // SSM_CHUNK), SSM_GROUPS_PROMPT)


def _split3(x):
    hi = x.astype(BF16)
    r1 = x - hi.astype(F32)
    mid = r1.astype(BF16)
    lo = (r1 - mid.astype(F32)).astype(BF16)
    return hi, mid, lo


def _ssm_kernel(uf_ref, ub_ref, a_ref, bf_ref, bb_ref, cf_ref, cb_ref, jin_ref, jout_ref, h0_ref,
                yf_ref, yb_ref, hfin_ref, bu_ref, st_ref, lhs_ref, yy_ref):
    @pl.when(_scan_chunk_first(pl.program_id(0)))
    def _():
        st_ref[...] = h0_ref[0]

    ub_rev = jnp.dot(jin_ref[...], ub_ref[...].astype(BF16), preferred_element_type=F32)
    gap = jnp.zeros((SSM_PITCH - SSM_CHUNK, SSM_WIDTH), F32)
    for s in range(SSM_SEQS):
        cols = slice(s * SSM_WIDTH, (s + 1) * SSM_WIDTH)
        lhs_ref[s * SSM_PITCH:s * SSM_PITCH + SSM_CHUNK, :] = uf_ref[:, cols]
        lhs_ref[s * SSM_PITCH + SSM_CHUNK:(s + 1) * SSM_PITCH, :] = gap
        lhs_ref[SSM_HALF + s * SSM_PITCH:SSM_HALF + s * SSM_PITCH + SSM_CHUNK, :] = ub_rev[:, cols]
        lhs_ref[SSM_HALF + s * SSM_PITCH + SSM_CHUNK:SSM_HALF + (s + 1) * SSM_PITCH, :] = gap
    bu_f = jnp.dot(lhs_ref[:SSM_HALF, :].astype(BF16), bf_ref[...], preferred_element_type=F32)
    bu_b = jnp.dot(lhs_ref[SSM_HALF:, :].astype(BF16), bb_ref[...], preferred_element_type=F32)
    for k in range(SSM_SLABS):
        bu_ref[k, :SSM_HALF, :] = bu_f[:, k * LANE:(k + 1) * LANE]
        bu_ref[k, SSM_HALF:, :] = bu_b[:, k * LANE:(k + 1) * LANE]

    half = SSM_SLABS // 2
    a_re = [a_ref[:, k * LANE:(k + 1) * LANE] for k in range(half)]
    a_im = [a_ref[:, SSM_COLS + k * LANE:SSM_COLS + (k + 1) * LANE] for k in range(half)]

    def step(t, carry):
        rows = pl.ds(t, SSM_ROWS, stride=SSM_PITCH)
        new_re, new_im = [], []
        for k in range(half):
            h_re, h_im = carry[k], carry[half + k]
            n_re = a_re[k] * h_re - a_im[k] * h_im + bu_ref[k, rows, :]
            n_im = a_re[k] * h_im + a_im[k] * h_re + bu_ref[half + k, rows, :]
            bu_ref[k, rows, :] = n_re
            bu_ref[half + k, rows, :] = n_im
            new_re.append(n_re)
            new_im.append(n_im)
        return tuple(new_re + new_im)

    init = tuple(st_ref[:, k * LANE:(k + 1) * LANE] for k in range(SSM_SLABS))
    fin = lax.fori_loop(0, SSM_CHUNK, step, init)
    for k in range(SSM_SLABS):
        st_ref[:, k * LANE:(k + 1) * LANE] = fin[k]
    hfin_ref[0] = st_ref[...]

    h_f = jnp.concatenate([bu_ref[k, :SSM_HALF, :] for k in range(SSM_SLABS)], axis=1).astype(BF16)
    h_b = jnp.concatenate([bu_ref[k, SSM_HALF:, :] for k in range(SSM_SLABS)], axis=1).astype(BF16)
    yy_ref[...] = jnp.dot(h_f, cf_ref[...], preferred_element_type=F32)
    for s in range(SSM_SEQS):
        yf_ref[:, s * SSM_WIDTH:(s + 1) * SSM_WIDTH] = yy_ref[s * SSM_PITCH:s * SSM_PITCH + SSM_CHUNK, :]
    y_b = jnp.dot(h_b, cb_ref[...], preferred_element_type=F32)
    y_nat = sum(jnp.dot(jout_ref[...], piece, preferred_element_type=F32) for piece in _split3(y_b))
    for s in range(SSM_SEQS):
        yb_ref[:, s * SSM_WIDTH:(s + 1) * SSM_WIDTH] = y_nat[s * SSM_CHUNK:(s + 1) * SSM_CHUNK, :]


def _ssm_scan(su8, h0_sample, a_rows, b_mat, c_mat):
    rows = SSM_ROWS
    width = SSM_SEQS * SSM_WIDTH
    n_groups = SSM_GROUPS_PROMPT + 1
    hs = h0_sample.astype(F32).reshape(SSM_SEQS, 2, SSM_COLS, 2).transpose(1, 0, 3, 2).reshape(1, rows, SSM_LANES)
    hh = jnp.concatenate([jnp.zeros((SSM_GROUPS_PROMPT, rows, SSM_LANES), F32), hs], axis=0)
    j_in = jnp.asarray(np.eye(SSM_CHUNK, dtype=np.float32)[::-1], dtype=BF16)
    sel = np.zeros((SSM_SEQS * SSM_CHUNK, SSM_HALF), np.float32)
    for s in range(SSM_SEQS):
        for t in range(SSM_CHUNK):
            sel[s * SSM_CHUNK + t, s * SSM_PITCH + SSM_CHUNK - 1 - t] = 1.0
    j_out = jnp.asarray(sel, dtype=BF16)
    fwd = lambda i: (i, 0)
    bwd = lambda i: (_scan_bwd_chunk(i), 0)
    grp = lambda i: (_scan_group(i), 0, 0)
    const = lambda i: (0, 0)
    y_f, y_b, hfin = pl.pallas_call(
        _ssm_kernel,
        grid=(SCAN_CHUNKS,),
        in_specs=[
            pl.BlockSpec((SSM_CHUNK, width), fwd),
            pl.BlockSpec((SSM_CHUNK, width), bwd),
            pl.BlockSpec((rows, SSM_LANES), const),
            pl.BlockSpec((SSM_WIDTH, SSM_LANES), const),
            pl.BlockSpec((SSM_WIDTH, SSM_LANES), const),
            pl.BlockSpec((SSM_LANES, SSM_WIDTH), const),
            pl.BlockSpec((SSM_LANES, SSM_WIDTH), const),
            pl.BlockSpec((SSM_CHUNK, SSM_CHUNK), const),
            pl.BlockSpec((SSM_SEQS * SSM_CHUNK, SSM_HALF), const),
            pl.BlockSpec((1, rows, SSM_LANES), grp),
        ],
        out_specs=[
            pl.BlockSpec((SSM_CHUNK, width), fwd),
            pl.BlockSpec((SSM_CHUNK, width), bwd),
            pl.BlockSpec((1, rows, SSM_LANES), grp),
        ],
        out_shape=[
            jax.ShapeDtypeStruct((SCAN_T, width), F32),
            jax.ShapeDtypeStruct((SCAN_T, width), F32),
            jax.ShapeDtypeStruct((n_groups, rows, SSM_LANES), F32),
        ],
        scratch_shapes=[pltpu.VMEM((SSM_SLABS, 2 * SSM_HALF, LANE), F32), pltpu.VMEM((rows, SSM_LANES), F32),
                        pltpu.VMEM((2 * SSM_HALF, SSM_WIDTH), F32), pltpu.VMEM((SSM_HALF, SSM_WIDTH), F32)],
        compiler_params=_params(("arbitrary",)),
        name="ssm_scan",
    )(su8, su8, a_rows, b_mat[:SSM_WIDTH], b_mat[SSM_WIDTH:], c_mat[:, :SSM_WIDTH], c_mat[:, SSM_WIDTH:],
      j_in, j_out, hh)
    fin = hfin[:SSM_GROUPS_PROMPT].reshape(SSM_GROUPS_PROMPT, 2, SSM_SEQS, 2, SSM_GROUPS, SSM_STATE)
    fin = fin.transpose(0, 2, 1, 4, 5, 3).reshape(BATCH, 2, SSM_GROUPS, SSM_STATE, 2)
    return y_f, y_b, fin


def _ssm_matrices(a_re, a_im, log_dt, b_re, b_im, c_re, c_im):
    lam = lax.complex(a_re.astype(F32), a_im.astype(F32))
    dt = jnp.exp(log_dt.astype(F32))[..., None]
    a_bar = jnp.exp(lam * dt)
    b_bar = ((a_bar - 1.0) / lam)[..., None] * lax.complex(b_re.astype(F32), b_im.astype(F32))
    a_dir = jnp.concatenate([jnp.real(a_bar).reshape(2, SSM_COLS), jnp.imag(a_bar).reshape(2, SSM_COLS)], axis=-1)
    a_rows = jnp.repeat(a_dir, SSM_SEQS, axis=0)
    eye = jnp.eye(SSM_GROUPS, dtype=F32)
    bt = jnp.transpose(b_bar, (0, 1, 3, 2))
    b_real = jnp.einsum('dghp,ge->dghep', jnp.real(bt), eye).reshape(2 * SSM_WIDTH, SSM_COLS)
    b_imag = jnp.einsum('dghp,ge->dghep', jnp.imag(bt), eye).reshape(2 * SSM_WIDTH, SSM_COLS)
    b_mat = jnp.concatenate([b_real, b_imag], axis=-1).astype(BF16)
    cr = jnp.transpose(c_re.astype(F32), (0, 1, 3, 2))
    ci = jnp.transpose(c_im.astype(F32), (0, 1, 3, 2))
    c_real = jnp.einsum('dgph,ge->gpdeh', cr, eye).reshape(SSM_COLS, 2 * SSM_WIDTH)
    c_imag = jnp.einsum('dgph,ge->gpdeh', -ci, eye).reshape(SSM_COLS, 2 * SSM_WIDTH)
    c_mat = jnp.concatenate([c_real, c_imag], axis=0).astype(BF16)
    return a_rows, b_mat, c_mat


def _merge_kernel(x_ref, mod_ref, g_ref, wg_ref, oa_ref, op_ref, su_ref, yf_ref, yb_ref, d_ref,
                  wglu_ref, bglu_ref, wua_ref, wup_ref, wus_ref, wo_ref, o_ref):
    x = x_ref[...]
    h = _norm_mod(x, g_ref[...], mod_ref[0, 0:1, :], mod_ref[0, 1:2, :])
    gates = _sigmoid(jnp.dot(h.astype(BF16), wg_ref[...], preferred_element_type=F32))
    y = d_ref[...] * su_ref[...] + yf_ref[...] + yb_ref[...]
    y = 0.5 * y * (1.0 + jnp.tanh(math.sqrt(2.0 / math.pi) * (y + 0.044715 * (y * y * y))))
    glu = jnp.dot(y.astype(BF16), wglu_ref[...], preferred_element_type=F32) + bglu_ref[...]
    o_ssm = y * _sigmoid(glu)
    m = (gates[:, :D_MODEL] * jnp.dot(oa_ref[...], wua_ref[...], preferred_element_type=F32)
         + gates[:, D_MODEL:2 * D_MODEL] * jnp.dot(op_ref[...], wup_ref[...], preferred_element_type=F32)
         + gates[:, 2 * D_MODEL:] * jnp.dot(o_ssm.astype(BF16), wus_ref[...], preferred_element_type=F32))
    o_ref[...] = x + mod_ref[0, 2:3, :] * jnp.dot(m.astype(BF16), wo_ref[...], preferred_element_type=F32)


def _merge(x, mod, norm_g, w_gates, o_attn, o_pool, ssm_u, y_f, y_b, d_skip, w_glu, b_glu,
           w_up_attn, w_up_pool, w_up_ssm, w_out):
    tm = SEQ_TILE
    row = lambda i: (i, 0)
    const = lambda i: (0, 0)
    return pl.pallas_call(
        _merge_kernel,
        grid=(N_TOK // tm,),
        in_specs=[
            pl.BlockSpec((tm, D_MODEL), row),
            pl.BlockSpec((1, N_MOD, D_MODEL), lambda i: (_mod_index(i, tm), 0, 0)),
            pl.BlockSpec((1, D_MODEL), const),
            pl.BlockSpec((D_MODEL, GATE_WIDTH), const),
            pl.BlockSpec((tm, ATTN_WIDTH), row),
            pl.BlockSpec((tm, POOL_WIDTH), row),
            pl.BlockSpec((tm, SSM_WIDTH), _scan_block),
            pl.BlockSpec((tm, SSM_WIDTH), _scan_block),
            pl.BlockSpec((tm, SSM_WIDTH), _scan_block),
            pl.BlockSpec((1, SSM_WIDTH), const),
            pl.BlockSpec((SSM_WIDTH, SSM_WIDTH), const),
            pl.BlockSpec((1, SSM_WIDTH), const),
            pl.BlockSpec((ATTN_WIDTH, D_MODEL), const),
            pl.BlockSpec((POOL_WIDTH, D_MODEL), const),
            pl.BlockSpec((SSM_WIDTH, D_MODEL), const),
            pl.BlockSpec((D_MODEL, D_MODEL), const),
        ],
        out_specs=pl.BlockSpec((tm, D_MODEL), row),
        out_shape=jax.ShapeDtypeStruct((N_TOK, D_MODEL), F32),
        compiler_params=_params(("parallel",)),
        name="merge_branches",
    )(x, mod, norm_g, w_gates, o_attn, o_pool, ssm_u, y_f, y_b, d_skip, w_glu, b_glu,
      w_up_attn, w_up_pool, w_up_ssm, w_out)


def _router_kernel(x_ref, mod_ref, g_ref, wr_ref, rb_ref, h_ref, gw_ref):
    h = _norm_mod(x_ref[...], g_ref[...], mod_ref[0, 3:4, :], mod_ref[0, 4:5, :])
    h_ref[...] = h.astype(BF16)
    logits = jnp.dot(h, wr_ref[...], preferred_element_type=F32, precision=HIGHEST)
    scores = _sigmoid(logits)
    work = scores + rb_ref[...]
    lane = lax.broadcasted_iota(jnp.int32, work.shape, 1).astype(F32)
    w = jnp.zeros_like(scores)
    for _ in range(TOP_K):
        mx = jnp.max(work, axis=-1, keepdims=True)
        first = jnp.min(jnp.where(work == mx, lane, float(ROUTER_LANES)), axis=-1, keepdims=True)
        pick = lane == first
        w = jnp.where(pick, scores, w)
        work = jnp.where(pick, -jnp.inf, work)
    gw_ref[...] = w / jnp.sum(w, axis=-1, keepdims=True) * ROUTE_SCALE


def _route(x, mod, norm_g, w_router, router_bias):
    tm = 512
    row = lambda i: (i, 0)
    const = lambda i: (0, 0)
    return pl.pallas_call(
        _router_kernel,
        grid=(N_TOK // tm,),
        in_specs=[
            pl.BlockSpec((tm, D_MODEL), row),
            pl.BlockSpec((1, N_MOD, D_MODEL), lambda i: (_mod_index(i, tm), 0, 0)),
            pl.BlockSpec((1, D_MODEL), const),
            pl.BlockSpec((D_MODEL, ROUTER_LANES), const),
            pl.BlockSpec((1, ROUTER_LANES), const),
        ],
        out_specs=[pl.BlockSpec((tm, D_MODEL), row), pl.BlockSpec((tm, ROUTER_LANES), row)],
        out_shape=[jax.ShapeDtypeStruct((N_TOK, D_MODEL), BF16),
                   jax.ShapeDtypeStruct((N_TOK, ROUTER_LANES), F32)],
        compiler_params=_params(("parallel",)),
        name="moe_router",
    )(x, mod, norm_g, w_router, router_bias)


def _experts_kernel(h_ref, gw_ref, wg_ref, wu_ref, wd_ref, o_ref):
    @pl.when(pl.program_id(1) == 0)
    def _():
        o_ref[...] = jnp.zeros_like(o_ref)

    h = h_ref[...]
    gw = gw_ref[0]
    mids = []
    for e in range(EXPERT_CHUNK):
        a = jnp.dot(h, wg_ref[e], preferred_element_type=F32)
        b = jnp.dot(h, wu_ref[e], preferred_element_type=F32)
        mids.append(((a * _sigmoid(a)) * b * gw[:, e:e + 1]).astype(BF16))
    mid = jnp.concatenate(mids, axis=1)
    wd = wd_ref[...].reshape(EXPERT_CHUNK * EXPERT_DIM, D_MODEL)
    o_ref[...] += jnp.dot(mid, wd, preferred_element_type=F32)


def _experts(h, gate_w, wg_all, wu_all, wd_all):
    tm = 1024
    n_chunks = N_EXPERT_ALL // EXPERT_CHUNK
    gw = gate_w.reshape(N_TOK, n_chunks, EXPERT_CHUNK).transpose(1, 0, 2)
    return pl.pallas_call(
        _experts_kernel,
        grid=(N_TOK // tm, n_chunks),
        in_specs=[
            pl.BlockSpec((tm, D_MODEL), lambda i, j: (i, 0)),
            pl.BlockSpec((1, tm, EXPERT_CHUNK), lambda i, j: (j, i, 0)),
            pl.BlockSpec((EXPERT_CHUNK, D_MODEL, EXPERT_DIM), lambda i, j: (j, 0, 0)),
            pl.BlockSpec((EXPERT_CHUNK, D_MODEL, EXPERT_DIM), lambda i, j: (j, 0, 0)),
            pl.BlockSpec((EXPERT_CHUNK, EXPERT_DIM, D_MODEL), lambda i, j: (j, 0, 0)),
        ],
        out_specs=pl.BlockSpec((tm, D_MODEL), lambda i, j: (i, 0)),
        out_shape=jax.ShapeDtypeStruct((N_TOK, D_MODEL), F32),
        compiler_params=_params(("parallel", "arbitrary")),
        name="moe_experts",
    )(h, gw, wg_all, wu_all, wd_all)


def _residual_kernel(x_ref, y_ref, mod_ref, fg_ref, o_ref, *, final):
    x = x_ref[...] + mod_ref[0, 5:6, :] * y_ref[...]
    if final:
        ms = jnp.mean(x * x, axis=-1, keepdims=True)
        x = x * lax.rsqrt(ms + EPS) * fg_ref[...]
    o_ref[...] = x


def _residual(x, y, mod, final_g, final, row0=0, n_rows=N_TOK):
    tm = 512
    base = row0 // tm
    row = lambda i: (base + i, 0)
    return pl.pallas_call(
        functools.partial(_residual_kernel, final=final),
        grid=(n_rows // tm,),
        in_specs=[
            pl.BlockSpec((tm, D_MODEL), row),
            pl.BlockSpec((tm, D_MODEL), row),
            pl.BlockSpec((1, N_MOD, D_MODEL), lambda i: (_mod_index(base + i, tm), 0, 0)),
            pl.BlockSpec((1, D_MODEL), lambda i: (0, 0)),
        ],
        out_specs=pl.BlockSpec((tm, D_MODEL), lambda i: (i, 0)),
        out_shape=jax.ShapeDtypeStruct((n_rows, D_MODEL), F32),
        compiler_params=_params(("parallel",)),
        name="moe_residual",
    )(x, y, mod, final_g)


def kernel(x_prompt, x_sample, c, cache_k, cache_v, state_ssm, c_ctx, w_mod, b_mod, norm1_g, norm2_g, w_in, q_norm_g, k_norm_g, w_up_attn, pool_w, pool_scale, w_up_pool, ssm_a_re, ssm_a_im, ssm_log_dt, ssm_b_re, ssm_b_im, ssm_c_re, ssm_c_im, ssm_d, w_glu, b_glu, w_up_ssm, w_out, w_router, router_bias, w_gate, w_up, w_down, ws_gate, ws_up, ws_down, final_norm_g):
    x = jnp.concatenate([x_prompt.reshape(N_PROMPT, D_MODEL), x_sample.reshape(N_SAMPLE, D_MODEL)], axis=0)
    cond = jnp.concatenate([c_ctx[None, :], c, jnp.zeros((COND_ROWS - N_COND, D_MODEL), F32)], axis=0)
    mod_all = _ada_all(cond, w_mod, b_mod)

    cos_t, sin_t = _rope_tables(SEQ_TILE)
    head_avg = jnp.kron(jnp.eye(QK_WIDTH // HEAD_DIM, dtype=F32),
                        jnp.full((HEAD_DIM, HEAD_DIM), 1.0 / HEAD_DIM, F32)).astype(BF16)
    final_g = final_norm_g.reshape(1, D_MODEL)

    new_k, new_v, new_s = [], [], []
    for l in range(DEPTH):
        mod = mod_all[l]
        qk_gain = jnp.concatenate([jnp.tile(q_norm_g[l], N_HEADS), jnp.tile(k_norm_g[l], N_KV_HEADS)])[None, :]
        q, k, v, pool_u, ssm_u = _in_project(
            x, mod, norm1_g[l][None, :], w_in[l][:, :MIX_WIDTH].astype(BF16), head_avg, qk_gain, cos_t, sin_t)

        k_p = k[:N_PROMPT].reshape(BATCH, SEQ, N_KV_HEADS, HEAD_DIM)
        v_p = v[:N_PROMPT].reshape(BATCH, SEQ, N_KV_HEADS, HEAD_DIM)
        k_s = k[N_PROMPT:].reshape(DEC_BATCH, DEC_SEQ, N_KV_HEADS, HEAD_DIM)
        v_s = v[N_PROMPT:].reshape(DEC_BATCH, DEC_SEQ, N_KV_HEADS, HEAD_DIM)
        new_k.append(k_p)
        new_v.append(v_p)
        keys = jnp.concatenate([cache_k[:, l], k_s], axis=1)
        vals = jnp.concatenate([cache_v[:, l], v_s], axis=1)
        o_attn = _attention(q, k_p, v_p, BATCH, SEQ, 0)
        o_attn = _attention(q, keys, vals, DEC_BATCH, DEC_SEQ, N_PROMPT, prev=o_attn)

        pool_bd = jax.scipy.linalg.block_diag(*[pool_w[l, g] for g in range(len(POOL_WINDOWS))]).astype(BF16)
        p_scale = pool_scale[l][None, :]
        o_pool = _pool(pool_u, pool_bd, p_scale, BATCH, SEQ, 0)
        o_pool = _pool(pool_u, pool_bd, p_scale, DEC_BATCH, DEC_SEQ, N_PROMPT, prev=o_pool)

        a_rows, b_mat, c_mat = _ssm_matrices(ssm_a_re[l], ssm_a_im[l], ssm_log_dt[l], ssm_b_re[l], ssm_b_im[l],
                                             ssm_c_re[l], ssm_c_im[l])
        y_f, y_b, st = _ssm_scan(ssm_u, state_ssm[:, l], a_rows, b_mat, c_mat)
        new_s.append(st)

        x = _merge(x, mod, norm1_g[l][None, :], w_in[l][:, MIX_WIDTH:].astype(BF16), o_attn, o_pool, ssm_u,
                   y_f, y_b, ssm_d[l].reshape(1, SSM_WIDTH), w_glu[l].astype(BF16), b_glu[l][None, :],
                   w_up_attn[l].astype(BF16), w_up_pool[l].astype(BF16), w_up_ssm[l].astype(BF16),
                   w_out[l].astype(BF16))

        w_r = jnp.pad(w_router[l], ((0, 0), (0, ROUTER_LANES - N_EXPERTS)))
        r_b = jnp.concatenate([router_bias[l], jnp.full((ROUTER_LANES - N_EXPERTS,), -jnp.inf, F32)])[None, :]
        h2, gate_w = _route(x, mod, norm2_g[l][None, :], w_r, r_b)
        gate_all = jnp.concatenate([gate_w[:, :N_EXPERTS], jnp.ones((N_TOK, 1), F32)], axis=1)
        wg_all = jnp.concatenate([w_gate[l], ws_gate[l][None]], axis=0).astype(BF16)
        wu_all = jnp.concatenate([w_up[l], ws_up[l][None]], axis=0).astype(BF16)
        wd_all = jnp.concatenate([w_down[l], ws_down[l][None]], axis=0).astype(BF16)
        y = _experts(h2, gate_all, wg_all, wu_all, wd_all)
        if l < DEPTH - 1:
            x = _residual(x, y, mod, final_g, final=False)
        else:
            y_prompt = _residual(x, y, mod, final_g, True, 0, N_PROMPT).reshape(BATCH, SEQ, D_MODEL)
            y_sample = _residual(x, y, mod, final_g, True, N_PROMPT, N_SAMPLE).reshape(DEC_BATCH, DEC_SEQ, D_MODEL)

    return (y_prompt, y_sample, jnp.stack(new_k, axis=1), jnp.stack(new_v, axis=1), jnp.stack(new_s, axis=1))
```

```python
import functools
import math

import jax
import jax.numpy as jnp
import numpy as np
from jax import lax
from jax.experimental import pallas as pl
from jax.experimental.pallas import tpu as pltpu

D_MODEL = 1024
BATCH = 32
SEQ = 256
DEPTH = 2
DEC_BATCH = 4
DEC_SEQ = 4096
PAST_LEN = 256
GRID_W = 64
EPS = 1e-6
N_MOD = 6
HEAD_DIM = 64
N_HEADS = 8
N_KV_HEADS = 2
ATTN_WIDTH = N_HEADS * HEAD_DIM
KV_WIDTH = N_KV_HEADS * HEAD_DIM
ROPE_BASE = 10000.0
ROPE_PAIRS_PER_AXIS = HEAD_DIM // 4
POOL_WINDOWS = (2, 4, 8, 16)
POOL_GROUP = 64
POOL_WIDTH = len(POOL_WINDOWS) * POOL_GROUP
SSM_H = 16
SSM_GROUPS = 16
SSM_WIDTH = SSM_H * SSM_GROUPS
SSM_STATE = 64
N_EXPERTS = 64
TOP_K = 8
EXPERT_DIM = 256
SHARED_DIM = 256
ROUTE_SCALE = 2.5

N_PROMPT = BATCH * SEQ
N_SAMPLE = DEC_BATCH * DEC_SEQ
N_TOK = N_PROMPT + N_SAMPLE
N_COND = 1 + DEC_BATCH
COND_ROWS = 8
QK_WIDTH = ATTN_WIDTH + KV_WIDTH
MIX_WIDTH = QK_WIDTH + KV_WIDTH + POOL_WIDTH + SSM_WIDTH
GATE_WIDTH = 3 * D_MODEL
SSM_COLS = SSM_GROUPS * SSM_STATE
SSM_LANES = 2 * SSM_COLS
SSM_SEQS = 4
SSM_CHUNK = 128
ROUTER_LANES = 128
EXPERT_CHUNK = 4

SEQ_TILE = SEQ
SSM_ROWS = 2 * SSM_SEQS
SSM_GROUPS_PROMPT = BATCH // SSM_SEQS
SCAN_T = SSM_GROUPS_PROMPT * SEQ + DEC_SEQ
SCAN_CHUNKS = SCAN_T // SSM_CHUNK
PROMPT_CHUNKS = SSM_GROUPS_PROMPT * SEQ // SSM_CHUNK
SSM_PITCH = SSM_CHUNK + 4
SSM_HALF = SSM_SEQS * SSM_PITCH

LANE = 128
SSM_SLABS = SSM_LANES // LANE
VMEM_LIMIT = 56 * 1024 * 1024

F32 = jnp.float32
BF16 = jnp.bfloat16
HIGHEST = lax.Precision.HIGHEST


def _sigmoid(x):
    return 1.0 / (1.0 + jnp.exp(-x))


def _params(dims, vmem=VMEM_LIMIT):
    return pltpu.CompilerParams(dimension_semantics=dims, vmem_limit_bytes=vmem)


def _mod_index(i, tm):
    p = N_PROMPT // tm
    t = DEC_SEQ // tm
    return jnp.where(i < p, 0, 1 + (i - p) // t)


def _scan_block(i):
    k = i - BATCH
    tiles = DEC_SEQ // SEQ_TILE
    rb = jnp.where(i < BATCH, i // SSM_SEQS, SSM_GROUPS_PROMPT + k % tiles)
    slot = jnp.where(i < BATCH, i % SSM_SEQS, k // tiles)
    return rb, slot


def _norm_mod(x, g, shift, scale):
    ms = jnp.mean(x * x, axis=-1, keepdims=True)
    return (x * lax.rsqrt(ms + EPS) * g) * (1.0 + scale) + shift


def _mod_kernel(cond_ref, w_ref, b_ref, o_ref):
    c = cond_ref[...]
    s = c * _sigmoid(c)
    o_ref[0] = jnp.dot(s, w_ref[0], preferred_element_type=F32, precision=HIGHEST) + b_ref[0]


def _ada_all(cond, w_mod, b_mod):
    tn = 1536
    width = N_MOD * D_MODEL
    out = pl.pallas_call(
        _mod_kernel,
        grid=(DEPTH, width // tn),
        in_specs=[
            pl.BlockSpec((COND_ROWS, D_MODEL), lambda l, j: (0, 0)),
            pl.BlockSpec((1, D_MODEL, tn), lambda l, j: (l, 0, j)),
            pl.BlockSpec((1, 1, tn), lambda l, j: (l, 0, j)),
        ],
        out_specs=pl.BlockSpec((1, COND_ROWS, tn), lambda l, j: (l, 0, j)),
        out_shape=jax.ShapeDtypeStruct((DEPTH, COND_ROWS, width), F32),
        compiler_params=_params(("parallel", "parallel")),
        name="ada_mod",
    )(cond, w_mod, b_mod.reshape(DEPTH, 1, width))
    return out.reshape(DEPTH, COND_ROWS, N_MOD, D_MODEL)


def _inproj_kernel(x_ref, mod_ref, g_ref, w_ref, bd_ref, qkg_ref, cos_ref, sin_ref,
                   q_ref, k_ref, v_ref, pu_ref, su_ref):
    h = _norm_mod(x_ref[...], g_ref[...], mod_ref[0, 0:1, :], mod_ref[0, 1:2, :])
    z = jnp.dot(h.astype(BF16), w_ref[...], preferred_element_type=F32)
    qk = z[:, :QK_WIDTH]
    qq = qk * qk
    hi = qq.astype(BF16)
    lo = (qq - hi.astype(F32)).astype(BF16)
    ms = (jnp.dot(hi, bd_ref[...], preferred_element_type=F32)
          + jnp.dot(lo, bd_ref[...], preferred_element_type=F32))
    qkn = qk * lax.rsqrt(ms + EPS) * qkg_ref[...]
    parts = []
    for c in range(QK_WIDTH // LANE):
        blk = qkn[:, c * LANE:(c + 1) * LANE]
        nxt = pltpu.roll(blk, LANE - 1, axis=1)
        prv = pltpu.roll(blk, 1, axis=1)
        lane = lax.broadcasted_iota(jnp.int32, blk.shape, 1)
        parts.append(jnp.where((lane & 1) == 0, nxt, prv))
    partner = jnp.concatenate(parts, axis=1)
    qkr = qkn * cos_ref[...] + partner * sin_ref[...]
    q_ref[...] = (qkr[:, :ATTN_WIDTH] * (HEAD_DIM ** -0.5 * math.log2(math.e))).astype(BF16)
    k_ref[...] = qkr[:, ATTN_WIDTH:QK_WIDTH]
    v_ref[...] = z[:, QK_WIDTH:QK_WIDTH + KV_WIDTH]
    pu_ref[...] = z[:, QK_WIDTH + KV_WIDTH:QK_WIDTH + KV_WIDTH + POOL_WIDTH]
    su_ref[...] = z[:, QK_WIDTH + KV_WIDTH + POOL_WIDTH:MIX_WIDTH]


def _in_project(x, mod, norm_g, w_mix, bd, qk_gain, cos_t, sin_t):
    tm = SEQ_TILE
    p_tiles = N_PROMPT // tm
    s_tiles = DEC_SEQ // tm

    def rope_idx(i):
        return (jnp.where(i < p_tiles, 0, 1 + (i - p_tiles) % s_tiles), 0)

    row = lambda i: (i, 0)
    const = lambda i: (0, 0)
    return pl.pallas_call(
        _inproj_kernel,
        grid=(N_TOK // tm,),
        in_specs=[
            pl.BlockSpec((tm, D_MODEL), row),
            pl.BlockSpec((1, N_MOD, D_MODEL), lambda i: (_mod_index(i, tm), 0, 0)),
            pl.BlockSpec((1, D_MODEL), const),
            pl.BlockSpec((D_MODEL, MIX_WIDTH), const),
            pl.BlockSpec((QK_WIDTH, QK_WIDTH), const),
            pl.BlockSpec((1, QK_WIDTH), const),
            pl.BlockSpec((tm, QK_WIDTH), rope_idx),
            pl.BlockSpec((tm, QK_WIDTH), rope_idx),
        ],
        out_specs=[
            pl.BlockSpec((tm, ATTN_WIDTH), row),
            pl.BlockSpec((tm, KV_WIDTH), row),
            pl.BlockSpec((tm, KV_WIDTH), row),
            pl.BlockSpec((tm, POOL_WIDTH), row),
            pl.BlockSpec((tm, SSM_WIDTH), _scan_block),
        ],
        out_shape=[
            jax.ShapeDtypeStruct((N_TOK, ATTN_WIDTH), BF16),
            jax.ShapeDtypeStruct((N_TOK, KV_WIDTH), F32),
            jax.ShapeDtypeStruct((N_TOK, KV_WIDTH), F32),
            jax.ShapeDtypeStruct((N_TOK, POOL_WIDTH), F32),
            jax.ShapeDtypeStruct((SCAN_T, SSM_SEQS * SSM_WIDTH), F32),
        ],
        compiler_params=_params(("parallel",)),
        name="in_project",
    )(x, mod, norm_g, w_mix, bd, qk_gain, cos_t, sin_t)


def _rope_tables(tm):
    rows = DEC_SEQ // GRID_W
    row = jnp.repeat(jnp.arange(rows, dtype=F32), GRID_W)
    col = jnp.tile(jnp.arange(GRID_W, dtype=F32), rows)
    inv_freq = ROPE_BASE ** (-jnp.arange(ROPE_PAIRS_PER_AXIS, dtype=F32) / ROPE_PAIRS_PER_AXIS)
    ang = jnp.concatenate([row[:, None] * inv_freq, col[:, None] * inv_freq], axis=-1)
    cos = jnp.repeat(jnp.cos(ang), 2, axis=-1)
    sin = jnp.repeat(jnp.sin(ang), 2, axis=-1) * jnp.tile(jnp.array([-1.0, 1.0], F32), HEAD_DIM // 2)
    n_rep = QK_WIDTH // HEAD_DIM
    cos = jnp.concatenate([jnp.ones((tm, HEAD_DIM), F32), cos], axis=0)
    sin = jnp.concatenate([jnp.zeros((tm, HEAD_DIM), F32), sin], axis=0)
    return jnp.tile(cos, (1, n_rep)), jnp.tile(sin, (1, n_rep))


def _attn_kernel(q_ref, kt_ref, ve_ref, vo_ref, o_ref):
    kt = kt_ref[0]
    lane = lax.broadcasted_iota(jnp.int32, (q_ref.shape[0], LANE), 1)
    first = lane < HEAD_DIM
    slabs = []
    for pair in range(q_ref.shape[1] // LANE):
        q = q_ref[:, pair * LANE:(pair + 1) * LANE]
        halves = []
        for keep, v_ref in ((first, ve_ref), (lane >= HEAD_DIM, vo_ref)):
            qh = jnp.where(keep, q, jnp.zeros_like(q))
            s = jnp.dot(qh, kt, preferred_element_type=F32)
            m = jnp.max(s, axis=-1, keepdims=True)
            p = jnp.exp2(s - m).astype(BF16)
            a = jnp.dot(p, v_ref[0], preferred_element_type=F32)
            halves.append(a / pltpu.roll(a, HEAD_DIM, axis=1))
        slabs.append(jnp.where(first, halves[0], halves[1]))
    o_ref[...] = jnp.concatenate(slabs, axis=1).astype(o_ref.dtype)


def _attention(q, keys, vals, n_batch, n_q, row0):
    n_keys = keys.shape[1]
    tq = 256
    kt = jnp.transpose(keys, (0, 2, 3, 1)).astype(BF16)
    kt = jnp.concatenate([kt, kt], axis=2).reshape(n_batch * N_KV_HEADS, 2 * HEAD_DIM, n_keys)
    vv = jnp.transpose(vals, (0, 2, 1, 3)).astype(BF16)
    ones = jnp.ones_like(vv)
    v_e = jnp.concatenate([vv, ones], axis=3).reshape(n_batch * N_KV_HEADS, n_keys, 2 * HEAD_DIM)
    v_o = jnp.concatenate([ones, vv], axis=3).reshape(n_batch * N_KV_HEADS, n_keys, 2 * HEAD_DIM)
    q_tiles = n_q // tq
    base = row0 // tq
    group_w = ATTN_WIDTH // N_KV_HEADS
    kv_idx = lambda b, g, i: (b * N_KV_HEADS + g, 0, 0)
    return pl.pallas_call(
        _attn_kernel,
        grid=(n_batch, N_KV_HEADS, q_tiles),
        in_specs=[
            pl.BlockSpec((tq, group_w), lambda b, g, i: (base + b * q_tiles + i, g)),
            pl.BlockSpec((1, LANE, n_keys), kv_idx),
            pl.BlockSpec((1, n_keys, LANE), kv_idx),
            pl.BlockSpec((1, n_keys, LANE), kv_idx),
        ],
        out_specs=pl.BlockSpec((tq, group_w), lambda b, g, i: (b * q_tiles + i, g)),
        out_shape=jax.ShapeDtypeStruct((n_batch * n_q, ATTN_WIDTH), BF16),
        compiler_params=_params(("parallel", "parallel", "parallel")),
        name="attention",
    )(q, kt, v_e, v_o)


def _pool_kernel(u_ref, w_ref, sc_ref, o_ref, pad_ref, *, n_seq):
    halo = 8
    u = u_ref[...]
    zeros = jnp.zeros((halo, POOL_WIDTH), F32)
    pad_ref[0:halo, :] = zeros
    pad_ref[halo + n_seq:2 * halo + n_seq, :] = zeros
    pad_ref[halo:halo + n_seq, :] = u

    def sh(j):
        return pad_ref[halo + j:halo + j + n_seq, :]

    t2 = sh(-1) + u
    t4 = t2 + sh(-2) + sh(1)
    t8 = t4 + sh(-4) + sh(-3) + sh(2) + sh(3)
    t16 = t8 + sh(-8) + sh(-7) + sh(-6) + sh(-5) + sh(4) + sh(5) + sh(6) + sh(7)
    grp = lax.broadcasted_iota(jnp.int32, u.shape, 1) >> 6
    t = lax.broadcasted_iota(jnp.int32, u.shape, 0)
    tot = jnp.where(grp == 0, t2, jnp.where(grp == 1, t4, jnp.where(grp == 2, t8, t16)))
    half = jnp.where(grp == 0, 1, jnp.where(grp == 1, 2, jnp.where(grp == 2, 4, 8)))
    lo = jnp.maximum(t - half, 0)
    hi = jnp.minimum(t + half, n_seq)
    pooled = tot / (hi - lo).astype(F32) - u
    mixed = jnp.dot(pooled.astype(BF16), w_ref[...], preferred_element_type=F32)
    o_ref[...] = (mixed * sc_ref[...]).astype(o_ref.dtype)


def _pool(u, w_bd, scale, n_batch, n_seq, row0):
    base = row0 // n_seq
    return pl.pallas_call(
        functools.partial(_pool_kernel, n_seq=n_seq),
        grid=(n_batch,),
        in_specs=[
            pl.BlockSpec((n_seq, POOL_WIDTH), lambda b: (base + b, 0)),
            pl.BlockSpec((POOL_WIDTH, POOL_WIDTH), lambda b: (0, 0)),
            pl.BlockSpec((1, POOL_WIDTH), lambda b: (0, 0)),
        ],
        out_specs=pl.BlockSpec((n_seq, POOL_WIDTH), lambda b: (b, 0)),
        out_shape=jax.ShapeDtypeStruct((n_batch * n_seq, POOL_WIDTH), BF16),
        scratch_shapes=[pltpu.VMEM((n_seq + 16, POOL_WIDTH), F32)],
        compiler_params=_params(("parallel",)),
        name="pool_mixer",
    )(u, w_bd, scale)


def _scan_chunk_first(i):
    per_seq = SEQ // SSM_CHUNK
    return jnp.where(i < PROMPT_CHUNKS, i % per_seq == 0, i == PROMPT_CHUNKS)


def _scan_bwd_chunk(i):
    per_seq = SEQ // SSM_CHUNK
    return jnp.where(i < PROMPT_CHUNKS, (i // per_seq) * per_seq + (per_seq - 1 - i % per_seq),
                     PROMPT_CHUNKS + SCAN_CHUNKS - 1 - i)


def _scan_group(i):
    return jnp.where(i < PROMPT_CHUNKS, i // (SEQ // SSM_CHUNK), SSM_GROUPS_PROMPT)


def _split3(x):
    hi = x.astype(BF16)
    r1 = x - hi.astype(F32)
    mid = r1.astype(BF16)
    lo = (r1 - mid.astype(F32)).astype(BF16)
    return hi, mid, lo


def _ssm_kernel(uf_ref, ub_ref, a_ref, bf_ref, bb_ref, cf_ref, cb_ref, jin_ref, jout_ref, h0_ref,
                yf_ref, yb_ref, hfin_ref, bu_ref, st_ref, lhs_ref, yy_ref):
    @pl.when(_scan_chunk_first(pl.program_id(0)))
    def _():
        st_ref[...] = h0_ref[0]

    ub_rev = jnp.dot(jin_ref[...], ub_ref[...].astype(BF16), preferred_element_type=F32)
    gap = jnp.zeros((SSM_PITCH - SSM_CHUNK, SSM_WIDTH), F32)
    for s in range(SSM_SEQS):
        cols = slice(s * SSM_WIDTH, (s + 1) * SSM_WIDTH)
        lhs_ref[s * SSM_PITCH:s * SSM_PITCH + SSM_CHUNK, :] = uf_ref[:, cols]
        lhs_ref[s * SSM_PITCH + SSM_CHUNK:(s + 1) * SSM_PITCH, :] = gap
        lhs_ref[SSM_HALF + s * SSM_PITCH:SSM_HALF + s * SSM_PITCH + SSM_CHUNK, :] = ub_rev[:, cols]
        lhs_ref[SSM_HALF + s * SSM_PITCH + SSM_CHUNK:SSM_HALF + (s + 1) * SSM_PITCH, :] = gap
    bu_f = jnp.dot(lhs_ref[:SSM_HALF, :].astype(BF16), bf_ref[...], preferred_element_type=F32)
    bu_b = jnp.dot(lhs_ref[SSM_HALF:, :].astype(BF16), bb_ref[...], preferred_element_type=F32)
    for k in range(SSM_SLABS):
        bu_ref[k, :SSM_HALF, :] = bu_f[:, k * LANE:(k + 1) * LANE]
        bu_ref[k, SSM_HALF:, :] = bu_b[:, k * LANE:(k + 1) * LANE]

    half = SSM_SLABS // 2
    a_re = [a_ref[:, k * LANE:(k + 1) * LANE] for k in range(half)]
    a_im = [a_ref[:, SSM_COLS + k * LANE:SSM_COLS + (k + 1) * LANE] for k in range(half)]

    def step(t, carry):
        rows = pl.ds(t, SSM_ROWS, stride=SSM_PITCH)
        new_re, new_im = [], []
        for k in range(half):
            h_re, h_im = carry[k], carry[half + k]
            n_re = a_re[k] * h_re - a_im[k] * h_im + bu_ref[k, rows, :]
            n_im = a_re[k] * h_im + a_im[k] * h_re + bu_ref[half + k, rows, :]
            bu_ref[k, rows, :] = n_re
            bu_ref[half + k, rows, :] = n_im
            new_re.append(n_re)
            new_im.append(n_im)
        return tuple(new_re + new_im)

    init = tuple(st_ref[:, k * LANE:(k + 1) * LANE] for k in range(SSM_SLABS))
    fin = lax.fori_loop(0, SSM_CHUNK, step, init)
    for k in range(SSM_SLABS):
        st_ref[:, k * LANE:(k + 1) * LANE] = fin[k]
    hfin_ref[0] = st_ref[...]

    h_f = jnp.concatenate([bu_ref[k, :SSM_HALF, :] for k in range(SSM_SLABS)], axis=1).astype(BF16)
    h_b = jnp.concatenate([bu_ref[k, SSM_HALF:, :] for k in range(SSM_SLABS)], axis=1).astype(BF16)
    yy_ref[...] = jnp.dot(h_f, cf_ref[...], preferred_element_type=F32)
    for s in range(SSM_SEQS):
        yf_ref[:, s * SSM_WIDTH:(s + 1) * SSM_WIDTH] = yy_ref[s * SSM_PITCH:s * SSM_PITCH + SSM_CHUNK, :]
    y_b = jnp.dot(h_b, cb_ref[...], preferred_element_type=F32)
    y_nat = sum(jnp.dot(jout_ref[...], piece, preferred_element_type=F32) for piece in _split3(y_b))
    for s in range(SSM_SEQS):
        yb_ref[:, s * SSM_WIDTH:(s + 1) * SSM_WIDTH] = y_nat[s * SSM_CHUNK:(s + 1) * SSM_CHUNK, :]


def _ssm_scan(su8, h0_sample, a_rows, b_mat, c_mat):
    rows = SSM_ROWS
    width = SSM_SEQS * SSM_WIDTH
    n_groups = SSM_GROUPS_PROMPT + 1
    hs = h0_sample.astype(F32).reshape(SSM_SEQS, 2, SSM_COLS, 2).transpose(1, 0, 3, 2).reshape(1, rows, SSM_LANES)
    hh = jnp.concatenate([jnp.zeros((SSM_GROUPS_PROMPT, rows, SSM_LANES), F32), hs], axis=0)
    j_in = jnp.asarray(np.eye(SSM_CHUNK, dtype=np.float32)[::-1], dtype=BF16)
    sel = np.zeros((SSM_SEQS * SSM_CHUNK, SSM_HALF), np.float32)
    for s in range(SSM_SEQS):
        for t in range(SSM_CHUNK):
            sel[s * SSM_CHUNK + t, s * SSM_PITCH + SSM_CHUNK - 1 - t] = 1.0
    j_out = jnp.asarray(sel, dtype=BF16)
    fwd = lambda i: (i, 0)
    bwd = lambda i: (_scan_bwd_chunk(i), 0)
    grp = lambda i: (_scan_group(i), 0, 0)
    const = lambda i: (0, 0)
    y_f, y_b, hfin = pl.pallas_call(
        _ssm_kernel,
        grid=(SCAN_CHUNKS,),
        in_specs=[
            pl.BlockSpec((SSM_CHUNK, width), fwd),
            pl.BlockSpec((SSM_CHUNK, width), bwd),
            pl.BlockSpec((rows, SSM_LANES), const),
            pl.BlockSpec((SSM_WIDTH, SSM_LANES), const),
            pl.BlockSpec((SSM_WIDTH, SSM_LANES), const),
            pl.BlockSpec((SSM_LANES, SSM_WIDTH), const),
            pl.BlockSpec((SSM_LANES, SSM_WIDTH), const),
            pl.BlockSpec((SSM_CHUNK, SSM_CHUNK), const),
            pl.BlockSpec((SSM_SEQS * SSM_CHUNK, SSM_HALF), const),
            pl.BlockSpec((1, rows, SSM_LANES), grp),
        ],
        out_specs=[
            pl.BlockSpec((SSM_CHUNK, width), fwd),
            pl.BlockSpec((SSM_CHUNK, width), bwd),
            pl.BlockSpec((1, rows, SSM_LANES), grp),
        ],
        out_shape=[
            jax.ShapeDtypeStruct((SCAN_T, width), F32),
            jax.ShapeDtypeStruct((SCAN_T, width), F32),
            jax.ShapeDtypeStruct((n_groups, rows, SSM_LANES), F32),
        ],
        scratch_shapes=[pltpu.VMEM((SSM_SLABS, 2 * SSM_HALF, LANE), F32), pltpu.VMEM((rows, SSM_LANES), F32),
                        pltpu.VMEM((2 * SSM_HALF, SSM_WIDTH), F32), pltpu.VMEM((SSM_HALF, SSM_WIDTH), F32)],
        compiler_params=_params(("arbitrary",)),
        name="ssm_scan",
    )(su8, su8, a_rows, b_mat[:SSM_WIDTH], b_mat[SSM_WIDTH:], c_mat[:, :SSM_WIDTH], c_mat[:, SSM_WIDTH:],
      j_in, j_out, hh)
    fin = hfin[:SSM_GROUPS_PROMPT].reshape(SSM_GROUPS_PROMPT, 2, SSM_SEQS, 2, SSM_GROUPS, SSM_STATE)
    fin = fin.transpose(0, 2, 1, 4, 5, 3).reshape(BATCH, 2, SSM_GROUPS, SSM_STATE, 2)
    return y_f, y_b, fin


def _ssm_matrices(a_re, a_im, log_dt, b_re, b_im, c_re, c_im):
    lam = lax.complex(a_re.astype(F32), a_im.astype(F32))
    dt = jnp.exp(log_dt.astype(F32))[..., None]
    a_bar = jnp.exp(lam * dt)
    b_bar = ((a_bar - 1.0) / lam)[..., None] * lax.complex(b_re.astype(F32), b_im.astype(F32))
    a_dir = jnp.concatenate([jnp.real(a_bar).reshape(2, SSM_COLS), jnp.imag(a_bar).reshape(2, SSM_COLS)], axis=-1)
    a_rows = jnp.repeat(a_dir, SSM_SEQS, axis=0)
    eye = jnp.eye(SSM_GROUPS, dtype=F32)
    bt = jnp.transpose(b_bar, (0, 1, 3, 2))
    b_real = jnp.einsum('dghp,ge->dghep', jnp.real(bt), eye).reshape(2 * SSM_WIDTH, SSM_COLS)
    b_imag = jnp.einsum('dghp,ge->dghep', jnp.imag(bt), eye).reshape(2 * SSM_WIDTH, SSM_COLS)
    b_mat = jnp.concatenate([b_real, b_imag], axis=-1).astype(BF16)
    cr = jnp.transpose(c_re.astype(F32), (0, 1, 3, 2))
    ci = jnp.transpose(c_im.astype(F32), (0, 1, 3, 2))
    c_real = jnp.einsum('dgph,ge->gpdeh', cr, eye).reshape(SSM_COLS, 2 * SSM_WIDTH)
    c_imag = jnp.einsum('dgph,ge->gpdeh', -ci, eye).reshape(SSM_COLS, 2 * SSM_WIDTH)
    c_mat = jnp.concatenate([c_real, c_imag], axis=0).astype(BF16)
    return a_rows, b_mat, c_mat


def _merge_kernel(x_ref, mod_ref, g1_ref, wg_ref, oap_ref, oas_ref, opp_ref, ops_ref, su_ref, yf_ref, yb_ref,
                  d_ref, wglu_ref, bglu_ref, wua_ref, wup_ref, wus_ref, wo_ref, g2_ref, wr_ref, rb_ref,
                  eg_ref, eu_ref, ed_ref,
                  o_ref, h2_ref, gw_ref, egc_ref, euc_ref, edc_ref):
    x = x_ref[...]
    h = _norm_mod(x, g1_ref[...], mod_ref[0, 0:1, :], mod_ref[0, 1:2, :])
    gates = _sigmoid(jnp.dot(h.astype(BF16), wg_ref[...], preferred_element_type=F32))
    y = d_ref[...] * su_ref[...] + yf_ref[...] + yb_ref[...]
    y = 0.5 * y * (1.0 + jnp.tanh(math.sqrt(2.0 / math.pi) * (y + 0.044715 * (y * y * y))))
    glu = jnp.dot(y.astype(BF16), wglu_ref[...], preferred_element_type=F32) + bglu_ref[...]
    o_ssm = y * _sigmoid(glu)
    is_prompt = pl.program_id(0) < N_PROMPT // SEQ_TILE
    o_attn = jnp.where(is_prompt, oap_ref[...], oas_ref[...])
    o_pool = jnp.where(is_prompt, opp_ref[...], ops_ref[...])
    m = (gates[:, :D_MODEL] * jnp.dot(o_attn, wua_ref[...], preferred_element_type=F32)
         + gates[:, D_MODEL:2 * D_MODEL] * jnp.dot(o_pool, wup_ref[...], preferred_element_type=F32)
         + gates[:, 2 * D_MODEL:] * jnp.dot(o_ssm.astype(BF16), wus_ref[...], preferred_element_type=F32))
    x1 = x + mod_ref[0, 2:3, :] * jnp.dot(m.astype(BF16), wo_ref[...], preferred_element_type=F32)
    o_ref[...] = x1

    h2 = _norm_mod(x1, g2_ref[...], mod_ref[0, 3:4, :], mod_ref[0, 4:5, :])
    h2_ref[...] = h2.astype(BF16)
    logits = jnp.dot(h2, wr_ref[...], preferred_element_type=F32, precision=HIGHEST)
    scores = _sigmoid(logits)
    work = scores + rb_ref[...]
    lane = lax.broadcasted_iota(jnp.int32, work.shape, 1).astype(F32)
    w = jnp.zeros_like(scores)
    for _ in range(TOP_K):
        mx = jnp.max(work, axis=-1, keepdims=True)
        first = jnp.min(jnp.where(work == mx, lane, float(ROUTER_LANES)), axis=-1, keepdims=True)
        pick = lane == first
        w = jnp.where(pick, scores, w)
        work = jnp.where(pick, -jnp.inf, work)
    gw_ref[...] = w / jnp.sum(w, axis=-1, keepdims=True) * ROUTE_SCALE

    egc_ref[...] = eg_ref[...].astype(BF16)
    euc_ref[...] = eu_ref[...].astype(BF16)
    edc_ref[...] = ed_ref[...].astype(BF16)


def _merge(x, mod, norm1_g, w_gates, oa_p, oa_s, op_p, op_s, ssm_u, y_f, y_b, d_skip, w_glu, b_glu,
           w_up_attn, w_up_pool, w_up_ssm, w_out, norm2_g, w_router, router_bias, e_gate, e_up, e_down):
    tm = SEQ_TILE
    p_tiles = N_PROMPT // tm
    row = lambda i: (i, 0)
    row_p = lambda i: (jnp.minimum(i, p_tiles - 1), 0)
    row_s = lambda i: (jnp.maximum(i - p_tiles, 0), 0)
    const = lambda i: (0, 0)
    expert = lambda i: (jnp.minimum(i, N_EXPERTS - 1), 0, 0)
    return pl.pallas_call(
        _merge_kernel,
        grid=(N_TOK // tm,),
        in_specs=[
            pl.BlockSpec((tm, D_MODEL), row),
            pl.BlockSpec((1, N_MOD, D_MODEL), lambda i: (_mod_index(i, tm), 0, 0)),
            pl.BlockSpec((1, D_MODEL), const),
            pl.BlockSpec((D_MODEL, GATE_WIDTH), const),
            pl.BlockSpec((tm, ATTN_WIDTH), row_p),
            pl.BlockSpec((tm, ATTN_WIDTH), row_s),
            pl.BlockSpec((tm, POOL_WIDTH), row_p),
            pl.BlockSpec((tm, POOL_WIDTH), row_s),
            pl.BlockSpec((tm, SSM_WIDTH), _scan_block),
            pl.BlockSpec((tm, SSM_WIDTH), _scan_block),
            pl.BlockSpec((tm, SSM_WIDTH), _scan_block),
            pl.BlockSpec((1, SSM_WIDTH), const),
            pl.BlockSpec((SSM_WIDTH, SSM_WIDTH), const),
            pl.BlockSpec((1, SSM_WIDTH), const),
            pl.BlockSpec((ATTN_WIDTH, D_MODEL), const),
            pl.BlockSpec((POOL_WIDTH, D_MODEL), const),
            pl.BlockSpec((SSM_WIDTH, D_MODEL), const),
            pl.BlockSpec((D_MODEL, D_MODEL), const),
            pl.BlockSpec((1, D_MODEL), const),
            pl.BlockSpec((D_MODEL, ROUTER_LANES), const),
            pl.BlockSpec((1, ROUTER_LANES), const),
            pl.BlockSpec((1, D_MODEL, EXPERT_DIM), expert),
            pl.BlockSpec((1, D_MODEL, EXPERT_DIM), expert),
            pl.BlockSpec((1, EXPERT_DIM, D_MODEL), expert),
        ],
        out_specs=[
            pl.BlockSpec((tm, D_MODEL), row),
            pl.BlockSpec((tm, D_MODEL), row),
            pl.BlockSpec((tm, ROUTER_LANES), row),
            pl.BlockSpec((1, D_MODEL, EXPERT_DIM), expert),
            pl.BlockSpec((1, D_MODEL, EXPERT_DIM), expert),
            pl.BlockSpec((1, EXPERT_DIM, D_MODEL), expert),
        ],
        out_shape=[
            jax.ShapeDtypeStruct((N_TOK, D_MODEL), F32),
            jax.ShapeDtypeStruct((N_TOK, D_MODEL), BF16),
            jax.ShapeDtypeStruct((N_TOK, ROUTER_LANES), F32),
            jax.ShapeDtypeStruct((N_EXPERTS, D_MODEL, EXPERT_DIM), BF16),
            jax.ShapeDtypeStruct((N_EXPERTS, D_MODEL, EXPERT_DIM), BF16),
            jax.ShapeDtypeStruct((N_EXPERTS, EXPERT_DIM, D_MODEL), BF16),
        ],
        compiler_params=_params(("arbitrary",)),
        name="merge_branches",
    )(x, mod, norm1_g, w_gates, oa_p, oa_s, op_p, op_s, ssm_u, y_f, y_b, d_skip, w_glu, b_glu,
      w_up_attn, w_up_pool, w_up_ssm, w_out, norm2_g, w_router, router_bias, e_gate, e_up, e_down)


def _swiglu_mid(h, w_gate, w_up):
    a = jnp.dot(h, w_gate, preferred_element_type=F32)
    b = jnp.dot(h, w_up, preferred_element_type=F32)
    return (a * _sigmoid(a)) * b


def _experts_kernel(h_ref, gw_ref, wg_ref, wu_ref, wd_ref, sg_ref, su_ref, sd_ref, x_ref, mod_ref, fg_ref,
                    o_ref, *, final):
    j = pl.program_id(1)
    h = h_ref[...]

    @pl.when(j == 0)
    def _():
        mid = _swiglu_mid(h, sg_ref[...], su_ref[...]).astype(BF16)
        o_ref[...] = jnp.dot(mid, sd_ref[...], preferred_element_type=F32)

    gw = gw_ref[0]
    mids = [(_swiglu_mid(h, wg_ref[e], wu_ref[e]) * gw[:, e:e + 1]).astype(BF16) for e in range(EXPERT_CHUNK)]
    mid = jnp.concatenate(mids, axis=1)
    wd = wd_ref[...].reshape(EXPERT_CHUNK * EXPERT_DIM, D_MODEL)
    o_ref[...] += jnp.dot(mid, wd, preferred_element_type=F32)

    @pl.when(j == pl.num_programs(1) - 1)
    def _():
        x = x_ref[...] + mod_ref[0, 5:6, :] * o_ref[...]
        if final:
            ms = jnp.mean(x * x, axis=-1, keepdims=True)
            x = x * lax.rsqrt(ms + EPS) * fg_ref[...]
        o_ref[...] = x


def _experts(h, gate_w, wg, wu, wd, sg, su, sd, x, mod, final_g, final):
    tm = 1024
    n_chunks = N_EXPERTS // EXPERT_CHUNK
    gw = gate_w[:, :N_EXPERTS].reshape(N_TOK, n_chunks, EXPERT_CHUNK).transpose(1, 0, 2)
    row = lambda i, j: (i, 0)
    const = lambda i, j: (0, 0)
    chunk = lambda i, j: (j, 0, 0)
    return pl.pallas_call(
        functools.partial(_experts_kernel, final=final),
        grid=(N_TOK // tm, n_chunks),
        in_specs=[
            pl.BlockSpec((tm, D_MODEL), row),
            pl.BlockSpec((1, tm, EXPERT_CHUNK), lambda i, j: (j, i, 0)),
            pl.BlockSpec((EXPERT_CHUNK, D_MODEL, EXPERT_DIM), chunk),
            pl.BlockSpec((EXPERT_CHUNK, D_MODEL, EXPERT_DIM), chunk),
            pl.BlockSpec((EXPERT_CHUNK, EXPERT_DIM, D_MODEL), chunk),
            pl.BlockSpec((D_MODEL, SHARED_DIM), const),
            pl.BlockSpec((D_MODEL, SHARED_DIM), const),
            pl.BlockSpec((SHARED_DIM, D_MODEL), const),
            pl.BlockSpec((tm, D_MODEL), row),
            pl.BlockSpec((1, N_MOD, D_MODEL), lambda i, j: (_mod_index(i, tm), 0, 0)),
            pl.BlockSpec((1, D_MODEL), const),
        ],
        out_specs=pl.BlockSpec((tm, D_MODEL), row),
        out_shape=jax.ShapeDtypeStruct((N_TOK, D_MODEL), F32),
        compiler_params=_params(("parallel", "arbitrary")),
        name="moe_experts",
    )(h, gw, wg, wu, wd, sg, su, sd, x, mod, final_g)


def kernel(x_prompt, x_sample, c, cache_k, cache_v, state_ssm, c_ctx, w_mod, b_mod, norm1_g, norm2_g, w_in, q_norm_g, k_norm_g, w_up_attn, pool_w, pool_scale, w_up_pool, ssm_a_re, ssm_a_im, ssm_log_dt, ssm_b_re, ssm_b_im, ssm_c_re, ssm_c_im, ssm_d, w_glu, b_glu, w_up_ssm, w_out, w_router, router_bias, w_gate, w_up, w_down, ws_gate, ws_up, ws_down, final_norm_g):
    x = jnp.concatenate([x_prompt.reshape(N_PROMPT, D_MODEL), x_sample.reshape(N_SAMPLE, D_MODEL)], axis=0)
    cond = jnp.concatenate([c_ctx[None, :], c, jnp.zeros((COND_ROWS - N_COND, D_MODEL), F32)], axis=0)
    mod_all = _ada_all(cond, w_mod, b_mod)

    cos_t, sin_t = _rope_tables(SEQ_TILE)
    head_avg = jnp.kron(jnp.eye(QK_WIDTH // HEAD_DIM, dtype=F32),
                        jnp.full((HEAD_DIM, HEAD_DIM), 1.0 / HEAD_DIM, F32)).astype(BF16)
    final_g = final_norm_g.reshape(1, D_MODEL)

    new_k, new_v, new_s = [], [], []
    for l in range(DEPTH):
        mod = mod_all[l]
        qk_gain = jnp.concatenate([jnp.tile(q_norm_g[l], N_HEADS), jnp.tile(k_norm_g[l], N_KV_HEADS)])[None, :]
        q, k, v, pool_u, ssm_u = _in_project(
            x, mod, norm1_g[l][None, :], w_in[l][:, :MIX_WIDTH].astype(BF16), head_avg, qk_gain, cos_t, sin_t)

        k_p = k[:N_PROMPT].reshape(BATCH, SEQ, N_KV_HEADS, HEAD_DIM)
        v_p = v[:N_PROMPT].reshape(BATCH, SEQ, N_KV_HEADS, HEAD_DIM)
        k_s = k[N_PROMPT:].reshape(DEC_BATCH, DEC_SEQ, N_KV_HEADS, HEAD_DIM)
        v_s = v[N_PROMPT:].reshape(DEC_BATCH, DEC_SEQ, N_KV_HEADS, HEAD_DIM)
        new_k.append(k_p)
        new_v.append(v_p)
        keys = jnp.concatenate([cache_k[:, l], k_s], axis=1)
        vals = jnp.concatenate([cache_v[:, l], v_s], axis=1)
        oa_p = _attention(q, k_p, v_p, BATCH, SEQ, 0)
        oa_s = _attention(q, keys, vals, DEC_BATCH, DEC_SEQ, N_PROMPT)

        pool_bd = jax.scipy.linalg.block_diag(*[pool_w[l, g] for g in range(len(POOL_WINDOWS))]).astype(BF16)
        p_scale = pool_scale[l][None, :]
        op_p = _pool(pool_u, pool_bd, p_scale, BATCH, SEQ, 0)
        op_s = _pool(pool_u, pool_bd, p_scale, DEC_BATCH, DEC_SEQ, N_PROMPT)

        a_rows, b_mat, c_mat = _ssm_matrices(ssm_a_re[l], ssm_a_im[l], ssm_log_dt[l], ssm_b_re[l], ssm_b_im[l],
                                             ssm_c_re[l], ssm_c_im[l])
        y_f, y_b, st = _ssm_scan(ssm_u, state_ssm[:, l], a_rows, b_mat, c_mat)
        new_s.append(st)

        w_r = jnp.pad(w_router[l], ((0, 0), (0, ROUTER_LANES - N_EXPERTS)))
        r_b = jnp.concatenate([router_bias[l], jnp.full((ROUTER_LANES - N_EXPERTS,), -jnp.inf, F32)])[None, :]
        x, h2, gate_w, wg, wu, wd = _merge(
            x, mod, norm1_g[l][None, :], w_in[l][:, MIX_WIDTH:].astype(BF16), oa_p, oa_s, op_p, op_s, ssm_u,
            y_f, y_b, ssm_d[l].reshape(1, SSM_WIDTH), w_glu[l].astype(BF16), b_glu[l][None, :],
            w_up_attn[l].astype(BF16), w_up_pool[l].astype(BF16), w_up_ssm[l].astype(BF16),
            w_out[l].astype(BF16), norm2_g[l][None, :], w_r, r_b, w_gate[l], w_up[l], w_down[l])
        x = _experts(h2, gate_w, wg, wu, wd, ws_gate[l].astype(BF16), ws_up[l].astype(BF16),
                     ws_down[l].astype(BF16), x, mod, final_g, final=(l == DEPTH - 1))

    y_prompt = x[:N_PROMPT].reshape(BATCH, SEQ, D_MODEL)
    y_sample = x[N_PROMPT:].reshape(DEC_BATCH, DEC_SEQ, D_MODEL)
    return (y_prompt, y_sample, jnp.stack(new_k, axis=1), jnp.stack(new_v, axis=1), jnp.stack(new_s, axis=1))
```

```python
import functools
import math

import jax
import jax.numpy as jnp
import numpy as np
from jax import lax
from jax.experimental import pallas as pl
from jax.experimental.pallas import tpu as pltpu

D_MODEL = 1024
BATCH = 32
SEQ = 256
DEPTH = 2
DEC_BATCH = 4
DEC_SEQ = 4096
PAST_LEN = 256
GRID_W = 64
EPS = 1e-6
N_MOD = 6
HEAD_DIM = 64
N_HEADS = 8
N_KV_HEADS = 2
ATTN_WIDTH = N_HEADS * HEAD_DIM
KV_WIDTH = N_KV_HEADS * HEAD_DIM
ROPE_BASE = 10000.0
ROPE_PAIRS_PER_AXIS = HEAD_DIM // 4
POOL_WINDOWS = (2, 4, 8, 16)
POOL_GROUP = 64
POOL_WIDTH = len(POOL_WINDOWS) * POOL_GROUP
SSM_H = 16
SSM_GROUPS = 16
SSM_WIDTH = SSM_H * SSM_GROUPS
SSM_STATE = 64
N_EXPERTS = 64
TOP_K = 8
EXPERT_DIM = 256
SHARED_DIM = 256
ROUTE_SCALE = 2.5

N_PROMPT = BATCH * SEQ
N_SAMPLE = DEC_BATCH * DEC_SEQ
N_TOK = N_PROMPT + N_SAMPLE
N_COND = 1 + DEC_BATCH
COND_ROWS = 8
QK_WIDTH = ATTN_WIDTH + KV_WIDTH
MIX_WIDTH = QK_WIDTH + KV_WIDTH + POOL_WIDTH + SSM_WIDTH
GATE_WIDTH = 3 * D_MODEL
SSM_COLS = SSM_GROUPS * SSM_STATE
SSM_LANES = 2 * SSM_COLS
SSM_SEQS = 4
SSM_CHUNK = 128
ROUTER_LANES = 128
EXPERT_CHUNK = 4

SEQ_TILE = SEQ
SSM_ROWS = 2 * SSM_SEQS
SSM_GROUPS_PROMPT = BATCH // SSM_SEQS
SCAN_T = SSM_GROUPS_PROMPT * SEQ + DEC_SEQ
SCAN_CHUNKS = SCAN_T // SSM_CHUNK
PROMPT_CHUNKS = SSM_GROUPS_PROMPT * SEQ // SSM_CHUNK
SSM_PITCH = SSM_CHUNK + 4
SSM_HALF = SSM_SEQS * SSM_PITCH

LANE = 128
SSM_SLABS = SSM_LANES // LANE
VMEM_LIMIT = 56 * 1024 * 1024

F32 = jnp.float32
BF16 = jnp.bfloat16
HIGHEST = lax.Precision.HIGHEST


def _sigmoid(x):
    return 1.0 / (1.0 + jnp.exp(-x))


def _params(dims, vmem=VMEM_LIMIT):
    return pltpu.CompilerParams(dimension_semantics=dims, vmem_limit_bytes=vmem)


def _mod_index(i, tm):
    p = N_PROMPT // tm
    t = DEC_SEQ // tm
    return jnp.where(i < p, 0, 1 + (i - p) // t)


def _scan_block(i):
    k = i - BATCH
    tiles = DEC_SEQ // SEQ_TILE
    rb = jnp.where(i < BATCH, i // SSM_SEQS, SSM_GROUPS_PROMPT + k % tiles)
    slot = jnp.where(i < BATCH, i % SSM_SEQS, k // tiles)
    return rb, slot


def _norm_mod(x, g, shift, scale):
    ms = jnp.mean(x * x, axis=-1, keepdims=True)
    return (x * lax.rsqrt(ms + EPS) * g) * (1.0 + scale) + shift


def _mod_kernel(cond_ref, w_ref, b_ref, o_ref):
    c = cond_ref[...]
    s = c * _sigmoid(c)
    o_ref[0] = jnp.dot(s, w_ref[0], preferred_element_type=F32, precision=HIGHEST) + b_ref[0]


def _ada_all(cond, w_mod, b_mod):
    tn = 1536
    width = N_MOD * D_MODEL
    out = pl.pallas_call(
        _mod_kernel,
        grid=(DEPTH, width // tn),
        in_specs=[
            pl.BlockSpec((COND_ROWS, D_MODEL), lambda l, j: (0, 0)),
            pl.BlockSpec((1, D_MODEL, tn), lambda l, j: (l, 0, j)),
            pl.BlockSpec((1, 1, tn), lambda l, j: (l, 0, j)),
        ],
        out_specs=pl.BlockSpec((1, COND_ROWS, tn), lambda l, j: (l, 0, j)),
        out_shape=jax.ShapeDtypeStruct((DEPTH, COND_ROWS, width), F32),
        compiler_params=_params(("parallel", "parallel")),
        name="ada_mod",
    )(cond, w_mod, b_mod.reshape(DEPTH, 1, width))
    return out.reshape(DEPTH, COND_ROWS, N_MOD, D_MODEL)


def _inproj_kernel(x_ref, mod_ref, g_ref, w_ref, bd_ref, qkg_ref, cos_ref, sin_ref,
                   q_ref, k_ref, v_ref, pu_ref, su_ref):
    h = _norm_mod(x_ref[...], g_ref[...], mod_ref[0, 0:1, :], mod_ref[0, 1:2, :])
    z = jnp.dot(h.astype(BF16), w_ref[...], preferred_element_type=F32)
    qk = z[:, :QK_WIDTH]
    qq = qk * qk
    hi = qq.astype(BF16)
    lo = (qq - hi.astype(F32)).astype(BF16)
    ms = (jnp.dot(hi, bd_ref[...], preferred_element_type=F32)
          + jnp.dot(lo, bd_ref[...], preferred_element_type=F32))
    qkn = qk * lax.rsqrt(ms + EPS) * qkg_ref[...]
    parts = []
    for c in range(QK_WIDTH // LANE):
        blk = qkn[:, c * LANE:(c + 1) * LANE]
        nxt = pltpu.roll(blk, LANE - 1, axis=1)
        prv = pltpu.roll(blk, 1, axis=1)
        lane = lax.broadcasted_iota(jnp.int32, blk.shape, 1)
        parts.append(jnp.where((lane & 1) == 0, nxt, prv))
    partner = jnp.concatenate(parts, axis=1)
    qkr = qkn * cos_ref[...] + partner * sin_ref[...]
    q_ref[...] = (qkr[:, :ATTN_WIDTH] * (HEAD_DIM ** -0.5 * math.log2(math.e))).astype(BF16)
    k_ref[...] = qkr[:, ATTN_WIDTH:QK_WIDTH]
    v_ref[...] = z[:, QK_WIDTH:QK_WIDTH + KV_WIDTH]
    pu_ref[...] = z[:, QK_WIDTH + KV_WIDTH:QK_WIDTH + KV_WIDTH + POOL_WIDTH]
    su_ref[...] = z[:, QK_WIDTH + KV_WIDTH + POOL_WIDTH:MIX_WIDTH]


def _in_project(x, mod, norm_g, w_mix, bd, qk_gain, cos_t, sin_t):
    tm = SEQ_TILE
    p_tiles = N_PROMPT // tm
    s_tiles = DEC_SEQ // tm

    def rope_idx(i):
        return (jnp.where(i < p_tiles, 0, 1 + (i - p_tiles) % s_tiles), 0)

    row = lambda i: (i, 0)
    const = lambda i: (0, 0)
    return pl.pallas_call(
        _inproj_kernel,
        grid=(N_TOK // tm,),
        in_specs=[
            pl.BlockSpec((tm, D_MODEL), row),
            pl.BlockSpec((1, N_MOD, D_MODEL), lambda i: (_mod_index(i, tm), 0, 0)),
            pl.BlockSpec((1, D_MODEL), const),
            pl.BlockSpec((D_MODEL, MIX_WIDTH), const),
            pl.BlockSpec((QK_WIDTH, QK_WIDTH), const),
            pl.BlockSpec((1, QK_WIDTH), const),
            pl.BlockSpec((tm, QK_WIDTH), rope_idx),
            pl.BlockSpec((tm, QK_WIDTH), rope_idx),
        ],
        out_specs=[
            pl.BlockSpec((tm, ATTN_WIDTH), row),
            pl.BlockSpec((tm, KV_WIDTH), row),
            pl.BlockSpec((tm, KV_WIDTH), row),
            pl.BlockSpec((tm, POOL_WIDTH), row),
            pl.BlockSpec((tm, SSM_WIDTH), _scan_block),
        ],
        out_shape=[
            jax.ShapeDtypeStruct((N_TOK, ATTN_WIDTH), BF16),
            jax.ShapeDtypeStruct((N_TOK, KV_WIDTH), F32),
            jax.ShapeDtypeStruct((N_TOK, KV_WIDTH), F32),
            jax.ShapeDtypeStruct((N_TOK, POOL_WIDTH), F32),
            jax.ShapeDtypeStruct((SCAN_T, SSM_SEQS * SSM_WIDTH), F32),
        ],
        compiler_params=_params(("parallel",)),
        name="in_project",
    )(x, mod, norm_g, w_mix, bd, qk_gain, cos_t, sin_t)


def _rope_tables(tm):
    rows = DEC_SEQ // GRID_W
    row = jnp.repeat(jnp.arange(rows, dtype=F32), GRID_W)
    col = jnp.tile(jnp.arange(GRID_W, dtype=F32), rows)
    inv_freq = ROPE_BASE ** (-jnp.arange(ROPE_PAIRS_PER_AXIS, dtype=F32) / ROPE_PAIRS_PER_AXIS)
    ang = jnp.concatenate([row[:, None] * inv_freq, col[:, None] * inv_freq], axis=-1)
    cos = jnp.repeat(jnp.cos(ang), 2, axis=-1)
    sin = jnp.repeat(jnp.sin(ang), 2, axis=-1) * jnp.tile(jnp.array([-1.0, 1.0], F32), HEAD_DIM // 2)
    n_rep = QK_WIDTH // HEAD_DIM
    cos = jnp.concatenate([jnp.ones((tm, HEAD_DIM), F32), cos], axis=0)
    sin = jnp.concatenate([jnp.zeros((tm, HEAD_DIM), F32), sin], axis=0)
    return jnp.tile(cos, (1, n_rep)), jnp.tile(sin, (1, n_rep))


def _attn_kernel(q_ref, kt_ref, ve_ref, vo_ref, *rest):
    n_cast = (len(rest) - 1) // 2
    o_ref = rest[n_cast]
    for src, dst in zip(rest[:n_cast], rest[n_cast + 1:]):
        dst[...] = src[...].astype(dst.dtype)
    kt = kt_ref[0]
    lane = lax.broadcasted_iota(jnp.int32, (q_ref.shape[0], LANE), 1)
    first = lane < HEAD_DIM
    slabs = []
    for pair in range(q_ref.shape[1] // LANE):
        q = q_ref[:, pair * LANE:(pair + 1) * LANE]
        halves = []
        for keep, v_ref in ((first, ve_ref), (lane >= HEAD_DIM, vo_ref)):
            qh = jnp.where(keep, q, jnp.zeros_like(q))
            s = jnp.dot(qh, kt, preferred_element_type=F32)
            m = jnp.max(s, axis=-1, keepdims=True)
            p = jnp.exp2(s - m).astype(BF16)
            a = jnp.dot(p, v_ref[0], preferred_element_type=F32)
            halves.append(a / pltpu.roll(a, HEAD_DIM, axis=1))
        slabs.append(jnp.where(first, halves[0], halves[1]))
    o_ref[...] = jnp.concatenate(slabs, axis=1).astype(o_ref.dtype)


def _attention(q, keys, vals, n_batch, n_q, row0, cast=None):
    n_keys = keys.shape[1]
    tq = 256
    kt = jnp.transpose(keys, (0, 2, 3, 1)).astype(BF16)
    kt = jnp.concatenate([kt, kt], axis=2).reshape(n_batch * N_KV_HEADS, 2 * HEAD_DIM, n_keys)
    vv = jnp.transpose(vals, (0, 2, 1, 3)).astype(BF16)
    ones = jnp.ones_like(vv)
    v_e = jnp.concatenate([vv, ones], axis=3).reshape(n_batch * N_KV_HEADS, n_keys, 2 * HEAD_DIM)
    v_o = jnp.concatenate([ones, vv], axis=3).reshape(n_batch * N_KV_HEADS, n_keys, 2 * HEAD_DIM)
    q_tiles = n_q // tq
    base = row0 // tq
    group_w = ATTN_WIDTH // N_KV_HEADS
    kv_idx = lambda b, g, i: (b * N_KV_HEADS + g, 0, 0)
    in_specs = [
        pl.BlockSpec((tq, group_w), lambda b, g, i: (base + b * q_tiles + i, g)),
        pl.BlockSpec((1, LANE, n_keys), kv_idx),
        pl.BlockSpec((1, n_keys, LANE), kv_idx),
        pl.BlockSpec((1, n_keys, LANE), kv_idx),
    ]
    out_specs = [pl.BlockSpec((tq, group_w), lambda b, g, i: (b * q_tiles + i, g))]
    out_shape = [jax.ShapeDtypeStruct((n_batch * n_q, ATTN_WIDTH), BF16)]
    args = [q, kt, v_e, v_o]
    if cast is not None:
        layer, tensors = cast
        steps = n_batch * N_KV_HEADS * q_tiles
        step = lambda b, g, i: (b * N_KV_HEADS + g) * q_tiles + i
        for w in tensors:
            _, n_e, rows, cols = w.shape
            part = n_e * rows // steps
            assert part * steps == n_e * rows and part % 16 == 0 and rows % part == 0
            in_specs.append(pl.BlockSpec((1, part, cols), lambda b, g, i: (layer * steps + step(b, g, i), 0, 0)))
            out_specs.append(pl.BlockSpec((1, part, cols), lambda b, g, i: (step(b, g, i), 0, 0)))
            out_shape.append(jax.ShapeDtypeStruct((steps, part, cols), BF16))
            args.append(w.reshape(DEPTH * steps, part, cols))
    outs = pl.pallas_call(
        _attn_kernel,
        grid=(n_batch, N_KV_HEADS, q_tiles),
        in_specs=in_specs,
        out_specs=out_specs,
        out_shape=out_shape,
        compiler_params=_params(("parallel", "parallel", "parallel")),
        name="attention",
    )(*args)
    if cast is None:
        return outs[0]
    return [outs[0]] + [o.reshape(w.shape[1:]) for o, w in zip(outs[1:], cast[1])]


def _pool_kernel(u_ref, w_ref, sc_ref, o_ref, pad_ref, *, n_seq):
    halo = 8
    u = u_ref[...]
    zeros = jnp.zeros((halo, POOL_WIDTH), F32)
    pad_ref[0:halo, :] = zeros
    pad_ref[halo + n_seq:2 * halo + n_seq, :] = zeros
    pad_ref[halo:halo + n_seq, :] = u

    def sh(j):
        return pad_ref[halo + j:halo + j + n_seq, :]

    t2 = sh(-1) + u
    t4 = t2 + sh(-2) + sh(1)
    t8 = t4 + sh(-4) + sh(-3) + sh(2) + sh(3)
    t16 = t8 + sh(-8) + sh(-7) + sh(-6) + sh(-5) + sh(4) + sh(5) + sh(6) + sh(7)
    grp = lax.broadcasted_iota(jnp.int32, u.shape, 1) >> 6
    t = lax.broadcasted_iota(jnp.int32, u.shape, 0)
    tot = jnp.where(grp == 0, t2, jnp.where(grp == 1, t4, jnp.where(grp == 2, t8, t16)))
    half = jnp.where(grp == 0, 1, jnp.where(grp == 1, 2, jnp.where(grp == 2, 4, 8)))
    lo = jnp.maximum(t - half, 0)
    hi = jnp.minimum(t + half, n_seq)
    pooled = tot / (hi - lo).astype(F32) - u
    mixed = jnp.dot(pooled.astype(BF16), w_ref[...], preferred_element_type=F32)
    o_ref[...] = (mixed * sc_ref[...]).astype(o_ref.dtype)


def _pool(u, w_bd, scale, n_batch, n_seq, row0):
    base = row0 // n_seq
    return pl.pallas_call(
        functools.partial(_pool_kernel, n_seq=n_seq),
        grid=(n_batch,),
        in_specs=[
            pl.BlockSpec((n_seq, POOL_WIDTH), lambda b: (base + b, 0)),
            pl.BlockSpec((POOL_WIDTH, POOL_WIDTH), lambda b: (0, 0)),
            pl.BlockSpec((1, POOL_WIDTH), lambda b: (0, 0)),
        ],
        out_specs=pl.BlockSpec((n_seq, POOL_WIDTH), lambda b: (b, 0)),
        out_shape=jax.ShapeDtypeStruct((n_batch * n_seq, POOL_WIDTH), BF16),
        scratch_shapes=[pltpu.VMEM((n_seq + 16, POOL_WIDTH), F32)],
        compiler_params=_params(("parallel",)),
        name="pool_mixer",
    )(u, w_bd, scale)


def _scan_chunk_first(i):
    per_seq = SEQ // SSM_CHUNK
    return jnp.where(i < PROMPT_CHUNKS, i % per_seq == 0, i == PROMPT_CHUNKS)


def _scan_bwd_chunk(i):
    per_seq = SEQ // SSM_CHUNK
    return jnp.where(i < PROMPT_CHUNKS, (i // per_seq) * per_seq + (per_seq - 1 - i % per_seq),
                     PROMPT_CHUNKS + SCAN_CHUNKS - 1 - i)


def _scan_group(i):
    return jnp.where(i < PROMPT_CHUNKS, i // (SEQ // SSM_CHUNK), SSM_GROUPS_PROMPT)


def _split3(x):
    hi = x.astype(BF16)
    r1 = x - hi.astype(F32)
    mid = r1.astype(BF16)
    lo = (r1 - mid.astype(F32)).astype(BF16)
    return hi, mid, lo


def _ssm_kernel(uf_ref, ub_ref, a_ref, bf_ref, bb_ref, cf_ref, cb_ref, jin_ref, jout_ref, h0_ref,
                yf_ref, yb_ref, hfin_ref, bu_ref, st_ref, lhs_ref, yy_ref):
    @pl.when(_scan_chunk_first(pl.program_id(0)))
    def _():
        st_ref[...] = h0_ref[0]

    ub_rev = jnp.dot(jin_ref[...], ub_ref[...].astype(BF16), preferred_element_type=F32)
    gap = jnp.zeros((SSM_PITCH - SSM_CHUNK, SSM_WIDTH), F32)
    for s in range(SSM_SEQS):
        cols = slice(s * SSM_WIDTH, (s + 1) * SSM_WIDTH)
        lhs_ref[s * SSM_PITCH:s * SSM_PITCH + SSM_CHUNK, :] = uf_ref[:, cols]
        lhs_ref[s * SSM_PITCH + SSM_CHUNK:(s + 1) * SSM_PITCH, :] = gap
        lhs_ref[SSM_HALF + s * SSM_PITCH:SSM_HALF + s * SSM_PITCH + SSM_CHUNK, :] = ub_rev[:, cols]
        lhs_ref[SSM_HALF + s * SSM_PITCH + SSM_CHUNK:SSM_HALF + (s + 1) * SSM_PITCH, :] = gap
    bu_f = jnp.dot(lhs_ref[:SSM_HALF, :].astype(BF16), bf_ref[...], preferred_element_type=F32)
    bu_b = jnp.dot(lhs_ref[SSM_HALF:, :].astype(BF16), bb_ref[...], preferred_element_type=F32)
    for k in range(SSM_SLABS):
        bu_ref[k, :SSM_HALF, :] = bu_f[:, k * LANE:(k + 1) * LANE]
        bu_ref[k, SSM_HALF:, :] = bu_b[:, k * LANE:(k + 1) * LANE]

    half = SSM_SLABS // 2
    a_re = [a_ref[:, k * LANE:(k + 1) * LANE] for k in range(half)]
    a_im = [a_ref[:, SSM_COLS + k * LANE:SSM_COLS + (k + 1) * LANE] for k in range(half)]

    def step(t, carry):
        rows = pl.ds(t, SSM_ROWS, stride=SSM_PITCH)
        new_re, new_im = [], []
        for k in range(half):
            h_re, h_im = carry[k], carry[half + k]
            n_re = a_re[k] * h_re - a_im[k] * h_im + bu_ref[k, rows, :]
            n_im = a_re[k] * h_im + a_im[k] * h_re + bu_ref[half + k, rows, :]
            bu_ref[k, rows, :] = n_re
            bu_ref[half + k, rows, :] = n_im
            new_re.append(n_re)
            new_im.append(n_im)
        return tuple(new_re + new_im)

    init = tuple(st_ref[:, k * LANE:(k + 1) * LANE] for k in range(SSM_SLABS))
    fin = lax.fori_loop(0, SSM_CHUNK, step, init)
    for k in range(SSM_SLABS):
        st_ref[:, k * LANE:(k + 1) * LANE] = fin[k]
    hfin_ref[0] = st_ref[...]

    h_f = jnp.concatenate([bu_ref[k, :SSM_HALF, :] for k in range(SSM_SLABS)], axis=1).astype(BF16)
    h_b = jnp.concatenate([bu_ref[k, SSM_HALF:, :] for k in range(SSM_SLABS)], axis=1).astype(BF16)
    yy_ref[...] = jnp.dot(h_f, cf_ref[...], preferred_element_type=F32)
    for s in range(SSM_SEQS):
        yf_ref[:, s * SSM_WIDTH:(s + 1) * SSM_WIDTH] = yy_ref[s * SSM_PITCH:s * SSM_PITCH + SSM_CHUNK, :]
    y_b = jnp.dot(h_b, cb_ref[...], preferred_element_type=F32)
    y_nat = sum(jnp.dot(jout_ref[...], piece, preferred_element_type=F32) for piece in _split3(y_b))
    for s in range(SSM_SEQS):
        yb_ref[:, s * SSM_WIDTH:(s + 1) * SSM_WIDTH] = y_nat[s * SSM_CHUNK:(s + 1) * SSM_CHUNK, :]


def _ssm_scan(su8, h0_sample, a_rows, b_mat, c_mat):
    rows = SSM_ROWS
    width = SSM_SEQS * SSM_WIDTH
    n_groups = SSM_GROUPS_PROMPT + 1
    hs = h0_sample.astype(F32).reshape(SSM_SEQS, 2, SSM_COLS, 2).transpose(1, 0, 3, 2).reshape(1, rows, SSM_LANES)
    hh = jnp.concatenate([jnp.zeros((SSM_GROUPS_PROMPT, rows, SSM_LANES), F32), hs], axis=0)
    j_in = jnp.asarray(np.eye(SSM_CHUNK, dtype=np.float32)[::-1], dtype=BF16)
    sel = np.zeros((SSM_SEQS * SSM_CHUNK, SSM_HALF), np.float32)
    for s in range(SSM_SEQS):
        for t in range(SSM_CHUNK):
            sel[s * SSM_CHUNK + t, s * SSM_PITCH + SSM_CHUNK - 1 - t] = 1.0
    j_out = jnp.asarray(sel, dtype=BF16)
    fwd = lambda i: (i, 0)
    bwd = lambda i: (_scan_bwd_chunk(i), 0)
    grp = lambda i: (_scan_group(i), 0, 0)
    const = lambda i: (0, 0)
    y_f, y_b, hfin = pl.pallas_call(
        _ssm_kernel,
        grid=(SCAN_CHUNKS,),
        in_specs=[
            pl.BlockSpec((SSM_CHUNK, width), fwd),
            pl.BlockSpec((SSM_CHUNK, width), bwd),
            pl.BlockSpec((rows, SSM_LANES), const),
            pl.BlockSpec((SSM_WIDTH, SSM_LANES), const),
            pl.BlockSpec((SSM_WIDTH, SSM_LANES), const),
            pl.BlockSpec((SSM_LANES, SSM_WIDTH), const),
            pl.BlockSpec((SSM_LANES, SSM_WIDTH), const),
            pl.BlockSpec((SSM_CHUNK, SSM_CHUNK), const),
            pl.BlockSpec((SSM_SEQS * SSM_CHUNK, SSM_HALF), const),
            pl.BlockSpec((1, rows, SSM_LANES), grp),
        ],
        out_specs=[
            pl.BlockSpec((SSM_CHUNK, width), fwd),
            pl.BlockSpec((SSM_CHUNK, width), bwd),
            pl.BlockSpec((1, rows, SSM_LANES), grp),
        ],
        out_shape=[
            jax.ShapeDtypeStruct((SCAN_T, width), F32),
            jax.ShapeDtypeStruct((SCAN_T, width), F32),
            jax.ShapeDtypeStruct((n_groups, rows, SSM_LANES), F32),
        ],
        scratch_shapes=[pltpu.VMEM((SSM_SLABS, 2 * SSM_HALF, LANE), F32), pltpu.VMEM((rows, SSM_LANES), F32),
                        pltpu.VMEM((2 * SSM_HALF, SSM_WIDTH), F32), pltpu.VMEM((SSM_HALF, SSM_WIDTH), F32)],
        compiler_params=_params(("arbitrary",)),
        name="ssm_scan",
    )(su8, su8, a_rows, b_mat[:SSM_WIDTH], b_mat[SSM_WIDTH:], c_mat[:, :SSM_WIDTH], c_mat[:, SSM_WIDTH:],
      j_in, j_out, hh)
    fin = hfin[:SSM_GROUPS_PROMPT].reshape(SSM_GROUPS_PROMPT, 2, SSM_SEQS, 2, SSM_GROUPS, SSM_STATE)
    fin = fin.transpose(0, 2, 1, 4, 5, 3).reshape(BATCH, 2, SSM_GROUPS, SSM_STATE, 2)
    return y_f, y_b, fin


def _ssm_matrices(a_re, a_im, log_dt, b_re, b_im, c_re, c_im):
    lam = lax.complex(a_re.astype(F32), a_im.astype(F32))
    dt = jnp.exp(log_dt.astype(F32))[..., None]
    a_bar = jnp.exp(lam * dt)
    b_bar = ((a_bar - 1.0) / lam)[..., None] * lax.complex(b_re.astype(F32), b_im.astype(F32))
    a_dir = jnp.concatenate([jnp.real(a_bar).reshape(2, SSM_COLS), jnp.imag(a_bar).reshape(2, SSM_COLS)], axis=-1)
    a_rows = jnp.repeat(a_dir, SSM_SEQS, axis=0)
    eye = jnp.eye(SSM_GROUPS, dtype=F32)
    bt = jnp.transpose(b_bar, (0, 1, 3, 2))
    b_real = jnp.einsum('dghp,ge->dghep', jnp.real(bt), eye).reshape(2 * SSM_WIDTH, SSM_COLS)
    b_imag = jnp.einsum('dghp,ge->dghep', jnp.imag(bt), eye).reshape(2 * SSM_WIDTH, SSM_COLS)
    b_mat = jnp.concatenate([b_real, b_imag], axis=-1).astype(BF16)
    cr = jnp.transpose(c_re.astype(F32), (0, 1, 3, 2))
    ci = jnp.transpose(c_im.astype(F32), (0, 1, 3, 2))
    c_real = jnp.einsum('dgph,ge->gpdeh', cr, eye).reshape(SSM_COLS, 2 * SSM_WIDTH)
    c_imag = jnp.einsum('dgph,ge->gpdeh', -ci, eye).reshape(SSM_COLS, 2 * SSM_WIDTH)
    c_mat = jnp.concatenate([c_real, c_imag], axis=0).astype(BF16)
    return a_rows, b_mat, c_mat


def _merge_kernel(x_ref, mod_ref, g1_ref, wg_ref, oap_ref, oas_ref, opp_ref, ops_ref, su_ref, yf_ref, yb_ref,
                  d_ref, wglu_ref, bglu_ref, wua_ref, wup_ref, wus_ref, wo_ref, o_ref):
    x = x_ref[...]
    h = _norm_mod(x, g1_ref[...], mod_ref[0, 0:1, :], mod_ref[0, 1:2, :])
    gates = _sigmoid(jnp.dot(h.astype(BF16), wg_ref[...], preferred_element_type=F32))
    y = d_ref[...] * su_ref[...] + yf_ref[...] + yb_ref[...]
    y = 0.5 * y * (1.0 + jnp.tanh(math.sqrt(2.0 / math.pi) * (y + 0.044715 * (y * y * y))))
    glu = jnp.dot(y.astype(BF16), wglu_ref[...], preferred_element_type=F32) + bglu_ref[...]
    o_ssm = y * _sigmoid(glu)
    is_prompt = pl.program_id(0) < N_PROMPT // SEQ_TILE
    o_attn = jnp.where(is_prompt, oap_ref[...], oas_ref[...])
    o_pool = jnp.where(is_prompt, opp_ref[...], ops_ref[...])
    m = (gates[:, :D_MODEL] * jnp.dot(o_attn, wua_ref[...], preferred_element_type=F32)
         + gates[:, D_MODEL:2 * D_MODEL] * jnp.dot(o_pool, wup_ref[...], preferred_element_type=F32)
         + gates[:, 2 * D_MODEL:] * jnp.dot(o_ssm.astype(BF16), wus_ref[...], preferred_element_type=F32))
    o_ref[...] = x + mod_ref[0, 2:3, :] * jnp.dot(m.astype(BF16), wo_ref[...], preferred_element_type=F32)


def _merge(x, mod, norm1_g, w_gates, oa_p, oa_s, op_p, op_s, ssm_u, y_f, y_b, d_skip, w_glu, b_glu,
           w_up_attn, w_up_pool, w_up_ssm, w_out):
    tm = SEQ_TILE
    p_tiles = N_PROMPT // tm
    row = lambda i: (i, 0)
    row_p = lambda i: (jnp.minimum(i, p_tiles - 1), 0)
    row_s = lambda i: (jnp.maximum(i - p_tiles, 0), 0)
    const = lambda i: (0, 0)
    return pl.pallas_call(
        _merge_kernel,
        grid=(N_TOK // tm,),
        in_specs=[
            pl.BlockSpec((tm, D_MODEL), row),
            pl.BlockSpec((1, N_MOD, D_MODEL), lambda i: (_mod_index(i, tm), 0, 0)),
            pl.BlockSpec((1, D_MODEL), const),
            pl.BlockSpec((D_MODEL, GATE_WIDTH), const),
            pl.BlockSpec((tm, ATTN_WIDTH), row_p),
            pl.BlockSpec((tm, ATTN_WIDTH), row_s),
            pl.BlockSpec((tm, POOL_WIDTH), row_p),
            pl.BlockSpec((tm, POOL_WIDTH), row_s),
            pl.BlockSpec((tm, SSM_WIDTH), _scan_block),
            pl.BlockSpec((tm, SSM_WIDTH), _scan_block),
            pl.BlockSpec((tm, SSM_WIDTH), _scan_block),
            pl.BlockSpec((1, SSM_WIDTH), const),
            pl.BlockSpec((SSM_WIDTH, SSM_WIDTH), const),
            pl.BlockSpec((1, SSM_WIDTH), const),
            pl.BlockSpec((ATTN_WIDTH, D_MODEL), const),
            pl.BlockSpec((POOL_WIDTH, D_MODEL), const),
            pl.BlockSpec((SSM_WIDTH, D_MODEL), const),
            pl.BlockSpec((D_MODEL, D_MODEL), const),
        ],
        out_specs=pl.BlockSpec((tm, D_MODEL), row),
        out_shape=jax.ShapeDtypeStruct((N_TOK, D_MODEL), F32),
        compiler_params=_params(("parallel",)),
        name="merge_branches",
    )(x, mod, norm1_g, w_gates, oa_p, oa_s, op_p, op_s, ssm_u, y_f, y_b, d_skip, w_glu, b_glu,
      w_up_attn, w_up_pool, w_up_ssm, w_out)


def _router_kernel(x_ref, mod_ref, g_ref, wrh_ref, wrl_ref, rb_ref, h_ref, gw_ref):
    h2 = _norm_mod(x_ref[...], g_ref[...], mod_ref[0, 3:4, :], mod_ref[0, 4:5, :])
    h_hi = h2.astype(BF16)
    h_ref[...] = h_hi
    h_lo = (h2 - h_hi.astype(F32)).astype(BF16)
    logits = (jnp.dot(h_hi, wrh_ref[...], preferred_element_type=F32)
              + jnp.dot(h_lo, wrh_ref[...], preferred_element_type=F32)
              + jnp.dot(h_hi, wrl_ref[...], preferred_element_type=F32))
    scores = _sigmoid(logits)
    sc_t = jnp.transpose(scores)[:N_EXPERTS, :]
    work = jnp.transpose(scores + rb_ref[...])[:N_EXPERTS, :]
    idx = lax.broadcasted_iota(jnp.int32, work.shape, 0).astype(F32)
    w_t = jnp.zeros_like(sc_t)
    for _ in range(TOP_K):
        mx = jnp.max(work, axis=0, keepdims=True)
        first = jnp.min(jnp.where(work == mx, idx, float(N_EXPERTS)), axis=0, keepdims=True)
        pick = idx == first
        w_t = jnp.where(pick, sc_t, w_t)
        work = jnp.where(pick, -jnp.inf, work)
    w_t = w_t / jnp.sum(w_t, axis=0, keepdims=True) * ROUTE_SCALE
    w_pad = jnp.concatenate([w_t, jnp.zeros((ROUTER_LANES - N_EXPERTS, w_t.shape[1]), F32)], axis=0)
    gw_ref[...] = jnp.transpose(w_pad)


def _route(x, mod, norm_g, w_router, router_bias):
    tm = 512
    row = lambda i: (i, 0)
    const = lambda i: (0, 0)
    w_hi = w_router.astype(BF16)
    w_lo = (w_router - w_hi.astype(F32)).astype(BF16)
    return pl.pallas_call(
        _router_kernel,
        grid=(N_TOK // tm,),
        in_specs=[
            pl.BlockSpec((tm, D_MODEL), row),
            pl.BlockSpec((1, N_MOD, D_MODEL), lambda i: (_mod_index(i, tm), 0, 0)),
            pl.BlockSpec((1, D_MODEL), const),
            pl.BlockSpec((D_MODEL, ROUTER_LANES), const),
            pl.BlockSpec((D_MODEL, ROUTER_LANES), const),
            pl.BlockSpec((1, ROUTER_LANES), const),
        ],
        out_specs=[pl.BlockSpec((tm, D_MODEL), row), pl.BlockSpec((tm, ROUTER_LANES), row)],
        out_shape=[jax.ShapeDtypeStruct((N_TOK, D_MODEL), BF16),
                   jax.ShapeDtypeStruct((N_TOK, ROUTER_LANES), F32)],
        compiler_params=_params(("parallel",)),
        name="moe_router",
    )(x, mod, norm_g, w_hi, w_lo, router_bias)


def _swiglu_mid(h, w_gate, w_up):
    a = jnp.dot(h, w_gate, preferred_element_type=F32)
    b = jnp.dot(h, w_up, preferred_element_type=F32)
    return (a * _sigmoid(a)) * b


def _experts_kernel(h_ref, gw_ref, wg_ref, wu_ref, wd_ref, sg_ref, su_ref, sd_ref, x_ref, mod_ref, fg_ref,
                    o_ref, *, final):
    j = pl.program_id(1)
    h = h_ref[...]

    @pl.when(j == 0)
    def _():
        mid = _swiglu_mid(h, sg_ref[...], su_ref[...]).astype(BF16)
        o_ref[...] = jnp.dot(mid, sd_ref[...], preferred_element_type=F32)

    gw = pltpu.roll(gw_ref[...], (ROUTER_LANES - j * EXPERT_CHUNK) & (ROUTER_LANES - 1), axis=1)
    mids =[(_swiglu_mid(h, wg_ref[e], wu_ref[e]) * gw[:, e:e + 1]).astype(BF16) for e in range(EXPERT_CHUNK)]
    mid = jnp.concatenate(mids, axis=1)
    wd = wd_ref[...].reshape(EXPERT_CHUNK * EXPERT_DIM, D_MODEL)
    o_ref[...] += jnp.dot(mid, wd, preferred_element_type=F32)

    @pl.when(j == pl.num_programs(1) - 1)
    def _():
        x = x_ref[...] + mod_ref[0, 5:6, :] * o_ref[...]
        if final:
            ms = jnp.mean(x * x, axis=-1, keepdims=True)
            x = x * lax.rsqrt(ms + EPS) * fg_ref[...]
        o_ref[...] = x


def _experts(h, gate_w, wg, wu, wd, sg, su, sd, x, mod, final_g, final):
    tm = 1024
    n_chunks = N_EXPERTS // EXPERT_CHUNK
    row = lambda i, j: (i, 0)
    const = lambda i, j: (0, 0)
    chunk = lambda i, j: (j, 0, 0)
    return pl.pallas_call(
        functools.partial(_experts_kernel, final=final),
        grid=(N_TOK // tm, n_chunks),
        in_specs=[
            pl.BlockSpec((tm, D_MODEL), row),
            pl.BlockSpec((tm, ROUTER_LANES), row),
            pl.BlockSpec((EXPERT_CHUNK, D_MODEL, EXPERT_DIM), chunk),
            pl.BlockSpec((EXPERT_CHUNK, D_MODEL, EXPERT_DIM), chunk),
            pl.BlockSpec((EXPERT_CHUNK, EXPERT_DIM, D_MODEL), chunk),
            pl.BlockSpec((D_MODEL, SHARED_DIM), const),
            pl.BlockSpec((D_MODEL, SHARED_DIM), const),
            pl.BlockSpec((SHARED_DIM, D_MODEL), const),
            pl.BlockSpec((tm, D_MODEL), row),
            pl.BlockSpec((1, N_MOD, D_MODEL), lambda i, j: (_mod_index(i, tm), 0, 0)),
            pl.BlockSpec((1, D_MODEL), const),
        ],
        out_specs=pl.BlockSpec((tm, D_MODEL), row),
        out_shape=jax.ShapeDtypeStruct((N_TOK, D_MODEL), F32),
        compiler_params=_params(("parallel", "arbitrary")),
        name="moe_experts",
    )(h, gate_w, wg, wu, wd, sg, su, sd, x, mod, final_g)


def kernel(x_prompt, x_sample, c, cache_k, cache_v, state_ssm, c_ctx, w_mod, b_mod, norm1_g, norm2_g, w_in, q_norm_g, k_norm_g, w_up_attn, pool_w, pool_scale, w_up_pool, ssm_a_re, ssm_a_im, ssm_log_dt, ssm_b_re, ssm_b_im, ssm_c_re, ssm_c_im, ssm_d, w_glu, b_glu, w_up_ssm, w_out, w_router, router_bias, w_gate, w_up, w_down, ws_gate, ws_up, ws_down, final_norm_g):
    x = jnp.concatenate([x_prompt.reshape(N_PROMPT, D_MODEL), x_sample.reshape(N_SAMPLE, D_MODEL)], axis=0)
    cond = jnp.concatenate([c_ctx[None, :], c, jnp.zeros((COND_ROWS - N_COND, D_MODEL), F32)], axis=0)
    mod_all = _ada_all(cond, w_mod, b_mod)

    cos_t, sin_t = _rope_tables(SEQ_TILE)
    head_avg = jnp.kron(jnp.eye(QK_WIDTH // HEAD_DIM, dtype=F32),
                        jnp.full((HEAD_DIM, HEAD_DIM), 1.0 / HEAD_DIM, F32)).astype(BF16)
    final_g = final_norm_g.reshape(1, D_MODEL)

    new_k, new_v, new_s = [], [], []
    for l in range(DEPTH):
        mod = mod_all[l]
        qk_gain = jnp.concatenate([jnp.tile(q_norm_g[l], N_HEADS), jnp.tile(k_norm_g[l], N_KV_HEADS)])[None, :]
        q, k, v, pool_u, ssm_u = _in_project(
            x, mod, norm1_g[l][None, :], w_in[l][:, :MIX_WIDTH].astype(BF16), head_avg, qk_gain, cos_t, sin_t)

        k_p = k[:N_PROMPT].reshape(BATCH, SEQ, N_KV_HEADS, HEAD_DIM)
        v_p = v[:N_PROMPT].reshape(BATCH, SEQ, N_KV_HEADS, HEAD_DIM)
        k_s = k[N_PROMPT:].reshape(DEC_BATCH, DEC_SEQ, N_KV_HEADS, HEAD_DIM)
        v_s = v[N_PROMPT:].reshape(DEC_BATCH, DEC_SEQ, N_KV_HEADS, HEAD_DIM)
        new_k.append(k_p)
        new_v.append(v_p)
        keys = jnp.concatenate([cache_k[:, l], k_s], axis=1)
        vals = jnp.concatenate([cache_v[:, l], v_s], axis=1)
        oa_p = _attention(q, k_p, v_p, BATCH, SEQ, 0)
        oa_s, wg, wu, wd = _attention(q, keys, vals, DEC_BATCH, DEC_SEQ, N_PROMPT, cast=(l, [w_gate, w_up, w_down]))

        pool_bd = jax.scipy.linalg.block_diag(*[pool_w[l, g] for g in range(len(POOL_WINDOWS))]).astype(BF16)
        p_scale = pool_scale[l][None, :]
        op_p = _pool(pool_u, pool_bd, p_scale, BATCH, SEQ, 0)
        op_s = _pool(pool_u, pool_bd, p_scale, DEC_BATCH, DEC_SEQ, N_PROMPT)

        a_rows, b_mat, c_mat = _ssm_matrices(ssm_a_re[l], ssm_a_im[l], ssm_log_dt[l], ssm_b_re[l], ssm_b_im[l],
                                             ssm_c_re[l], ssm_c_im[l])
        y_f, y_b, st = _ssm_scan(ssm_u, state_ssm[:, l], a_rows, b_mat, c_mat)
        new_s.append(st)

        w_r = jnp.pad(w_router[l], ((0, 0), (0, ROUTER_LANES - N_EXPERTS)))
        r_b = jnp.concatenate([router_bias[l], jnp.full((ROUTER_LANES - N_EXPERTS,), -jnp.inf, F32)])[None, :]
        x = _merge(
            x, mod, norm1_g[l][None, :], w_in[l][:, MIX_WIDTH:].astype(BF16), oa_p, oa_s, op_p, op_s, ssm_u,
            y_f, y_b, ssm_d[l].reshape(1, SSM_WIDTH), w_glu[l].astype(BF16), b_glu[l][None, :],
            w_up_attn[l].astype(BF16), w_up_pool[l].astype(BF16), w_up_ssm[l].astype(BF16),
            w_out[l].astype(BF16))
        h2, gate_w = _route(x, mod, norm2_g[l][None, :], w_r, r_b)
        x = _experts(h2, gate_w, wg, wu, wd, ws_gate[l].astype(BF16), ws_up[l].astype(BF16),
                     ws_down[l].astype(BF16), x, mod, final_g, final=(l == DEPTH - 1))

    y_prompt = x[:N_PROMPT].reshape(BATCH, SEQ, D_MODEL)
    y_sample = x[N_PROMPT:].reshape(DEC_BATCH, DEC_SEQ, D_MODEL)
    return (y_prompt, y_sample, jnp.stack(new_k, axis=1), jnp.stack(new_v, axis=1), jnp.stack(new_s, axis=1))
```

```python
import functools
import math

import jax
import jax.numpy as jnp
import numpy as np
from jax import lax
from jax.experimental import pallas as pl
from jax.experimental.pallas import tpu as pltpu

D_MODEL = 1024
BATCH = 32
SEQ = 256
DEPTH = 2
DEC_BATCH = 4
DEC_SEQ = 4096
PAST_LEN = 256
GRID_W = 64
EPS = 1e-6
N_MOD = 6
HEAD_DIM = 64
N_HEADS = 8
N_KV_HEADS = 2
ATTN_WIDTH = N_HEADS * HEAD_DIM
KV_WIDTH = N_KV_HEADS * HEAD_DIM
ROPE_BASE = 10000.0
ROPE_PAIRS_PER_AXIS = HEAD_DIM // 4
POOL_WINDOWS = (2, 4, 8, 16)
POOL_GROUP = 64
POOL_WIDTH = len(POOL_WINDOWS) * POOL_GROUP
SSM_H = 16
SSM_GROUPS = 16
SSM_WIDTH = SSM_H * SSM_GROUPS
SSM_STATE = 64
N_EXPERTS = 64
TOP_K = 8
EXPERT_DIM = 256
SHARED_DIM = 256
ROUTE_SCALE = 2.5

N_PROMPT = BATCH * SEQ
N_SAMPLE = DEC_BATCH * DEC_SEQ
N_TOK = N_PROMPT + N_SAMPLE
N_COND = 1 + DEC_BATCH
COND_ROWS = 8
QK_WIDTH = ATTN_WIDTH + KV_WIDTH
MIX_WIDTH = QK_WIDTH + KV_WIDTH + POOL_WIDTH + SSM_WIDTH
GATE_WIDTH = 3 * D_MODEL
SSM_COLS = SSM_GROUPS * SSM_STATE
SSM_LANES = 2 * SSM_COLS
SSM_SEQS = 4
SSM_CHUNK = 128
ROUTER_LANES = 128
EXPERT_CHUNK = 4
MOE_TILE = 1024
EXPERT_CAP = 160
EXPERT_OVF = 128

SEQ_TILE = SEQ
SSM_ROWS = 2 * SSM_SEQS
SSM_GROUPS_PROMPT = BATCH // SSM_SEQS
SCAN_T = SSM_GROUPS_PROMPT * SEQ + DEC_SEQ
SCAN_CHUNKS = SCAN_T // SSM_CHUNK
PROMPT_CHUNKS = SSM_GROUPS_PROMPT * SEQ // SSM_CHUNK
SSM_PITCH = SSM_CHUNK + 4
SSM_HALF = SSM_SEQS * SSM_PITCH

LANE = 128
SSM_SLABS = SSM_LANES // LANE
VMEM_LIMIT = 56 * 1024 * 1024

F32 = jnp.float32
BF16 = jnp.bfloat16
HIGHEST = lax.Precision.HIGHEST


def _sigmoid(x):
    return 1.0 / (1.0 + jnp.exp(-x))


def _params(dims, vmem=VMEM_LIMIT):
    return pltpu.CompilerParams(dimension_semantics=dims, vmem_limit_bytes=vmem)


def _mod_index(i, tm):
    p = N_PROMPT // tm
    t = DEC_SEQ // tm
    return jnp.where(i < p, 0, 1 + (i - p) // t)


def _scan_block(i):
    k = i - BATCH
    tiles = DEC_SEQ // SEQ_TILE
    rb = jnp.where(i < BATCH, i // SSM_SEQS, SSM_GROUPS_PROMPT + k % tiles)
    slot = jnp.where(i < BATCH, i % SSM_SEQS, k // tiles)
    return rb, slot


def _norm_mod(x, g, shift, scale):
    ms = jnp.mean(x * x, axis=-1, keepdims=True)
    return (x * lax.rsqrt(ms + EPS) * g) * (1.0 + scale) + shift


def _mod_kernel(cond_ref, w_ref, b_ref, o_ref):
    c = cond_ref[...]
    s = c * _sigmoid(c)
    o_ref[0] = jnp.dot(s, w_ref[0], preferred_element_type=F32, precision=HIGHEST) + b_ref[0]


def _ada_all(cond, w_mod, b_mod):
    tn = 1536
    width = N_MOD * D_MODEL
    out = pl.pallas_call(
        _mod_kernel,
        grid=(DEPTH, width // tn),
        in_specs=[
            pl.BlockSpec((COND_ROWS, D_MODEL), lambda l, j: (0, 0)),
            pl.BlockSpec((1, D_MODEL, tn), lambda l, j: (l, 0, j)),
            pl.BlockSpec((1, 1, tn), lambda l, j: (l, 0, j)),
        ],
        out_specs=pl.BlockSpec((1, COND_ROWS, tn), lambda l, j: (l, 0, j)),
        out_shape=jax.ShapeDtypeStruct((DEPTH, COND_ROWS, width), F32),
        compiler_params=_params(("parallel", "parallel")),
        name="ada_mod",
    )(cond, w_mod, b_mod.reshape(DEPTH, 1, width))
    return out.reshape(DEPTH, COND_ROWS, N_MOD, D_MODEL)


def _inproj_kernel(x_ref, mod_ref, g_ref, w_ref, bd_ref, qkg_ref, cos_ref, sin_ref,
                   q_ref, k_ref, v_ref, pu_ref, su_ref):
    h = _norm_mod(x_ref[...], g_ref[...], mod_ref[0, 0:1, :], mod_ref[0, 1:2, :])
    z = jnp.dot(h.astype(BF16), w_ref[...], preferred_element_type=F32)
    qk = z[:, :QK_WIDTH]
    qq = qk * qk
    hi = qq.astype(BF16)
    lo = (qq - hi.astype(F32)).astype(BF16)
    ms = (jnp.dot(hi, bd_ref[...], preferred_element_type=F32)
          + jnp.dot(lo, bd_ref[...], preferred_element_type=F32))
    qkn = qk * lax.rsqrt(ms + EPS) * qkg_ref[...]
    parts = []
    for c in range(QK_WIDTH // LANE):
        blk = qkn[:, c * LANE:(c + 1) * LANE]
        nxt = pltpu.roll(blk, LANE - 1, axis=1)
        prv = pltpu.roll(blk, 1, axis=1)
        lane = lax.broadcasted_iota(jnp.int32, blk.shape, 1)
        parts.append(jnp.where((lane & 1) == 0, nxt, prv))
    partner = jnp.concatenate(parts, axis=1)
    qkr = qkn * cos_ref[...] + partner * sin_ref[...]
    q_ref[...] = (qkr[:, :ATTN_WIDTH] * (HEAD_DIM ** -0.5 * math.log2(math.e))).astype(BF16)
    k_ref[...] = qkr[:, ATTN_WIDTH:QK_WIDTH]
    v_ref[...] = z[:, QK_WIDTH:QK_WIDTH + KV_WIDTH]
    pu_ref[...] = z[:, QK_WIDTH + KV_WIDTH:QK_WIDTH + KV_WIDTH + POOL_WIDTH]
    su_ref[...] = z[:, QK_WIDTH + KV_WIDTH + POOL_WIDTH:MIX_WIDTH]


def _in_project(x, mod, norm_g, w_mix, bd, qk_gain, cos_t, sin_t):
    tm = SEQ_TILE
    p_tiles = N_PROMPT // tm
    s_tiles = DEC_SEQ // tm

    def rope_idx(i):
        return (jnp.where(i < p_tiles, 0, 1 + (i - p_tiles) % s_tiles), 0)

    row = lambda i: (i, 0)
    const = lambda i: (0, 0)
    return pl.pallas_call(
        _inproj_kernel,
        grid=(N_TOK // tm,),
        in_specs=[
            pl.BlockSpec((tm, D_MODEL), row),
            pl.BlockSpec((1, N_MOD, D_MODEL), lambda i: (_mod_index(i, tm), 0, 0)),
            pl.BlockSpec((1, D_MODEL), const),
            pl.BlockSpec((D_MODEL, MIX_WIDTH), const),
            pl.BlockSpec((QK_WIDTH, QK_WIDTH), const),
            pl.BlockSpec((1, QK_WIDTH), const),
            pl.BlockSpec((tm, QK_WIDTH), rope_idx),
            pl.BlockSpec((tm, QK_WIDTH), rope_idx),
        ],
        out_specs=[
            pl.BlockSpec((tm, ATTN_WIDTH), row),
            pl.BlockSpec((tm, KV_WIDTH), row),
            pl.BlockSpec((tm, KV_WIDTH), row),
            pl.BlockSpec((tm, POOL_WIDTH), row),
            pl.BlockSpec((tm, SSM_WIDTH), _scan_block),
        ],
        out_shape=[
            jax.ShapeDtypeStruct((N_TOK, ATTN_WIDTH), BF16),
            jax.ShapeDtypeStruct((N_TOK, KV_WIDTH), F32),
            jax.ShapeDtypeStruct((N_TOK, KV_WIDTH), F32),
            jax.ShapeDtypeStruct((N_TOK, POOL_WIDTH), F32),
            jax.ShapeDtypeStruct((SCAN_T, SSM_SEQS * SSM_WIDTH), F32),
        ],
        compiler_params=_params(("parallel",)),
        name="in_project",
    )(x, mod, norm_g, w_mix, bd, qk_gain, cos_t, sin_t)


def _rope_tables(tm):
    rows = DEC_SEQ // GRID_W
    row = jnp.repeat(jnp.arange(rows, dtype=F32), GRID_W)
    col = jnp.tile(jnp.arange(GRID_W, dtype=F32), rows)
    inv_freq = ROPE_BASE ** (-jnp.arange(ROPE_PAIRS_PER_AXIS, dtype=F32) / ROPE_PAIRS_PER_AXIS)
    ang = jnp.concatenate([row[:, None] * inv_freq, col[:, None] * inv_freq], axis=-1)
    cos = jnp.repeat(jnp.cos(ang), 2, axis=-1)
    sin = jnp.repeat(jnp.sin(ang), 2, axis=-1) * jnp.tile(jnp.array([-1.0, 1.0], F32), HEAD_DIM // 2)
    n_rep = QK_WIDTH // HEAD_DIM
    cos = jnp.concatenate([jnp.ones((tm, HEAD_DIM), F32), cos], axis=0)
    sin = jnp.concatenate([jnp.zeros((tm, HEAD_DIM), F32), sin], axis=0)
    return jnp.tile(cos, (1, n_rep)), jnp.tile(sin, (1, n_rep))


def _attn_kernel(q_ref, kt_ref, ve_ref, vo_ref, *rest):
    n_cast = (len(rest) - 1) // 2
    o_ref = rest[n_cast]
    for src, dst in zip(rest[:n_cast], rest[n_cast + 1:]):
        dst[...] = src[...].astype(dst.dtype)
    kt = kt_ref[0]
    lane = lax.broadcasted_iota(jnp.int32, (q_ref.shape[0], LANE), 1)
    first = lane < HEAD_DIM
    slabs = []
    for pair in range(q_ref.shape[1] // LANE):
        q = q_ref[:, pair * LANE:(pair + 1) * LANE]
        halves = []
        for keep, v_ref in ((first, ve_ref), (lane >= HEAD_DIM, vo_ref)):
            qh = jnp.where(keep, q, jnp.zeros_like(q))
            s = jnp.dot(qh, kt, preferred_element_type=F32)
            m = jnp.max(s, axis=-1, keepdims=True)
            p = jnp.exp2(s - m).astype(BF16)
            a = jnp.dot(p, v_ref[0], preferred_element_type=F32)
            halves.append(a / pltpu.roll(a, HEAD_DIM, axis=1))
        slabs.append(jnp.where(first, halves[0], halves[1]))
    o_ref[...] = jnp.concatenate(slabs, axis=1).astype(o_ref.dtype)


def _attention(q, keys, vals, n_batch, n_q, row0, cast=None):
    n_keys = keys.shape[1]
    tq = 256
    kt = jnp.transpose(keys, (0, 2, 3, 1)).astype(BF16)
    kt = jnp.concatenate([kt, kt], axis=2).reshape(n_batch * N_KV_HEADS, 2 * HEAD_DIM, n_keys)
    vv = jnp.transpose(vals, (0, 2, 1, 3)).astype(BF16)
    ones = jnp.ones_like(vv)
    v_e = jnp.concatenate([vv, ones], axis=3).reshape(n_batch * N_KV_HEADS, n_keys, 2 * HEAD_DIM)
    v_o = jnp.concatenate([ones, vv], axis=3).reshape(n_batch * N_KV_HEADS, n_keys, 2 * HEAD_DIM)
    q_tiles = n_q // tq
    base = row0 // tq
    group_w = ATTN_WIDTH // N_KV_HEADS
    kv_idx = lambda b, g, i: (b * N_KV_HEADS + g, 0, 0)
    in_specs = [
        pl.BlockSpec((tq, group_w), lambda b, g, i: (base + b * q_tiles + i, g)),
        pl.BlockSpec((1, LANE, n_keys), kv_idx),
        pl.BlockSpec((1, n_keys, LANE), kv_idx),
        pl.BlockSpec((1, n_keys, LANE), kv_idx),
    ]
    out_specs = [pl.BlockSpec((tq, group_w), lambda b, g, i: (b * q_tiles + i, g))]
    out_shape = [jax.ShapeDtypeStruct((n_batch * n_q, ATTN_WIDTH), BF16)]
    args = [q, kt, v_e, v_o]
    if cast is not None:
        layer, tensors = cast
        steps = n_batch * N_KV_HEADS * q_tiles
        step = lambda b, g, i: (b * N_KV_HEADS + g) * q_tiles + i
        for w in tensors:
            _, n_e, rows, cols = w.shape
            part = n_e * rows // steps
            assert part * steps == n_e * rows and part % 16 == 0 and rows % part == 0
            in_specs.append(pl.BlockSpec((1, part, cols), lambda b, g, i: (layer * steps + step(b, g, i), 0, 0)))
            out_specs.append(pl.BlockSpec((1, part, cols), lambda b, g, i: (step(b, g, i), 0, 0)))
            out_shape.append(jax.ShapeDtypeStruct((steps, part, cols), BF16))
            args.append(w.reshape(DEPTH * steps, part, cols))
    outs = pl.pallas_call(
        _attn_kernel,
        grid=(n_batch, N_KV_HEADS, q_tiles),
        in_specs=in_specs,
        out_specs=out_specs,
        out_shape=out_shape,
        compiler_params=_params(("parallel", "parallel", "parallel")),
        name="attention",
    )(*args)
    if cast is None:
        return outs[0]
    return [outs[0]] + [o.reshape(w.shape[1:]) for o, w in zip(outs[1:], cast[1])]


def _pool_kernel(u_ref, w_ref, sc_ref, o_ref, pad_ref, *, n_seq):
    halo = 8
    u = u_ref[...]
    zeros = jnp.zeros((halo, POOL_WIDTH), F32)
    pad_ref[0:halo, :] = zeros
    pad_ref[halo + n_seq:2 * halo + n_seq, :] = zeros
    pad_ref[halo:halo + n_seq, :] = u

    def sh(j):
        return pad_ref[halo + j:halo + j + n_seq, :]

    t2 = sh(-1) + u
    t4 = t2 + sh(-2) + sh(1)
    t8 = t4 + sh(-4) + sh(-3) + sh(2) + sh(3)
    t16 = t8 + sh(-8) + sh(-7) + sh(-6) + sh(-5) + sh(4) + sh(5) + sh(6) + sh(7)
    grp = lax.broadcasted_iota(jnp.int32, u.shape, 1) >> 6
    t = lax.broadcasted_iota(jnp.int32, u.shape, 0)
    tot = jnp.where(grp == 0, t2, jnp.where(grp == 1, t4, jnp.where(grp == 2, t8, t16)))
    half = jnp.where(grp == 0, 1, jnp.where(grp == 1, 2, jnp.where(grp == 2, 4, 8)))
    lo = jnp.maximum(t - half, 0)
    hi = jnp.minimum(t + half, n_seq)
    pooled = tot / (hi - lo).astype(F32) - u
    mixed = jnp.dot(pooled.astype(BF16), w_ref[...], preferred_element_type=F32)
    o_ref[...] = (mixed * sc_ref[...]).astype(o_ref.dtype)


def _pool(u, w_bd, scale, n_batch, n_seq, row0):
    base = row0 // n_seq
    return pl.pallas_call(
        functools.partial(_pool_kernel, n_seq=n_seq),
        grid=(n_batch,),
        in_specs=[
            pl.BlockSpec((n_seq, POOL_WIDTH), lambda b: (base + b, 0)),
            pl.BlockSpec((POOL_WIDTH, POOL_WIDTH), lambda b: (0, 0)),
            pl.BlockSpec((1, POOL_WIDTH), lambda b: (0, 0)),
        ],
        out_specs=pl.BlockSpec((n_seq, POOL_WIDTH), lambda b: (b, 0)),
        out_shape=jax.ShapeDtypeStruct((n_batch * n_seq, POOL_WIDTH), BF16),
        scratch_shapes=[pltpu.VMEM((n_seq + 16, POOL_WIDTH), F32)],
        compiler_params=_params(("parallel",)),
        name="pool_mixer",
    )(u, w_bd, scale)


def _scan_chunk_first(i):
    per_seq = SEQ // SSM_CHUNK
    return jnp.where(i < PROMPT_CHUNKS, i % per_seq == 0, i == PROMPT_CHUNKS)


def _scan_bwd_chunk(i):
    per_seq = SEQ // SSM_CHUNK
    return jnp.where(i < PROMPT_CHUNKS, (i // per_seq) * per_seq + (per_seq - 1 - i % per_seq),
                     PROMPT_CHUNKS + SCAN_CHUNKS - 1 - i)


def _scan_group(i):
    return jnp.where(i < PROMPT_CHUNKS, i // (SEQ // SSM_CHUNK), SSM_GROUPS_PROMPT)


def _split3(x):
    hi = x.astype(BF16)
    r1 = x - hi.astype(F32)
    mid = r1.astype(BF16)
    lo = (r1 - mid.astype(F32)).astype(BF16)
    return hi, mid, lo


def _ssm_kernel(uf_ref, ub_ref, a_ref, bf_ref, bb_ref, cf_ref, cb_ref, jin_ref, jout_ref, h0_ref,
                yf_ref, yb_ref, hfin_ref, bu_ref, st_ref, lhs_ref, yy_ref):
    @pl.when(_scan_chunk_first(pl.program_id(0)))
    def _():
        st_ref[...] = h0_ref[0]

    ub_rev = jnp.dot(jin_ref[...], ub_ref[...].astype(BF16), preferred_element_type=F32)
    gap = jnp.zeros((SSM_PITCH - SSM_CHUNK, SSM_WIDTH), F32)
    for s in range(SSM_SEQS):
        cols = slice(s * SSM_WIDTH, (s + 1) * SSM_WIDTH)
        lhs_ref[s * SSM_PITCH:s * SSM_PITCH + SSM_CHUNK, :] = uf_ref[:, cols]
        lhs_ref[s * SSM_PITCH + SSM_CHUNK:(s + 1) * SSM_PITCH, :] = gap
        lhs_ref[SSM_HALF + s * SSM_PITCH:SSM_HALF + s * SSM_PITCH + SSM_CHUNK, :] = ub_rev[:, cols]
        lhs_ref[SSM_HALF + s * SSM_PITCH + SSM_CHUNK:SSM_HALF + (s + 1) * SSM_PITCH, :] = gap
    bu_f = jnp.dot(lhs_ref[:SSM_HALF, :].astype(BF16), bf_ref[...], preferred_element_type=F32)
    bu_b = jnp.dot(lhs_ref[SSM_HALF:, :].astype(BF16), bb_ref[...], preferred_element_type=F32)
    for k in range(SSM_SLABS):
        bu_ref[k, :SSM_HALF, :] = bu_f[:, k * LANE:(k + 1) * LANE]
        bu_ref[k, SSM_HALF:, :] = bu_b[:, k * LANE:(k + 1) * LANE]

    half = SSM_SLABS // 2
    a_re = [a_ref[:, k * LANE:(k + 1) * LANE] for k in range(half)]
    a_im = [a_ref[:, SSM_COLS + k * LANE:SSM_COLS + (k + 1) * LANE] for k in range(half)]

    def step(t, carry):
        rows = pl.ds(t, SSM_ROWS, stride=SSM_PITCH)
        new_re, new_im = [], []
        for k in range(half):
            h_re, h_im = carry[k], carry[half + k]
            n_re = a_re[k] * h_re - a_im[k] * h_im + bu_ref[k, rows, :]
            n_im = a_re[k] * h_im + a_im[k] * h_re + bu_ref[half + k, rows, :]
            bu_ref[k, rows, :] = n_re
            bu_ref[half + k, rows, :] = n_im
            new_re.append(n_re)
            new_im.append(n_im)
        return tuple(new_re + new_im)

    init = tuple(st_ref[:, k * LANE:(k + 1) * LANE] for k in range(SSM_SLABS))
    fin = lax.fori_loop(0, SSM_CHUNK, step, init)
    for k in range(SSM_SLABS):
        st_ref[:, k * LANE:(k + 1) * LANE] = fin[k]
    hfin_ref[0] = st_ref[...]

    h_f = jnp.concatenate([bu_ref[k, :SSM_HALF, :] for k in range(SSM_SLABS)], axis=1).astype(BF16)
    h_b = jnp.concatenate([bu_ref[k, SSM_HALF:, :] for k in range(SSM_SLABS)], axis=1).astype(BF16)
    yy_ref[...] = jnp.dot(h_f, cf_ref[...], preferred_element_type=F32)
    for s in range(SSM_SEQS):
        yf_ref[:, s * SSM_WIDTH:(s + 1) * SSM_WIDTH] = yy_ref[s * SSM_PITCH:s * SSM_PITCH + SSM_CHUNK, :]
    y_b = jnp.dot(h_b, cb_ref[...], preferred_element_type=F32)
    y_nat = sum(jnp.dot(jout_ref[...], piece, preferred_element_type=F32) for piece in _split3(y_b))
    for s in range(SSM_SEQS):
        yb_ref[:, s * SSM_WIDTH:(s + 1) * SSM_WIDTH] = y_nat[s * SSM_CHUNK:(s + 1) * SSM_CHUNK, :]


def _ssm_scan(su8, h0_sample, a_rows, b_mat, c_mat):
    rows = SSM_ROWS
    width = SSM_SEQS * SSM_WIDTH
    n_groups = SSM_GROUPS_PROMPT + 1
    hs = h0_sample.astype(F32).reshape(SSM_SEQS, 2, SSM_COLS, 2).transpose(1, 0, 3, 2).reshape(1, rows, SSM_LANES)
    hh = jnp.concatenate([jnp.zeros((SSM_GROUPS_PROMPT, rows, SSM_LANES), F32), hs], axis=0)
    j_in = jnp.asarray(np.eye(SSM_CHUNK, dtype=np.float32)[::-1], dtype=BF16)
    sel = np.zeros((SSM_SEQS * SSM_CHUNK, SSM_HALF), np.float32)
    for s in range(SSM_SEQS):
        for t in range(SSM_CHUNK):
            sel[s * SSM_CHUNK + t, s * SSM_PITCH + SSM_CHUNK - 1 - t] = 1.0
    j_out = jnp.asarray(sel, dtype=BF16)
    fwd = lambda i: (i, 0)
    bwd = lambda i: (_scan_bwd_chunk(i), 0)
    grp = lambda i: (_scan_group(i), 0, 0)
    const = lambda i: (0, 0)
    y_f, y_b, hfin = pl.pallas_call(
        _ssm_kernel,
        grid=(SCAN_CHUNKS,),
        in_specs=[
            pl.BlockSpec((SSM_CHUNK, width), fwd),
            pl.BlockSpec((SSM_CHUNK, width), bwd),
            pl.BlockSpec((rows, SSM_LANES), const),
            pl.BlockSpec((SSM_WIDTH, SSM_LANES), const),
            pl.BlockSpec((SSM_WIDTH, SSM_LANES), const),
            pl.BlockSpec((SSM_LANES, SSM_WIDTH), const),
            pl.BlockSpec((SSM_LANES, SSM_WIDTH), const),
            pl.BlockSpec((SSM_CHUNK, SSM_CHUNK), const),
            pl.BlockSpec((SSM_SEQS * SSM_CHUNK, SSM_HALF), const),
            pl.BlockSpec((1, rows, SSM_LANES), grp),
        ],
        out_specs=[
            pl.BlockSpec((SSM_CHUNK, width), fwd),
            pl.BlockSpec((SSM_CHUNK, width), bwd),
            pl.BlockSpec((1, rows, SSM_LANES), grp),
        ],
        out_shape=[
            jax.ShapeDtypeStruct((SCAN_T, width), F32),
            jax.ShapeDtypeStruct((SCAN_T, width), F32),
            jax.ShapeDtypeStruct((n_groups, rows, SSM_LANES), F32),
        ],
        scratch_shapes=[pltpu.VMEM((SSM_SLABS, 2 * SSM_HALF, LANE), F32), pltpu.VMEM((rows, SSM_LANES), F32),
                        pltpu.VMEM((2 * SSM_HALF, SSM_WIDTH), F32), pltpu.VMEM((SSM_HALF, SSM_WIDTH), F32)],
        compiler_params=_params(("arbitrary",)),
        name="ssm_scan",
    )(su8, su8, a_rows, b_mat[:SSM_WIDTH], b_mat[SSM_WIDTH:], c_mat[:, :SSM_WIDTH], c_mat[:, SSM_WIDTH:],
      j_in, j_out, hh)
    fin = hfin[:SSM_GROUPS_PROMPT].reshape(SSM_GROUPS_PROMPT, 2, SSM_SEQS, 2, SSM_GROUPS, SSM_STATE)
    fin = fin.transpose(0, 2, 1, 4, 5, 3).reshape(BATCH, 2, SSM_GROUPS, SSM_STATE, 2)
    return y_f, y_b, fin


def _ssm_matrices(a_re, a_im, log_dt, b_re, b_im, c_re, c_im):
    lam = lax.complex(a_re.astype(F32), a_im.astype(F32))
    dt = jnp.exp(log_dt.astype(F32))[..., None]
    a_bar = jnp.exp(lam * dt)
    b_bar = ((a_bar - 1.0) / lam)[..., None] * lax.complex(b_re.astype(F32), b_im.astype(F32))
    a_dir = jnp.concatenate([jnp.real(a_bar).reshape(2, SSM_COLS), jnp.imag(a_bar).reshape(2, SSM_COLS)], axis=-1)
    a_rows = jnp.repeat(a_dir, SSM_SEQS, axis=0)
    eye = jnp.eye(SSM_GROUPS, dtype=F32)
    bt = jnp.transpose(b_bar, (0, 1, 3, 2))
    b_real = jnp.einsum('dghp,ge->dghep', jnp.real(bt), eye).reshape(2 * SSM_WIDTH, SSM_COLS)
    b_imag = jnp.einsum('dghp,ge->dghep', jnp.imag(bt), eye).reshape(2 * SSM_WIDTH, SSM_COLS)
    b_mat = jnp.concatenate([b_real, b_imag], axis=-1).astype(BF16)
    cr = jnp.transpose(c_re.astype(F32), (0, 1, 3, 2))
    ci = jnp.transpose(c_im.astype(F32), (0, 1, 3, 2))
    c_real = jnp.einsum('dgph,ge->gpdeh', cr, eye).reshape(SSM_COLS, 2 * SSM_WIDTH)
    c_imag = jnp.einsum('dgph,ge->gpdeh', -ci, eye).reshape(SSM_COLS, 2 * SSM_WIDTH)
    c_mat = jnp.concatenate([c_real, c_imag], axis=0).astype(BF16)
    return a_rows, b_mat, c_mat


def _merge_kernel(x_ref, mod_ref, g1_ref, wg_ref, oap_ref, oas_ref, opp_ref, ops_ref, su_ref, yf_ref, yb_ref,
                  d_ref, wglu_ref, bglu_ref, wua_ref, wup_ref, wus_ref, wo_ref, o_ref):
    x = x_ref[...]
    h = _norm_mod(x, g1_ref[...], mod_ref[0, 0:1, :], mod_ref[0, 1:2, :])
    gates = _sigmoid(jnp.dot(h.astype(BF16), wg_ref[...], preferred_element_type=F32))
    y = d_ref[...] * su_ref[...] + yf_ref[...] + yb_ref[...]
    y = 0.5 * y * (1.0 + jnp.tanh(math.sqrt(2.0 / math.pi) * (y + 0.044715 * (y * y * y))))
    glu = jnp.dot(y.astype(BF16), wglu_ref[...], preferred_element_type=F32) + bglu_ref[...]
    o_ssm = y * _sigmoid(glu)
    is_prompt = pl.program_id(0) < N_PROMPT // SEQ_TILE
    o_attn = jnp.where(is_prompt, oap_ref[...], oas_ref[...])
    o_pool = jnp.where(is_prompt, opp_ref[...], ops_ref[...])
    m = (gates[:, :D_MODEL] * jnp.dot(o_attn, wua_ref[...], preferred_element_type=F32)
         + gates[:, D_MODEL:2 * D_MODEL] * jnp.dot(o_pool, wup_ref[...], preferred_element_type=F32)
         + gates[:, 2 * D_MODEL:] * jnp.dot(o_ssm.astype(BF16), wus_ref[...], preferred_element_type=F32))
    o_ref[...] = x + mod_ref[0, 2:3, :] * jnp.dot(m.astype(BF16), wo_ref[...], preferred_element_type=F32)


def _merge(x, mod, norm1_g, w_gates, oa_p, oa_s, op_p, op_s, ssm_u, y_f, y_b, d_skip, w_glu, b_glu,
           w_up_attn, w_up_pool, w_up_ssm, w_out):
    tm = SEQ_TILE
    p_tiles = N_PROMPT // tm
    row = lambda i: (i, 0)
    row_p = lambda i: (jnp.minimum(i, p_tiles - 1), 0)
    row_s = lambda i: (jnp.maximum(i - p_tiles, 0), 0)
    const = lambda i: (0, 0)
    return pl.pallas_call(
        _merge_kernel,
        grid=(N_TOK // tm,),
        in_specs=[
            pl.BlockSpec((tm, D_MODEL), row),
            pl.BlockSpec((1, N_MOD, D_MODEL), lambda i: (_mod_index(i, tm), 0, 0)),
            pl.BlockSpec((1, D_MODEL), const),
            pl.BlockSpec((D_MODEL, GATE_WIDTH), const),
            pl.BlockSpec((tm, ATTN_WIDTH), row_p),
            pl.BlockSpec((tm, ATTN_WIDTH), row_s),
            pl.BlockSpec((tm, POOL_WIDTH), row_p),
            pl.BlockSpec((tm, POOL_WIDTH), row_s),
            pl.BlockSpec((tm, SSM_WIDTH), _scan_block),
            pl.BlockSpec((tm, SSM_WIDTH), _scan_block),
            pl.BlockSpec((tm, SSM_WIDTH), _scan_block),
            pl.BlockSpec((1, SSM_WIDTH), const),
            pl.BlockSpec((SSM_WIDTH, SSM_WIDTH), const),
            pl.BlockSpec((1, SSM_WIDTH), const),
            pl.BlockSpec((ATTN_WIDTH, D_MODEL), const),
            pl.BlockSpec((POOL_WIDTH, D_MODEL), const),
            pl.BlockSpec((SSM_WIDTH, D_MODEL), const),
            pl.BlockSpec((D_MODEL, D_MODEL), const),
        ],
        out_specs=pl.BlockSpec((tm, D_MODEL), row),
        out_shape=jax.ShapeDtypeStruct((N_TOK, D_MODEL), F32),
        compiler_params=_params(("parallel",)),
        name="merge_branches",
    )(x, mod, norm1_g, w_gates, oa_p, oa_s, op_p, op_s, ssm_u, y_f, y_b, d_skip, w_glu, b_glu,
      w_up_attn, w_up_pool, w_up_ssm, w_out)


def _router_kernel(x_ref, mod_ref, g_ref, wrh_ref, wrl_ref, rb_ref, tri_ref, h_ref, gw_ref, pos_ref, post_ref):
    h2 = _norm_mod(x_ref[...], g_ref[...], mod_ref[0, 3:4, :], mod_ref[0, 4:5, :])
    h_hi = h2.astype(BF16)
    h_ref[...] = h_hi
    h_lo = (h2 - h_hi.astype(F32)).astype(BF16)
    logits = (jnp.dot(h_hi, wrh_ref[...], preferred_element_type=F32)
              + jnp.dot(h_lo, wrh_ref[...], preferred_element_type=F32)
              + jnp.dot(h_hi, wrl_ref[...], preferred_element_type=F32))
    scores = _sigmoid(logits)
    sc_t = jnp.transpose(scores)[:N_EXPERTS, :]
    work = jnp.transpose(scores + rb_ref[...])[:N_EXPERTS, :]
    idx = lax.broadcasted_iota(jnp.int32, work.shape, 0).astype(F32)
    w_t = jnp.zeros_like(sc_t)
    sel_t = jnp.zeros_like(sc_t)
    for _ in range(TOP_K):
        mx = jnp.max(work, axis=0, keepdims=True)
        first = jnp.min(jnp.where(work == mx, idx, float(N_EXPERTS)), axis=0, keepdims=True)
        pick = idx == first
        w_t = jnp.where(pick, sc_t, w_t)
        sel_t = jnp.where(pick, 1.0, sel_t)
        work = jnp.where(pick, -jnp.inf, work)
    w_t = w_t / jnp.sum(w_t, axis=0, keepdims=True) * ROUTE_SCALE
    rank_t = jnp.dot(sel_t.astype(BF16), tri_ref[...], preferred_element_type=F32)
    pos_t = jnp.where(sel_t > 0.0, rank_t, -1.0)
    post_ref[...] = pos_t
    n_pad = ROUTER_LANES - N_EXPERTS
    gw_ref[...] = jnp.transpose(jnp.concatenate([w_t, jnp.zeros((n_pad, w_t.shape[1]), F32)], axis=0))
    pos_ref[...] = jnp.transpose(jnp.concatenate([pos_t, jnp.full((n_pad, w_t.shape[1]), -1.0, F32)], axis=0))


def _route(x, mod, norm_g, w_router, router_bias):
    tm = MOE_TILE
    row = lambda i: (i, 0)
    const = lambda i: (0, 0)
    w_hi = w_router.astype(BF16)
    w_lo = (w_router - w_hi.astype(F32)).astype(BF16)
    tri = jnp.asarray(np.triu(np.ones((tm, tm), np.float32), k=1), dtype=BF16)
    return pl.pallas_call(
        _router_kernel,
        grid=(N_TOK // tm,),
        in_specs=[
            pl.BlockSpec((tm, D_MODEL), row),
            pl.BlockSpec((1, N_MOD, D_MODEL), lambda i: (_mod_index(i, tm), 0, 0)),
            pl.BlockSpec((1, D_MODEL), const),
            pl.BlockSpec((D_MODEL, ROUTER_LANES), const),
            pl.BlockSpec((D_MODEL, ROUTER_LANES), const),
            pl.BlockSpec((1, ROUTER_LANES), const),
            pl.BlockSpec((tm, tm), const),
        ],
        out_specs=[pl.BlockSpec((tm, D_MODEL), row), pl.BlockSpec((tm, ROUTER_LANES), row),
                   pl.BlockSpec((tm, ROUTER_LANES), row), pl.BlockSpec((N_EXPERTS, tm), lambda i: (0, i))],
        out_shape=[jax.ShapeDtypeStruct((N_TOK, D_MODEL), BF16),
                   jax.ShapeDtypeStruct((N_TOK, ROUTER_LANES), F32),
                   jax.ShapeDtypeStruct((N_TOK, ROUTER_LANES), F32),
                   jax.ShapeDtypeStruct((N_EXPERTS, N_TOK), F32)],
        compiler_params=_params(("parallel",)),
        name="moe_router",
    )(x, mod, norm_g, w_hi, w_lo, router_bias, tri)


def _swiglu_mid(h, w_gate, w_up):
    a = jnp.dot(h, w_gate, preferred_element_type=F32)
    b = jnp.dot(h, w_up, preferred_element_type=F32)
    return (a * _sigmoid(a)) * b


def _experts_kernel(h_ref, gw_ref, pos_ref, post_ref, wg_ref, wu_ref, wd_ref, sg_ref, su_ref, sd_ref,
                    x_ref, mod_ref, fg_ref, o_ref, *, final):
    j = pl.program_id(1)
    tm = h_ref.shape[0]
    h = h_ref[...]

    @pl.when(j == 0)
    def _():
        mid = _swiglu_mid(h, sg_ref[...], su_ref[...]).astype(BF16)
        o_ref[...] = jnp.dot(mid, sd_ref[...], preferred_element_type=F32)

    shift = (ROUTER_LANES - j * EXPERT_CHUNK) & (ROUTER_LANES - 1)
    gw = pltpu.roll(gw_ref[...], shift, axis=1)
    pos = pltpu.roll(pos_ref[...], shift, axis=1)
    slot_rows = [post_ref[pl.ds(j * EXPERT_CHUNK + e, 1), :] for e in range(EXPERT_CHUNK)]

    def run_expert(e, rows):
        mid = _swiglu_mid(rows, wg_ref[e], wu_ref[e]).astype(BF16)
        return jnp.dot(mid, wd_ref[e], preferred_element_type=F32).astype(BF16)

    cap_iota = lax.broadcasted_iota(jnp.int32, (EXPERT_CAP, tm), 0).astype(F32)
    gather = jnp.concatenate([jnp.where(slot_rows[e] == cap_iota, 1.0, 0.0).astype(BF16)
                              for e in range(EXPERT_CHUNK)], axis=0)
    rows = jnp.dot(gather, h, preferred_element_type=F32).astype(BF16)
    outs = jnp.concatenate([run_expert(e, rows[e * EXPERT_CAP:(e + 1) * EXPERT_CAP, :])
                            for e in range(EXPERT_CHUNK)], axis=0)
    col = lax.broadcasted_iota(jnp.int32, (tm, EXPERT_CHUNK * EXPERT_CAP), 1)
    col_pos, col_w, col_slot = pos[:, 0:1], gw[:, 0:1], col
    for e in range(1, EXPERT_CHUNK):
        later = col >= e * EXPERT_CAP
        col_pos = jnp.where(later, pos[:, e:e + 1], col_pos)
        col_w = jnp.where(later, gw[:, e:e + 1], col_w)
        col_slot = jnp.where(later, col - e * EXPERT_CAP, col_slot)
    scatter = jnp.where(col_pos == col_slot.astype(F32), col_w, 0.0).astype(BF16)
    o_ref[...] += jnp.dot(scatter, outs, preferred_element_type=F32)

    ovf_iota = lax.broadcasted_iota(jnp.int32, (EXPERT_OVF, tm), 0).astype(F32)
    ovf_col = lax.broadcasted_iota(jnp.int32, (tm, EXPERT_OVF), 1).astype(F32)
    for e in range(EXPERT_CHUNK):
        count = jnp.max(slot_rows[e]).astype(jnp.int32) + 1
        n_ovf = jnp.maximum(count - EXPERT_CAP + EXPERT_OVF - 1, 0) // EXPERT_OVF

        def ovf_block(b, carry, e=e):
            base = (EXPERT_CAP + b * EXPERT_OVF).astype(F32)
            sel = jnp.where(slot_rows[e] == ovf_iota + base, 1.0, 0.0).astype(BF16)
            out = run_expert(e, jnp.dot(sel, h, preferred_element_type=F32).astype(BF16))
            back = jnp.where(pos[:, e:e + 1] == ovf_col + base, gw[:, e:e + 1], 0.0).astype(BF16)
            o_ref[...] += jnp.dot(back, out, preferred_element_type=F32)
            return carry

        lax.fori_loop(0, n_ovf, ovf_block, 0)

    @pl.when(j == pl.num_programs(1) - 1)
    def _():
        x = x_ref[...] + mod_ref[0, 5:6, :] * o_ref[...]
        if final:
            ms = jnp.mean(x * x, axis=-1, keepdims=True)
            x = x * lax.rsqrt(ms + EPS) * fg_ref[...]
        o_ref[...] = x


def _experts(h, gate_w, slot, slot_t, wg, wu, wd, sg, su, sd, x, mod, final_g, final):
    tm = MOE_TILE
    n_chunks = N_EXPERTS // EXPERT_CHUNK
    row = lambda i, j: (i, 0)
    const = lambda i, j: (0, 0)
    chunk = lambda i, j: (j, 0, 0)
    return pl.pallas_call(
        functools.partial(_experts_kernel, final=final),
        grid=(N_TOK // tm, n_chunks),
        in_specs=[
            pl.BlockSpec((tm, D_MODEL), row),
            pl.BlockSpec((tm, ROUTER_LANES), row),
            pl.BlockSpec((tm, ROUTER_LANES), row),
            pl.BlockSpec((N_EXPERTS, tm), lambda i, j: (0, i)),
            pl.BlockSpec((EXPERT_CHUNK, D_MODEL, EXPERT_DIM), chunk),
            pl.BlockSpec((EXPERT_CHUNK, D_MODEL, EXPERT_DIM), chunk),
            pl.BlockSpec((EXPERT_CHUNK, EXPERT_DIM, D_MODEL), chunk),
            pl.BlockSpec((D_MODEL, SHARED_DIM), const),
            pl.BlockSpec((D_MODEL, SHARED_DIM), const),
            pl.BlockSpec((SHARED_DIM, D_MODEL), const),
            pl.BlockSpec((tm, D_MODEL), row),
            pl.BlockSpec((1, N_MOD, D_MODEL), lambda i, j: (_mod_index(i, tm), 0, 0)),
            pl.BlockSpec((1, D_MODEL), const),
        ],
        out_specs=pl.BlockSpec((tm, D_MODEL), row),
        out_shape=jax.ShapeDtypeStruct((N_TOK, D_MODEL), F32),
        compiler_params=_params(("parallel", "arbitrary")),
        name="moe_experts",
    )(h, gate_w, slot, slot_t, wg, wu, wd, sg, su, sd, x, mod, final_g)


def kernel(x_prompt, x_sample, c, cache_k, cache_v, state_ssm, c_ctx, w_mod, b_mod, norm1_g, norm2_g, w_in, q_norm_g, k_norm_g, w_up_attn, pool_w, pool_scale, w_up_pool, ssm_a_re, ssm_a_im, ssm_log_dt, ssm_b_re, ssm_b_im, ssm_c_re, ssm_c_im, ssm_d, w_glu, b_glu, w_up_ssm, w_out, w_router, router_bias, w_gate, w_up, w_down, ws_gate, ws_up, ws_down, final_norm_g):
    x = jnp.concatenate([x_prompt.reshape(N_PROMPT, D_MODEL), x_sample.reshape(N_SAMPLE, D_MODEL)], axis=0)
    cond = jnp.concatenate([c_ctx[None, :], c, jnp.zeros((COND_ROWS - N_COND, D_MODEL), F32)], axis=0)
    mod_all = _ada_all(cond, w_mod, b_mod)

    cos_t, sin_t = _rope_tables(SEQ_TILE)
    head_avg = jnp.kron(jnp.eye(QK_WIDTH // HEAD_DIM, dtype=F32),
                        jnp.full((HEAD_DIM, HEAD_DIM), 1.0 / HEAD_DIM, F32)).astype(BF16)
    final_g = final_norm_g.reshape(1, D_MODEL)

    new_k, new_v, new_s = [], [], []
    for l in range(DEPTH):
        mod = mod_all[l]
        qk_gain = jnp.concatenate([jnp.tile(q_norm_g[l], N_HEADS), jnp.tile(k_norm_g[l], N_KV_HEADS)])[None, :]
        q, k, v, pool_u, ssm_u = _in_project(
            x, mod, norm1_g[l][None, :], w_in[l][:, :MIX_WIDTH].astype(BF16), head_avg, qk_gain, cos_t, sin_t)

        k_p = k[:N_PROMPT].reshape(BATCH, SEQ, N_KV_HEADS, HEAD_DIM)
        v_p = v[:N_PROMPT].reshape(BATCH, SEQ, N_KV_HEADS, HEAD_DIM)
        k_s = k[N_PROMPT:].reshape(DEC_BATCH, DEC_SEQ, N_KV_HEADS, HEAD_DIM)
        v_s = v[N_PROMPT:].reshape(DEC_BATCH, DEC_SEQ, N_KV_HEADS, HEAD_DIM)
        new_k.append(k_p)
        new_v.append(v_p)
        keys = jnp.concatenate([cache_k[:, l], k_s], axis=1)
        vals = jnp.concatenate([cache_v[:, l], v_s], axis=1)
        oa_p = _attention(q, k_p, v_p, BATCH, SEQ, 0)
        oa_s, wg, wu, wd = _attention(q, keys, vals, DEC_BATCH, DEC_SEQ, N_PROMPT, cast=(l, [w_gate, w_up, w_down]))

        pool_bd = jax.scipy.linalg.block_diag(*[pool_w[l, g] for g in range(len(POOL_WINDOWS))]).astype(BF16)
        p_scale = pool_scale[l][None, :]
        op_p = _pool(pool_u, pool_bd, p_scale, BATCH, SEQ, 0)
        op_s = _pool(pool_u, pool_bd, p_scale, DEC_BATCH, DEC_SEQ, N_PROMPT)

        a_rows, b_mat, c_mat = _ssm_matrices(ssm_a_re[l], ssm_a_im[l], ssm_log_dt[l], ssm_b_re[l], ssm_b_im[l],
                                             ssm_c_re[l], ssm_c_im[l])
        y_f, y_b, st = _ssm_scan(ssm_u, state_ssm[:, l], a_rows, b_mat, c_mat)
        new_s.append(st)

        w_r = jnp.pad(w_router[l], ((0, 0), (0, ROUTER_LANES - N_EXPERTS)))
        r_b = jnp.concatenate([router_bias[l], jnp.full((ROUTER_LANES - N_EXPERTS,), -jnp.inf, F32)])[None, :]
        x = _merge(
            x, mod, norm1_g[l][None, :], w_in[l][:, MIX_WIDTH:].astype(BF16), oa_p, oa_s, op_p, op_s, ssm_u,
            y_f, y_b, ssm_d[l].reshape(1, SSM_WIDTH), w_glu[l].astype(BF16), b_glu[l][None, :],
            w_up_attn[l].astype(BF16), w_up_pool[l].astype(BF16), w_up_ssm[l].astype(BF16),
            w_out[l].astype(BF16))
        h2, gate_w, slot, slot_t = _route(x, mod, norm2_g[l][None, :], w_r, r_b)
        x = _experts(h2, gate_w, slot, slot_t, wg, wu, wd, ws_gate[l].astype(BF16), ws_up[l].astype(BF16),
                     ws_down[l].astype(BF16), x, mod, final_g, final=(l == DEPTH - 1))

    y_prompt = x[:N_PROMPT].reshape(BATCH, SEQ, D_MODEL)
    y_sample = x[N_PROMPT:].reshape(DEC_BATCH, DEC_SEQ, D_MODEL)
    return (y_prompt, y_sample, jnp.stack(new_k, axis=1), jnp.stack(new_v, axis=1), jnp.stack(new_s, axis=1))
```

```python
import functools
import math

import jax
import jax.numpy as jnp
import numpy as np
from jax import lax
from jax.experimental import pallas as pl
from jax.experimental.pallas import tpu as pltpu

D_MODEL = 1024
BATCH = 32
SEQ = 256
DEPTH = 2
DEC_BATCH = 4
DEC_SEQ = 4096
PAST_LEN = 256
GRID_W = 64
EPS = 1e-6
N_MOD = 6
HEAD_DIM = 64
N_HEADS = 8
N_KV_HEADS = 2
ATTN_WIDTH = N_HEADS * HEAD_DIM
KV_WIDTH = N_KV_HEADS * HEAD_DIM
ROPE_BASE = 10000.0
ROPE_PAIRS_PER_AXIS = HEAD_DIM // 4
POOL_WINDOWS = (2, 4, 8, 16)
POOL_GROUP = 64
POOL_WIDTH = len(POOL_WINDOWS) * POOL_GROUP
SSM_H = 16
SSM_GROUPS = 16
SSM_WIDTH = SSM_H * SSM_GROUPS
SSM_STATE = 64
N_EXPERTS = 64
TOP_K = 8
EXPERT_DIM = 256
SHARED_DIM = 256
ROUTE_SCALE = 2.5

N_PROMPT = BATCH * SEQ
N_SAMPLE = DEC_BATCH * DEC_SEQ
N_TOK = N_PROMPT + N_SAMPLE
N_COND = 1 + DEC_BATCH
COND_ROWS = 8
QK_WIDTH = ATTN_WIDTH + KV_WIDTH
MIX_WIDTH = QK_WIDTH + KV_WIDTH + POOL_WIDTH + SSM_WIDTH
GATE_WIDTH = 3 * D_MODEL
SSM_COLS = SSM_GROUPS * SSM_STATE
SSM_LANES = 2 * SSM_COLS
SSM_SEQS = 4
SSM_CHUNK = 128
ROUTER_LANES = 128
EXPERT_CHUNK = 4

SEQ_TILE = SEQ
SSM_ROWS = 2 * SSM_SEQS
SSM_GROUPS_PROMPT = BATCH // SSM_SEQS
SCAN_T = SSM_GROUPS_PROMPT * SEQ + DEC_SEQ
SCAN_CHUNKS = SCAN_T // SSM_CHUNK
PROMPT_CHUNKS = SSM_GROUPS_PROMPT * SEQ // SSM_CHUNK
SSM_PITCH = SSM_CHUNK + 4
SSM_HALF = SSM_SEQS * SSM_PITCH

LANE = 128
SSM_SLABS = SSM_LANES // LANE
VMEM_LIMIT = 56 * 1024 * 1024

F32 = jnp.float32
BF16 = jnp.bfloat16
HIGHEST = lax.Precision.HIGHEST


def _sigmoid(x):
    return 1.0 / (1.0 + jnp.exp(-x))


def _params(dims, vmem=VMEM_LIMIT):
    return pltpu.CompilerParams(dimension_semantics=dims, vmem_limit_bytes=vmem)


def _mod_index(i, tm):
    p = N_PROMPT // tm
    t = DEC_SEQ // tm
    return jnp.where(i < p, 0, 1 + (i - p) // t)


def _scan_block(i):
    k = i - BATCH
    tiles = DEC_SEQ // SEQ_TILE
    rb = jnp.where(i < BATCH, i // SSM_SEQS, SSM_GROUPS_PROMPT + k % tiles)
    slot = jnp.where(i < BATCH, i % SSM_SEQS, k // tiles)
    return rb, slot


def _norm_mod(x, g, shift, scale):
    ms = jnp.mean(x * x, axis=-1, keepdims=True)
    return (x * lax.rsqrt(ms + EPS) * g) * (1.0 + scale) + shift


def _stream_rows(xp_ref, xs_ref):
    return jnp.where(pl.program_id(0) < N_PROMPT // SEQ_TILE, xp_ref[...], xs_ref[...])


def _stream_specs(width, s_base):
    p_tiles = N_PROMPT // SEQ_TILE
    return [pl.BlockSpec((SEQ_TILE, width), lambda i: (jnp.minimum(i, p_tiles - 1), 0)),
            pl.BlockSpec((SEQ_TILE, width), lambda i: (s_base + jnp.maximum(i - p_tiles, 0), 0))]


def _mod_kernel(cond_ref, w_ref, b_ref, o_ref):
    c = cond_ref[...]
    s = c * _sigmoid(c)
    o_ref[0] = jnp.dot(s, w_ref[0], preferred_element_type=F32, precision=HIGHEST) + b_ref[0]


def _ada_all(cond, w_mod, b_mod):
    tn = 1536
    width = N_MOD * D_MODEL
    out = pl.pallas_call(
        _mod_kernel,
        grid=(DEPTH, width // tn),
        in_specs=[
            pl.BlockSpec((COND_ROWS, D_MODEL), lambda l, j: (0, 0)),
            pl.BlockSpec((1, D_MODEL, tn), lambda l, j: (l, 0, j)),
            pl.BlockSpec((1, 1, tn), lambda l, j: (l, 0, j)),
        ],
        out_specs=pl.BlockSpec((1, COND_ROWS, tn), lambda l, j: (l, 0, j)),
        out_shape=jax.ShapeDtypeStruct((DEPTH, COND_ROWS, width), F32),
        compiler_params=_params(("parallel", "parallel")),
        name="ada_mod",
    )(cond, w_mod, b_mod.reshape(DEPTH, 1, width))
    return out.reshape(DEPTH, COND_ROWS, N_MOD, D_MODEL)


def _inproj_kernel(xp_ref, xs_ref, mod_ref, g_ref, w_ref, bd_ref, qkg_ref, cos_ref, sin_ref,
                   q_ref, k_ref, v_ref, pu_ref, su_ref):
    h = _norm_mod(_stream_rows(xp_ref, xs_ref), g_ref[...], mod_ref[0, 0:1, :], mod_ref[0, 1:2, :])
    z = jnp.dot(h.astype(BF16), w_ref[...], preferred_element_type=F32)
    qk = z[:, :QK_WIDTH]
    qq = qk * qk
    hi = qq.astype(BF16)
    lo = (qq - hi.astype(F32)).astype(BF16)
    bd = bd_ref[...]
    ms = jnp.concatenate(
        [jnp.dot(hi[:, c * LANE:(c + 1) * LANE], bd, preferred_element_type=F32)
         + jnp.dot(lo[:, c * LANE:(c + 1) * LANE], bd, preferred_element_type=F32)
         for c in range(QK_WIDTH // LANE)], axis=1)
    qkn = qk * lax.rsqrt(ms + EPS) * qkg_ref[...]
    parts = []
    for c in range(QK_WIDTH // LANE):
        blk = qkn[:, c * LANE:(c + 1) * LANE]
        nxt = pltpu.roll(blk, LANE - 1, axis=1)
        prv = pltpu.roll(blk, 1, axis=1)
        lane = lax.broadcasted_iota(jnp.int32, blk.shape, 1)
        parts.append(jnp.where((lane & 1) == 0, nxt, prv))
    partner = jnp.concatenate(parts, axis=1)
    qkr = qkn * cos_ref[...] + partner * sin_ref[...]
    q_ref[...] = (qkr[:, :ATTN_WIDTH] * (HEAD_DIM ** -0.5 * math.log2(math.e))).astype(BF16)
    k_ref[...] = qkr[:, ATTN_WIDTH:QK_WIDTH]
    v_ref[...] = z[:, QK_WIDTH:QK_WIDTH + KV_WIDTH]
    pu_ref[...] = z[:, QK_WIDTH + KV_WIDTH:QK_WIDTH + KV_WIDTH + POOL_WIDTH]
    su_ref[...] = z[:, QK_WIDTH + KV_WIDTH + POOL_WIDTH:MIX_WIDTH]


def _in_project(x_pair, mod, norm_g, w_mix, bd, qk_gain, cos_t, sin_t):
    tm = SEQ_TILE
    p_tiles = N_PROMPT // tm
    s_tiles = DEC_SEQ // tm

    def rope_idx(i):
        return (jnp.where(i < p_tiles, 0, 1 + (i - p_tiles) % s_tiles), 0)

    row = lambda i: (i, 0)
    const = lambda i: (0, 0)
    return pl.pallas_call(
        _inproj_kernel,
        grid=(N_TOK // tm,),
        in_specs=_stream_specs(D_MODEL, x_pair[2]) + [
            pl.BlockSpec((1, N_MOD, D_MODEL), lambda i: (_mod_index(i, tm), 0, 0)),
            pl.BlockSpec((1, D_MODEL), const),
            pl.BlockSpec((D_MODEL, MIX_WIDTH), const),
            pl.BlockSpec((LANE, LANE), const),
            pl.BlockSpec((1, QK_WIDTH), const),
            pl.BlockSpec((tm, QK_WIDTH), rope_idx),
            pl.BlockSpec((tm, QK_WIDTH), rope_idx),
        ],
        out_specs=[
            pl.BlockSpec((tm, ATTN_WIDTH), row),
            pl.BlockSpec((tm, KV_WIDTH), row),
            pl.BlockSpec((tm, KV_WIDTH), row),
            pl.BlockSpec((tm, POOL_WIDTH), row),
            pl.BlockSpec((tm, SSM_WIDTH), _scan_block),
        ],
        out_shape=[
            jax.ShapeDtypeStruct((N_TOK, ATTN_WIDTH), BF16),
            jax.ShapeDtypeStruct((N_TOK, KV_WIDTH), F32),
            jax.ShapeDtypeStruct((N_TOK, KV_WIDTH), F32),
            jax.ShapeDtypeStruct((N_TOK, POOL_WIDTH), F32),
            jax.ShapeDtypeStruct((SCAN_T, SSM_SEQS * SSM_WIDTH), F32),
        ],
        compiler_params=_params(("parallel",)),
        name="in_project",
    )(x_pair[0], x_pair[1], mod, norm_g, w_mix, bd, qk_gain, cos_t, sin_t)


def _rope_tables(tm):
    rows = DEC_SEQ // GRID_W
    row = jnp.repeat(jnp.arange(rows, dtype=F32), GRID_W)
    col = jnp.tile(jnp.arange(GRID_W, dtype=F32), rows)
    inv_freq = ROPE_BASE ** (-jnp.arange(ROPE_PAIRS_PER_AXIS, dtype=F32) / ROPE_PAIRS_PER_AXIS)
    ang = jnp.concatenate([row[:, None] * inv_freq, col[:, None] * inv_freq], axis=-1)
    cos = jnp.repeat(jnp.cos(ang), 2, axis=-1)
    sin = jnp.repeat(jnp.sin(ang), 2, axis=-1) * jnp.tile(jnp.array([-1.0, 1.0], F32), HEAD_DIM // 2)
    n_rep = QK_WIDTH // HEAD_DIM
    cos = jnp.concatenate([jnp.ones((tm, HEAD_DIM), F32), cos], axis=0)
    sin = jnp.concatenate([jnp.zeros((tm, HEAD_DIM), F32), sin], axis=0)
    return jnp.tile(cos, (1, n_rep)), jnp.tile(sin, (1, n_rep))


def _attn_kernel(q_ref, kt_ref, ve_ref, vo_ref, *rest):
    n_cast = (len(rest) - 1) // 2
    o_ref = rest[n_cast]
    for src, dst in zip(rest[:n_cast], rest[n_cast + 1:]):
        dst[...] = src[...].astype(dst.dtype)
    kt = kt_ref[0]
    lane = lax.broadcasted_iota(jnp.int32, (q_ref.shape[0], LANE), 1)
    first = lane < HEAD_DIM
    slabs = []
    for pair in range(q_ref.shape[1] // LANE):
        q = q_ref[:, pair * LANE:(pair + 1) * LANE]
        halves = []
        for keep, v_ref in ((first, ve_ref), (lane >= HEAD_DIM, vo_ref)):
            qh = jnp.where(keep, q, jnp.zeros_like(q))
            s = jnp.dot(qh, kt, preferred_element_type=F32)
            m = jnp.max(s, axis=-1, keepdims=True)
            p = jnp.exp2(s - m).astype(BF16)
            a = jnp.dot(p, v_ref[0], preferred_element_type=F32)
            halves.append(a / pltpu.roll(a, HEAD_DIM, axis=1))
        slabs.append(jnp.where(first, halves[0], halves[1]))
    o_ref[...] = jnp.concatenate(slabs, axis=1).astype(o_ref.dtype)


def _attention(q, keys, vals, n_batch, n_q, row0, cast=None):
    n_keys = keys.shape[1]
    tq = 256
    kt = jnp.transpose(keys, (0, 2, 3, 1)).astype(BF16)
    kt = jnp.concatenate([kt, kt], axis=2).reshape(n_batch * N_KV_HEADS, 2 * HEAD_DIM, n_keys)
    vv = jnp.transpose(vals, (0, 2, 1, 3)).astype(BF16)
    ones = jnp.ones_like(vv)
    v_e = jnp.concatenate([vv, ones], axis=3).reshape(n_batch * N_KV_HEADS, n_keys, 2 * HEAD_DIM)
    v_o = jnp.concatenate([ones, vv], axis=3).reshape(n_batch * N_KV_HEADS, n_keys, 2 * HEAD_DIM)
    q_tiles = n_q // tq
    base = row0 // tq
    group_w = ATTN_WIDTH // N_KV_HEADS
    kv_idx = lambda b, g, i: (b * N_KV_HEADS + g, 0, 0)
    in_specs = [
        pl.BlockSpec((tq, group_w), lambda b, g, i: (base + b * q_tiles + i, g)),
        pl.BlockSpec((1, LANE, n_keys), kv_idx),
        pl.BlockSpec((1, n_keys, LANE), kv_idx),
        pl.BlockSpec((1, n_keys, LANE), kv_idx),
    ]
    out_specs = [pl.BlockSpec((tq, group_w), lambda b, g, i: (b * q_tiles + i, g))]
    out_shape = [jax.ShapeDtypeStruct((n_batch * n_q, ATTN_WIDTH), BF16)]
    args = [q, kt, v_e, v_o]
    if cast is not None:
        layer, tensors = cast
        steps = n_batch * N_KV_HEADS * q_tiles
        step = lambda b, g, i: (b * N_KV_HEADS + g) * q_tiles + i
        for w in tensors:
            _, n_e, rows, cols = w.shape
            part = n_e * rows // steps
            assert part * steps == n_e * rows and part % 16 == 0 and rows % part == 0
            in_specs.append(pl.BlockSpec((1, part, cols), lambda b, g, i: (layer * steps + step(b, g, i), 0, 0)))
            out_specs.append(pl.BlockSpec((1, part, cols), lambda b, g, i: (step(b, g, i), 0, 0)))
            out_shape.append(jax.ShapeDtypeStruct((steps, part, cols), BF16))
            args.append(w.reshape(DEPTH * steps, part, cols))
    outs = pl.pallas_call(
        _attn_kernel,
        grid=(n_batch, N_KV_HEADS, q_tiles),
        in_specs=in_specs,
        out_specs=out_specs,
        out_shape=out_shape,
        compiler_params=_params(("parallel", "parallel", "parallel")),
        name="attention",
    )(*args)
    if cast is None:
        return outs[0]
    return [outs[0]] + [o.reshape(w.shape[1:]) for o, w in zip(outs[1:], cast[1])]


def _pool_kernel(u_ref, w_ref, sc_ref, o_ref, pad_ref, *, n_seq):
    halo = 8
    u = u_ref[...]
    zeros = jnp.zeros((halo, POOL_WIDTH), F32)
    pad_ref[0:halo, :] = zeros
    pad_ref[halo + n_seq:2 * halo + n_seq, :] = zeros
    pad_ref[halo:halo + n_seq, :] = u

    def sh(j):
        return pad_ref[halo + j:halo + j + n_seq, :]

    t2 = sh(-1) + u
    t4 = t2 + sh(-2) + sh(1)
    t8 = t4 + sh(-4) + sh(-3) + sh(2) + sh(3)
    t16 = t8 + sh(-8) + sh(-7) + sh(-6) + sh(-5) + sh(4) + sh(5) + sh(6) + sh(7)
    grp = lax.broadcasted_iota(jnp.int32, u.shape, 1) >> 6
    t = lax.broadcasted_iota(jnp.int32, u.shape, 0)
    tot = jnp.where(grp == 0, t2, jnp.where(grp == 1, t4, jnp.where(grp == 2, t8, t16)))
    half = jnp.where(grp == 0, 1, jnp.where(grp == 1, 2, jnp.where(grp == 2, 4, 8)))
    lo = jnp.maximum(t - half, 0)
    hi = jnp.minimum(t + half, n_seq)
    pooled = tot / (hi - lo).astype(F32) - u
    mixed = jnp.dot(pooled.astype(BF16), w_ref[...], preferred_element_type=F32)
    o_ref[...] = (mixed * sc_ref[...]).astype(o_ref.dtype)


def _pool(u, w_bd, scale, n_batch, n_seq, row0):
    base = row0 // n_seq
    return pl.pallas_call(
        functools.partial(_pool_kernel, n_seq=n_seq),
        grid=(n_batch,),
        in_specs=[
            pl.BlockSpec((n_seq, POOL_WIDTH), lambda b: (base + b, 0)),
            pl.BlockSpec((POOL_WIDTH, POOL_WIDTH), lambda b: (0, 0)),
            pl.BlockSpec((1, POOL_WIDTH), lambda b: (0, 0)),
        ],
        out_specs=pl.BlockSpec((n_seq, POOL_WIDTH), lambda b: (b, 0)),
        out_shape=jax.ShapeDtypeStruct((n_batch * n_seq, POOL_WIDTH), BF16),
        scratch_shapes=[pltpu.VMEM((n_seq + 16, POOL_WIDTH), F32)],
        compiler_params=_params(("parallel",)),
        name="pool_mixer",
    )(u, w_bd, scale)


def _scan_chunk_first(i):
    per_seq = SEQ // SSM_CHUNK
    return jnp.where(i < PROMPT_CHUNKS, i % per_seq == 0, i == PROMPT_CHUNKS)


def _scan_bwd_chunk(i):
    per_seq = SEQ // SSM_CHUNK
    return jnp.where(i < PROMPT_CHUNKS, (i // per_seq) * per_seq + (per_seq - 1 - i % per_seq),
                     PROMPT_CHUNKS + SCAN_CHUNKS - 1 - i)


def _scan_group(i):
    return jnp.where(i < PROMPT_CHUNKS, i // (SEQ // SSM_CHUNK), SSM_GROUPS_PROMPT)


def _split3(x):
    hi = x.astype(BF16)
    r1 = x - hi.astype(F32)
    mid = r1.astype(BF16)
    lo = (r1 - mid.astype(F32)).astype(BF16)
    return hi, mid, lo


def _ssm_kernel(uf_ref, ub_ref, a_ref, bf_ref, bb_ref, cf_ref, cb_ref, jin_ref, jout_ref, h0_ref,
                yf_ref, yb_ref, hfin_ref, bu_ref, st_ref, lhs_ref, yy_ref):
    @pl.when(_scan_chunk_first(pl.program_id(0)))
    def _():
        st_ref[...] = h0_ref[0]

    ub_rev = jnp.dot(jin_ref[...], ub_ref[...].astype(BF16), preferred_element_type=F32)
    gap = jnp.zeros((SSM_PITCH - SSM_CHUNK, SSM_WIDTH), F32)
    for s in range(SSM_SEQS):
        cols = slice(s * SSM_WIDTH, (s + 1) * SSM_WIDTH)
        lhs_ref[s * SSM_PITCH:s * SSM_PITCH + SSM_CHUNK, :] = uf_ref[:, cols]
        lhs_ref[s * SSM_PITCH + SSM_CHUNK:(s + 1) * SSM_PITCH, :] = gap
        lhs_ref[SSM_HALF + s * SSM_PITCH:SSM_HALF + s * SSM_PITCH + SSM_CHUNK, :] = ub_rev[:, cols]
        lhs_ref[SSM_HALF + s * SSM_PITCH + SSM_CHUNK:SSM_HALF + (s + 1) * SSM_PITCH, :] = gap
    bu_f = jnp.dot(lhs_ref[:SSM_HALF, :].astype(BF16), bf_ref[...], preferred_element_type=F32)
    bu_b = jnp.dot(lhs_ref[SSM_HALF:, :].astype(BF16), bb_ref[...], preferred_element_type=F32)
    for k in range(SSM_SLABS):
        bu_ref[k, :SSM_HALF, :] = bu_f[:, k * LANE:(k + 1) * LANE]
        bu_ref[k, SSM_HALF:, :] = bu_b[:, k * LANE:(k + 1) * LANE]

    half = SSM_SLABS // 2
    a_re = [a_ref[:, k * LANE:(k + 1) * LANE] for k in range(half)]
    a_im = [a_ref[:, SSM_COLS + k * LANE:SSM_COLS + (k + 1) * LANE] for k in range(half)]

    def step(t, carry):
        rows = pl.ds(t, SSM_ROWS, stride=SSM_PITCH)
        new_re, new_im = [], []
        for k in range(half):
            h_re, h_im = carry[k], carry[half + k]
            n_re = a_re[k] * h_re - a_im[k] * h_im + bu_ref[k, rows, :]
            n_im = a_re[k] * h_im + a_im[k] * h_re + bu_ref[half + k, rows, :]
            bu_ref[k, rows, :] = n_re
            bu_ref[half + k, rows, :] = n_im
            new_re.append(n_re)
            new_im.append(n_im)
        return tuple(new_re + new_im)

    init = tuple(st_ref[:, k * LANE:(k + 1) * LANE] for k in range(SSM_SLABS))
    fin = lax.fori_loop(0, SSM_CHUNK, step, init)
    for k in range(SSM_SLABS):
        st_ref[:, k * LANE:(k + 1) * LANE] = fin[k]
    hfin_ref[0] = st_ref[...]

    h_f = jnp.concatenate([bu_ref[k, :SSM_HALF, :] for k in range(SSM_SLABS)], axis=1).astype(BF16)
    h_b = jnp.concatenate([bu_ref[k, SSM_HALF:, :] for k in range(SSM_SLABS)], axis=1).astype(BF16)
    yy_ref[...] = jnp.dot(h_f, cf_ref[...], preferred_element_type=F32)
    for s in range(SSM_SEQS):
        yf_ref[:, s * SSM_WIDTH:(s + 1) * SSM_WIDTH] = yy_ref[s * SSM_PITCH:s * SSM_PITCH + SSM_CHUNK, :]
    y_b = jnp.dot(h_b, cb_ref[...], preferred_element_type=F32)
    y_nat = sum(jnp.dot(jout_ref[...], piece, preferred_element_type=F32) for piece in _split3(y_b))
    for s in range(SSM_SEQS):
        yb_ref[:, s * SSM_WIDTH:(s + 1) * SSM_WIDTH] = y_nat[s * SSM_CHUNK:(s + 1) * SSM_CHUNK, :]


def _ssm_scan(su8, h0_sample, a_rows, b_mat, c_mat):
    rows = SSM_ROWS
    width = SSM_SEQS * SSM_WIDTH
    n_groups = SSM_GROUPS_PROMPT + 1
    hs = h0_sample.astype(F32).reshape(SSM_SEQS, 2, SSM_COLS, 2).transpose(1, 0, 3, 2).reshape(1, rows, SSM_LANES)
    hh = jnp.concatenate([jnp.zeros((SSM_GROUPS_PROMPT, rows, SSM_LANES), F32), hs], axis=0)
    j_in = jnp.asarray(np.eye(SSM_CHUNK, dtype=np.float32)[::-1], dtype=BF16)
    sel = np.zeros((SSM_SEQS * SSM_CHUNK, SSM_HALF), np.float32)
    for s in range(SSM_SEQS):
        for t in range(SSM_CHUNK):
            sel[s * SSM_CHUNK + t, s * SSM_PITCH + SSM_CHUNK - 1 - t] = 1.0
    j_out = jnp.asarray(sel, dtype=BF16)
    fwd = lambda i: (i, 0)
    bwd = lambda i: (_scan_bwd_chunk(i), 0)
    grp = lambda i: (_scan_group(i), 0, 0)
    const = lambda i: (0, 0)
    y_f, y_b, hfin = pl.pallas_call(
        _ssm_kernel,
        grid=(SCAN_CHUNKS,),
        in_specs=[
            pl.BlockSpec((SSM_CHUNK, width), fwd),
            pl.BlockSpec((SSM_CHUNK, width), bwd),
            pl.BlockSpec((rows, SSM_LANES), const),
            pl.BlockSpec((SSM_WIDTH, SSM_LANES), const),
            pl.BlockSpec((SSM_WIDTH, SSM_LANES), const),
            pl.BlockSpec((SSM_LANES, SSM_WIDTH), const),
            pl.BlockSpec((SSM_LANES, SSM_WIDTH), const),
            pl.BlockSpec((SSM_CHUNK, SSM_CHUNK), const),
            pl.BlockSpec((SSM_SEQS * SSM_CHUNK, SSM_HALF), const),
            pl.BlockSpec((1, rows, SSM_LANES), grp),
        ],
        out_specs=[
            pl.BlockSpec((SSM_CHUNK, width), fwd),
            pl.BlockSpec((SSM_CHUNK, width), bwd),
            pl.BlockSpec((1, rows, SSM_LANES), grp),
        ],
        out_shape=[
            jax.ShapeDtypeStruct((SCAN_T, width), F32),
            jax.ShapeDtypeStruct((SCAN_T, width), F32),
            jax.ShapeDtypeStruct((n_groups, rows, SSM_LANES), F32),
        ],
        scratch_shapes=[pltpu.VMEM((SSM_SLABS, 2 * SSM_HALF, LANE), F32), pltpu.VMEM((rows, SSM_LANES), F32),
                        pltpu.VMEM((2 * SSM_HALF, SSM_WIDTH), F32), pltpu.VMEM((SSM_HALF, SSM_WIDTH), F32)],
        compiler_params=_params(("arbitrary",)),
        name="ssm_scan",
    )(su8, su8, a_rows, b_mat[:SSM_WIDTH], b_mat[SSM_WIDTH:], c_mat[:, :SSM_WIDTH], c_mat[:, SSM_WIDTH:],
      j_in, j_out, hh)
    fin = hfin[:SSM_GROUPS_PROMPT].reshape(SSM_GROUPS_PROMPT, 2, SSM_SEQS, 2, SSM_GROUPS, SSM_STATE)
    fin = fin.transpose(0, 2, 1, 4, 5, 3).reshape(BATCH, 2, SSM_GROUPS, SSM_STATE, 2)
    return y_f, y_b, fin


def _ssm_matrices(a_re, a_im, log_dt, b_re, b_im, c_re, c_im):
    lam = lax.complex(a_re.astype(F32), a_im.astype(F32))
    dt = jnp.exp(log_dt.astype(F32))[..., None]
    a_bar = jnp.exp(lam * dt)
    b_bar = ((a_bar - 1.0) / lam)[..., None] * lax.complex(b_re.astype(F32), b_im.astype(F32))
    a_dir = jnp.concatenate([jnp.real(a_bar).reshape(2, SSM_COLS), jnp.imag(a_bar).reshape(2, SSM_COLS)], axis=-1)
    a_rows = jnp.repeat(a_dir, SSM_SEQS, axis=0)
    eye = jnp.eye(SSM_GROUPS, dtype=F32)
    bt = jnp.transpose(b_bar, (0, 1, 3, 2))
    b_real = jnp.einsum('dghp,ge->dghep', jnp.real(bt), eye).reshape(2 * SSM_WIDTH, SSM_COLS)
    b_imag = jnp.einsum('dghp,ge->dghep', jnp.imag(bt), eye).reshape(2 * SSM_WIDTH, SSM_COLS)
    b_mat = jnp.concatenate([b_real, b_imag], axis=-1).astype(BF16)
    cr = jnp.transpose(c_re.astype(F32), (0, 1, 3, 2))
    ci = jnp.transpose(c_im.astype(F32), (0, 1, 3, 2))
    c_real = jnp.einsum('dgph,ge->gpdeh', cr, eye).reshape(SSM_COLS, 2 * SSM_WIDTH)
    c_imag = jnp.einsum('dgph,ge->gpdeh', -ci, eye).reshape(SSM_COLS, 2 * SSM_WIDTH)
    c_mat = jnp.concatenate([c_real, c_imag], axis=0).astype(BF16)
    return a_rows, b_mat, c_mat


def _merge_kernel(xp_ref, xs_ref, mod_ref, g1_ref, wg_ref, oap_ref, oas_ref, opp_ref, ops_ref, su_ref, yf_ref, yb_ref,
                  d_ref, wglu_ref, bglu_ref, wua_ref, wup_ref, wus_ref, wo_ref, o_ref):
    x = _stream_rows(xp_ref, xs_ref)
    h = _norm_mod(x, g1_ref[...], mod_ref[0, 0:1, :], mod_ref[0, 1:2, :])
    gates = _sigmoid(jnp.dot(h.astype(BF16), wg_ref[...], preferred_element_type=F32))
    y = d_ref[...] * su_ref[...] + yf_ref[...] + yb_ref[...]
    y = 0.5 * y * (1.0 + jnp.tanh(math.sqrt(2.0 / math.pi) * (y + 0.044715 * (y * y * y))))
    glu = jnp.dot(y.astype(BF16), wglu_ref[...], preferred_element_type=F32) + bglu_ref[...]
    o_ssm = y * _sigmoid(glu)
    o_attn = _stream_rows(oap_ref, oas_ref)
    o_pool = _stream_rows(opp_ref, ops_ref)
    m = (gates[:, :D_MODEL] * jnp.dot(o_attn, wua_ref[...], preferred_element_type=F32)
         + gates[:, D_MODEL:2 * D_MODEL] * jnp.dot(o_pool, wup_ref[...], preferred_element_type=F32)
         + gates[:, 2 * D_MODEL:] * jnp.dot(o_ssm.astype(BF16), wus_ref[...], preferred_element_type=F32))
    o_ref[...] = x + mod_ref[0, 2:3, :] * jnp.dot(m.astype(BF16), wo_ref[...], preferred_element_type=F32)


def _merge(x_pair, mod, norm1_g, w_gates, oa_p, oa_s, op_p, op_s, ssm_u, y_f, y_b, d_skip, w_glu, b_glu,
           w_up_attn, w_up_pool, w_up_ssm, w_out):
    tm = SEQ_TILE
    row = lambda i: (i, 0)
    const = lambda i: (0, 0)
    return pl.pallas_call(
        _merge_kernel,
        grid=(N_TOK // tm,),
        in_specs=_stream_specs(D_MODEL, x_pair[2]) + [
            pl.BlockSpec((1, N_MOD, D_MODEL), lambda i: (_mod_index(i, tm), 0, 0)),
            pl.BlockSpec((1, D_MODEL), const),
            pl.BlockSpec((D_MODEL, GATE_WIDTH), const),
        ] + _stream_specs(ATTN_WIDTH, 0) + _stream_specs(POOL_WIDTH, 0) + [
            pl.BlockSpec((tm, SSM_WIDTH), _scan_block),
            pl.BlockSpec((tm, SSM_WIDTH), _scan_block),
            pl.BlockSpec((tm, SSM_WIDTH), _scan_block),
            pl.BlockSpec((1, SSM_WIDTH), const),
            pl.BlockSpec((SSM_WIDTH, SSM_WIDTH), const),
            pl.BlockSpec((1, SSM_WIDTH), const),
            pl.BlockSpec((ATTN_WIDTH, D_MODEL), const),
            pl.BlockSpec((POOL_WIDTH, D_MODEL), const),
            pl.BlockSpec((SSM_WIDTH, D_MODEL), const),
            pl.BlockSpec((D_MODEL, D_MODEL), const),
        ],
        out_specs=pl.BlockSpec((tm, D_MODEL), row),
        out_shape=jax.ShapeDtypeStruct((N_TOK, D_MODEL), F32),
        compiler_params=_params(("parallel",)),
        name="merge_branches",
    )(x_pair[0], x_pair[1], mod, norm1_g, w_gates, oa_p, oa_s, op_p, op_s, ssm_u, y_f, y_b, d_skip, w_glu, b_glu,
      w_up_attn, w_up_pool, w_up_ssm, w_out)


def _router_kernel(x_ref, mod_ref, g_ref, wrh_ref, wrl_ref, rb_ref, h_ref, gw_ref):
    h2 = _norm_mod(x_ref[...], g_ref[...], mod_ref[0, 3:4, :], mod_ref[0, 4:5, :])
    h_hi = h2.astype(BF16)
    h_ref[...] = h_hi
    h_lo = (h2 - h_hi.astype(F32)).astype(BF16)
    logits = (jnp.dot(h_hi, wrh_ref[...], preferred_element_type=F32)
              + jnp.dot(h_lo, wrh_ref[...], preferred_element_type=F32)
              + jnp.dot(h_hi, wrl_ref[...], preferred_element_type=F32))
    scores = _sigmoid(logits)
    sc_t = jnp.transpose(scores)[:N_EXPERTS, :]
    work = jnp.transpose(scores + rb_ref[...])[:N_EXPERTS, :]
    idx = lax.broadcasted_iota(jnp.int32, work.shape, 0).astype(F32)
    w_t = jnp.zeros_like(sc_t)
    for _ in range(TOP_K):
        mx = jnp.max(work, axis=0, keepdims=True)
        first = jnp.min(jnp.where(work == mx, idx, float(N_EXPERTS)), axis=0, keepdims=True)
        pick = idx == first
        w_t = jnp.where(pick, sc_t, w_t)
        work = jnp.where(pick, -jnp.inf, work)
    w_t = w_t / jnp.sum(w_t, axis=0, keepdims=True) * ROUTE_SCALE
    w_pad = jnp.concatenate([w_t, jnp.zeros((ROUTER_LANES - N_EXPERTS, w_t.shape[1]), F32)], axis=0)
    gw_ref[...] = jnp.transpose(w_pad)


def _route(x, mod, norm_g, w_router, router_bias):
    tm = 512
    row = lambda i: (i, 0)
    const = lambda i: (0, 0)
    w_hi = w_router.astype(BF16)
    w_lo = (w_router - w_hi.astype(F32)).astype(BF16)
    return pl.pallas_call(
        _router_kernel,
        grid=(N_TOK // tm,),
        in_specs=[
            pl.BlockSpec((tm, D_MODEL), row),
            pl.BlockSpec((1, N_MOD, D_MODEL), lambda i: (_mod_index(i, tm), 0, 0)),
            pl.BlockSpec((1, D_MODEL), const),
            pl.BlockSpec((D_MODEL, ROUTER_LANES), const),
            pl.BlockSpec((D_MODEL, ROUTER_LANES), const),
            pl.BlockSpec((1, ROUTER_LANES), const),
        ],
        out_specs=[pl.BlockSpec((tm, D_MODEL), row), pl.BlockSpec((tm, ROUTER_LANES), row)],
        out_shape=[jax.ShapeDtypeStruct((N_TOK, D_MODEL), BF16),
                   jax.ShapeDtypeStruct((N_TOK, ROUTER_LANES), F32)],
        compiler_params=_params(("parallel",)),
        name="moe_router",
    )(x, mod, norm_g, w_hi, w_lo, router_bias)


def _swiglu_mid(h, w_gate, w_up):
    a = jnp.dot(h, w_gate, preferred_element_type=F32)
    b = jnp.dot(h, w_up, preferred_element_type=F32)
    return (a * _sigmoid(a)) * b


def _experts_kernel(h_ref, gw_ref, wg_ref, wu_ref, wd_ref, sg_ref, su_ref, sd_ref, x_ref, mod_ref, fg_ref,
                    o_ref, *, final):
    j = pl.program_id(1)
    h = h_ref[...]

    @pl.when(j == 0)
    def _():
        mid = _swiglu_mid(h, sg_ref[...], su_ref[...]).astype(BF16)
        o_ref[...] = jnp.dot(mid, sd_ref[...], preferred_element_type=F32)

    gw = pltpu.roll(gw_ref[...], (ROUTER_LANES - j * EXPERT_CHUNK) & (ROUTER_LANES - 1), axis=1)
    mids = [(_swiglu_mid(h, wg_ref[e], wu_ref[e]) * gw[:, e:e + 1]).astype(BF16) for e in range(EXPERT_CHUNK)]
    mid = jnp.concatenate(mids, axis=1)
    wd = wd_ref[...].reshape(EXPERT_CHUNK * EXPERT_DIM, D_MODEL)
    o_ref[...] += jnp.dot(mid, wd, preferred_element_type=F32)

    @pl.when(j == pl.num_programs(1) - 1)
    def _():
        x = x_ref[...] + mod_ref[0, 5:6, :] * o_ref[...]
        if final:
            ms = jnp.mean(x * x, axis=-1, keepdims=True)
            x = x * lax.rsqrt(ms + EPS) * fg_ref[...]
        o_ref[...] = x


def _experts(h, gate_w, wg, wu, wd, sg, su, sd, x, mod, final_g, final, row0=0, n_rows=N_TOK):
    tm = 1024
    n_chunks = N_EXPERTS // EXPERT_CHUNK
    base = row0 // tm
    row = lambda i, j: (base + i, 0)
    const = lambda i, j: (0, 0)
    chunk = lambda i, j: (j, 0, 0)
    return pl.pallas_call(
        functools.partial(_experts_kernel, final=final),
        grid=(n_rows // tm, n_chunks),
        in_specs=[
            pl.BlockSpec((tm, D_MODEL), row),
            pl.BlockSpec((tm, ROUTER_LANES), row),
            pl.BlockSpec((EXPERT_CHUNK, D_MODEL, EXPERT_DIM), chunk),
            pl.BlockSpec((EXPERT_CHUNK, D_MODEL, EXPERT_DIM), chunk),
            pl.BlockSpec((EXPERT_CHUNK, EXPERT_DIM, D_MODEL), chunk),
            pl.BlockSpec((D_MODEL, SHARED_DIM), const),
            pl.BlockSpec((D_MODEL, SHARED_DIM), const),
            pl.BlockSpec((SHARED_DIM, D_MODEL), const),
            pl.BlockSpec((tm, D_MODEL), row),
            pl.BlockSpec((1, N_MOD, D_MODEL), lambda i, j: (_mod_index(base + i, tm), 0, 0)),
            pl.BlockSpec((1, D_MODEL), const),
        ],
        out_specs=pl.BlockSpec((tm, D_MODEL), lambda i, j: (i, 0)),
        out_shape=jax.ShapeDtypeStruct((n_rows, D_MODEL), F32),
        compiler_params=_params(("parallel", "arbitrary")),
        name="moe_experts",
    )(h, gate_w, wg, wu, wd, sg, su, sd, x, mod, final_g)


def kernel(x_prompt, x_sample, c, cache_k, cache_v, state_ssm, c_ctx, w_mod, b_mod, norm1_g, norm2_g, w_in, q_norm_g, k_norm_g, w_up_attn, pool_w, pool_scale, w_up_pool, ssm_a_re, ssm_a_im, ssm_log_dt, ssm_b_re, ssm_b_im, ssm_c_re, ssm_c_im, ssm_d, w_glu, b_glu, w_up_ssm, w_out, w_router, router_bias, w_gate, w_up, w_down, ws_gate, ws_up, ws_down, final_norm_g):
    x_pair = (x_prompt.reshape(N_PROMPT, D_MODEL), x_sample.reshape(N_SAMPLE, D_MODEL), 0)
    cond = jnp.concatenate([c_ctx[None, :], c, jnp.zeros((COND_ROWS - N_COND, D_MODEL), F32)], axis=0)
    mod_all = _ada_all(cond, w_mod, b_mod)

    cos_t, sin_t = _rope_tables(SEQ_TILE)
    head_avg = jnp.kron(jnp.eye(LANE // HEAD_DIM, dtype=F32),
                        jnp.full((HEAD_DIM, HEAD_DIM), 1.0 / HEAD_DIM, F32)).astype(BF16)
    final_g = final_norm_g.reshape(1, D_MODEL)

    new_k, new_v, new_s = [], [], []
    for l in range(DEPTH):
        mod = mod_all[l]
        qk_gain = jnp.concatenate([jnp.tile(q_norm_g[l], N_HEADS), jnp.tile(k_norm_g[l], N_KV_HEADS)])[None, :]
        q, k, v, pool_u, ssm_u = _in_project(
            x_pair, mod, norm1_g[l][None, :], w_in[l][:, :MIX_WIDTH].astype(BF16), head_avg, qk_gain, cos_t, sin_t)

        k_p = k[:N_PROMPT].reshape(BATCH, SEQ, N_KV_HEADS, HEAD_DIM)
        v_p = v[:N_PROMPT].reshape(BATCH, SEQ, N_KV_HEADS, HEAD_DIM)
        k_s = k[N_PROMPT:].reshape(DEC_BATCH, DEC_SEQ, N_KV_HEADS, HEAD_DIM)
        v_s = v[N_PROMPT:].reshape(DEC_BATCH, DEC_SEQ, N_KV_HEADS, HEAD_DIM)
        new_k.append(k_p)
        new_v.append(v_p)
        keys = jnp.concatenate([cache_k[:, l], k_s], axis=1)
        vals = jnp.concatenate([cache_v[:, l], v_s], axis=1)
        oa_p = _attention(q, k_p, v_p, BATCH, SEQ, 0)
        oa_s, wg, wu, wd = _attention(q, keys, vals, DEC_BATCH, DEC_SEQ, N_PROMPT, cast=(l, [w_gate, w_up, w_down]))

        pool_bd = jax.scipy.linalg.block_diag(*[pool_w[l, g] for g in range(len(POOL_WINDOWS))]).astype(BF16)
        p_scale = pool_scale[l][None, :]
        op_p = _pool(pool_u, pool_bd, p_scale, BATCH, SEQ, 0)
        op_s = _pool(pool_u, pool_bd, p_scale, DEC_BATCH, DEC_SEQ, N_PROMPT)

        a_rows, b_mat, c_mat = _ssm_matrices(ssm_a_re[l], ssm_a_im[l], ssm_log_dt[l], ssm_b_re[l], ssm_b_im[l],
                                             ssm_c_re[l], ssm_c_im[l])
        y_f, y_b, st = _ssm_scan(ssm_u, state_ssm[:, l], a_rows, b_mat, c_mat)
        new_s.append(st)

        w_r = jnp.pad(w_router[l], ((0, 0), (0, ROUTER_LANES - N_EXPERTS)))
        r_b = jnp.concatenate([router_bias[l], jnp.full((ROUTER_LANES - N_EXPERTS,), -jnp.inf, F32)])[None, :]
        x = _merge(
            x_pair, mod, norm1_g[l][None, :], w_in[l][:, MIX_WIDTH:].astype(BF16), oa_p, oa_s, op_p, op_s, ssm_u,
            y_f, y_b, ssm_d[l].reshape(1, SSM_WIDTH), w_glu[l].astype(BF16), b_glu[l][None, :],
            w_up_attn[l].astype(BF16), w_up_pool[l].astype(BF16), w_up_ssm[l].astype(BF16),
            w_out[l].astype(BF16))
        h2, gate_w = _route(x, mod, norm2_g[l][None, :], w_r, r_b)
        shared = (ws_gate[l].astype(BF16), ws_up[l].astype(BF16), ws_down[l].astype(BF16))
        if l < DEPTH - 1:
            x = _experts(h2, gate_w, wg, wu, wd, *shared, x, mod, final_g, final=False)
            x_pair = (x, x, N_PROMPT // SEQ_TILE)
        else:
            y_prompt = _experts(h2, gate_w, wg, wu, wd, *shared, x, mod, final_g, True, 0, N_PROMPT)
            y_sample = _experts(h2, gate_w, wg, wu, wd, *shared, x, mod, final_g, True, N_PROMPT, N_SAMPLE)

    y_prompt = y_prompt.reshape(BATCH, SEQ, D_MODEL)
    y_sample = y_sample.reshape(DEC_BATCH, DEC_SEQ, D_MODEL)
    return (y_prompt, y_sample, jnp.stack(new_k, axis=1), jnp.stack(new_v, axis=1), jnp.stack(new_s, axis=1))
```

```python
import functools
import math

import jax
import jax.numpy as jnp
import numpy as np
from jax import lax
from jax.experimental import pallas as pl
from jax.experimental.pallas import tpu as pltpu

D_MODEL = 1024
BATCH = 32
SEQ = 256
DEPTH = 2
DEC_BATCH = 4
DEC_SEQ = 4096
PAST_LEN = 256
GRID_W = 64
EPS = 1e-6
N_MOD = 6
HEAD_DIM = 64
N_HEADS = 8
N_KV_HEADS = 2
ATTN_WIDTH = N_HEADS * HEAD_DIM
KV_WIDTH = N_KV_HEADS * HEAD_DIM
ROPE_BASE = 10000.0
ROPE_PAIRS_PER_AXIS = HEAD_DIM // 4
POOL_WINDOWS = (2, 4, 8, 16)
POOL_GROUP = 64
POOL_WIDTH = len(POOL_WINDOWS) * POOL_GROUP
SSM_H = 16
SSM_GROUPS = 16
SSM_WIDTH = SSM_H * SSM_GROUPS
SSM_STATE = 64
N_EXPERTS = 64
TOP_K = 8
EXPERT_DIM = 256
SHARED_DIM = 256
ROUTE_SCALE = 2.5

N_PROMPT = BATCH * SEQ
N_SAMPLE = DEC_BATCH * DEC_SEQ
N_TOK = N_PROMPT + N_SAMPLE
N_COND = 1 + DEC_BATCH
COND_ROWS = 8
QK_WIDTH = ATTN_WIDTH + KV_WIDTH
MIX_WIDTH = QK_WIDTH + KV_WIDTH + POOL_WIDTH + SSM_WIDTH
GATE_WIDTH = 3 * D_MODEL
SSM_COLS = SSM_GROUPS * SSM_STATE
SSM_LANES = 2 * SSM_COLS
SSM_SEQS = 4
SSM_CHUNK = 128
ROUTER_LANES = 128
EXPERT_CHUNK = 4

SEQ_TILE = SEQ
SSM_ROWS = 2 * SSM_SEQS
SSM_GROUPS_PROMPT = BATCH // SSM_SEQS
SCAN_T = SSM_GROUPS_PROMPT * SEQ + DEC_SEQ
SCAN_CHUNKS = SCAN_T // SSM_CHUNK
PROMPT_CHUNKS = SSM_GROUPS_PROMPT * SEQ // SSM_CHUNK
SSM_PITCH = SSM_CHUNK + 4
SSM_HALF = SSM_SEQS * SSM_PITCH

LANE = 128
SSM_SLABS = SSM_LANES // LANE
VMEM_LIMIT = 56 * 1024 * 1024

F32 = jnp.float32
BF16 = jnp.bfloat16
HIGHEST = lax.Precision.HIGHEST


def _sigmoid(x):
    return 1.0 / (1.0 + jnp.exp(-x))


def _params(dims, vmem=VMEM_LIMIT):
    return pltpu.CompilerParams(dimension_semantics=dims, vmem_limit_bytes=vmem)


def _mod_index(i, tm):
    p = N_PROMPT // tm
    t = DEC_SEQ // tm
    return jnp.where(i < p, 0, 1 + (i - p) // t)


def _mod_spec(layer, tm, rank=1, base=0):
    if rank == 1:
        return pl.BlockSpec((None, 1, N_MOD, D_MODEL), lambda i: (layer, _mod_index(base + i, tm), 0, 0))
    return pl.BlockSpec((None, 1, N_MOD, D_MODEL), lambda i, j: (layer, _mod_index(base + i, tm), 0, 0))


def _layer_spec(rows, cols, layer, rank=1, row_block=0, col_block=0):
    if rank == 1:
        return pl.BlockSpec((None, rows, cols), lambda i: (layer, row_block, col_block))
    return pl.BlockSpec((None, rows, cols), lambda i, j: (layer, row_block, col_block))


def _scan_block(i):
    k = i - BATCH
    tiles = DEC_SEQ // SEQ_TILE
    rb = jnp.where(i < BATCH, i // SSM_SEQS, SSM_GROUPS_PROMPT + k % tiles)
    slot = jnp.where(i < BATCH, i % SSM_SEQS, k // tiles)
    return rb, slot


def _norm_mod(x, g, shift, scale):
    ms = jnp.mean(x * x, axis=-1, keepdims=True)
    return (x * lax.rsqrt(ms + EPS) * g) * (1.0 + scale) + shift


def _stream_rows(xp_ref, xs_ref):
    return jnp.where(pl.program_id(0) < N_PROMPT // SEQ_TILE, xp_ref[...], xs_ref[...])


def _stream_specs(width, s_base):
    p_tiles = N_PROMPT // SEQ_TILE
    return [pl.BlockSpec((SEQ_TILE, width), lambda i: (jnp.minimum(i, p_tiles - 1), 0)),
            pl.BlockSpec((SEQ_TILE, width), lambda i: (s_base + jnp.maximum(i - p_tiles, 0), 0))]


def _mod_kernel(cond_ref, w_ref, b_ref, o_ref):
    c = cond_ref[...]
    s = c * _sigmoid(c)
    o_ref[0] = jnp.dot(s, w_ref[0], preferred_element_type=F32, precision=HIGHEST) + b_ref[0]


def _ada_all(cond, w_mod, b_mod):
    tn = 1536
    width = N_MOD * D_MODEL
    out = pl.pallas_call(
        _mod_kernel,
        grid=(DEPTH, width // tn),
        in_specs=[
            pl.BlockSpec((COND_ROWS, D_MODEL), lambda l, j: (0, 0)),
            pl.BlockSpec((1, D_MODEL, tn), lambda l, j: (l, 0, j)),
            pl.BlockSpec((1, 1, tn), lambda l, j: (l, 0, j)),
        ],
        out_specs=pl.BlockSpec((1, COND_ROWS, tn), lambda l, j: (l, 0, j)),
        out_shape=jax.ShapeDtypeStruct((DEPTH, COND_ROWS, width), F32),
        compiler_params=_params(("parallel", "parallel")),
        name="ada_mod",
    )(cond, w_mod, b_mod.reshape(DEPTH, 1, width))
    return out.reshape(DEPTH, COND_ROWS, N_MOD, D_MODEL)


def _inproj_kernel(xp_ref, xs_ref, mod_ref, g_ref, w_ref, bd_ref, qkg_ref, cos_ref, sin_ref,
                   q_ref, k_ref, v_ref, pu_ref, su_ref):
    h = _norm_mod(_stream_rows(xp_ref, xs_ref), g_ref[...], mod_ref[0, 0:1, :], mod_ref[0, 1:2, :])
    z = jnp.dot(h.astype(BF16), w_ref[...], preferred_element_type=F32)
    qk = z[:, :QK_WIDTH]
    qq = qk * qk
    hi = qq.astype(BF16)
    lo = (qq - hi.astype(F32)).astype(BF16)
    bd = bd_ref[...]
    ms = jnp.concatenate(
        [jnp.dot(hi[:, c * LANE:(c + 1) * LANE], bd, preferred_element_type=F32)
         + jnp.dot(lo[:, c * LANE:(c + 1) * LANE], bd, preferred_element_type=F32)
         for c in range(QK_WIDTH // LANE)], axis=1)
    qkn = qk * lax.rsqrt(ms + EPS) * qkg_ref[...]
    cos, sin = cos_ref[...], sin_ref[...]
    parts = []
    for c in range(QK_WIDTH // LANE):
        blk = qkn[:, c * LANE:(c + 1) * LANE]
        nxt = pltpu.roll(blk, LANE - 1, axis=1)
        prv = pltpu.roll(blk, 1, axis=1)
        lane = lax.broadcasted_iota(jnp.int32, blk.shape, 1)
        parts.append(blk * cos + jnp.where((lane & 1) == 0, nxt, prv) * sin)
    qkr = jnp.concatenate(parts, axis=1)
    q_ref[...] = (qkr[:, :ATTN_WIDTH] * (HEAD_DIM ** -0.5 * math.log2(math.e))).astype(BF16)
    k_ref[...] = qkr[:, ATTN_WIDTH:QK_WIDTH]
    v_ref[...] = z[:, QK_WIDTH:QK_WIDTH + KV_WIDTH]
    pu_ref[...] = z[:, QK_WIDTH + KV_WIDTH:QK_WIDTH + KV_WIDTH + POOL_WIDTH]
    su_ref[...] = z[:, QK_WIDTH + KV_WIDTH + POOL_WIDTH:MIX_WIDTH]


def _in_project(layer, x_pair, mod, norm_g, w_mix, bd, qk_gain, cos_t, sin_t):
    tm = SEQ_TILE
    p_tiles = N_PROMPT // tm
    s_tiles = DEC_SEQ // tm

    def rope_idx(i):
        return (jnp.where(i < p_tiles, 0, 1 + (i - p_tiles) % s_tiles), 0)

    row = lambda i: (i, 0)
    const = lambda i: (0, 0)
    return pl.pallas_call(
        _inproj_kernel,
        grid=(N_TOK // tm,),
        in_specs=_stream_specs(D_MODEL, x_pair[2]) + [
            _mod_spec(layer, tm),
            _layer_spec(1, D_MODEL, layer),
            pl.BlockSpec((D_MODEL, MIX_WIDTH), const),
            pl.BlockSpec((LANE, LANE), const),
            pl.BlockSpec((1, QK_WIDTH), const),
            pl.BlockSpec((tm, LANE), rope_idx),
            pl.BlockSpec((tm, LANE), rope_idx),
        ],
        out_specs=[
            pl.BlockSpec((tm, ATTN_WIDTH), row),
            pl.BlockSpec((tm, KV_WIDTH), row),
            pl.BlockSpec((tm, KV_WIDTH), row),
            pl.BlockSpec((tm, POOL_WIDTH), row),
            pl.BlockSpec((tm, SSM_WIDTH), _scan_block),
        ],
        out_shape=[
            jax.ShapeDtypeStruct((N_TOK, ATTN_WIDTH), BF16),
            jax.ShapeDtypeStruct((N_TOK, KV_WIDTH), F32),
            jax.ShapeDtypeStruct((N_TOK, KV_WIDTH), F32),
            jax.ShapeDtypeStruct((N_TOK, POOL_WIDTH), F32),
            jax.ShapeDtypeStruct((SCAN_T, SSM_SEQS * SSM_WIDTH), F32),
        ],
        compiler_params=_params(("parallel",)),
        name="in_project",
    )(x_pair[0], x_pair[1], mod, norm_g, w_mix, bd, qk_gain, cos_t, sin_t)


def _rope_tables(tm):
    f32 = np.float32
    rows = DEC_SEQ // GRID_W
    row = np.repeat(np.arange(rows, dtype=f32), GRID_W)
    col = np.tile(np.arange(GRID_W, dtype=f32), rows)
    inv_freq = (f32(ROPE_BASE) ** (-np.arange(ROPE_PAIRS_PER_AXIS, dtype=f32) / f32(ROPE_PAIRS_PER_AXIS))).astype(f32)
    ang = np.concatenate([row[:, None] * inv_freq, col[:, None] * inv_freq], axis=-1).astype(f32)
    cos = np.repeat(np.cos(ang), 2, axis=-1)
    sin = np.repeat(np.sin(ang), 2, axis=-1) * np.tile(np.array([-1.0, 1.0], f32), HEAD_DIM // 2)
    cos = np.concatenate([np.ones((tm, HEAD_DIM), f32), cos], axis=0)
    sin = np.concatenate([np.zeros((tm, HEAD_DIM), f32), sin], axis=0)
    n_rep = LANE // HEAD_DIM
    return jnp.asarray(np.tile(cos, (1, n_rep)), F32), jnp.asarray(np.tile(sin, (1, n_rep)), F32)


def _attn_kernel(q_ref, kt_ref, ve_ref, vo_ref, *rest):
    n_cast = (len(rest) - 1) // 2
    o_ref = rest[n_cast]
    for src, dst in zip(rest[:n_cast], rest[n_cast + 1:]):
        dst[...] = src[...].astype(dst.dtype)
    kt = kt_ref[0]
    lane = lax.broadcasted_iota(jnp.int32, (q_ref.shape[0], LANE), 1)
    first = lane < HEAD_DIM
    slabs = []
    for pair in range(q_ref.shape[1] // LANE):
        q = q_ref[:, pair * LANE:(pair + 1) * LANE]
        halves = []
        for keep, v_ref in ((first, ve_ref), (lane >= HEAD_DIM, vo_ref)):
            qh = jnp.where(keep, q, jnp.zeros_like(q))
            s = jnp.dot(qh, kt, preferred_element_type=F32)
            m = jnp.max(s, axis=-1, keepdims=True)
            p = jnp.exp2(s - m).astype(BF16)
            a = jnp.dot(p, v_ref[0], preferred_element_type=F32)
            halves.append(a / pltpu.roll(a, HEAD_DIM, axis=1))
        slabs.append(jnp.where(first, halves[0], halves[1]))
    o_ref[...] = jnp.concatenate(slabs, axis=1).astype(o_ref.dtype)


def _attention(q, keys, vals, n_batch, n_q, row0, cast=None):
    n_keys = keys.shape[1]
    tq = 256
    kt = jnp.transpose(keys, (0, 2, 3, 1)).astype(BF16)
    kt = jnp.concatenate([kt, kt], axis=2).reshape(n_batch * N_KV_HEADS, 2 * HEAD_DIM, n_keys)
    vv = jnp.transpose(vals, (0, 2, 1, 3)).astype(BF16)
    ones = jnp.ones_like(vv)
    v_e = jnp.concatenate([vv, ones], axis=3).reshape(n_batch * N_KV_HEADS, n_keys, 2 * HEAD_DIM)
    v_o = jnp.concatenate([ones, vv], axis=3).reshape(n_batch * N_KV_HEADS, n_keys, 2 * HEAD_DIM)
    q_tiles = n_q // tq
    base = row0 // tq
    group_w = ATTN_WIDTH // N_KV_HEADS
    kv_idx = lambda b, g, i: (b * N_KV_HEADS + g, 0, 0)
    in_specs = [
        pl.BlockSpec((tq, group_w), lambda b, g, i: (base + b * q_tiles + i, g)),
        pl.BlockSpec((1, LANE, n_keys), kv_idx),
        pl.BlockSpec((1, n_keys, LANE), kv_idx),
        pl.BlockSpec((1, n_keys, LANE), kv_idx),
    ]
    out_specs = [pl.BlockSpec((tq, group_w), lambda b, g, i: (b * q_tiles + i, g))]
    out_shape = [jax.ShapeDtypeStruct((n_batch * n_q, ATTN_WIDTH), BF16)]
    args = [q, kt, v_e, v_o]
    if cast is not None:
        layer, tensors = cast
        steps = n_batch * N_KV_HEADS * q_tiles
        step = lambda b, g, i: (b * N_KV_HEADS + g) * q_tiles + i
        for w in tensors:
            _, n_e, rows, cols = w.shape
            part = n_e * rows // steps
            assert part * steps == n_e * rows and part % 16 == 0 and rows % part == 0
            in_specs.append(pl.BlockSpec((1, part, cols), lambda b, g, i: (layer * steps + step(b, g, i), 0, 0)))
            out_specs.append(pl.BlockSpec((1, part, cols), lambda b, g, i: (step(b, g, i), 0, 0)))
            out_shape.append(jax.ShapeDtypeStruct((steps, part, cols), BF16))
            args.append(w.reshape(DEPTH * steps, part, cols))
    outs = pl.pallas_call(
        _attn_kernel,
        grid=(n_batch, N_KV_HEADS, q_tiles),
        in_specs=in_specs,
        out_specs=out_specs,
        out_shape=out_shape,
        compiler_params=_params(("parallel", "parallel", "parallel")),
        name="attention",
    )(*args)
    if cast is None:
        return outs[0]
    return [outs[0]] + [o.reshape(w.shape[1:]) for o, w in zip(outs[1:], cast[1])]


def _pool_kernel(u_ref, w_ref, sc_ref, o_ref, pad_ref, *, n_seq):
    halo = 8
    u = u_ref[...]
    zeros = jnp.zeros((halo, POOL_WIDTH), F32)
    pad_ref[0:halo, :] = zeros
    pad_ref[halo + n_seq:2 * halo + n_seq, :] = zeros
    pad_ref[halo:halo + n_seq, :] = u

    def sh(j):
        return pad_ref[halo + j:halo + j + n_seq, :]

    t2 = sh(-1) + u
    t4 = t2 + sh(-2) + sh(1)
    t8 = t4 + sh(-4) + sh(-3) + sh(2) + sh(3)
    t16 = t8 + sh(-8) + sh(-7) + sh(-6) + sh(-5) + sh(4) + sh(5) + sh(6) + sh(7)
    grp = lax.broadcasted_iota(jnp.int32, u.shape, 1) >> 6
    t = lax.broadcasted_iota(jnp.int32, u.shape, 0)
    tot = jnp.where(grp == 0, t2, jnp.where(grp == 1, t4, jnp.where(grp == 2, t8, t16)))
    half = jnp.where(grp == 0, 1, jnp.where(grp == 1, 2, jnp.where(grp == 2, 4, 8)))
    lo = jnp.maximum(t - half, 0)
    hi = jnp.minimum(t + half, n_seq)
    pooled = tot / (hi - lo).astype(F32) - u
    mixed = jnp.dot(pooled.astype(BF16), w_ref[...], preferred_element_type=F32)
    o_ref[...] = (mixed * sc_ref[...]).astype(o_ref.dtype)


def _pool(layer, u, w_bd, scale, n_batch, n_seq, row0):
    base = row0 // n_seq
    return pl.pallas_call(
        functools.partial(_pool_kernel, n_seq=n_seq),
        grid=(n_batch,),
        in_specs=[
            pl.BlockSpec((n_seq, POOL_WIDTH), lambda b: (base + b, 0)),
            pl.BlockSpec((POOL_WIDTH, POOL_WIDTH), lambda b: (0, 0)),
            _layer_spec(1, POOL_WIDTH, layer),
        ],
        out_specs=pl.BlockSpec((n_seq, POOL_WIDTH), lambda b: (b, 0)),
        out_shape=jax.ShapeDtypeStruct((n_batch * n_seq, POOL_WIDTH), BF16),
        scratch_shapes=[pltpu.VMEM((n_seq + 16, POOL_WIDTH), F32)],
        compiler_params=_params(("parallel",)),
        name="pool_mixer",
    )(u, w_bd, scale)


def _scan_chunk_first(i):
    per_seq = SEQ // SSM_CHUNK
    return jnp.where(i < PROMPT_CHUNKS, i % per_seq == 0, i == PROMPT_CHUNKS)


def _scan_bwd_chunk(i):
    per_seq = SEQ // SSM_CHUNK
    return jnp.where(i < PROMPT_CHUNKS, (i // per_seq) * per_seq + (per_seq - 1 - i % per_seq),
                     PROMPT_CHUNKS + SCAN_CHUNKS - 1 - i)


def _scan_group(i):
    return jnp.where(i < PROMPT_CHUNKS, i // (SEQ // SSM_CHUNK), SSM_GROUPS_PROMPT)


def _split3(x):
    hi = x.astype(BF16)
    r1 = x - hi.astype(F32)
    mid = r1.astype(BF16)
    lo = (r1 - mid.astype(F32)).astype(BF16)
    return hi, mid, lo


def _ssm_kernel(uf_ref, ub_ref, a_ref, bf_ref, bb_ref, cf_ref, cb_ref, jin_ref, jout_ref, h0_ref,
                yf_ref, yb_ref, hfin_ref, bu_ref, st_ref, lhs_ref, yy_ref):
    @pl.when(_scan_chunk_first(pl.program_id(0)))
    def _():
        st_ref[...] = h0_ref[0]

    ub_rev = jnp.dot(jin_ref[...], ub_ref[...].astype(BF16), preferred_element_type=F32)
    gap = jnp.zeros((SSM_PITCH - SSM_CHUNK, SSM_WIDTH), F32)
    for s in range(SSM_SEQS):
        cols = slice(s * SSM_WIDTH, (s + 1) * SSM_WIDTH)
        lhs_ref[s * SSM_PITCH:s * SSM_PITCH + SSM_CHUNK, :] = uf_ref[:, cols]
        lhs_ref[s * SSM_PITCH + SSM_CHUNK:(s + 1) * SSM_PITCH, :] = gap
        lhs_ref[SSM_HALF + s * SSM_PITCH:SSM_HALF + s * SSM_PITCH + SSM_CHUNK, :] = ub_rev[:, cols]
        lhs_ref[SSM_HALF + s * SSM_PITCH + SSM_CHUNK:SSM_HALF + (s + 1) * SSM_PITCH, :] = gap
    bu_f = jnp.dot(lhs_ref[:SSM_HALF, :].astype(BF16), bf_ref[...], preferred_element_type=F32)
    bu_b = jnp.dot(lhs_ref[SSM_HALF:, :].astype(BF16), bb_ref[...], preferred_element_type=F32)
    for k in range(SSM_SLABS):
        bu_ref[k, :SSM_HALF, :] = bu_f[:, k * LANE:(k + 1) * LANE]
        bu_ref[k, SSM_HALF:, :] = bu_b[:, k * LANE:(k + 1) * LANE]

    half = SSM_SLABS // 2
    a_re = [a_ref[:, k * LANE:(k + 1) * LANE] for k in range(half)]
    a_im = [a_ref[:, SSM_COLS + k * LANE:SSM_COLS + (k + 1) * LANE] for k in range(half)]

    def step(t, carry):
        rows = pl.ds(t, SSM_ROWS, stride=SSM_PITCH)
        new_re, new_im = [], []
        for k in range(half):
            h_re, h_im = carry[k], carry[half + k]
            n_re = a_re[k] * h_re - a_im[k] * h_im + bu_ref[k, rows, :]
            n_im = a_re[k] * h_im + a_im[k] * h_re + bu_ref[half + k, rows, :]
            bu_ref[k, rows, :] = n_re
            bu_ref[half + k, rows, :] = n_im
            new_re.append(n_re)
            new_im.append(n_im)
        return tuple(new_re + new_im)

    init = tuple(st_ref[:, k * LANE:(k + 1) * LANE] for k in range(SSM_SLABS))
    fin = lax.fori_loop(0, SSM_CHUNK, step, init)
    for k in range(SSM_SLABS):
        st_ref[:, k * LANE:(k + 1) * LANE] = fin[k]
    hfin_ref[0] = st_ref[...]

    h_f = jnp.concatenate([bu_ref[k, :SSM_HALF, :] for k in range(SSM_SLABS)], axis=1).astype(BF16)
    h_b = jnp.concatenate([bu_ref[k, SSM_HALF:, :] for k in range(SSM_SLABS)], axis=1).astype(BF16)
    yy_ref[...] = jnp.dot(h_f, cf_ref[...], preferred_element_type=F32)
    for s in range(SSM_SEQS):
        yf_ref[:, s * SSM_WIDTH:(s + 1) * SSM_WIDTH] = yy_ref[s * SSM_PITCH:s * SSM_PITCH + SSM_CHUNK, :]
    y_b = jnp.dot(h_b, cb_ref[...], preferred_element_type=F32)
    y_nat = sum(jnp.dot(jout_ref[...], piece, preferred_element_type=F32) for piece in _split3(y_b))
    for s in range(SSM_SEQS):
        yb_ref[:, s * SSM_WIDTH:(s + 1) * SSM_WIDTH] = y_nat[s * SSM_CHUNK:(s + 1) * SSM_CHUNK, :]


def _ssm_scan(layer, su8, h0_sample, a_rows, b_mat, c_mat):
    rows = SSM_ROWS
    width = SSM_SEQS * SSM_WIDTH
    n_groups = SSM_GROUPS_PROMPT + 1
    hs = h0_sample.astype(F32).reshape(SSM_SEQS, 2, SSM_COLS, 2).transpose(1, 0, 3, 2).reshape(1, rows, SSM_LANES)
    hh = jnp.concatenate([jnp.zeros((SSM_GROUPS_PROMPT, rows, SSM_LANES), F32), hs], axis=0)
    j_in = jnp.asarray(np.eye(SSM_CHUNK, dtype=np.float32)[::-1], dtype=BF16)
    sel = np.zeros((SSM_SEQS * SSM_CHUNK, SSM_HALF), np.float32)
    for s in range(SSM_SEQS):
        for t in range(SSM_CHUNK):
            sel[s * SSM_CHUNK + t, s * SSM_PITCH + SSM_CHUNK - 1 - t] = 1.0
    j_out = jnp.asarray(sel, dtype=BF16)
    fwd = lambda i: (i, 0)
    bwd = lambda i: (_scan_bwd_chunk(i), 0)
    grp = lambda i: (_scan_group(i), 0, 0)
    const = lambda i: (0, 0)
    y_f, y_b, hfin = pl.pallas_call(
        _ssm_kernel,
        grid=(SCAN_CHUNKS,),
        in_specs=[
            pl.BlockSpec((SSM_CHUNK, width), fwd),
            pl.BlockSpec((SSM_CHUNK, width), bwd),
            _layer_spec(rows, SSM_LANES, layer),
            _layer_spec(SSM_WIDTH, SSM_LANES, layer, row_block=0),
            _layer_spec(SSM_WIDTH, SSM_LANES, layer, row_block=1),
            _layer_spec(SSM_LANES, SSM_WIDTH, layer, col_block=0),
            _layer_spec(SSM_LANES, SSM_WIDTH, layer, col_block=1),
            pl.BlockSpec((SSM_CHUNK, SSM_CHUNK), const),
            pl.BlockSpec((SSM_SEQS * SSM_CHUNK, SSM_HALF), const),
            pl.BlockSpec((1, rows, SSM_LANES), grp),
        ],
        out_specs=[
            pl.BlockSpec((SSM_CHUNK, width), fwd),
            pl.BlockSpec((SSM_CHUNK, width), bwd),
            pl.BlockSpec((1, rows, SSM_LANES), grp),
        ],
        out_shape=[
            jax.ShapeDtypeStruct((SCAN_T, width), F32),
            jax.ShapeDtypeStruct((SCAN_T, width), F32),
            jax.ShapeDtypeStruct((n_groups, rows, SSM_LANES), F32),
        ],
        scratch_shapes=[pltpu.VMEM((SSM_SLABS, 2 * SSM_HALF, LANE), F32), pltpu.VMEM((rows, SSM_LANES), F32),
                        pltpu.VMEM((2 * SSM_HALF, SSM_WIDTH), F32), pltpu.VMEM((SSM_HALF, SSM_WIDTH), F32)],
        compiler_params=_params(("arbitrary",)),
        name="ssm_scan",
    )(su8, su8, a_rows, b_mat, b_mat, c_mat, c_mat, j_in, j_out, hh)
    fin = hfin[:SSM_GROUPS_PROMPT].reshape(SSM_GROUPS_PROMPT, 2, SSM_SEQS, 2, SSM_GROUPS, SSM_STATE)
    fin = fin.transpose(0, 2, 1, 4, 5, 3).reshape(BATCH, 2, SSM_GROUPS, SSM_STATE, 2)
    return y_f, y_b, fin


def _ssm_matrices(a_re, a_im, log_dt, b_re, b_im, c_re, c_im):
    n_l = a_re.shape[0]
    lam = lax.complex(a_re.astype(F32), a_im.astype(F32))
    dt = jnp.exp(log_dt.astype(F32))[..., None]
    a_bar = jnp.exp(lam * dt)
    b_bar = ((a_bar - 1.0) / lam)[..., None] * lax.complex(b_re.astype(F32), b_im.astype(F32))
    a_dir = jnp.concatenate([jnp.real(a_bar).reshape(n_l, 2, SSM_COLS),
                             jnp.imag(a_bar).reshape(n_l, 2, SSM_COLS)], axis=-1)
    a_rows = jnp.repeat(a_dir, SSM_SEQS, axis=1)
    eye = jnp.eye(SSM_GROUPS, dtype=F32)
    bt = jnp.transpose(b_bar, (0, 1, 2, 4, 3))
    b_real = jnp.einsum('ldghp,ge->ldghep', jnp.real(bt), eye).reshape(n_l, 2 * SSM_WIDTH, SSM_COLS)
    b_imag = jnp.einsum('ldghp,ge->ldghep', jnp.imag(bt), eye).reshape(n_l, 2 * SSM_WIDTH, SSM_COLS)
    b_mat = jnp.concatenate([b_real, b_imag], axis=-1).astype(BF16)
    cr = jnp.transpose(c_re.astype(F32), (0, 1, 2, 4, 3))
    ci = jnp.transpose(c_im.astype(F32), (0, 1, 2, 4, 3))
    c_real = jnp.einsum('ldgph,ge->lgpdeh', cr, eye).reshape(n_l, SSM_COLS, 2 * SSM_WIDTH)
    c_imag = jnp.einsum('ldgph,ge->lgpdeh', -ci, eye).reshape(n_l, SSM_COLS, 2 * SSM_WIDTH)
    c_mat = jnp.concatenate([c_real, c_imag], axis=1).astype(BF16)
    return a_rows, b_mat, c_mat


def _merge_kernel(xp_ref, xs_ref, mod_ref, g1_ref, wg_ref, oap_ref, oas_ref, opp_ref, ops_ref, su_ref, yf_ref, yb_ref,
                  d_ref, wglu_ref, bglu_ref, wua_ref, wup_ref, wus_ref, wo_ref, o_ref):
    x = _stream_rows(xp_ref, xs_ref)
    h = _norm_mod(x, g1_ref[...], mod_ref[0, 0:1, :], mod_ref[0, 1:2, :])
    gates = _sigmoid(jnp.dot(h.astype(BF16), wg_ref[...], preferred_element_type=F32))
    y = d_ref[...] * su_ref[...] + yf_ref[...] + yb_ref[...]
    y = 0.5 * y * (1.0 + jnp.tanh(math.sqrt(2.0 / math.pi) * (y + 0.044715 * (y * y * y))))
    glu = jnp.dot(y.astype(BF16), wglu_ref[...], preferred_element_type=F32) + bglu_ref[...]
    o_ssm = y * _sigmoid(glu)
    o_attn = _stream_rows(oap_ref, oas_ref)
    o_pool = _stream_rows(opp_ref, ops_ref)
    m = (gates[:, :D_MODEL] * jnp.dot(o_attn, wua_ref[...], preferred_element_type=F32)
         + gates[:, D_MODEL:2 * D_MODEL] * jnp.dot(o_pool, wup_ref[...], preferred_element_type=F32)
         + gates[:, 2 * D_MODEL:] * jnp.dot(o_ssm.astype(BF16), wus_ref[...], preferred_element_type=F32))
    o_ref[...] = x + mod_ref[0, 2:3, :] * jnp.dot(m.astype(BF16), wo_ref[...], preferred_element_type=F32)


def _merge(layer, x_pair, mod, norm1_g, w_gates, oa_p, oa_s, op_p, op_s, ssm_u, y_f, y_b, d_skip, w_glu, b_glu,
           w_up_attn, w_up_pool, w_up_ssm, w_out):
    tm = SEQ_TILE
    row = lambda i: (i, 0)
    const = lambda i: (0, 0)
    return pl.pallas_call(
        _merge_kernel,
        grid=(N_TOK // tm,),
        in_specs=_stream_specs(D_MODEL, x_pair[2]) + [
            _mod_spec(layer, tm),
            _layer_spec(1, D_MODEL, layer),
            pl.BlockSpec((D_MODEL, GATE_WIDTH), const),
        ] + _stream_specs(ATTN_WIDTH, 0) + _stream_specs(POOL_WIDTH, 0) + [
            pl.BlockSpec((tm, SSM_WIDTH), _scan_block),
            pl.BlockSpec((tm, SSM_WIDTH), _scan_block),
            pl.BlockSpec((tm, SSM_WIDTH), _scan_block),
            _layer_spec(1, SSM_WIDTH, layer),
            _layer_spec(SSM_WIDTH, SSM_WIDTH, layer),
            _layer_spec(1, SSM_WIDTH, layer),
            _layer_spec(ATTN_WIDTH, D_MODEL, layer),
            _layer_spec(POOL_WIDTH, D_MODEL, layer),
            _layer_spec(SSM_WIDTH, D_MODEL, layer),
            _layer_spec(D_MODEL, D_MODEL, layer),
        ],
        out_specs=pl.BlockSpec((tm, D_MODEL), row),
        out_shape=jax.ShapeDtypeStruct((N_TOK, D_MODEL), F32),
        compiler_params=_params(("parallel",)),
        name="merge_branches",
    )(x_pair[0], x_pair[1], mod, norm1_g, w_gates, oa_p, oa_s, op_p, op_s, ssm_u, y_f, y_b, d_skip, w_glu, b_glu,
      w_up_attn, w_up_pool, w_up_ssm, w_out)


def _router_kernel(x_ref, mod_ref, g_ref, wrh_ref, wrl_ref, rb_ref, h_ref, gw_ref):
    h2 = _norm_mod(x_ref[...], g_ref[...], mod_ref[0, 3:4, :], mod_ref[0, 4:5, :])
    h_hi = h2.astype(BF16)
    h_ref[...] = h_hi
    h_lo = (h2 - h_hi.astype(F32)).astype(BF16)
    logits = (jnp.dot(h_hi, wrh_ref[...], preferred_element_type=F32)
              + jnp.dot(h_lo, wrh_ref[...], preferred_element_type=F32)
              + jnp.dot(h_hi, wrl_ref[...], preferred_element_type=F32))
    scores = _sigmoid(logits)
    sc_t = jnp.transpose(scores)[:N_EXPERTS, :]
    work = jnp.transpose(scores + rb_ref[...])[:N_EXPERTS, :]
    idx = lax.broadcasted_iota(jnp.int32, work.shape, 0).astype(F32)
    w_t = jnp.zeros_like(sc_t)
    for _ in range(TOP_K):
        mx = jnp.max(work, axis=0, keepdims=True)
        first = jnp.min(jnp.where(work == mx, idx, float(N_EXPERTS)), axis=0, keepdims=True)
        pick = idx == first
        w_t = jnp.where(pick, sc_t, w_t)
        work = jnp.where(pick, -jnp.inf, work)
    w_t = w_t / jnp.sum(w_t, axis=0, keepdims=True) * ROUTE_SCALE
    w_pad = jnp.concatenate([w_t, jnp.zeros((ROUTER_LANES - N_EXPERTS, w_t.shape[1]), F32)], axis=0)
    gw_ref[...] = jnp.transpose(w_pad)


def _route(layer, x, mod, norm_g, w_router, router_bias):
    tm = 512
    row = lambda i: (i, 0)
    const = lambda i: (0, 0)
    w_hi = w_router.astype(BF16)
    w_lo = (w_router - w_hi.astype(F32)).astype(BF16)
    return pl.pallas_call(
        _router_kernel,
        grid=(N_TOK // tm,),
        in_specs=[
            pl.BlockSpec((tm, D_MODEL), row),
            _mod_spec(layer, tm),
            _layer_spec(1, D_MODEL, layer),
            pl.BlockSpec((D_MODEL, ROUTER_LANES), const),
            pl.BlockSpec((D_MODEL, ROUTER_LANES), const),
            pl.BlockSpec((1, ROUTER_LANES), const),
        ],
        out_specs=[pl.BlockSpec((tm, D_MODEL), row), pl.BlockSpec((tm, ROUTER_LANES), row)],
        out_shape=[jax.ShapeDtypeStruct((N_TOK, D_MODEL), BF16),
                   jax.ShapeDtypeStruct((N_TOK, ROUTER_LANES), F32)],
        compiler_params=_params(("parallel",)),
        name="moe_router",
    )(x, mod, norm_g, w_hi, w_lo, router_bias)


def _swiglu_mid(h, w_gate, w_up):
    a = jnp.dot(h, w_gate, preferred_element_type=F32)
    b = jnp.dot(h, w_up, preferred_element_type=F32)
    return (a * _sigmoid(a)) * b


def _experts_kernel(h_ref, gw_ref, wg_ref, wu_ref, wd_ref, sg_ref, su_ref, sd_ref, x_ref, mod_ref, fg_ref,
                    o_ref, *, final):
    j = pl.program_id(1)
    h = h_ref[...]

    @pl.when(j == 0)
    def _():
        mid = _swiglu_mid(h, sg_ref[...], su_ref[...]).astype(BF16)
        o_ref[...] = jnp.dot(mid, sd_ref[...], preferred_element_type=F32)

    gw = pltpu.roll(gw_ref[...], (ROUTER_LANES - j * EXPERT_CHUNK) & (ROUTER_LANES - 1), axis=1)
    mids = [(_swiglu_mid(h, wg_ref[e], wu_ref[e]) * gw[:, e:e + 1]).astype(BF16) for e in range(EXPERT_CHUNK)]
    mid = jnp.concatenate(mids, axis=1)
    wd = wd_ref[...].reshape(EXPERT_CHUNK * EXPERT_DIM, D_MODEL)
    o_ref[...] += jnp.dot(mid, wd, preferred_element_type=F32)

    @pl.when(j == pl.num_programs(1) - 1)
    def _():
        x = x_ref[...] + mod_ref[0, 5:6, :] * o_ref[...]
        if final:
            ms = jnp.mean(x * x, axis=-1, keepdims=True)
            x = x * lax.rsqrt(ms + EPS) * fg_ref[...]
        o_ref[...] = x


def _experts(layer, h, gate_w, wg, wu, wd, sg, su, sd, x, mod, final_g, final, row0=0, n_rows=N_TOK):
    tm = 1024
    n_chunks = N_EXPERTS // EXPERT_CHUNK
    base = row0 // tm
    row = lambda i, j: (base + i, 0)
    const = lambda i, j: (0, 0)
    chunk = lambda i, j: (j, 0, 0)
    return pl.pallas_call(
        functools.partial(_experts_kernel, final=final),
        grid=(n_rows // tm, n_chunks),
        in_specs=[
            pl.BlockSpec((tm, D_MODEL), row),
            pl.BlockSpec((tm, ROUTER_LANES), row),
            pl.BlockSpec((EXPERT_CHUNK, D_MODEL, EXPERT_DIM), chunk),
            pl.BlockSpec((EXPERT_CHUNK, D_MODEL, EXPERT_DIM), chunk),
            pl.BlockSpec((EXPERT_CHUNK, EXPERT_DIM, D_MODEL), chunk),
            _layer_spec(D_MODEL, SHARED_DIM, layer, rank=2),
            _layer_spec(D_MODEL, SHARED_DIM, layer, rank=2),
            _layer_spec(SHARED_DIM, D_MODEL, layer, rank=2),
            pl.BlockSpec((tm, D_MODEL), row),
            _mod_spec(layer, tm, rank=2, base=base),
            pl.BlockSpec((1, D_MODEL), const),
        ],
        out_specs=pl.BlockSpec((tm, D_MODEL), lambda i, j: (i, 0)),
        out_shape=jax.ShapeDtypeStruct((n_rows, D_MODEL), F32),
        compiler_params=_params(("parallel", "arbitrary")),
        name="moe_experts",
    )(h, gate_w, wg, wu, wd, sg, su, sd, x, mod, final_g)


def kernel(x_prompt, x_sample, c, cache_k, cache_v, state_ssm, c_ctx, w_mod, b_mod, norm1_g, norm2_g, w_in, q_norm_g, k_norm_g, w_up_attn, pool_w, pool_scale, w_up_pool, ssm_a_re, ssm_a_im, ssm_log_dt, ssm_b_re, ssm_b_im, ssm_c_re, ssm_c_im, ssm_d, w_glu, b_glu, w_up_ssm, w_out, w_router, router_bias, w_gate, w_up, w_down, ws_gate, ws_up, ws_down, final_norm_g):
    x_pair = (x_prompt.reshape(N_PROMPT, D_MODEL), x_sample.reshape(N_SAMPLE, D_MODEL), 0)
    cond = jnp.concatenate([c_ctx[None, :], c, jnp.zeros((COND_ROWS - N_COND, D_MODEL), F32)], axis=0)
    mod_all = _ada_all(cond, w_mod, b_mod)

    cos_t, sin_t = _rope_tables(SEQ_TILE)
    head_avg = jnp.asarray(np.kron(np.eye(LANE // HEAD_DIM, dtype=np.float32),
                                   np.full((HEAD_DIM, HEAD_DIM), 1.0 / HEAD_DIM, np.float32)), BF16)
    final_g = final_norm_g.reshape(1, D_MODEL)
    norm1 = norm1_g.reshape(DEPTH, 1, D_MODEL)
    norm2 = norm2_g.reshape(DEPTH, 1, D_MODEL)
    p_scale = pool_scale.reshape(DEPTH, 1, POOL_WIDTH)
    d_skip = ssm_d.reshape(DEPTH, 1, SSM_WIDTH)
    glu_b = b_glu.reshape(DEPTH, 1, SSM_WIDTH)
    glu_w, up_attn, up_pool, up_ssm, out_w = (w.astype(BF16) for w in (w_glu, w_up_attn, w_up_pool, w_up_ssm, w_out))
    shared = tuple(w.astype(BF16) for w in (ws_gate, ws_up, ws_down))
    a_rows, b_mat, c_mat = _ssm_matrices(ssm_a_re, ssm_a_im, ssm_log_dt, ssm_b_re, ssm_b_im, ssm_c_re, ssm_c_im)

    new_k, new_v, new_s = [], [], []
    for l in range(DEPTH):
        qk_gain = jnp.concatenate([jnp.tile(q_norm_g[l], N_HEADS), jnp.tile(k_norm_g[l], N_KV_HEADS)])[None, :]
        q, k, v, pool_u, ssm_u = _in_project(
            l, x_pair, mod_all, norm1, w_in[l][:, :MIX_WIDTH].astype(BF16), head_avg, qk_gain, cos_t, sin_t)

        k_p = k[:N_PROMPT].reshape(BATCH, SEQ, N_KV_HEADS, HEAD_DIM)
        v_p = v[:N_PROMPT].reshape(BATCH, SEQ, N_KV_HEADS, HEAD_DIM)
        k_s = k[N_PROMPT:].reshape(DEC_BATCH, DEC_SEQ, N_KV_HEADS, HEAD_DIM)
        v_s = v[N_PROMPT:].reshape(DEC_BATCH, DEC_SEQ, N_KV_HEADS, HEAD_DIM)
        new_k.append(k_p)
        new_v.append(v_p)
        keys = jnp.concatenate([cache_k[:, l], k_s], axis=1)
        vals = jnp.concatenate([cache_v[:, l], v_s], axis=1)
        oa_p = _attention(q, k_p, v_p, BATCH, SEQ, 0)
        oa_s, wg, wu, wd = _attention(q, keys, vals, DEC_BATCH, DEC_SEQ, N_PROMPT, cast=(l, [w_gate, w_up, w_down]))

        pool_bd = jax.scipy.linalg.block_diag(*[pool_w[l, g] for g in range(len(POOL_WINDOWS))]).astype(BF16)
        op_p = _pool(l, pool_u, pool_bd, p_scale, BATCH, SEQ, 0)
        op_s = _pool(l, pool_u, pool_bd, p_scale, DEC_BATCH, DEC_SEQ, N_PROMPT)

        y_f, y_b, st = _ssm_scan(l, ssm_u, state_ssm[:, l], a_rows, b_mat, c_mat)
        new_s.append(st)

        w_r = jnp.pad(w_router[l], ((0, 0), (0, ROUTER_LANES - N_EXPERTS)))
        r_b = jnp.concatenate([router_bias[l], jnp.full((ROUTER_LANES - N_EXPERTS,), -jnp.inf, F32)])[None, :]
        x = _merge(
            l, x_pair, mod_all, norm1, w_in[l][:, MIX_WIDTH:].astype(BF16), oa_p, oa_s, op_p, op_s, ssm_u,
            y_f, y_b, d_skip, glu_w, glu_b, up_attn, up_pool, up_ssm, out_w)
        h2, gate_w = _route(l, x, mod_all, norm2, w_r, r_b)
        if l < DEPTH - 1:
            x = _experts(l, h2, gate_w, wg, wu, wd, *shared, x, mod_all, final_g, final=False)
            x_pair = (x, x, N_PROMPT // SEQ_TILE)
        else:
            y_prompt = _experts(l, h2, gate_w, wg, wu, wd, *shared, x, mod_all, final_g, True, 0, N_PROMPT)
            y_sample = _experts(l, h2, gate_w, wg, wu, wd, *shared, x, mod_all, final_g, True, N_PROMPT, N_SAMPLE)

    y_prompt = y_prompt.reshape(BATCH, SEQ, D_MODEL)
    y_sample = y_sample.reshape(DEC_BATCH, DEC_SEQ, D_MODEL)
    return (y_prompt, y_sample, jnp.stack(new_k, axis=1), jnp.stack(new_v, axis=1), jnp.stack(new_s, axis=1))
```

```python
import functools
import math

import jax
import jax.numpy as jnp
import numpy as np
from jax import lax
from jax.experimental import pallas as pl
from jax.experimental.pallas import tpu as pltpu

D_MODEL = 1024
BATCH = 32
SEQ = 256
DEPTH = 2
DEC_BATCH = 4
DEC_SEQ = 4096
PAST_LEN = 256
GRID_W = 64
EPS = 1e-6
N_MOD = 6
HEAD_DIM = 64
N_HEADS = 8
N_KV_HEADS = 2
ATTN_WIDTH = N_HEADS * HEAD_DIM
KV_WIDTH = N_KV_HEADS * HEAD_DIM
ROPE_BASE = 10000.0
ROPE_PAIRS_PER_AXIS = HEAD_DIM // 4
POOL_WINDOWS = (2, 4, 8, 16)
POOL_GROUP = 64
POOL_WIDTH = len(POOL_WINDOWS) * POOL_GROUP
SSM_H = 16
SSM_GROUPS = 16
SSM_WIDTH = SSM_H * SSM_GROUPS
SSM_STATE = 64
N_EXPERTS = 64
TOP_K = 8
EXPERT_DIM = 256
SHARED_DIM = 256
ROUTE_SCALE = 2.5

N_PROMPT = BATCH * SEQ
N_SAMPLE = DEC_BATCH * DEC_SEQ
N_TOK = N_PROMPT + N_SAMPLE
N_COND = 1 + DEC_BATCH
COND_ROWS = 8
QK_WIDTH = ATTN_WIDTH + KV_WIDTH
MIX_WIDTH = QK_WIDTH + KV_WIDTH + POOL_WIDTH + SSM_WIDTH
GATE_WIDTH = 3 * D_MODEL
SSM_COLS = SSM_GROUPS * SSM_STATE
SSM_LANES = 2 * SSM_COLS
SSM_SEQS = 4
SSM_CHUNK = 128
ROUTER_LANES = 128
EXPERT_CHUNK = 4

SEQ_TILE = SEQ
SSM_ROWS = 2 * SSM_SEQS
SSM_GROUPS_PROMPT = BATCH // SSM_SEQS
SCAN_T = SSM_GROUPS_PROMPT * SEQ + DEC_SEQ
SCAN_CHUNKS = SCAN_T // SSM_CHUNK
PROMPT_CHUNKS = SSM_GROUPS_PROMPT * SEQ // SSM_CHUNK
SSM_PITCH = SSM_CHUNK + 4
SSM_HALF = SSM_SEQS * SSM_PITCH

LANE = 128
SSM_SLABS = SSM_LANES // LANE
VMEM_LIMIT = 56 * 1024 * 1024

F32 = jnp.float32
BF16 = jnp.bfloat16
HIGHEST = lax.Precision.HIGHEST


def _sigmoid(x):
    return 1.0 / (1.0 + jnp.exp(-x))


def _params(dims, vmem=VMEM_LIMIT):
    return pltpu.CompilerParams(dimension_semantics=dims, vmem_limit_bytes=vmem)


def _mod_index(i, tm):
    p = N_PROMPT // tm
    t = DEC_SEQ // tm
    return jnp.where(i < p, 0, 1 + (i - p) // t)


def _mod_spec(layer, tm, rank=1, base=0):
    if rank == 1:
        return pl.BlockSpec((None, 1, N_MOD, D_MODEL), lambda i: (layer, _mod_index(base + i, tm), 0, 0))
    return pl.BlockSpec((None, 1, N_MOD, D_MODEL), lambda i, j: (layer, _mod_index(base + i, tm), 0, 0))


def _layer_spec(rows, cols, layer, rank=1, row_block=0, col_block=0):
    if rank == 1:
        return pl.BlockSpec((None, rows, cols), lambda i: (layer, row_block, col_block))
    return pl.BlockSpec((None, rows, cols), lambda i, j: (layer, row_block, col_block))


def _scan_block(i):
    k = i - BATCH
    tiles = DEC_SEQ // SEQ_TILE
    rb = jnp.where(i < BATCH, i // SSM_SEQS, SSM_GROUPS_PROMPT + k % tiles)
    slot = jnp.where(i < BATCH, i % SSM_SEQS, k // tiles)
    return rb, slot


def _norm_mod(x, g, shift, scale):
    ms = jnp.mean(x * x, axis=-1, keepdims=True)
    return (x * lax.rsqrt(ms + EPS) * g) * (1.0 + scale) + shift


def _stream_rows(xp_ref, xs_ref):
    return jnp.where(pl.program_id(0) < N_PROMPT // SEQ_TILE, xp_ref[...], xs_ref[...])


def _stream_specs(width, s_base):
    p_tiles = N_PROMPT // SEQ_TILE
    return [pl.BlockSpec((SEQ_TILE, width), lambda i: (jnp.minimum(i, p_tiles - 1), 0)),
            pl.BlockSpec((SEQ_TILE, width), lambda i: (s_base + jnp.maximum(i - p_tiles, 0), 0))]


def _mod_kernel(cond_ref, w_ref, b_ref, o_ref):
    c = cond_ref[...]
    s = c * _sigmoid(c)
    o_ref[0] = jnp.dot(s, w_ref[0], preferred_element_type=F32, precision=HIGHEST) + b_ref[0]


def _ada_all(cond, w_mod, b_mod):
    tn = 1536
    width = N_MOD * D_MODEL
    out = pl.pallas_call(
        _mod_kernel,
        grid=(DEPTH, width // tn),
        in_specs=[
            pl.BlockSpec((COND_ROWS, D_MODEL), lambda l, j: (0, 0)),
            pl.BlockSpec((1, D_MODEL, tn), lambda l, j: (l, 0, j)),
            pl.BlockSpec((1, 1, tn), lambda l, j: (l, 0, j)),
        ],
        out_specs=pl.BlockSpec((1, COND_ROWS, tn), lambda l, j: (l, 0, j)),
        out_shape=jax.ShapeDtypeStruct((DEPTH, COND_ROWS, width), F32),
        compiler_params=_params(("parallel", "parallel")),
        name="ada_mod",
    )(cond, w_mod, b_mod.reshape(DEPTH, 1, width))
    return out.reshape(DEPTH, COND_ROWS, N_MOD, D_MODEL)


def _kv_layouts(k, v):
    kt = jnp.transpose(k).astype(BF16)
    lane = lax.broadcasted_iota(jnp.int32, v.shape, 1)
    first = lane < HEAD_DIM
    v_sw = pltpu.roll(v, HEAD_DIM, axis=1)
    one = jnp.ones_like(v)
    kts, ves, vos = [], [], []
    for g in range(N_KV_HEADS):
        kg = kt[g * HEAD_DIM:(g + 1) * HEAD_DIM, :]
        kts.append(jnp.concatenate([kg, kg], axis=0))
        lo_half = v if g == 0 else v_sw
        hi_half = v_sw if g == 0 else v
        ves.append(jnp.where(first, lo_half, one).astype(BF16))
        vos.append(jnp.where(first, one, hi_half).astype(BF16))
    return kts, ves, vos


def _inproj_kernel(xp_ref, xs_ref, mod_ref, g_ref, w_ref, bd_ref, qkg_ref, cos_ref, sin_ref, *rest):
    (q_ref, kp_ref, vp_ref, pu_ref, su_ref, ktp_ref, vep_ref, vop_ref, kts_ref, ves_ref, vos_ref) = rest[3:]
    h = _norm_mod(_stream_rows(xp_ref, xs_ref), g_ref[...], mod_ref[0, 0:1, :], mod_ref[0, 1:2, :])
    z = jnp.dot(h.astype(BF16), w_ref[...], preferred_element_type=F32)
    qk = z[:, :QK_WIDTH]
    qq = qk * qk
    hi = qq.astype(BF16)
    lo = (qq - hi.astype(F32)).astype(BF16)
    bd = bd_ref[...]
    ms = jnp.concatenate(
        [jnp.dot(hi[:, c * LANE:(c + 1) * LANE], bd, preferred_element_type=F32)
         + jnp.dot(lo[:, c * LANE:(c + 1) * LANE], bd, preferred_element_type=F32)
         for c in range(QK_WIDTH // LANE)], axis=1)
    qkn = qk * lax.rsqrt(ms + EPS) * qkg_ref[...]
    cos, sin = cos_ref[...], sin_ref[...]
    parts = []
    for c in range(QK_WIDTH // LANE):
        blk = qkn[:, c * LANE:(c + 1) * LANE]
        nxt = pltpu.roll(blk, LANE - 1, axis=1)
        prv = pltpu.roll(blk, 1, axis=1)
        lane = lax.broadcasted_iota(jnp.int32, blk.shape, 1)
        parts.append(blk * cos + jnp.where((lane & 1) == 0, nxt, prv) * sin)
    qkr = jnp.concatenate(parts, axis=1)
    q_ref[...] = (qkr[:, :ATTN_WIDTH] * (HEAD_DIM ** -0.5 * math.log2(math.e))).astype(BF16)
    k = qkr[:, ATTN_WIDTH:QK_WIDTH]
    v = z[:, QK_WIDTH:QK_WIDTH + KV_WIDTH]
    kts, ves, vos = _kv_layouts(k, v)
    is_prompt = pl.program_id(0) < N_PROMPT // SEQ_TILE

    @pl.when(is_prompt)
    def _():
        kp_ref[...] = k
        vp_ref[...] = v
        for g in range(N_KV_HEADS):
            ktp_ref[g], vep_ref[g], vop_ref[g] = kts[g], ves[g], vos[g]

    @pl.when(jnp.logical_not(is_prompt))
    def _():
        for g in range(N_KV_HEADS):
            kts_ref[g], ves_ref[g], vos_ref[g] = kts[g], ves[g], vos[g]

    pu_ref[...] = z[:, QK_WIDTH + KV_WIDTH:QK_WIDTH + KV_WIDTH + POOL_WIDTH]
    su_ref[...] = z[:, QK_WIDTH + KV_WIDTH + POOL_WIDTH:MIX_WIDTH]


def _in_project(layer, x_pair, mod, norm_g, w_mix, bd, qk_gain, cos_t, sin_t, cache_k, cache_v):
    tm = SEQ_TILE
    p_tiles = N_PROMPT // tm
    s_tiles = DEC_SEQ // tm
    n_keys = PAST_LEN + DEC_SEQ
    past_tiles = PAST_LEN // tm

    def rope_idx(i):
        return (jnp.where(i < p_tiles, 0, 1 + (i - p_tiles) % s_tiles), 0)

    kc = jnp.transpose(cache_k, (0, 2, 3, 1)).astype(BF16)
    kc = jnp.concatenate([kc, kc], axis=2).reshape(DEC_BATCH * N_KV_HEADS, LANE, PAST_LEN)
    vc = jnp.transpose(cache_v, (0, 2, 1, 3)).astype(BF16)
    ones = jnp.ones_like(vc)
    ve_c = jnp.concatenate([vc, ones], axis=3).reshape(DEC_BATCH * N_KV_HEADS, PAST_LEN, LANE)
    vo_c = jnp.concatenate([ones, vc], axis=3).reshape(DEC_BATCH * N_KV_HEADS, PAST_LEN, LANE)
    kt_s0 = jnp.pad(kc, ((0, 0), (0, 0), (0, DEC_SEQ)))
    ve_s0 = jnp.pad(ve_c, ((0, 0), (0, DEC_SEQ), (0, 0)))
    vo_s0 = jnp.pad(vo_c, ((0, 0), (0, DEC_SEQ), (0, 0)))

    p_blk = lambda i: jnp.minimum(i, p_tiles - 1)
    s_seq = lambda i: jnp.maximum(i - p_tiles, 0) // s_tiles
    s_blk = lambda i: past_tiles + jnp.maximum(i - p_tiles, 0) % s_tiles
    row = lambda i: (i, 0)
    const = lambda i: (0, 0)
    any_spec = pl.BlockSpec(memory_space=pl.ANY)
    outs = pl.pallas_call(
        _inproj_kernel,
        grid=(N_TOK // tm,),
        in_specs=_stream_specs(D_MODEL, x_pair[2]) + [
            _mod_spec(layer, tm),
            _layer_spec(1, D_MODEL, layer),
            pl.BlockSpec((D_MODEL, MIX_WIDTH), const),
            pl.BlockSpec((LANE, LANE), const),
            pl.BlockSpec((1, QK_WIDTH), const),
            pl.BlockSpec((tm, LANE), rope_idx),
            pl.BlockSpec((tm, LANE), rope_idx),
            any_spec, any_spec, any_spec,
        ],
        out_specs=[
            pl.BlockSpec((tm, ATTN_WIDTH), row),
            pl.BlockSpec((tm, KV_WIDTH), lambda i: (p_blk(i), 0)),
            pl.BlockSpec((tm, KV_WIDTH), lambda i: (p_blk(i), 0)),
            pl.BlockSpec((tm, POOL_WIDTH), row),
            pl.BlockSpec((tm, SSM_WIDTH), _scan_block),
            pl.BlockSpec((N_KV_HEADS, LANE, tm), lambda i: (p_blk(i), 0, 0)),
            pl.BlockSpec((N_KV_HEADS, tm, LANE), lambda i: (p_blk(i), 0, 0)),
            pl.BlockSpec((N_KV_HEADS, tm, LANE), lambda i: (p_blk(i), 0, 0)),
            pl.BlockSpec((N_KV_HEADS, LANE, tm), lambda i: (s_seq(i), 0, s_blk(i))),
            pl.BlockSpec((N_KV_HEADS, tm, LANE), lambda i: (s_seq(i), s_blk(i), 0)),
            pl.BlockSpec((N_KV_HEADS, tm, LANE), lambda i: (s_seq(i), s_blk(i), 0)),
        ],
        out_shape=[
            jax.ShapeDtypeStruct((N_TOK, ATTN_WIDTH), BF16),
            jax.ShapeDtypeStruct((N_PROMPT, KV_WIDTH), F32),
            jax.ShapeDtypeStruct((N_PROMPT, KV_WIDTH), F32),
            jax.ShapeDtypeStruct((N_TOK, POOL_WIDTH), F32),
            jax.ShapeDtypeStruct((SCAN_T, SSM_SEQS * SSM_WIDTH), F32),
            jax.ShapeDtypeStruct((BATCH * N_KV_HEADS, LANE, SEQ), BF16),
            jax.ShapeDtypeStruct((BATCH * N_KV_HEADS, SEQ, LANE), BF16),
            jax.ShapeDtypeStruct((BATCH * N_KV_HEADS, SEQ, LANE), BF16),
            jax.ShapeDtypeStruct((DEC_BATCH * N_KV_HEADS, LANE, n_keys), BF16),
            jax.ShapeDtypeStruct((DEC_BATCH * N_KV_HEADS, n_keys, LANE), BF16),
            jax.ShapeDtypeStruct((DEC_BATCH * N_KV_HEADS, n_keys, LANE), BF16),
        ],
        input_output_aliases={9: 8, 10: 9, 11: 10},
        compiler_params=_params(("arbitrary",)),
        name="in_project",
    )(x_pair[0], x_pair[1], mod, norm_g, w_mix, bd, qk_gain, cos_t, sin_t, kt_s0, ve_s0, vo_s0)
    return outs[0], outs[1], outs[2], outs[3], outs[4], tuple(outs[5:8]), tuple(outs[8:11])


def _rope_tables(tm):
    f32 = np.float32
    rows = DEC_SEQ // GRID_W
    row = np.repeat(np.arange(rows, dtype=f32), GRID_W)
    col = np.tile(np.arange(GRID_W, dtype=f32), rows)
    inv_freq = (f32(ROPE_BASE) ** (-np.arange(ROPE_PAIRS_PER_AXIS, dtype=f32) / f32(ROPE_PAIRS_PER_AXIS))).astype(f32)
    ang = np.concatenate([row[:, None] * inv_freq, col[:, None] * inv_freq], axis=-1).astype(f32)
    cos = np.repeat(np.cos(ang), 2, axis=-1)
    sin = np.repeat(np.sin(ang), 2, axis=-1) * np.tile(np.array([-1.0, 1.0], f32), HEAD_DIM // 2)
    cos = np.concatenate([np.ones((tm, HEAD_DIM), f32), cos], axis=0)
    sin = np.concatenate([np.zeros((tm, HEAD_DIM), f32), sin], axis=0)
    n_rep = LANE // HEAD_DIM
    return jnp.asarray(np.tile(cos, (1, n_rep)), F32), jnp.asarray(np.tile(sin, (1, n_rep)), F32)


def _attn_kernel(q_ref, kt_ref, ve_ref, vo_ref, *rest):
    n_cast = (len(rest) - 1) // 2
    o_ref = rest[n_cast]
    for src, dst in zip(rest[:n_cast], rest[n_cast + 1:]):
        dst[...] = src[...].astype(dst.dtype)
    kt = kt_ref[0]
    lane = lax.broadcasted_iota(jnp.int32, (q_ref.shape[0], LANE), 1)
    first = lane < HEAD_DIM
    slabs = []
    for pair in range(q_ref.shape[1] // LANE):
        q = q_ref[:, pair * LANE:(pair + 1) * LANE]
        halves = []
        for keep, v_ref in ((first, ve_ref), (lane >= HEAD_DIM, vo_ref)):
            qh = jnp.where(keep, q, jnp.zeros_like(q))
            s = jnp.dot(qh, kt, preferred_element_type=F32)
            m = jnp.max(s, axis=-1, keepdims=True)
            p = jnp.exp2(s - m).astype(BF16)
            a = jnp.dot(p, v_ref[0], preferred_element_type=F32)
            halves.append(a / pltpu.roll(a, HEAD_DIM, axis=1))
        slabs.append(jnp.where(first, halves[0], halves[1]))
    o_ref[...] = jnp.concatenate(slabs, axis=1).astype(o_ref.dtype)


def _attention(q, kv_ops, n_batch, n_q, row0, cast=None):
    kt, v_e, v_o = kv_ops
    n_keys = kt.shape[2]
    tq = 256
    q_tiles = n_q // tq
    base = row0 // tq
    group_w = ATTN_WIDTH // N_KV_HEADS
    kv_idx = lambda b, g, i: (b * N_KV_HEADS + g, 0, 0)
    in_specs = [
        pl.BlockSpec((tq, group_w), lambda b, g, i: (base + b * q_tiles + i, g)),
        pl.BlockSpec((1, LANE, n_keys), kv_idx),
        pl.BlockSpec((1, n_keys, LANE), kv_idx),
        pl.BlockSpec((1, n_keys, LANE), kv_idx),
    ]
    out_specs = [pl.BlockSpec((tq, group_w), lambda b, g, i: (b * q_tiles + i, g))]
    out_shape = [jax.ShapeDtypeStruct((n_batch * n_q, ATTN_WIDTH), BF16)]
    args = [q, kt, v_e, v_o]
    if cast is not None:
        layer, tensors = cast
        steps = n_batch * N_KV_HEADS * q_tiles
        step = lambda b, g, i: (b * N_KV_HEADS + g) * q_tiles + i
        for w in tensors:
            _, n_e, rows, cols = w.shape
            part = n_e * rows // steps
            assert part * steps == n_e * rows and part % 16 == 0 and rows % part == 0
            in_specs.append(pl.BlockSpec((1, part, cols), lambda b, g, i: (layer * steps + step(b, g, i), 0, 0)))
            out_specs.append(pl.BlockSpec((1, part, cols), lambda b, g, i: (step(b, g, i), 0, 0)))
            out_shape.append(jax.ShapeDtypeStruct((steps, part, cols), BF16))
            args.append(w.reshape(DEPTH * steps, part, cols))
    outs = pl.pallas_call(
        _attn_kernel,
        grid=(n_batch, N_KV_HEADS, q_tiles),
        in_specs=in_specs,
        out_specs=out_specs,
        out_shape=out_shape,
        compiler_params=_params(("parallel", "parallel", "parallel")),
        name="attention",
    )(*args)
    if cast is None:
        return outs[0]
    return [outs[0]] + [o.reshape(w.shape[1:]) for o, w in zip(outs[1:], cast[1])]


def _pool_kernel(u_ref, w_ref, sc_ref, o_ref, pad_ref, *, n_seq):
    halo = 8
    u = u_ref[...]
    zeros = jnp.zeros((halo, POOL_WIDTH), F32)
    pad_ref[0:halo, :] = zeros
    pad_ref[halo + n_seq:2 * halo + n_seq, :] = zeros
    pad_ref[halo:halo + n_seq, :] = u

    def sh(j):
        return pad_ref[halo + j:halo + j + n_seq, :]

    t2 = sh(-1) + u
    t4 = t2 + sh(-2) + sh(1)
    t8 = t4 + sh(-4) + sh(-3) + sh(2) + sh(3)
    t16 = t8 + sh(-8) + sh(-7) + sh(-6) + sh(-5) + sh(4) + sh(5) + sh(6) + sh(7)
    grp = lax.broadcasted_iota(jnp.int32, u.shape, 1) >> 6
    t = lax.broadcasted_iota(jnp.int32, u.shape, 0)
    tot = jnp.where(grp == 0, t2, jnp.where(grp == 1, t4, jnp.where(grp == 2, t8, t16)))
    half = jnp.where(grp == 0, 1, jnp.where(grp == 1, 2, jnp.where(grp == 2, 4, 8)))
    lo = jnp.maximum(t - half, 0)
    hi = jnp.minimum(t + half, n_seq)
    pooled = tot / (hi - lo).astype(F32) - u
    mixed = jnp.dot(pooled.astype(BF16), w_ref[...], preferred_element_type=F32)
    o_ref[...] = (mixed * sc_ref[...]).astype(o_ref.dtype)


def _pool(layer, u, w_bd, scale, n_batch, n_seq, row0):
    base = row0 // n_seq
    return pl.pallas_call(
        functools.partial(_pool_kernel, n_seq=n_seq),
        grid=(n_batch,),
        in_specs=[
            pl.BlockSpec((n_seq, POOL_WIDTH), lambda b: (base + b, 0)),
            pl.BlockSpec((POOL_WIDTH, POOL_WIDTH), lambda b: (0, 0)),
            _layer_spec(1, POOL_WIDTH, layer),
        ],
        out_specs=pl.BlockSpec((n_seq, POOL_WIDTH), lambda b: (b, 0)),
        out_shape=jax.ShapeDtypeStruct((n_batch * n_seq, POOL_WIDTH), BF16),
        scratch_shapes=[pltpu.VMEM((n_seq + 16, POOL_WIDTH), F32)],
        compiler_params=_params(("parallel",)),
        name="pool_mixer",
    )(u, w_bd, scale)


def _scan_chunk_first(i):
    per_seq = SEQ // SSM_CHUNK
    return jnp.where(i < PROMPT_CHUNKS, i % per_seq == 0, i == PROMPT_CHUNKS)


def _scan_bwd_chunk(i):
    per_seq = SEQ // SSM_CHUNK
    return jnp.where(i < PROMPT_CHUNKS, (i // per_seq) * per_seq + (per_seq - 1 - i % per_seq),
                     PROMPT_CHUNKS + SCAN_CHUNKS - 1 - i)


def _scan_group(i):
    return jnp.where(i < PROMPT_CHUNKS, i // (SEQ // SSM_CHUNK), SSM_GROUPS_PROMPT)


def _split3(x):
    hi = x.astype(BF16)
    r1 = x - hi.astype(F32)
    mid = r1.astype(BF16)
    lo = (r1 - mid.astype(F32)).astype(BF16)
    return hi, mid, lo


def _ssm_kernel(uf_ref, ub_ref, a_ref, bf_ref, bb_ref, cf_ref, cb_ref, jin_ref, jout_ref, h0_ref,
                yf_ref, yb_ref, hfin_ref, bu_ref, st_ref, lhs_ref, yy_ref):
    @pl.when(_scan_chunk_first(pl.program_id(0)))
    def _():
        st_ref[...] = h0_ref[0]

    ub_rev = jnp.dot(jin_ref[...], ub_ref[...].astype(BF16), preferred_element_type=F32)
    gap = jnp.zeros((SSM_PITCH - SSM_CHUNK, SSM_WIDTH), F32)
    for s in range(SSM_SEQS):
        cols = slice(s * SSM_WIDTH, (s + 1) * SSM_WIDTH)
        lhs_ref[s * SSM_PITCH:s * SSM_PITCH + SSM_CHUNK, :] = uf_ref[:, cols]
        lhs_ref[s * SSM_PITCH + SSM_CHUNK:(s + 1) * SSM_PITCH, :] = gap
        lhs_ref[SSM_HALF + s * SSM_PITCH:SSM_HALF + s * SSM_PITCH + SSM_CHUNK, :] = ub_rev[:, cols]
        lhs_ref[SSM_HALF + s * SSM_PITCH + SSM_CHUNK:SSM_HALF + (s + 1) * SSM_PITCH, :] = gap
    bu_f = jnp.dot(lhs_ref[:SSM_HALF, :].astype(BF16), bf_ref[...], preferred_element_type=F32)
    bu_b = jnp.dot(lhs_ref[SSM_HALF:, :].astype(BF16), bb_ref[...], preferred_element_type=F32)
    for k in range(SSM_SLABS):
        bu_ref[k, :SSM_HALF, :] = bu_f[:, k * LANE:(k + 1) * LANE]
        bu_ref[k, SSM_HALF:, :] = bu_b[:, k * LANE:(k + 1) * LANE]

    half = SSM_SLABS // 2
    a_re = [a_ref[:, k * LANE:(k + 1) * LANE] for k in range(half)]
    a_im = [a_ref[:, SSM_COLS + k * LANE:SSM_COLS + (k + 1) * LANE] for k in range(half)]

    def step(t, carry):
        rows = pl.ds(t, SSM_ROWS, stride=SSM_PITCH)
        new_re, new_im = [], []
        for k in range(half):
            h_re, h_im = carry[k], carry[half + k]
            n_re = a_re[k] * h_re - a_im[k] * h_im + bu_ref[k, rows, :]
            n_im = a_re[k] * h_im + a_im[k] * h_re + bu_ref[half + k, rows, :]
            bu_ref[k, rows, :] = n_re
            bu_ref[half + k, rows, :] = n_im
            new_re.append(n_re)
            new_im.append(n_im)
        return tuple(new_re + new_im)

    init = tuple(st_ref[:, k * LANE:(k + 1) * LANE] for k in range(SSM_SLABS))
    fin = lax.fori_loop(0, SSM_CHUNK, step, init)
    for k in range(SSM_SLABS):
        st_ref[:, k * LANE:(k + 1) * LANE] = fin[k]
    hfin_ref[0] = st_ref[...]

    h_f = jnp.concatenate([bu_ref[k, :SSM_HALF, :] for k in range(SSM_SLABS)], axis=1).astype(BF16)
    h_b = jnp.concatenate([bu_ref[k, SSM_HALF:, :] for k in range(SSM_SLABS)], axis=1).astype(BF16)
    yy_ref[...] = jnp.dot(h_f, cf_ref[...], preferred_element_type=F32)
    for s in range(SSM_SEQS):
        yf_ref[:, s * SSM_WIDTH:(s + 1) * SSM_WIDTH] = yy_ref[s * SSM_PITCH:s * SSM_PITCH + SSM_CHUNK, :]
    y_b = jnp.dot(h_b, cb_ref[...], preferred_element_type=F32)
    y_nat = sum(jnp.dot(jout_ref[...], piece, preferred_element_type=F32) for piece in _split3(y_b))
    for s in range(SSM_SEQS):
        yb_ref[:, s * SSM_WIDTH:(s + 1) * SSM_WIDTH] = y_nat[s * SSM_CHUNK:(s + 1) * SSM_CHUNK, :]


def _ssm_scan(layer, su8, h0_sample, a_rows, b_mat, c_mat):
    rows = SSM_ROWS
    width = SSM_SEQS * SSM_WIDTH
    n_groups = SSM_GROUPS_PROMPT + 1
    hs = h0_sample.astype(F32).reshape(SSM_SEQS, 2, SSM_COLS, 2).transpose(1, 0, 3, 2).reshape(1, rows, SSM_LANES)
    hh = jnp.concatenate([jnp.zeros((SSM_GROUPS_PROMPT, rows, SSM_LANES), F32), hs], axis=0)
    j_in = jnp.asarray(np.eye(SSM_CHUNK, dtype=np.float32)[::-1], dtype=BF16)
    sel = np.zeros((SSM_SEQS * SSM_CHUNK, SSM_HALF), np.float32)
    for s in range(SSM_SEQS):
        for t in range(SSM_CHUNK):
            sel[s * SSM_CHUNK + t, s * SSM_PITCH + SSM_CHUNK - 1 - t] = 1.0
    j_out = jnp.asarray(sel, dtype=BF16)
    fwd = lambda i: (i, 0)
    bwd = lambda i: (_scan_bwd_chunk(i), 0)
    grp = lambda i: (_scan_group(i), 0, 0)
    const = lambda i: (0, 0)
    y_f, y_b, hfin = pl.pallas_call(
        _ssm_kernel,
        grid=(SCAN_CHUNKS,),
        in_specs=[
            pl.BlockSpec((SSM_CHUNK, width), fwd),
            pl.BlockSpec((SSM_CHUNK, width), bwd),
            _layer_spec(rows, SSM_LANES, layer),
            _layer_spec(SSM_WIDTH, SSM_LANES, layer, row_block=0),
            _layer_spec(SSM_WIDTH, SSM_LANES, layer, row_block=1),
            _layer_spec(SSM_LANES, SSM_WIDTH, layer, col_block=0),
            _layer_spec(SSM_LANES, SSM_WIDTH, layer, col_block=1),
            pl.BlockSpec((SSM_CHUNK, SSM_CHUNK), const),
            pl.BlockSpec((SSM_SEQS * SSM_CHUNK, SSM_HALF), const),
            pl.BlockSpec((1, rows, SSM_LANES), grp),
        ],
        out_specs=[
            pl.BlockSpec((SSM_CHUNK, width), fwd),
            pl.BlockSpec((SSM_CHUNK, width), bwd),
            pl.BlockSpec((1, rows, SSM_LANES), grp),
        ],
        out_shape=[
            jax.ShapeDtypeStruct((SCAN_T, width), F32),
            jax.ShapeDtypeStruct((SCAN_T, width), F32),
            jax.ShapeDtypeStruct((n_groups, rows, SSM_LANES), F32),
        ],
        scratch_shapes=[pltpu.VMEM((SSM_SLABS, 2 * SSM_HALF, LANE), F32), pltpu.VMEM((rows, SSM_LANES), F32),
                        pltpu.VMEM((2 * SSM_HALF, SSM_WIDTH), F32), pltpu.VMEM((SSM_HALF, SSM_WIDTH), F32)],
        compiler_params=_params(("arbitrary",)),
        name="ssm_scan",
    )(su8, su8, a_rows, b_mat, b_mat, c_mat, c_mat, j_in, j_out, hh)
    fin = hfin[:SSM_GROUPS_PROMPT].reshape(SSM_GROUPS_PROMPT, 2, SSM_SEQS, 2, SSM_GROUPS, SSM_STATE)
    fin = fin.transpose(0, 2, 1, 4, 5, 3).reshape(BATCH, 2, SSM_GROUPS, SSM_STATE, 2)
    return y_f, y_b, fin


def _ssm_matrices(a_re, a_im, log_dt, b_re, b_im, c_re, c_im):
    n_l = a_re.shape[0]
    lam = lax.complex(a_re.astype(F32), a_im.astype(F32))
    dt = jnp.exp(log_dt.astype(F32))[..., None]
    a_bar = jnp.exp(lam * dt)
    b_bar = ((a_bar - 1.0) / lam)[..., None] * lax.complex(b_re.astype(F32), b_im.astype(F32))
    a_dir = jnp.concatenate([jnp.real(a_bar).reshape(n_l, 2, SSM_COLS),
                             jnp.imag(a_bar).reshape(n_l, 2, SSM_COLS)], axis=-1)
    a_rows = jnp.repeat(a_dir, SSM_SEQS, axis=1)
    eye = jnp.eye(SSM_GROUPS, dtype=F32)
    bt = jnp.transpose(b_bar, (0, 1, 2, 4, 3))
    b_real = jnp.einsum('ldghp,ge->ldghep', jnp.real(bt), eye).reshape(n_l, 2 * SSM_WIDTH, SSM_COLS)
    b_imag = jnp.einsum('ldghp,ge->ldghep', jnp.imag(bt), eye).reshape(n_l, 2 * SSM_WIDTH, SSM_COLS)
    b_mat = jnp.concatenate([b_real, b_imag], axis=-1).astype(BF16)
    cr = jnp.transpose(c_re.astype(F32), (0, 1, 2, 4, 3))
    ci = jnp.transpose(c_im.astype(F32), (0, 1, 2, 4, 3))
    c_real = jnp.einsum('ldgph,ge->lgpdeh', cr, eye).reshape(n_l, SSM_COLS, 2 * SSM_WIDTH)
    c_imag = jnp.einsum('ldgph,ge->lgpdeh', -ci, eye).reshape(n_l, SSM_COLS, 2 * SSM_WIDTH)
    c_mat = jnp.concatenate([c_real, c_imag], axis=1).astype(BF16)
    return a_rows, b_mat, c_mat


def _merge_kernel(xp_ref, xs_ref, mod_ref, g1_ref, wg_ref, oap_ref, oas_ref, opp_ref, ops_ref, su_ref, yf_ref, yb_ref,
                  d_ref, wglu_ref, bglu_ref, wua_ref, wup_ref, wus_ref, wo_ref, o_ref):
    x = _stream_rows(xp_ref, xs_ref)
    h = _norm_mod(x, g1_ref[...], mod_ref[0, 0:1, :], mod_ref[0, 1:2, :])
    gates = _sigmoid(jnp.dot(h.astype(BF16), wg_ref[...], preferred_element_type=F32))
    y = d_ref[...] * su_ref[...] + yf_ref[...] + yb_ref[...]
    y = 0.5 * y * (1.0 + jnp.tanh(math.sqrt(2.0 / math.pi) * (y + 0.044715 * (y * y * y))))
    glu = jnp.dot(y.astype(BF16), wglu_ref[...], preferred_element_type=F32) + bglu_ref[...]
    o_ssm = y * _sigmoid(glu)
    o_attn = _stream_rows(oap_ref, oas_ref)
    o_pool = _stream_rows(opp_ref, ops_ref)
    m = (gates[:, :D_MODEL] * jnp.dot(o_attn, wua_ref[...], preferred_element_type=F32)
         + gates[:, D_MODEL:2 * D_MODEL] * jnp.dot(o_pool, wup_ref[...], preferred_element_type=F32)
         + gates[:, 2 * D_MODEL:] * jnp.dot(o_ssm.astype(BF16), wus_ref[...], preferred_element_type=F32))
    o_ref[...] = x + mod_ref[0, 2:3, :] * jnp.dot(m.astype(BF16), wo_ref[...], preferred_element_type=F32)


def _merge(layer, x_pair, mod, norm1_g, w_gates, oa_p, oa_s, op_p, op_s, ssm_u, y_f, y_b, d_skip, w_glu, b_glu,
           w_up_attn, w_up_pool, w_up_ssm, w_out):
    tm = SEQ_TILE
    row = lambda i: (i, 0)
    const = lambda i: (0, 0)
    return pl.pallas_call(
        _merge_kernel,
        grid=(N_TOK // tm,),
        in_specs=_stream_specs(D_MODEL, x_pair[2]) + [
            _mod_spec(layer, tm),
            _layer_spec(1, D_MODEL, layer),
            pl.BlockSpec((D_MODEL, GATE_WIDTH), const),
        ] + _stream_specs(ATTN_WIDTH, 0) + _stream_specs(POOL_WIDTH, 0) + [
            pl.BlockSpec((tm, SSM_WIDTH), _scan_block),
            pl.BlockSpec((tm, SSM_WIDTH), _scan_block),
            pl.BlockSpec((tm, SSM_WIDTH), _scan_block),
            _layer_spec(1, SSM_WIDTH, layer),
            _layer_spec(SSM_WIDTH, SSM_WIDTH, layer),
            _layer_spec(1, SSM_WIDTH, layer),
            _layer_spec(ATTN_WIDTH, D_MODEL, layer),
            _layer_spec(POOL_WIDTH, D_MODEL, layer),
            _layer_spec(SSM_WIDTH, D_MODEL, layer),
            _layer_spec(D_MODEL, D_MODEL, layer),
        ],
        out_specs=pl.BlockSpec((tm, D_MODEL), row),
        out_shape=jax.ShapeDtypeStruct((N_TOK, D_MODEL), F32),
        compiler_params=_params(("parallel",)),
        name="merge_branches",
    )(x_pair[0], x_pair[1], mod, norm1_g, w_gates, oa_p, oa_s, op_p, op_s, ssm_u, y_f, y_b, d_skip, w_glu, b_glu,
      w_up_attn, w_up_pool, w_up_ssm, w_out)


def _router_kernel(x_ref, mod_ref, g_ref, wrh_ref, wrl_ref, rb_ref, h_ref, gw_ref):
    h2 = _norm_mod(x_ref[...], g_ref[...], mod_ref[0, 3:4, :], mod_ref[0, 4:5, :])
    h_hi = h2.astype(BF16)
    h_ref[...] = h_hi
    h_lo = (h2 - h_hi.astype(F32)).astype(BF16)
    logits = (jnp.dot(h_hi, wrh_ref[...], preferred_element_type=F32)
              + jnp.dot(h_lo, wrh_ref[...], preferred_element_type=F32)
              + jnp.dot(h_hi, wrl_ref[...], preferred_element_type=F32))
    scores = _sigmoid(logits)
    sc_t = jnp.transpose(scores)[:N_EXPERTS, :]
    work = jnp.transpose(scores + rb_ref[...])[:N_EXPERTS, :]
    idx = lax.broadcasted_iota(jnp.int32, work.shape, 0).astype(F32)
    w_t = jnp.zeros_like(sc_t)
    for _ in range(TOP_K):
        mx = jnp.max(work, axis=0, keepdims=True)
        first = jnp.min(jnp.where(work == mx, idx, float(N_EXPERTS)), axis=0, keepdims=True)
        pick = idx == first
        w_t = jnp.where(pick, sc_t, w_t)
        work = jnp.where(pick, -jnp.inf, work)
    w_t = w_t / jnp.sum(w_t, axis=0, keepdims=True) * ROUTE_SCALE
    w_pad = jnp.concatenate([w_t, jnp.zeros((ROUTER_LANES - N_EXPERTS, w_t.shape[1]), F32)], axis=0)
    gw_ref[...] = jnp.transpose(w_pad)


def _route(layer, x, mod, norm_g, w_router, router_bias):
    tm = 512
    row = lambda i: (i, 0)
    const = lambda i: (0, 0)
    w_hi = w_router.astype(BF16)
    w_lo = (w_router - w_hi.astype(F32)).astype(BF16)
    return pl.pallas_call(
        _router_kernel,
        grid=(N_TOK // tm,),
        in_specs=[
            pl.BlockSpec((tm, D_MODEL), row),
            _mod_spec(layer, tm),
            _layer_spec(1, D_MODEL, layer),
            pl.BlockSpec((D_MODEL, ROUTER_LANES), const),
            pl.BlockSpec((D_MODEL, ROUTER_LANES), const),
            pl.BlockSpec((1, ROUTER_LANES), const),
        ],
        out_specs=[pl.BlockSpec((tm, D_MODEL), row), pl.BlockSpec((tm, ROUTER_LANES), row)],
        out_shape=[jax.ShapeDtypeStruct((N_TOK, D_MODEL), BF16),
                   jax.ShapeDtypeStruct((N_TOK, ROUTER_LANES), F32)],
        compiler_params=_params(("parallel",)),
        name="moe_router",
    )(x, mod, norm_g, w_hi, w_lo, router_bias)


def _swiglu_mid(h, w_gate, w_up):
    a = jnp.dot(h, w_gate, preferred_element_type=F32)
    b = jnp.dot(h, w_up, preferred_element_type=F32)
    return (a * _sigmoid(a)) * b


def _experts_kernel(h_ref, gw_ref, wg_ref, wu_ref, wd_ref, sg_ref, su_ref, sd_ref, x_ref, mod_ref, fg_ref,
                    o_ref, *, final):
    j = pl.program_id(1)
    h = h_ref[...]

    @pl.when(j == 0)
    def _():
        mid = _swiglu_mid(h, sg_ref[...], su_ref[...]).astype(BF16)
        o_ref[...] = jnp.dot(mid, sd_ref[...], preferred_element_type=F32)

    gw = pltpu.roll(gw_ref[...], (ROUTER_LANES - j * EXPERT_CHUNK) & (ROUTER_LANES - 1), axis=1)
    mids = [(_swiglu_mid(h, wg_ref[e], wu_ref[e]) * gw[:, e:e + 1]).astype(BF16) for e in range(EXPERT_CHUNK)]
    mid = jnp.concatenate(mids, axis=1)
    wd = wd_ref[...].reshape(EXPERT_CHUNK * EXPERT_DIM, D_MODEL)
    o_ref[...] += jnp.dot(mid, wd, preferred_element_type=F32)

    @pl.when(j == pl.num_programs(1) - 1)
    def _():
        x = x_ref[...] + mod_ref[0, 5:6, :] * o_ref[...]
        if final:
            ms = jnp.mean(x * x, axis=-1, keepdims=True)
            x = x * lax.rsqrt(ms + EPS) * fg_ref[...]
        o_ref[...] = x


def _experts(layer, h, gate_w, wg, wu, wd, sg, su, sd, x, mod, final_g, final, row0=0, n_rows=N_TOK):
    tm = 1024
    n_chunks = N_EXPERTS // EXPERT_CHUNK
    base = row0 // tm
    row = lambda i, j: (base + i, 0)
    const = lambda i, j: (0, 0)
    chunk = lambda i, j: (j, 0, 0)
    return pl.pallas_call(
        functools.partial(_experts_kernel, final=final),
        grid=(n_rows // tm, n_chunks),
        in_specs=[
            pl.BlockSpec((tm, D_MODEL), row),
            pl.BlockSpec((tm, ROUTER_LANES), row),
            pl.BlockSpec((EXPERT_CHUNK, D_MODEL, EXPERT_DIM), chunk),
            pl.BlockSpec((EXPERT_CHUNK, D_MODEL, EXPERT_DIM), chunk),
            pl.BlockSpec((EXPERT_CHUNK, EXPERT_DIM, D_MODEL), chunk),
            _layer_spec(D_MODEL, SHARED_DIM, layer, rank=2),
            _layer_spec(D_MODEL, SHARED_DIM, layer, rank=2),
            _layer_spec(SHARED_DIM, D_MODEL, layer, rank=2),
            pl.BlockSpec((tm, D_MODEL), row),
            _mod_spec(layer, tm, rank=2, base=base),
            pl.BlockSpec((1, D_MODEL), const),
        ],
        out_specs=pl.BlockSpec((tm, D_MODEL), lambda i, j: (i, 0)),
        out_shape=jax.ShapeDtypeStruct((n_rows, D_MODEL), F32),
        compiler_params=_params(("parallel", "arbitrary")),
        name="moe_experts",
    )(h, gate_w, wg, wu, wd, sg, su, sd, x, mod, final_g)


def kernel(x_prompt, x_sample, c, cache_k, cache_v, state_ssm, c_ctx, w_mod, b_mod, norm1_g, norm2_g, w_in, q_norm_g, k_norm_g, w_up_attn, pool_w, pool_scale, w_up_pool, ssm_a_re, ssm_a_im, ssm_log_dt, ssm_b_re, ssm_b_im, ssm_c_re, ssm_c_im, ssm_d, w_glu, b_glu, w_up_ssm, w_out, w_router, router_bias, w_gate, w_up, w_down, ws_gate, ws_up, ws_down, final_norm_g):
    x_pair = (x_prompt.reshape(N_PROMPT, D_MODEL), x_sample.reshape(N_SAMPLE, D_MODEL), 0)
    cond = jnp.concatenate([c_ctx[None, :], c, jnp.zeros((COND_ROWS - N_COND, D_MODEL), F32)], axis=0)
    mod_all = _ada_all(cond, w_mod, b_mod)

    cos_t, sin_t = _rope_tables(SEQ_TILE)
    head_avg = jnp.asarray(np.kron(np.eye(LANE // HEAD_DIM, dtype=np.float32),
                                   np.full((HEAD_DIM, HEAD_DIM), 1.0 / HEAD_DIM, np.float32)), BF16)
    final_g = final_norm_g.reshape(1, D_MODEL)
    norm1 = norm1_g.reshape(DEPTH, 1, D_MODEL)
    norm2 = norm2_g.reshape(DEPTH, 1, D_MODEL)
    p_scale = pool_scale.reshape(DEPTH, 1, POOL_WIDTH)
    d_skip = ssm_d.reshape(DEPTH, 1, SSM_WIDTH)
    glu_b = b_glu.reshape(DEPTH, 1, SSM_WIDTH)
    glu_w, up_attn, up_pool, up_ssm, out_w = (w.astype(BF16) for w in (w_glu, w_up_attn, w_up_pool, w_up_ssm, w_out))
    shared = tuple(w.astype(BF16) for w in (ws_gate, ws_up, ws_down))
    a_rows, b_mat, c_mat = _ssm_matrices(ssm_a_re, ssm_a_im, ssm_log_dt, ssm_b_re, ssm_b_im, ssm_c_re, ssm_c_im)

    new_k, new_v, new_s = [], [], []
    for l in range(DEPTH):
        qk_gain = jnp.concatenate([jnp.tile(q_norm_g[l], N_HEADS), jnp.tile(k_norm_g[l], N_KV_HEADS)])[None, :]
        q, k_p, v_p, pool_u, ssm_u, kv_p, kv_s = _in_project(
            l, x_pair, mod_all, norm1, w_in[l][:, :MIX_WIDTH].astype(BF16), head_avg, qk_gain, cos_t, sin_t,
            cache_k[:, l], cache_v[:, l])
        new_k.append(k_p.reshape(BATCH, SEQ, N_KV_HEADS, HEAD_DIM))
        new_v.append(v_p.reshape(BATCH, SEQ, N_KV_HEADS, HEAD_DIM))
        oa_p = _attention(q, kv_p, BATCH, SEQ, 0)
        oa_s, wg, wu, wd = _attention(q, kv_s, DEC_BATCH, DEC_SEQ, N_PROMPT, cast=(l, [w_gate, w_up, w_down]))

        pool_bd = jax.scipy.linalg.block_diag(*[pool_w[l, g] for g in range(len(POOL_WINDOWS))]).astype(BF16)
        op_p = _pool(l, pool_u, pool_bd, p_scale, BATCH, SEQ, 0)
        op_s = _pool(l, pool_u, pool_bd, p_scale, DEC_BATCH, DEC_SEQ, N_PROMPT)

        y_f, y_b, st = _ssm_scan(l, ssm_u, state_ssm[:, l], a_rows, b_mat, c_mat)
        new_s.append(st)

        w_r = jnp.pad(w_router[l], ((0, 0), (0, ROUTER_LANES - N_EXPERTS)))
        r_b = jnp.concatenate([router_bias[l], jnp.full((ROUTER_LANES - N_EXPERTS,), -jnp.inf, F32)])[None, :]
        x = _merge(
            l, x_pair, mod_all, norm1, w_in[l][:, MIX_WIDTH:].astype(BF16), oa_p, oa_s, op_p, op_s, ssm_u,
            y_f, y_b, d_skip, glu_w, glu_b, up_attn, up_pool, up_ssm, out_w)
        h2, gate_w = _route(l, x, mod_all, norm2, w_r, r_b)
        if l < DEPTH - 1:
            x = _experts(l, h2, gate_w, wg, wu, wd, *shared, x, mod_all, final_g, final=False)
            x_pair = (x, x, N_PROMPT // SEQ_TILE)
        else:
            y_prompt = _experts(l, h2, gate_w, wg, wu, wd, *shared, x, mod_all, final_g, True, 0, N_PROMPT)
            y_sample = _experts(l, h2, gate_w, wg, wu, wd, *shared, x, mod_all, final_g, True, N_PROMPT, N_SAMPLE)

    y_prompt = y_prompt.reshape(BATCH, SEQ, D_MODEL)
    y_sample = y_sample.reshape(DEC_BATCH, DEC_SEQ, D_MODEL)
    return (y_prompt, y_sample, jnp.stack(new_k, axis=1), jnp.stack(new_v, axis=1), jnp.stack(new_s, axis=1))
```

```python
import functools
import math

import jax
import jax.numpy as jnp
import numpy as np
from jax import lax
from jax.experimental import pallas as pl
from jax.experimental.pallas import tpu as pltpu

D_MODEL = 1024
BATCH = 32
SEQ = 256
DEPTH = 2
DEC_BATCH = 4
DEC_SEQ = 4096
PAST_LEN = 256
GRID_W = 64
EPS = 1e-6
N_MOD = 6
HEAD_DIM = 64
N_HEADS = 8
N_KV_HEADS = 2
ATTN_WIDTH = N_HEADS * HEAD_DIM
KV_WIDTH = N_KV_HEADS * HEAD_DIM
ROPE_BASE = 10000.0
ROPE_PAIRS_PER_AXIS = HEAD_DIM // 4
POOL_WINDOWS = (2, 4, 8, 16)
POOL_GROUP = 64
POOL_WIDTH = len(POOL_WINDOWS) * POOL_GROUP
SSM_H = 16
SSM_GROUPS = 16
SSM_WIDTH = SSM_H * SSM_GROUPS
SSM_STATE = 64
N_EXPERTS = 64
TOP_K = 8
EXPERT_DIM = 256
SHARED_DIM = 256
ROUTE_SCALE = 2.5

N_PROMPT = BATCH * SEQ
N_SAMPLE = DEC_BATCH * DEC_SEQ
N_TOK = N_PROMPT + N_SAMPLE
N_COND = 1 + DEC_BATCH
COND_ROWS = 8
QK_WIDTH = ATTN_WIDTH + KV_WIDTH
MIX_WIDTH = QK_WIDTH + KV_WIDTH + POOL_WIDTH + SSM_WIDTH
GATE_WIDTH = 3 * D_MODEL
SSM_COLS = SSM_GROUPS * SSM_STATE
SSM_LANES = 2 * SSM_COLS
SSM_SEQS = 4
SSM_CHUNK = 128
ROUTER_LANES = 128
EXPERT_CHUNK = 8

SEQ_TILE = SEQ
SSM_ROWS = 2 * SSM_SEQS
SSM_GROUPS_PROMPT = BATCH // SSM_SEQS
SCAN_T = SSM_GROUPS_PROMPT * SEQ + DEC_SEQ
SCAN_CHUNKS = SCAN_T // SSM_CHUNK
PROMPT_CHUNKS = SSM_GROUPS_PROMPT * SEQ // SSM_CHUNK
SSM_PITCH = SSM_CHUNK + 4
SSM_HALF = SSM_SEQS * SSM_PITCH

LANE = 128
SSM_SLABS = SSM_LANES // LANE
VMEM_LIMIT = 56 * 1024 * 1024

F32 = jnp.float32
BF16 = jnp.bfloat16
HIGHEST = lax.Precision.HIGHEST


def _sigmoid(x):
    return 1.0 / (1.0 + jnp.exp(-x))


def _params(dims, vmem=VMEM_LIMIT):
    return pltpu.CompilerParams(dimension_semantics=dims, vmem_limit_bytes=vmem)


def _mod_index(i, tm):
    p = N_PROMPT // tm
    t = DEC_SEQ // tm
    return jnp.where(i < p, 0, 1 + (i - p) // t)


def _mod_spec(layer, tm, rank=1, base=0):
    if rank == 1:
        return pl.BlockSpec((None, 1, N_MOD, D_MODEL), lambda i: (layer, _mod_index(base + i, tm), 0, 0))
    return pl.BlockSpec((None, 1, N_MOD, D_MODEL), lambda i, j: (layer, _mod_index(base + i, tm), 0, 0))


def _layer_spec(rows, cols, layer, rank=1, row_block=0, col_block=0):
    if rank == 1:
        return pl.BlockSpec((None, rows, cols), lambda i: (layer, row_block, col_block))
    return pl.BlockSpec((None, rows, cols), lambda i, j: (layer, row_block, col_block))


def _scan_block(i):
    k = i - BATCH
    tiles = DEC_SEQ // SEQ_TILE
    rb = jnp.where(i < BATCH, i // SSM_SEQS, SSM_GROUPS_PROMPT + k % tiles)
    slot = jnp.where(i < BATCH, i % SSM_SEQS, k // tiles)
    return rb, slot


def _norm_mod(x, g, shift, scale):
    ms = jnp.mean(x * x, axis=-1, keepdims=True)
    return (x * lax.rsqrt(ms + EPS) * g) * (1.0 + scale) + shift


def _stream_rows(xp_ref, xs_ref):
    return jnp.where(pl.program_id(0) < N_PROMPT // SEQ_TILE, xp_ref[...], xs_ref[...])


def _stream_specs(width, s_base):
    p_tiles = N_PROMPT // SEQ_TILE
    return [pl.BlockSpec((SEQ_TILE, width), lambda i: (jnp.minimum(i, p_tiles - 1), 0)),
            pl.BlockSpec((SEQ_TILE, width), lambda i: (s_base + jnp.maximum(i - p_tiles, 0), 0))]


def _mod_kernel(cond_ref, w_ref, b_ref, o_ref):
    c = cond_ref[...]
    s = c * _sigmoid(c)
    o_ref[0] = jnp.dot(s, w_ref[0], preferred_element_type=F32, precision=HIGHEST) + b_ref[0]


def _ada_all(cond, w_mod, b_mod):
    tn = 1536
    width = N_MOD * D_MODEL
    out = pl.pallas_call(
        _mod_kernel,
        grid=(DEPTH, width // tn),
        in_specs=[
            pl.BlockSpec((COND_ROWS, D_MODEL), lambda l, j: (0, 0)),
            pl.BlockSpec((1, D_MODEL, tn), lambda l, j: (l, 0, j)),
            pl.BlockSpec((1, 1, tn), lambda l, j: (l, 0, j)),
        ],
        out_specs=pl.BlockSpec((1, COND_ROWS, tn), lambda l, j: (l, 0, j)),
        out_shape=jax.ShapeDtypeStruct((DEPTH, COND_ROWS, width), F32),
        compiler_params=_params(("parallel", "parallel")),
        name="ada_mod",
    )(cond, w_mod, b_mod.reshape(DEPTH, 1, width))
    return out.reshape(DEPTH, COND_ROWS, N_MOD, D_MODEL)


def _kv_layouts(k, v):
    kt = jnp.transpose(k).astype(BF16)
    lane = lax.broadcasted_iota(jnp.int32, v.shape, 1)
    first = lane < HEAD_DIM
    v_sw = pltpu.roll(v, HEAD_DIM, axis=1)
    one = jnp.ones_like(v)
    kts, ves, vos = [], [], []
    for g in range(N_KV_HEADS):
        kg = kt[g * HEAD_DIM:(g + 1) * HEAD_DIM, :]
        kts.append(jnp.concatenate([kg, kg], axis=0))
        lo_half = v if g == 0 else v_sw
        hi_half = v_sw if g == 0 else v
        ves.append(jnp.where(first, lo_half, one).astype(BF16))
        vos.append(jnp.where(first, one, hi_half).astype(BF16))
    return kts, ves, vos


def _inproj_kernel(xp_ref, xs_ref, mod_ref, g_ref, w_ref, bd_ref, qkg_ref, cos_ref, sin_ref, *rest):
    (q_ref, kp_ref, vp_ref, pu_ref, su_ref, ktp_ref, vep_ref, vop_ref, kts_ref, ves_ref, vos_ref) = rest[3:]
    h = _norm_mod(_stream_rows(xp_ref, xs_ref), g_ref[...], mod_ref[0, 0:1, :], mod_ref[0, 1:2, :])
    z = jnp.dot(h.astype(BF16), w_ref[...], preferred_element_type=F32)
    qk = z[:, :QK_WIDTH]
    qq = qk * qk
    hi = qq.astype(BF16)
    lo = (qq - hi.astype(F32)).astype(BF16)
    bd = bd_ref[...]
    ms = jnp.concatenate(
        [jnp.dot(hi[:, c * LANE:(c + 1) * LANE], bd, preferred_element_type=F32)
         + jnp.dot(lo[:, c * LANE:(c + 1) * LANE], bd, preferred_element_type=F32)
         for c in range(QK_WIDTH // LANE)], axis=1)
    qkn = qk * lax.rsqrt(ms + EPS) * qkg_ref[...]
    cos, sin = cos_ref[...], sin_ref[...]
    parts = []
    for c in range(QK_WIDTH // LANE):
        blk = qkn[:, c * LANE:(c + 1) * LANE]
        nxt = pltpu.roll(blk, LANE - 1, axis=1)
        prv = pltpu.roll(blk, 1, axis=1)
        lane = lax.broadcasted_iota(jnp.int32, blk.shape, 1)
        parts.append(blk * cos + jnp.where((lane & 1) == 0, nxt, prv) * sin)
    qkr = jnp.concatenate(parts, axis=1)
    q_ref[...] = (qkr[:, :ATTN_WIDTH] * (HEAD_DIM ** -0.5 * math.log2(math.e))).astype(BF16)
    k = qkr[:, ATTN_WIDTH:QK_WIDTH]
    v = z[:, QK_WIDTH:QK_WIDTH + KV_WIDTH]
    kts, ves, vos = _kv_layouts(k, v)
    is_prompt = pl.program_id(0) < N_PROMPT // SEQ_TILE

    @pl.when(is_prompt)
    def _():
        kp_ref[...] = k
        vp_ref[...] = v
        for g in range(N_KV_HEADS):
            ktp_ref[g], vep_ref[g], vop_ref[g] = kts[g], ves[g], vos[g]

    @pl.when(jnp.logical_not(is_prompt))
    def _():
        for g in range(N_KV_HEADS):
            kts_ref[g], ves_ref[g], vos_ref[g] = kts[g], ves[g], vos[g]

    pu_ref[...] = z[:, QK_WIDTH + KV_WIDTH:QK_WIDTH + KV_WIDTH + POOL_WIDTH]
    su_ref[...] = z[:, QK_WIDTH + KV_WIDTH + POOL_WIDTH:MIX_WIDTH]


def _in_project(layer, x_pair, mod, norm_g, w_mix, bd, qk_gain, cos_t, sin_t, cache_k, cache_v):
    tm = SEQ_TILE
    p_tiles = N_PROMPT // tm
    s_tiles = DEC_SEQ // tm
    n_keys = PAST_LEN + DEC_SEQ
    past_tiles = PAST_LEN // tm

    def rope_idx(i):
        return (jnp.where(i < p_tiles, 0, 1 + (i - p_tiles) % s_tiles), 0)

    kc = jnp.transpose(cache_k, (0, 2, 3, 1)).astype(BF16)
    kc = jnp.concatenate([kc, kc], axis=2).reshape(DEC_BATCH * N_KV_HEADS, LANE, PAST_LEN)
    vc = jnp.transpose(cache_v, (0, 2, 1, 3)).astype(BF16)
    ones = jnp.ones_like(vc)
    ve_c = jnp.concatenate([vc, ones], axis=3).reshape(DEC_BATCH * N_KV_HEADS, PAST_LEN, LANE)
    vo_c = jnp.concatenate([ones, vc], axis=3).reshape(DEC_BATCH * N_KV_HEADS, PAST_LEN, LANE)
    kt_s0 = jnp.pad(kc, ((0, 0), (0, 0), (0, DEC_SEQ)))
    ve_s0 = jnp.pad(ve_c, ((0, 0), (0, DEC_SEQ), (0, 0)))
    vo_s0 = jnp.pad(vo_c, ((0, 0), (0, DEC_SEQ), (0, 0)))

    p_blk = lambda i: jnp.minimum(i, p_tiles - 1)
    s_seq = lambda i: jnp.maximum(i - p_tiles, 0) // s_tiles
    s_blk = lambda i: past_tiles + jnp.maximum(i - p_tiles, 0) % s_tiles
    row = lambda i: (i, 0)
    const = lambda i: (0, 0)
    any_spec = pl.BlockSpec(memory_space=pl.ANY)
    outs = pl.pallas_call(
        _inproj_kernel,
        grid=(N_TOK // tm,),
        in_specs=_stream_specs(D_MODEL, x_pair[2]) + [
            _mod_spec(layer, tm),
            _layer_spec(1, D_MODEL, layer),
            pl.BlockSpec((D_MODEL, MIX_WIDTH), const),
            pl.BlockSpec((LANE, LANE), const),
            pl.BlockSpec((1, QK_WIDTH), const),
            pl.BlockSpec((tm, LANE), rope_idx),
            pl.BlockSpec((tm, LANE), rope_idx),
            any_spec, any_spec, any_spec,
        ],
        out_specs=[
            pl.BlockSpec((tm, ATTN_WIDTH), row),
            pl.BlockSpec((tm, KV_WIDTH), lambda i: (p_blk(i), 0)),
            pl.BlockSpec((tm, KV_WIDTH), lambda i: (p_blk(i), 0)),
            pl.BlockSpec((tm, POOL_WIDTH), row),
            pl.BlockSpec((tm, SSM_WIDTH), _scan_block),
            pl.BlockSpec((N_KV_HEADS, LANE, tm), lambda i: (p_blk(i), 0, 0)),
            pl.BlockSpec((N_KV_HEADS, tm, LANE), lambda i: (p_blk(i), 0, 0)),
            pl.BlockSpec((N_KV_HEADS, tm, LANE), lambda i: (p_blk(i), 0, 0)),
            pl.BlockSpec((N_KV_HEADS, LANE, tm), lambda i: (s_seq(i), 0, s_blk(i))),
            pl.BlockSpec((N_KV_HEADS, tm, LANE), lambda i: (s_seq(i), s_blk(i), 0)),
            pl.BlockSpec((N_KV_HEADS, tm, LANE), lambda i: (s_seq(i), s_blk(i), 0)),
        ],
        out_shape=[
            jax.ShapeDtypeStruct((N_TOK, ATTN_WIDTH), BF16),
            jax.ShapeDtypeStruct((N_PROMPT, KV_WIDTH), F32),
            jax.ShapeDtypeStruct((N_PROMPT, KV_WIDTH), F32),
            jax.ShapeDtypeStruct((N_TOK, POOL_WIDTH), F32),
            jax.ShapeDtypeStruct((SCAN_T, SSM_SEQS * SSM_WIDTH), F32),
            jax.ShapeDtypeStruct((BATCH * N_KV_HEADS, LANE, SEQ), BF16),
            jax.ShapeDtypeStruct((BATCH * N_KV_HEADS, SEQ, LANE), BF16),
            jax.ShapeDtypeStruct((BATCH * N_KV_HEADS, SEQ, LANE), BF16),
            jax.ShapeDtypeStruct((DEC_BATCH * N_KV_HEADS, LANE, n_keys), BF16),
            jax.ShapeDtypeStruct((DEC_BATCH * N_KV_HEADS, n_keys, LANE), BF16),
            jax.ShapeDtypeStruct((DEC_BATCH * N_KV_HEADS, n_keys, LANE), BF16),
        ],
        input_output_aliases={9: 8, 10: 9, 11: 10},
        compiler_params=_params(("arbitrary",)),
        name="in_project",
    )(x_pair[0], x_pair[1], mod, norm_g, w_mix, bd, qk_gain, cos_t, sin_t, kt_s0, ve_s0, vo_s0)
    return outs[0], outs[1], outs[2], outs[3], outs[4], tuple(outs[5:8]), tuple(outs[8:11])


def _rope_tables(tm):
    f32 = np.float32
    rows = DEC_SEQ // GRID_W
    row = np.repeat(np.arange(rows, dtype=f32), GRID_W)
    col = np.tile(np.arange(GRID_W, dtype=f32), rows)
    inv_freq = (f32(ROPE_BASE) ** (-np.arange(ROPE_PAIRS_PER_AXIS, dtype=f32) / f32(ROPE_PAIRS_PER_AXIS))).astype(f32)
    ang = np.concatenate([row[:, None] * inv_freq, col[:, None] * inv_freq], axis=-1).astype(f32)
    cos = np.repeat(np.cos(ang), 2, axis=-1)
    sin = np.repeat(np.sin(ang), 2, axis=-1) * np.tile(np.array([-1.0, 1.0], f32), HEAD_DIM // 2)
    cos = np.concatenate([np.ones((tm, HEAD_DIM), f32), cos], axis=0)
    sin = np.concatenate([np.zeros((tm, HEAD_DIM), f32), sin], axis=0)
    n_rep = LANE // HEAD_DIM
    return jnp.asarray(np.tile(cos, (1, n_rep)), F32), jnp.asarray(np.tile(sin, (1, n_rep)), F32)


def _attn_kernel(q_ref, kt_ref, ve_ref, vo_ref, *rest):
    n_cast = (len(rest) - 1) // 2
    o_ref = rest[n_cast]
    for src, dst in zip(rest[:n_cast], rest[n_cast + 1:]):
        dst[...] = src[...].astype(dst.dtype)
    kt = kt_ref[0]
    lane = lax.broadcasted_iota(jnp.int32, (q_ref.shape[0], LANE), 1)
    first = lane < HEAD_DIM
    slabs = []
    for pair in range(q_ref.shape[1] // LANE):
        q = q_ref[:, pair * LANE:(pair + 1) * LANE]
        halves = []
        for keep, v_ref in ((first, ve_ref), (lane >= HEAD_DIM, vo_ref)):
            qh = jnp.where(keep, q, jnp.zeros_like(q))
            s = jnp.dot(qh, kt, preferred_element_type=F32)
            m = jnp.max(s, axis=-1, keepdims=True)
            p = jnp.exp2(s - m).astype(BF16)
            a = jnp.dot(p, v_ref[0], preferred_element_type=F32)
            halves.append(a / pltpu.roll(a, HEAD_DIM, axis=1))
        slabs.append(jnp.where(first, halves[0], halves[1]))
    o_ref[...] = jnp.concatenate(slabs, axis=1).astype(o_ref.dtype)


def _attention(q, kv_ops, n_batch, n_q, row0, cast=None):
    kt, v_e, v_o = kv_ops
    n_keys = kt.shape[2]
    tq = 256
    q_tiles = n_q // tq
    base = row0 // tq
    group_w = ATTN_WIDTH // N_KV_HEADS
    kv_idx = lambda b, g, i: (b * N_KV_HEADS + g, 0, 0)
    in_specs = [
        pl.BlockSpec((tq, group_w), lambda b, g, i: (base + b * q_tiles + i, g)),
        pl.BlockSpec((1, LANE, n_keys), kv_idx),
        pl.BlockSpec((1, n_keys, LANE), kv_idx),
        pl.BlockSpec((1, n_keys, LANE), kv_idx),
    ]
    out_specs = [pl.BlockSpec((tq, group_w), lambda b, g, i: (b * q_tiles + i, g))]
    out_shape = [jax.ShapeDtypeStruct((n_batch * n_q, ATTN_WIDTH), BF16)]
    args = [q, kt, v_e, v_o]
    if cast is not None:
        layer, tensors = cast
        steps = n_batch * N_KV_HEADS * q_tiles
        step = lambda b, g, i: (b * N_KV_HEADS + g) * q_tiles + i
        for w in tensors:
            _, n_e, rows, cols = w.shape
            part = n_e * rows // steps
            assert part * steps == n_e * rows and part % 16 == 0 and rows % part == 0
            in_specs.append(pl.BlockSpec((1, part, cols), lambda b, g, i: (layer * steps + step(b, g, i), 0, 0)))
            out_specs.append(pl.BlockSpec((1, part, cols), lambda b, g, i: (step(b, g, i), 0, 0)))
            out_shape.append(jax.ShapeDtypeStruct((steps, part, cols), BF16))
            args.append(w.reshape(DEPTH * steps, part, cols))
    outs = pl.pallas_call(
        _attn_kernel,
        grid=(n_batch, N_KV_HEADS, q_tiles),
        in_specs=in_specs,
        out_specs=out_specs,
        out_shape=out_shape,
        compiler_params=_params(("parallel", "parallel", "parallel")),
        name="attention",
    )(*args)
    if cast is None:
        return outs[0]
    return [outs[0]] + [o.reshape(w.shape[1:]) for o, w in zip(outs[1:], cast[1])]


def _pool_kernel(u_ref, w_ref, sc_ref, o_ref, pad_ref, *, n_seq):
    halo = 8
    u = u_ref[...]
    zeros = jnp.zeros((halo, POOL_WIDTH), F32)
    pad_ref[0:halo, :] = zeros
    pad_ref[halo + n_seq:2 * halo + n_seq, :] = zeros
    pad_ref[halo:halo + n_seq, :] = u

    def sh(j):
        return pad_ref[halo + j:halo + j + n_seq, :]

    t2 = sh(-1) + u
    t4 = t2 + sh(-2) + sh(1)
    t8 = t4 + sh(-4) + sh(-3) + sh(2) + sh(3)
    t16 = t8 + sh(-8) + sh(-7) + sh(-6) + sh(-5) + sh(4) + sh(5) + sh(6) + sh(7)
    grp = lax.broadcasted_iota(jnp.int32, u.shape, 1) >> 6
    t = lax.broadcasted_iota(jnp.int32, u.shape, 0)
    tot = jnp.where(grp == 0, t2, jnp.where(grp == 1, t4, jnp.where(grp == 2, t8, t16)))
    half = jnp.where(grp == 0, 1, jnp.where(grp == 1, 2, jnp.where(grp == 2, 4, 8)))
    lo = jnp.maximum(t - half, 0)
    hi = jnp.minimum(t + half, n_seq)
    pooled = tot / (hi - lo).astype(F32) - u
    mixed = jnp.dot(pooled.astype(BF16), w_ref[...], preferred_element_type=F32)
    o_ref[...] = (mixed * sc_ref[...]).astype(o_ref.dtype)


def _pool(layer, u, w_bd, scale, n_batch, n_seq, row0):
    base = row0 // n_seq
    return pl.pallas_call(
        functools.partial(_pool_kernel, n_seq=n_seq),
        grid=(n_batch,),
        in_specs=[
            pl.BlockSpec((n_seq, POOL_WIDTH), lambda b: (base + b, 0)),
            pl.BlockSpec((POOL_WIDTH, POOL_WIDTH), lambda b: (0, 0)),
            _layer_spec(1, POOL_WIDTH, layer),
        ],
        out_specs=pl.BlockSpec((n_seq, POOL_WIDTH), lambda b: (b, 0)),
        out_shape=jax.ShapeDtypeStruct((n_batch * n_seq, POOL_WIDTH), BF16),
        scratch_shapes=[pltpu.VMEM((n_seq + 16, POOL_WIDTH), F32)],
        compiler_params=_params(("parallel",)),
        name="pool_mixer",
    )(u, w_bd, scale)


def _scan_chunk_first(i):
    per_seq = SEQ // SSM_CHUNK
    return jnp.where(i < PROMPT_CHUNKS, i % per_seq == 0, i == PROMPT_CHUNKS)


def _scan_bwd_chunk(i):
    per_seq = SEQ // SSM_CHUNK
    return jnp.where(i < PROMPT_CHUNKS, (i // per_seq) * per_seq + (per_seq - 1 - i % per_seq),
                     PROMPT_CHUNKS + SCAN_CHUNKS - 1 - i)


def _scan_group(i):
    return jnp.where(i < PROMPT_CHUNKS, i // (SEQ // SSM_CHUNK), SSM_GROUPS_PROMPT)


def _split3(x):
    hi = x.astype(BF16)
    r1 = x - hi.astype(F32)
    mid = r1.astype(BF16)
    lo = (r1 - mid.astype(F32)).astype(BF16)
    return hi, mid, lo


def _ssm_kernel(uf_ref, ub_ref, a_ref, bf_ref, bb_ref, cf_ref, cb_ref, jin_ref, jout_ref, h0_ref,
                yf_ref, yb_ref, hfin_ref, bu_ref, st_ref, lhs_ref, yy_ref):
    @pl.when(_scan_chunk_first(pl.program_id(0)))
    def _():
        st_ref[...] = h0_ref[0]

    ub_rev = jnp.dot(jin_ref[...], ub_ref[...].astype(BF16), preferred_element_type=F32)
    gap = jnp.zeros((SSM_PITCH - SSM_CHUNK, SSM_WIDTH), F32)
    for s in range(SSM_SEQS):
        cols = slice(s * SSM_WIDTH, (s + 1) * SSM_WIDTH)
        lhs_ref[s * SSM_PITCH:s * SSM_PITCH + SSM_CHUNK, :] = uf_ref[:, cols]
        lhs_ref[s * SSM_PITCH + SSM_CHUNK:(s + 1) * SSM_PITCH, :] = gap
        lhs_ref[SSM_HALF + s * SSM_PITCH:SSM_HALF + s * SSM_PITCH + SSM_CHUNK, :] = ub_rev[:, cols]
        lhs_ref[SSM_HALF + s * SSM_PITCH + SSM_CHUNK:SSM_HALF + (s + 1) * SSM_PITCH, :] = gap
    bu_f = jnp.dot(lhs_ref[:SSM_HALF, :].astype(BF16), bf_ref[...], preferred_element_type=F32)
    bu_b = jnp.dot(lhs_ref[SSM_HALF:, :].astype(BF16), bb_ref[...], preferred_element_type=F32)
    for k in range(SSM_SLABS):
        bu_ref[k, :SSM_HALF, :] = bu_f[:, k * LANE:(k + 1) * LANE]
        bu_ref[k, SSM_HALF:, :] = bu_b[:, k * LANE:(k + 1) * LANE]

    half = SSM_SLABS // 2
    a_re = [a_ref[:, k * LANE:(k + 1) * LANE] for k in range(half)]
    a_im = [a_ref[:, SSM_COLS + k * LANE:SSM_COLS + (k + 1) * LANE] for k in range(half)]

    def step(t, carry):
        rows = pl.ds(t, SSM_ROWS, stride=SSM_PITCH)
        new_re, new_im = [], []
        for k in range(half):
            h_re, h_im = carry[k], carry[half + k]
            n_re = a_re[k] * h_re - a_im[k] * h_im + bu_ref[k, rows, :]
            n_im = a_re[k] * h_im + a_im[k] * h_re + bu_ref[half + k, rows, :]
            bu_ref[k, rows, :] = n_re
            bu_ref[half + k, rows, :] = n_im
            new_re.append(n_re)
            new_im.append(n_im)
        return tuple(new_re + new_im)

    init = tuple(st_ref[:, k * LANE:(k + 1) * LANE] for k in range(SSM_SLABS))
    fin = lax.fori_loop(0, SSM_CHUNK, step, init)
    for k in range(SSM_SLABS):
        st_ref[:, k * LANE:(k + 1) * LANE] = fin[k]
    hfin_ref[0] = st_ref[...]

    h_f = jnp.concatenate([bu_ref[k, :SSM_HALF, :] for k in range(SSM_SLABS)], axis=1).astype(BF16)
    h_b = jnp.concatenate([bu_ref[k, SSM_HALF:, :] for k in range(SSM_SLABS)], axis=1).astype(BF16)
    yy_ref[...] = jnp.dot(h_f, cf_ref[...], preferred_element_type=F32)
    for s in range(SSM_SEQS):
        yf_ref[:, s * SSM_WIDTH:(s + 1) * SSM_WIDTH] = yy_ref[s * SSM_PITCH:s * SSM_PITCH + SSM_CHUNK, :]
    y_b = jnp.dot(h_b, cb_ref[...], preferred_element_type=F32)
    y_nat = sum(jnp.dot(jout_ref[...], piece, preferred_element_type=F32) for piece in _split3(y_b))
    for s in range(SSM_SEQS):
        yb_ref[:, s * SSM_WIDTH:(s + 1) * SSM_WIDTH] = y_nat[s * SSM_CHUNK:(s + 1) * SSM_CHUNK, :]


def _ssm_scan(layer, su8, h0_sample, a_rows, b_mat, c_mat):
    rows = SSM_ROWS
    width = SSM_SEQS * SSM_WIDTH
    n_groups = SSM_GROUPS_PROMPT + 1
    hs = h0_sample.astype(F32).reshape(SSM_SEQS, 2, SSM_COLS, 2).transpose(1, 0, 3, 2).reshape(1, rows, SSM_LANES)
    hh = jnp.concatenate([jnp.zeros((SSM_GROUPS_PROMPT, rows, SSM_LANES), F32), hs], axis=0)
    j_in = jnp.asarray(np.eye(SSM_CHUNK, dtype=np.float32)[::-1], dtype=BF16)
    sel = np.zeros((SSM_SEQS * SSM_CHUNK, SSM_HALF), np.float32)
    for s in range(SSM_SEQS):
        for t in range(SSM_CHUNK):
            sel[s * SSM_CHUNK + t, s * SSM_PITCH + SSM_CHUNK - 1 - t] = 1.0
    j_out = jnp.asarray(sel, dtype=BF16)
    fwd = lambda i: (i, 0)
    bwd = lambda i: (_scan_bwd_chunk(i), 0)
    grp = lambda i: (_scan_group(i), 0, 0)
    const = lambda i: (0, 0)
    y_f, y_b, hfin = pl.pallas_call(
        _ssm_kernel,
        grid=(SCAN_CHUNKS,),
        in_specs=[
            pl.BlockSpec((SSM_CHUNK, width), fwd),
            pl.BlockSpec((SSM_CHUNK, width), bwd),
            _layer_spec(rows, SSM_LANES, layer),
            _layer_spec(SSM_WIDTH, SSM_LANES, layer, row_block=0),
            _layer_spec(SSM_WIDTH, SSM_LANES, layer, row_block=1),
            _layer_spec(SSM_LANES, SSM_WIDTH, layer, col_block=0),
            _layer_spec(SSM_LANES, SSM_WIDTH, layer, col_block=1),
            pl.BlockSpec((SSM_CHUNK, SSM_CHUNK), const),
            pl.BlockSpec((SSM_SEQS * SSM_CHUNK, SSM_HALF), const),
            pl.BlockSpec((1, rows, SSM_LANES), grp),
        ],
        out_specs=[
            pl.BlockSpec((SSM_CHUNK, width), fwd),
            pl.BlockSpec((SSM_CHUNK, width), bwd),
            pl.BlockSpec((1, rows, SSM_LANES), grp),
        ],
        out_shape=[
            jax.ShapeDtypeStruct((SCAN_T, width), F32),
            jax.ShapeDtypeStruct((SCAN_T, width), F32),
            jax.ShapeDtypeStruct((n_groups, rows, SSM_LANES), F32),
        ],
        scratch_shapes=[pltpu.VMEM((SSM_SLABS, 2 * SSM_HALF, LANE), F32), pltpu.VMEM((rows, SSM_LANES), F32),
                        pltpu.VMEM((2 * SSM_HALF, SSM_WIDTH), F32), pltpu.VMEM((SSM_HALF, SSM_WIDTH), F32)],
        compiler_params=_params(("arbitrary",)),
        name="ssm_scan",
    )(su8, su8, a_rows, b_mat, b_mat, c_mat, c_mat, j_in, j_out, hh)
    fin = hfin[:SSM_GROUPS_PROMPT].reshape(SSM_GROUPS_PROMPT, 2, SSM_SEQS, 2, SSM_GROUPS, SSM_STATE)
    fin = fin.transpose(0, 2, 1, 4, 5, 3).reshape(BATCH, 2, SSM_GROUPS, SSM_STATE, 2)
    return y_f, y_b, fin


def _ssm_matrices(a_re, a_im, log_dt, b_re, b_im, c_re, c_im):
    n_l = a_re.shape[0]
    lam = lax.complex(a_re.astype(F32), a_im.astype(F32))
    dt = jnp.exp(log_dt.astype(F32))[..., None]
    a_bar = jnp.exp(lam * dt)
    b_bar = ((a_bar - 1.0) / lam)[..., None] * lax.complex(b_re.astype(F32), b_im.astype(F32))
    a_dir = jnp.concatenate([jnp.real(a_bar).reshape(n_l, 2, SSM_COLS),
                             jnp.imag(a_bar).reshape(n_l, 2, SSM_COLS)], axis=-1)
    a_rows = jnp.repeat(a_dir, SSM_SEQS, axis=1)
    eye = jnp.eye(SSM_GROUPS, dtype=F32)
    bt = jnp.transpose(b_bar, (0, 1, 2, 4, 3))
    b_real = jnp.einsum('ldghp,ge->ldghep', jnp.real(bt), eye).reshape(n_l, 2 * SSM_WIDTH, SSM_COLS)
    b_imag = jnp.einsum('ldghp,ge->ldghep', jnp.imag(bt), eye).reshape(n_l, 2 * SSM_WIDTH, SSM_COLS)
    b_mat = jnp.concatenate([b_real, b_imag], axis=-1).astype(BF16)
    cr = jnp.transpose(c_re.astype(F32), (0, 1, 2, 4, 3))
    ci = jnp.transpose(c_im.astype(F32), (0, 1, 2, 4, 3))
    c_real = jnp.einsum('ldgph,ge->lgpdeh', cr, eye).reshape(n_l, SSM_COLS, 2 * SSM_WIDTH)
    c_imag = jnp.einsum('ldgph,ge->lgpdeh', -ci, eye).reshape(n_l, SSM_COLS, 2 * SSM_WIDTH)
    c_mat = jnp.concatenate([c_real, c_imag], axis=1).astype(BF16)
    return a_rows, b_mat, c_mat


def _merge_kernel(xp_ref, xs_ref, mod_ref, g1_ref, wg_ref, oap_ref, oas_ref, opp_ref, ops_ref, su_ref, yf_ref, yb_ref,
                  d_ref, wglu_ref, bglu_ref, wua_ref, wup_ref, wus_ref, wo_ref, o_ref):
    x = _stream_rows(xp_ref, xs_ref)
    h = _norm_mod(x, g1_ref[...], mod_ref[0, 0:1, :], mod_ref[0, 1:2, :])
    gates = _sigmoid(jnp.dot(h.astype(BF16), wg_ref[...], preferred_element_type=F32))
    y = d_ref[...] * su_ref[...] + yf_ref[...] + yb_ref[...]
    y = 0.5 * y * (1.0 + jnp.tanh(math.sqrt(2.0 / math.pi) * (y + 0.044715 * (y * y * y))))
    glu = jnp.dot(y.astype(BF16), wglu_ref[...], preferred_element_type=F32) + bglu_ref[...]
    o_ssm = y * _sigmoid(glu)
    o_attn = _stream_rows(oap_ref, oas_ref)
    o_pool = _stream_rows(opp_ref, ops_ref)
    m = (gates[:, :D_MODEL] * jnp.dot(o_attn, wua_ref[...], preferred_element_type=F32)
         + gates[:, D_MODEL:2 * D_MODEL] * jnp.dot(o_pool, wup_ref[...], preferred_element_type=F32)
         + gates[:, 2 * D_MODEL:] * jnp.dot(o_ssm.astype(BF16), wus_ref[...], preferred_element_type=F32))
    o_ref[...] = x + mod_ref[0, 2:3, :] * jnp.dot(m.astype(BF16), wo_ref[...], preferred_element_type=F32)


def _merge(layer, x_pair, mod, norm1_g, w_gates, oa_p, oa_s, op_p, op_s, ssm_u, y_f, y_b, d_skip, w_glu, b_glu,
           w_up_attn, w_up_pool, w_up_ssm, w_out):
    tm = SEQ_TILE
    row = lambda i: (i, 0)
    const = lambda i: (0, 0)
    return pl.pallas_call(
        _merge_kernel,
        grid=(N_TOK // tm,),
        in_specs=_stream_specs(D_MODEL, x_pair[2]) + [
            _mod_spec(layer, tm),
            _layer_spec(1, D_MODEL, layer),
            pl.BlockSpec((D_MODEL, GATE_WIDTH), const),
        ] + _stream_specs(ATTN_WIDTH, 0) + _stream_specs(POOL_WIDTH, 0) + [
            pl.BlockSpec((tm, SSM_WIDTH), _scan_block),
            pl.BlockSpec((tm, SSM_WIDTH), _scan_block),
            pl.BlockSpec((tm, SSM_WIDTH), _scan_block),
            _layer_spec(1, SSM_WIDTH, layer),
            _layer_spec(SSM_WIDTH, SSM_WIDTH, layer),
            _layer_spec(1, SSM_WIDTH, layer),
            _layer_spec(ATTN_WIDTH, D_MODEL, layer),
            _layer_spec(POOL_WIDTH, D_MODEL, layer),
            _layer_spec(SSM_WIDTH, D_MODEL, layer),
            _layer_spec(D_MODEL, D_MODEL, layer),
        ],
        out_specs=pl.BlockSpec((tm, D_MODEL), row),
        out_shape=jax.ShapeDtypeStruct((N_TOK, D_MODEL), F32),
        compiler_params=_params(("parallel",)),
        name="merge_branches",
    )(x_pair[0], x_pair[1], mod, norm1_g, w_gates, oa_p, oa_s, op_p, op_s, ssm_u, y_f, y_b, d_skip, w_glu, b_glu,
      w_up_attn, w_up_pool, w_up_ssm, w_out)


def _router_kernel(x_ref, mod_ref, g_ref, wrh_ref, wrl_ref, rb_ref, h_ref, gw_ref):
    h2 = _norm_mod(x_ref[...], g_ref[...], mod_ref[0, 3:4, :], mod_ref[0, 4:5, :])
    h_hi = h2.astype(BF16)
    h_ref[...] = h_hi
    h_lo = (h2 - h_hi.astype(F32)).astype(BF16)
    logits = (jnp.dot(h_hi, wrh_ref[...], preferred_element_type=F32)
              + jnp.dot(h_lo, wrh_ref[...], preferred_element_type=F32)
              + jnp.dot(h_hi, wrl_ref[...], preferred_element_type=F32))
    scores = _sigmoid(logits)
    sc_t = jnp.transpose(scores)[:N_EXPERTS, :]
    work = jnp.transpose(scores + rb_ref[...])[:N_EXPERTS, :]
    idx = lax.broadcasted_iota(jnp.int32, work.shape, 0).astype(F32)
    w_t = jnp.zeros_like(sc_t)
    for _ in range(TOP_K):
        mx = jnp.max(work, axis=0, keepdims=True)
        first = jnp.min(jnp.where(work == mx, idx, float(N_EXPERTS)), axis=0, keepdims=True)
        pick = idx == first
        w_t = jnp.where(pick, sc_t, w_t)
        work = jnp.where(pick, -jnp.inf, work)
    w_t = w_t / jnp.sum(w_t, axis=0, keepdims=True) * ROUTE_SCALE
    w_pad = jnp.concatenate([w_t, jnp.zeros((ROUTER_LANES - N_EXPERTS, w_t.shape[1]), F32)], axis=0)
    gw_ref[...] = jnp.transpose(w_pad)


def _route(layer, x, mod, norm_g, w_router, router_bias):
    tm = 512
    row = lambda i: (i, 0)
    const = lambda i: (0, 0)
    w_hi = w_router.astype(BF16)
    w_lo = (w_router - w_hi.astype(F32)).astype(BF16)
    return pl.pallas_call(
        _router_kernel,
        grid=(N_TOK // tm,),
        in_specs=[
            pl.BlockSpec((tm, D_MODEL), row),
            _mod_spec(layer, tm),
            _layer_spec(1, D_MODEL, layer),
            pl.BlockSpec((D_MODEL, ROUTER_LANES), const),
            pl.BlockSpec((D_MODEL, ROUTER_LANES), const),
            pl.BlockSpec((1, ROUTER_LANES), const),
        ],
        out_specs=[pl.BlockSpec((tm, D_MODEL), row), pl.BlockSpec((tm, ROUTER_LANES), row)],
        out_shape=[jax.ShapeDtypeStruct((N_TOK, D_MODEL), BF16),
                   jax.ShapeDtypeStruct((N_TOK, ROUTER_LANES), F32)],
        compiler_params=_params(("parallel",)),
        name="moe_router",
    )(x, mod, norm_g, w_hi, w_lo, router_bias)


def _swiglu_mid(h, w_gate, w_up):
    a = jnp.dot(h, w_gate, preferred_element_type=F32)
    b = jnp.dot(h, w_up, preferred_element_type=F32)
    return (a * _sigmoid(a)) * b


def _experts_kernel(h_ref, gw_ref, wg_ref, wu_ref, wd_ref, sg_ref, su_ref, sd_ref, x_ref, mod_ref, fg_ref,
                    o_ref, *, final):
    j = pl.program_id(1)
    h = h_ref[...]

    @pl.when(j == 0)
    def _():
        mid = _swiglu_mid(h, sg_ref[...], su_ref[...]).astype(BF16)
        o_ref[...] = jnp.dot(mid, sd_ref[...], preferred_element_type=F32)

    gw = pltpu.roll(gw_ref[...], (ROUTER_LANES - j * EXPERT_CHUNK) & (ROUTER_LANES - 1), axis=1)
    mids = [(_swiglu_mid(h, wg_ref[e], wu_ref[e]) * gw[:, e:e + 1]).astype(BF16) for e in range(EXPERT_CHUNK)]
    mid = jnp.concatenate(mids, axis=1)
    wd = wd_ref[...].reshape(EXPERT_CHUNK * EXPERT_DIM, D_MODEL)
    o_ref[...] += jnp.dot(mid, wd, preferred_element_type=F32)

    @pl.when(j == pl.num_programs(1) - 1)
    def _():
        x = x_ref[...] + mod_ref[0, 5:6, :] * o_ref[...]
        if final:
            ms = jnp.mean(x * x, axis=-1, keepdims=True)
            x = x * lax.rsqrt(ms + EPS) * fg_ref[...]
        o_ref[...] = x


def _experts(layer, h, gate_w, wg, wu, wd, sg, su, sd, x, mod, final_g, final, row0=0, n_rows=N_TOK):
    tm = 1024
    n_chunks = N_EXPERTS // EXPERT_CHUNK
    base = row0 // tm
    row = lambda i, j: (base + i, 0)
    const = lambda i, j: (0, 0)
    chunk = lambda i, j: (j, 0, 0)
    return pl.pallas_call(
        functools.partial(_experts_kernel, final=final),
        grid=(n_rows // tm, n_chunks),
        in_specs=[
            pl.BlockSpec((tm, D_MODEL), row),
            pl.BlockSpec((tm, ROUTER_LANES), row),
            pl.BlockSpec((EXPERT_CHUNK, D_MODEL, EXPERT_DIM), chunk),
            pl.BlockSpec((EXPERT_CHUNK, D_MODEL, EXPERT_DIM), chunk),
            pl.BlockSpec((EXPERT_CHUNK, EXPERT_DIM, D_MODEL), chunk),
            _layer_spec(D_MODEL, SHARED_DIM, layer, rank=2),
            _layer_spec(D_MODEL, SHARED_DIM, layer, rank=2),
            _layer_spec(SHARED_DIM, D_MODEL, layer, rank=2),
            pl.BlockSpec((tm, D_MODEL), row),
            _mod_spec(layer, tm, rank=2, base=base),
            pl.BlockSpec((1, D_MODEL), const),
        ],
        out_specs=pl.BlockSpec((tm, D_MODEL), lambda i, j: (i, 0)),
        out_shape=jax.ShapeDtypeStruct((n_rows, D_MODEL), F32),
        compiler_params=_params(("parallel", "arbitrary")),
        name="moe_experts",
    )(h, gate_w, wg, wu, wd, sg, su, sd, x, mod, final_g)


def kernel(x_prompt, x_sample, c, cache_k, cache_v, state_ssm, c_ctx, w_mod, b_mod, norm1_g, norm2_g, w_in, q_norm_g, k_norm_g, w_up_attn, pool_w, pool_scale, w_up_pool, ssm_a_re, ssm_a_im, ssm_log_dt, ssm_b_re, ssm_b_im, ssm_c_re, ssm_c_im, ssm_d, w_glu, b_glu, w_up_ssm, w_out, w_router, router_bias, w_gate, w_up, w_down, ws_gate, ws_up, ws_down, final_norm_g):
    x_pair = (x_prompt.reshape(N_PROMPT, D_MODEL), x_sample.reshape(N_SAMPLE, D_MODEL), 0)
    cond = jnp.concatenate([c_ctx[None, :], c, jnp.zeros((COND_ROWS - N_COND, D_MODEL), F32)], axis=0)
    mod_all = _ada_all(cond, w_mod, b_mod)

    cos_t, sin_t = _rope_tables(SEQ_TILE)
    head_avg = jnp.asarray(np.kron(np.eye(LANE // HEAD_DIM, dtype=np.float32),
                                   np.full((HEAD_DIM, HEAD_DIM), 1.0 / HEAD_DIM, np.float32)), BF16)
    final_g = final_norm_g.reshape(1, D_MODEL)
    norm1 = norm1_g.reshape(DEPTH, 1, D_MODEL)
    norm2 = norm2_g.reshape(DEPTH, 1, D_MODEL)
    p_scale = pool_scale.reshape(DEPTH, 1, POOL_WIDTH)
    d_skip = ssm_d.reshape(DEPTH, 1, SSM_WIDTH)
    glu_b = b_glu.reshape(DEPTH, 1, SSM_WIDTH)
    glu_w, up_attn, up_pool, up_ssm, out_w = (w.astype(BF16) for w in (w_glu, w_up_attn, w_up_pool, w_up_ssm, w_out))
    shared = tuple(w.astype(BF16) for w in (ws_gate, ws_up, ws_down))
    a_rows, b_mat, c_mat = _ssm_matrices(ssm_a_re, ssm_a_im, ssm_log_dt, ssm_b_re, ssm_b_im, ssm_c_re, ssm_c_im)

    new_k, new_v, new_s = [], [], []
    for l in range(DEPTH):
        qk_gain = jnp.concatenate([jnp.tile(q_norm_g[l], N_HEADS), jnp.tile(k_norm_g[l], N_KV_HEADS)])[None, :]
        q, k_p, v_p, pool_u, ssm_u, kv_p, kv_s = _in_project(
            l, x_pair, mod_all, norm1, w_in[l][:, :MIX_WIDTH].astype(BF16), head_avg, qk_gain, cos_t, sin_t,
            cache_k[:, l], cache_v[:, l])
        new_k.append(k_p.reshape(BATCH, SEQ, N_KV_HEADS, HEAD_DIM))
        new_v.append(v_p.reshape(BATCH, SEQ, N_KV_HEADS, HEAD_DIM))
        oa_p = _attention(q, kv_p, BATCH, SEQ, 0)
        oa_s, wg, wu, wd = _attention(q, kv_s, DEC_BATCH, DEC_SEQ, N_PROMPT, cast=(l, [w_gate, w_up, w_down]))

        pool_bd = jax.scipy.linalg.block_diag(*[pool_w[l, g] for g in range(len(POOL_WINDOWS))]).astype(BF16)
        op_p = _pool(l, pool_u, pool_bd, p_scale, BATCH, SEQ, 0)
        op_s = _pool(l, pool_u, pool_bd, p_scale, DEC_BATCH, DEC_SEQ, N_PROMPT)

        y_f, y_b, st = _ssm_scan(l, ssm_u, state_ssm[:, l], a_rows, b_mat, c_mat)
        new_s.append(st)

        w_r = jnp.pad(w_router[l], ((0, 0), (0, ROUTER_LANES - N_EXPERTS)))
        r_b = jnp.concatenate([router_bias[l], jnp.full((ROUTER_LANES - N_EXPERTS,), -jnp.inf, F32)])[None, :]
        x = _merge(
            l, x_pair, mod_all, norm1, w_in[l][:, MIX_WIDTH:].astype(BF16), oa_p, oa_s, op_p, op_s, ssm_u,
            y_f, y_b, d_skip, glu_w, glu_b, up_attn, up_pool, up_ssm, out_w)
        h2, gate_w = _route(l, x, mod_all, norm2, w_r, r_b)
        if l < DEPTH - 1:
            x = _experts(l, h2, gate_w, wg, wu, wd, *shared, x, mod_all, final_g, final=False)
            x_pair = (x, x, N_PROMPT // SEQ_TILE)
        else:
            y_prompt = _experts(l, h2, gate_w, wg, wu, wd, *shared, x, mod_all, final_g, True, 0, N_PROMPT)
            y_sample = _experts(l, h2, gate_w, wg, wu, wd, *shared, x, mod_all, final_g, True, N_PROMPT, N_SAMPLE)

    y_prompt = y_prompt.reshape(BATCH, SEQ, D_MODEL)
    y_sample = y_sample.reshape(DEC_BATCH, DEC_SEQ, D_MODEL)
    return (y_prompt, y_sample, jnp.stack(new_k, axis=1), jnp.stack(new_v, axis=1), jnp.stack(new_s, axis=1))
```

```python
import functools
import math

import jax
import jax.numpy as jnp
import numpy as np
from jax import lax
from jax.experimental import pallas as pl
from jax.experimental.pallas import tpu as pltpu

D_MODEL = 1024
BATCH = 32
SEQ = 256
DEPTH = 2
DEC_BATCH = 4
DEC_SEQ = 4096
PAST_LEN = 256
GRID_W = 64
EPS = 1e-6
N_MOD = 6
HEAD_DIM = 64
N_HEADS = 8
N_KV_HEADS = 2
ATTN_WIDTH = N_HEADS * HEAD_DIM
KV_WIDTH = N_KV_HEADS * HEAD_DIM
ROPE_BASE = 10000.0
ROPE_PAIRS_PER_AXIS = HEAD_DIM // 4
POOL_WINDOWS = (2, 4, 8, 16)
POOL_GROUP = 64
POOL_WIDTH = len(POOL_WINDOWS) * POOL_GROUP
SSM_H = 16
SSM_GROUPS = 16
SSM_WIDTH = SSM_H * SSM_GROUPS
SSM_STATE = 64
N_EXPERTS = 64
TOP_K = 8
EXPERT_DIM = 256
SHARED_DIM = 256
ROUTE_SCALE = 2.5

N_PROMPT = BATCH * SEQ
N_SAMPLE = DEC_BATCH * DEC_SEQ
N_TOK = N_PROMPT + N_SAMPLE
N_COND = 1 + DEC_BATCH
COND_ROWS = 8
QK_WIDTH = ATTN_WIDTH + KV_WIDTH
MIX_WIDTH = QK_WIDTH + KV_WIDTH + POOL_WIDTH + SSM_WIDTH
GATE_WIDTH = 3 * D_MODEL
SSM_COLS = SSM_GROUPS * SSM_STATE
SSM_LANES = 2 * SSM_COLS
SSM_SEQS = 4
SSM_CHUNK = 128
ROUTER_LANES = 128
EXPERT_CHUNK = 8

SEQ_TILE = SEQ
SSM_ROWS = 2 * SSM_SEQS
SSM_GROUPS_PROMPT = BATCH // SSM_SEQS
SCAN_T = SSM_GROUPS_PROMPT * SEQ + DEC_SEQ
SCAN_CHUNKS = SCAN_T // SSM_CHUNK
PROMPT_CHUNKS = SSM_GROUPS_PROMPT * SEQ // SSM_CHUNK
SSM_PITCH = SSM_CHUNK + 4
SSM_HALF = SSM_SEQS * SSM_PITCH

LANE = 128
SSM_SLABS = SSM_LANES // LANE
VMEM_LIMIT = 56 * 1024 * 1024

F32 = jnp.float32
BF16 = jnp.bfloat16
HIGHEST = lax.Precision.HIGHEST


def _sigmoid(x):
    return 1.0 / (1.0 + jnp.exp(-x))


def _params(dims, vmem=VMEM_LIMIT):
    return pltpu.CompilerParams(dimension_semantics=dims, vmem_limit_bytes=vmem)


def _mod_index(i, tm):
    p = N_PROMPT // tm
    t = DEC_SEQ // tm
    return jnp.where(i < p, 0, 1 + (i - p) // t)


def _mod_spec(layer, tm, rank=1, base=0):
    if rank == 1:
        return pl.BlockSpec((None, 1, N_MOD, D_MODEL), lambda i: (layer, _mod_index(base + i, tm), 0, 0))
    return pl.BlockSpec((None, 1, N_MOD, D_MODEL), lambda i, j: (layer, _mod_index(base + i, tm), 0, 0))


def _layer_spec(rows, cols, layer, rank=1, row_block=0, col_block=0):
    if rank == 1:
        return pl.BlockSpec((None, rows, cols), lambda i: (layer, row_block, col_block))
    return pl.BlockSpec((None, rows, cols), lambda i, j: (layer, row_block, col_block))


def _scan_block(i):
    k = i - BATCH
    tiles = DEC_SEQ // SEQ_TILE
    rb = jnp.where(i < BATCH, i // SSM_SEQS, SSM_GROUPS_PROMPT + k % tiles)
    slot = jnp.where(i < BATCH, i % SSM_SEQS, k // tiles)
    return rb, slot


def _norm_mod(x, g, shift, scale):
    ms = jnp.mean(x * x, axis=-1, keepdims=True)
    return (x * lax.rsqrt(ms + EPS) * g) * (1.0 + scale) + shift


def _stream_rows(xp_ref, xs_ref):
    return jnp.where(pl.program_id(0) < N_PROMPT // SEQ_TILE, xp_ref[...], xs_ref[...])


def _stream_specs(width, s_base):
    p_tiles = N_PROMPT // SEQ_TILE
    return [pl.BlockSpec((SEQ_TILE, width), lambda i: (jnp.minimum(i, p_tiles - 1), 0)),
            pl.BlockSpec((SEQ_TILE, width), lambda i: (s_base + jnp.maximum(i - p_tiles, 0), 0))]


def _mod_kernel(cond_ref, w_ref, b_ref, o_ref):
    c = cond_ref[...]
    s = c * _sigmoid(c)
    o_ref[0] = jnp.dot(s, w_ref[0], preferred_element_type=F32, precision=HIGHEST) + b_ref[0]


def _ada_all(cond, w_mod, b_mod):
    tn = 1536
    width = N_MOD * D_MODEL
    out = pl.pallas_call(
        _mod_kernel,
        grid=(DEPTH, width // tn),
        in_specs=[
            pl.BlockSpec((COND_ROWS, D_MODEL), lambda l, j: (0, 0)),
            pl.BlockSpec((1, D_MODEL, tn), lambda l, j: (l, 0, j)),
            pl.BlockSpec((1, 1, tn), lambda l, j: (l, 0, j)),
        ],
        out_specs=pl.BlockSpec((1, COND_ROWS, tn), lambda l, j: (l, 0, j)),
        out_shape=jax.ShapeDtypeStruct((DEPTH, COND_ROWS, width), F32),
        compiler_params=_params(("parallel", "parallel")),
        name="ada_mod",
    )(cond, w_mod, b_mod.reshape(DEPTH, 1, width))
    return out.reshape(DEPTH, COND_ROWS, N_MOD, D_MODEL)


def _kv_layouts(k, v):
    kt = jnp.transpose(k).astype(BF16)
    lane = lax.broadcasted_iota(jnp.int32, v.shape, 1)
    first = lane < HEAD_DIM
    v_sw = pltpu.roll(v, HEAD_DIM, axis=1)
    one = jnp.ones_like(v)
    kts, ves, vos = [], [], []
    for g in range(N_KV_HEADS):
        kg = kt[g * HEAD_DIM:(g + 1) * HEAD_DIM, :]
        kts.append(jnp.concatenate([kg, kg], axis=0))
        lo_half = v if g == 0 else v_sw
        hi_half = v_sw if g == 0 else v
        ves.append(jnp.where(first, lo_half, one).astype(BF16))
        vos.append(jnp.where(first, one, hi_half).astype(BF16))
    return kts, ves, vos


def _inproj_kernel(xp_ref, xs_ref, mod_ref, g_ref, w_ref, bd_ref, qkg_ref, cos_ref, sin_ref, *rest):
    (q_ref, kp_ref, vp_ref, pu_ref, su_ref, ktp_ref, vep_ref, vop_ref, kts_ref, ves_ref, vos_ref) = rest[3:]
    h = _norm_mod(_stream_rows(xp_ref, xs_ref), g_ref[...], mod_ref[0, 0:1, :], mod_ref[0, 1:2, :])
    z = jnp.dot(h.astype(BF16), w_ref[...], preferred_element_type=F32)
    qk = z[:, :QK_WIDTH]
    qq = qk * qk
    hi = qq.astype(BF16)
    lo = (qq - hi.astype(F32)).astype(BF16)
    bd = bd_ref[...]
    ms = jnp.concatenate(
        [jnp.dot(hi[:, c * LANE:(c + 1) * LANE], bd, preferred_element_type=F32)
         + jnp.dot(lo[:, c * LANE:(c + 1) * LANE], bd, preferred_element_type=F32)
         for c in range(QK_WIDTH // LANE)], axis=1)
    qkn = qk * lax.rsqrt(ms + EPS) * qkg_ref[...]
    cos, sin = cos_ref[...], sin_ref[...]
    parts = []
    for c in range(QK_WIDTH // LANE):
        blk = qkn[:, c * LANE:(c + 1) * LANE]
        nxt = pltpu.roll(blk, LANE - 1, axis=1)
        prv = pltpu.roll(blk, 1, axis=1)
        lane = lax.broadcasted_iota(jnp.int32, blk.shape, 1)
        parts.append(blk * cos + jnp.where((lane & 1) == 0, nxt, prv) * sin)
    qkr = jnp.concatenate(parts, axis=1)
    q_ref[...] = (qkr[:, :ATTN_WIDTH] * (HEAD_DIM ** -0.5 * math.log2(math.e))).astype(BF16)
    k = qkr[:, ATTN_WIDTH:QK_WIDTH]
    v = z[:, QK_WIDTH:QK_WIDTH + KV_WIDTH]
    kts, ves, vos = _kv_layouts(k, v)
    is_prompt = pl.program_id(0) < N_PROMPT // SEQ_TILE

    @pl.when(is_prompt)
    def _():
        kp_ref[...] = k
        vp_ref[...] = v
        for g in range(N_KV_HEADS):
            ktp_ref[g], vep_ref[g], vop_ref[g] = kts[g], ves[g], vos[g]

    @pl.when(jnp.logical_not(is_prompt))
    def _():
        for g in range(N_KV_HEADS):
            kts_ref[g], ves_ref[g], vos_ref[g] = kts[g], ves[g], vos[g]

    pu_ref[...] = z[:, QK_WIDTH + KV_WIDTH:QK_WIDTH + KV_WIDTH + POOL_WIDTH]
    su_ref[...] = z[:, QK_WIDTH + KV_WIDTH + POOL_WIDTH:MIX_WIDTH]


def _in_project(layer, x_pair, mod, norm_g, w_mix, bd, qk_gain, cos_t, sin_t, cache_k, cache_v):
    tm = SEQ_TILE
    p_tiles = N_PROMPT // tm
    s_tiles = DEC_SEQ // tm
    n_keys = PAST_LEN + DEC_SEQ
    past_tiles = PAST_LEN // tm

    def rope_idx(i):
        return (jnp.where(i < p_tiles, 0, 1 + (i - p_tiles) % s_tiles), 0)

    kc = jnp.transpose(cache_k, (0, 2, 3, 1)).astype(BF16)
    kc = jnp.concatenate([kc, kc], axis=2).reshape(DEC_BATCH * N_KV_HEADS, LANE, PAST_LEN)
    vc = jnp.transpose(cache_v, (0, 2, 1, 3)).astype(BF16)
    ones = jnp.ones_like(vc)
    ve_c = jnp.concatenate([vc, ones], axis=3).reshape(DEC_BATCH * N_KV_HEADS, PAST_LEN, LANE)
    vo_c = jnp.concatenate([ones, vc], axis=3).reshape(DEC_BATCH * N_KV_HEADS, PAST_LEN, LANE)
    kt_s0 = jnp.pad(kc, ((0, 0), (0, 0), (0, DEC_SEQ)))
    ve_s0 = jnp.pad(ve_c, ((0, 0), (0, DEC_SEQ), (0, 0)))
    vo_s0 = jnp.pad(vo_c, ((0, 0), (0, DEC_SEQ), (0, 0)))

    p_blk = lambda i: jnp.minimum(i, p_tiles - 1)
    s_seq = lambda i: jnp.maximum(i - p_tiles, 0) // s_tiles
    s_blk = lambda i: past_tiles + jnp.maximum(i - p_tiles, 0) % s_tiles
    row = lambda i: (i, 0)
    const = lambda i: (0, 0)
    any_spec = pl.BlockSpec(memory_space=pl.ANY)
    outs = pl.pallas_call(
        _inproj_kernel,
        grid=(N_TOK // tm,),
        in_specs=_stream_specs(D_MODEL, x_pair[2]) + [
            _mod_spec(layer, tm),
            _layer_spec(1, D_MODEL, layer),
            pl.BlockSpec((D_MODEL, MIX_WIDTH), const),
            pl.BlockSpec((LANE, LANE), const),
            pl.BlockSpec((1, QK_WIDTH), const),
            pl.BlockSpec((tm, LANE), rope_idx),
            pl.BlockSpec((tm, LANE), rope_idx),
            any_spec, any_spec, any_spec,
        ],
        out_specs=[
            pl.BlockSpec((tm, ATTN_WIDTH), row),
            pl.BlockSpec((tm, KV_WIDTH), lambda i: (p_blk(i), 0)),
            pl.BlockSpec((tm, KV_WIDTH), lambda i: (p_blk(i), 0)),
            pl.BlockSpec((tm, POOL_WIDTH), row),
            pl.BlockSpec((tm, SSM_WIDTH), _scan_block),
            pl.BlockSpec((N_KV_HEADS, LANE, tm), lambda i: (p_blk(i), 0, 0)),
            pl.BlockSpec((N_KV_HEADS, tm, LANE), lambda i: (p_blk(i), 0, 0)),
            pl.BlockSpec((N_KV_HEADS, tm, LANE), lambda i: (p_blk(i), 0, 0)),
            pl.BlockSpec((N_KV_HEADS, LANE, tm), lambda i: (s_seq(i), 0, s_blk(i))),
            pl.BlockSpec((N_KV_HEADS, tm, LANE), lambda i: (s_seq(i), s_blk(i), 0)),
            pl.BlockSpec((N_KV_HEADS, tm, LANE), lambda i: (s_seq(i), s_blk(i), 0)),
        ],
        out_shape=[
            jax.ShapeDtypeStruct((N_TOK, ATTN_WIDTH), BF16),
            jax.ShapeDtypeStruct((N_PROMPT, KV_WIDTH), F32),
            jax.ShapeDtypeStruct((N_PROMPT, KV_WIDTH), F32),
            jax.ShapeDtypeStruct((N_TOK, POOL_WIDTH), F32),
            jax.ShapeDtypeStruct((SCAN_T, SSM_SEQS * SSM_WIDTH), F32),
            jax.ShapeDtypeStruct((BATCH * N_KV_HEADS, LANE, SEQ), BF16),
            jax.ShapeDtypeStruct((BATCH * N_KV_HEADS, SEQ, LANE), BF16),
            jax.ShapeDtypeStruct((BATCH * N_KV_HEADS, SEQ, LANE), BF16),
            jax.ShapeDtypeStruct((DEC_BATCH * N_KV_HEADS, LANE, n_keys), BF16),
            jax.ShapeDtypeStruct((DEC_BATCH * N_KV_HEADS, n_keys, LANE), BF16),
            jax.ShapeDtypeStruct((DEC_BATCH * N_KV_HEADS, n_keys, LANE), BF16),
        ],
        input_output_aliases={9: 8, 10: 9, 11: 10},
        compiler_params=_params(("arbitrary",)),
        name="in_project",
    )(x_pair[0], x_pair[1], mod, norm_g, w_mix, bd, qk_gain, cos_t, sin_t, kt_s0, ve_s0, vo_s0)
    return outs[0], outs[1], outs[2], outs[3], outs[4], tuple(outs[5:8]), tuple(outs[8:11])


def _rope_tables(tm):
    f32 = np.float32
    rows = DEC_SEQ // GRID_W
    row = np.repeat(np.arange(rows, dtype=f32), GRID_W)
    col = np.tile(np.arange(GRID_W, dtype=f32), rows)
    inv_freq = (f32(ROPE_BASE) ** (-np.arange(ROPE_PAIRS_PER_AXIS, dtype=f32) / f32(ROPE_PAIRS_PER_AXIS))).astype(f32)
    ang = np.concatenate([row[:, None] * inv_freq, col[:, None] * inv_freq], axis=-1).astype(f32)
    cos = np.repeat(np.cos(ang), 2, axis=-1)
    sin = np.repeat(np.sin(ang), 2, axis=-1) * np.tile(np.array([-1.0, 1.0], f32), HEAD_DIM // 2)
    cos = np.concatenate([np.ones((tm, HEAD_DIM), f32), cos], axis=0)
    sin = np.concatenate([np.zeros((tm, HEAD_DIM), f32), sin], axis=0)
    n_rep = LANE // HEAD_DIM
    return jnp.asarray(np.tile(cos, (1, n_rep)), F32), jnp.asarray(np.tile(sin, (1, n_rep)), F32)


def _attn_kernel(q_ref, kt_ref, ve_ref, vo_ref, *rest):
    n_cast = (len(rest) - 1) // 2
    o_ref = rest[n_cast]
    for src, dst in zip(rest[:n_cast], rest[n_cast + 1:]):
        dst[...] = src[...].astype(dst.dtype)
    kt = kt_ref[0]
    lane = lax.broadcasted_iota(jnp.int32, (q_ref.shape[0], LANE), 1)
    first = lane < HEAD_DIM
    slabs = []
    for pair in range(q_ref.shape[1] // LANE):
        q = q_ref[:, pair * LANE:(pair + 1) * LANE]
        halves = []
        for keep, v_ref in ((first, ve_ref), (lane >= HEAD_DIM, vo_ref)):
            qh = jnp.where(keep, q, jnp.zeros_like(q))
            s = jnp.dot(qh, kt, preferred_element_type=F32)
            m = jnp.max(s, axis=-1, keepdims=True)
            p = jnp.exp2(s - m).astype(BF16)
            a = jnp.dot(p, v_ref[0], preferred_element_type=F32)
            halves.append(a / pltpu.roll(a, HEAD_DIM, axis=1))
        slabs.append(jnp.where(first, halves[0], halves[1]))
    o_ref[...] = jnp.concatenate(slabs, axis=1).astype(o_ref.dtype)


def _attention(q, kv_ops, n_batch, n_q, row0, cast=None):
    kt, v_e, v_o = kv_ops
    n_keys = kt.shape[2]
    tq = 256
    q_tiles = n_q // tq
    base = row0 // tq
    group_w = ATTN_WIDTH // N_KV_HEADS
    kv_idx = lambda b, g, i: (b * N_KV_HEADS + g, 0, 0)
    in_specs = [
        pl.BlockSpec((tq, group_w), lambda b, g, i: (base + b * q_tiles + i, g)),
        pl.BlockSpec((1, LANE, n_keys), kv_idx),
        pl.BlockSpec((1, n_keys, LANE), kv_idx),
        pl.BlockSpec((1, n_keys, LANE), kv_idx),
    ]
    out_specs = [pl.BlockSpec((tq, group_w), lambda b, g, i: (b * q_tiles + i, g))]
    out_shape = [jax.ShapeDtypeStruct((n_batch * n_q, ATTN_WIDTH), BF16)]
    args = [q, kt, v_e, v_o]
    if cast is not None:
        layer, tensors = cast
        steps = n_batch * N_KV_HEADS * q_tiles
        step = lambda b, g, i: (b * N_KV_HEADS + g) * q_tiles + i
        for w in tensors:
            _, n_e, rows, cols = w.shape
            part = n_e * rows // steps
            assert part * steps == n_e * rows and part % 16 == 0 and rows % part == 0
            in_specs.append(pl.BlockSpec((1, part, cols), lambda b, g, i: (layer * steps + step(b, g, i), 0, 0)))
            out_specs.append(pl.BlockSpec((1, part, cols), lambda b, g, i: (step(b, g, i), 0, 0)))
            out_shape.append(jax.ShapeDtypeStruct((steps, part, cols), BF16))
            args.append(w.reshape(DEPTH * steps, part, cols))
    outs = pl.pallas_call(
        _attn_kernel,
        grid=(n_batch, N_KV_HEADS, q_tiles),
        in_specs=in_specs,
        out_specs=out_specs,
        out_shape=out_shape,
        compiler_params=_params(("parallel", "parallel", "parallel")),
        name="attention",
    )(*args)
    if cast is None:
        return outs[0]
    return [outs[0]] + [o.reshape(w.shape[1:]) for o, w in zip(outs[1:], cast[1])]


def _pool_kernel(u_ref, w_ref, sc_ref, o_ref, pad_ref, *, n_seq):
    halo = 8
    u = u_ref[...]
    zeros = jnp.zeros((halo, POOL_WIDTH), F32)
    pad_ref[0:halo, :] = zeros
    pad_ref[halo + n_seq:2 * halo + n_seq, :] = zeros
    pad_ref[halo:halo + n_seq, :] = u

    def sh(j):
        return pad_ref[halo + j:halo + j + n_seq, :]

    t2 = sh(-1) + u
    t4 = t2 + sh(-2) + sh(1)
    t8 = t4 + sh(-4) + sh(-3) + sh(2) + sh(3)
    t16 = t8 + sh(-8) + sh(-7) + sh(-6) + sh(-5) + sh(4) + sh(5) + sh(6) + sh(7)
    grp = lax.broadcasted_iota(jnp.int32, u.shape, 1) >> 6
    t = lax.broadcasted_iota(jnp.int32, u.shape, 0)
    tot = jnp.where(grp == 0, t2, jnp.where(grp == 1, t4, jnp.where(grp == 2, t8, t16)))
    half = jnp.where(grp == 0, 1, jnp.where(grp == 1, 2, jnp.where(grp == 2, 4, 8)))
    lo = jnp.maximum(t - half, 0)
    hi = jnp.minimum(t + half, n_seq)
    pooled = tot / (hi - lo).astype(F32) - u
    mixed = jnp.dot(pooled.astype(BF16), w_ref[...], preferred_element_type=F32)
    o_ref[...] = (mixed * sc_ref[...]).astype(o_ref.dtype)


def _pool(layer, u, w_bd, scale, n_batch, n_seq, row0):
    base = row0 // n_seq
    return pl.pallas_call(
        functools.partial(_pool_kernel, n_seq=n_seq),
        grid=(n_batch,),
        in_specs=[
            pl.BlockSpec((n_seq, POOL_WIDTH), lambda b: (base + b, 0)),
            pl.BlockSpec((POOL_WIDTH, POOL_WIDTH), lambda b: (0, 0)),
            _layer_spec(1, POOL_WIDTH, layer),
        ],
        out_specs=pl.BlockSpec((n_seq, POOL_WIDTH), lambda b: (b, 0)),
        out_shape=jax.ShapeDtypeStruct((n_batch * n_seq, POOL_WIDTH), BF16),
        scratch_shapes=[pltpu.VMEM((n_seq + 16, POOL_WIDTH), F32)],
        compiler_params=_params(("parallel",)),
        name="pool_mixer",
    )(u, w_bd, scale)


def _scan_chunk_first(i):
    per_seq = SEQ // SSM_CHUNK
    return jnp.where(i < PROMPT_CHUNKS, i % per_seq == 0, i == PROMPT_CHUNKS)


def _scan_bwd_chunk(i):
    per_seq = SEQ // SSM_CHUNK
    return jnp.where(i < PROMPT_CHUNKS, (i // per_seq) * per_seq + (per_seq - 1 - i % per_seq),
                     PROMPT_CHUNKS + SCAN_CHUNKS - 1 - i)


def _scan_group(i):
    return jnp.where(i < PROMPT_CHUNKS, i // (SEQ // SSM_CHUNK), SSM_GROUPS_PROMPT)


def _split3(x):
    hi = x.astype(BF16)
    r1 = x - hi.astype(F32)
    mid = r1.astype(BF16)
    lo = (r1 - mid.astype(F32)).astype(BF16)
    return hi, mid, lo


def _ssm_kernel(uf_ref, ub_ref, a_ref, bf_ref, bb_ref, cf_ref, cb_ref, jin_ref, jout_ref, h0_ref,
                yf_ref, yb_ref, hfin_ref, bu_ref, st_ref, lhs_ref, yy_ref):
    @pl.when(_scan_chunk_first(pl.program_id(0)))
    def _():
        st_ref[...] = h0_ref[0]

    ub_rev = jnp.dot(jin_ref[...], ub_ref[...].astype(BF16), preferred_element_type=F32)
    gap = jnp.zeros((SSM_PITCH - SSM_CHUNK, SSM_WIDTH), F32)
    for s in range(SSM_SEQS):
        cols = slice(s * SSM_WIDTH, (s + 1) * SSM_WIDTH)
        lhs_ref[s * SSM_PITCH:s * SSM_PITCH + SSM_CHUNK, :] = uf_ref[:, cols]
        lhs_ref[s * SSM_PITCH + SSM_CHUNK:(s + 1) * SSM_PITCH, :] = gap
        lhs_ref[SSM_HALF + s * SSM_PITCH:SSM_HALF + s * SSM_PITCH + SSM_CHUNK, :] = ub_rev[:, cols]
        lhs_ref[SSM_HALF + s * SSM_PITCH + SSM_CHUNK:SSM_HALF + (s + 1) * SSM_PITCH, :] = gap
    bu_f = jnp.dot(lhs_ref[:SSM_HALF, :].astype(BF16), bf_ref[...], preferred_element_type=F32)
    bu_b = jnp.dot(lhs_ref[SSM_HALF:, :].astype(BF16), bb_ref[...], preferred_element_type=F32)
    for k in range(SSM_SLABS):
        bu_ref[k, :SSM_HALF, :] = bu_f[:, k * LANE:(k + 1) * LANE]
        bu_ref[k, SSM_HALF:, :] = bu_b[:, k * LANE:(k + 1) * LANE]

    half = SSM_SLABS // 2
    a_re = [a_ref[:, k * LANE:(k + 1) * LANE] for k in range(half)]
    a_im = [a_ref[:, SSM_COLS + k * LANE:SSM_COLS + (k + 1) * LANE] for k in range(half)]

    def step(t, carry):
        rows = pl.ds(t, SSM_ROWS, stride=SSM_PITCH)
        new_re, new_im = [], []
        for k in range(half):
            h_re, h_im = carry[k], carry[half + k]
            n_re = a_re[k] * h_re - a_im[k] * h_im + bu_ref[k, rows, :]
            n_im = a_re[k] * h_im + a_im[k] * h_re + bu_ref[half + k, rows, :]
            bu_ref[k, rows, :] = n_re
            bu_ref[half + k, rows, :] = n_im
            new_re.append(n_re)
            new_im.append(n_im)
        return tuple(new_re + new_im)

    init = tuple(st_ref[:, k * LANE:(k + 1) * LANE] for k in range(SSM_SLABS))
    fin = lax.fori_loop(0, SSM_CHUNK, step, init, unroll=2)
    for k in range(SSM_SLABS):
        st_ref[:, k * LANE:(k + 1) * LANE] = fin[k]
    hfin_ref[0] = st_ref[...]

    h_f = jnp.concatenate([bu_ref[k, :SSM_HALF, :] for k in range(SSM_SLABS)], axis=1).astype(BF16)
    h_b = jnp.concatenate([bu_ref[k, SSM_HALF:, :] for k in range(SSM_SLABS)], axis=1).astype(BF16)
    yy_ref[...] = jnp.dot(h_f, cf_ref[...], preferred_element_type=F32)
    for s in range(SSM_SEQS):
        yf_ref[:, s * SSM_WIDTH:(s + 1) * SSM_WIDTH] = yy_ref[s * SSM_PITCH:s * SSM_PITCH + SSM_CHUNK, :]
    y_b = jnp.dot(h_b, cb_ref[...], preferred_element_type=F32)
    y_nat = sum(jnp.dot(jout_ref[...], piece, preferred_element_type=F32) for piece in _split3(y_b))
    for s in range(SSM_SEQS):
        yb_ref[:, s * SSM_WIDTH:(s + 1) * SSM_WIDTH] = y_nat[s * SSM_CHUNK:(s + 1) * SSM_CHUNK, :]


def _ssm_scan(layer, su8, h0_sample, a_rows, b_mat, c_mat):
    rows = SSM_ROWS
    width = SSM_SEQS * SSM_WIDTH
    n_groups = SSM_GROUPS_PROMPT + 1
    hs = h0_sample.astype(F32).reshape(SSM_SEQS, 2, SSM_COLS, 2).transpose(1, 0, 3, 2).reshape(1, rows, SSM_LANES)
    hh = jnp.concatenate([jnp.zeros((SSM_GROUPS_PROMPT, rows, SSM_LANES), F32), hs], axis=0)
    j_in = jnp.asarray(np.eye(SSM_CHUNK, dtype=np.float32)[::-1], dtype=BF16)
    sel = np.zeros((SSM_SEQS * SSM_CHUNK, SSM_HALF), np.float32)
    for s in range(SSM_SEQS):
        for t in range(SSM_CHUNK):
            sel[s * SSM_CHUNK + t, s * SSM_PITCH + SSM_CHUNK - 1 - t] = 1.0
    j_out = jnp.asarray(sel, dtype=BF16)
    fwd = lambda i: (i, 0)
    bwd = lambda i: (_scan_bwd_chunk(i), 0)
    grp = lambda i: (_scan_group(i), 0, 0)
    const = lambda i: (0, 0)
    y_f, y_b, hfin = pl.pallas_call(
        _ssm_kernel,
        grid=(SCAN_CHUNKS,),
        in_specs=[
            pl.BlockSpec((SSM_CHUNK, width), fwd),
            pl.BlockSpec((SSM_CHUNK, width), bwd),
            _layer_spec(rows, SSM_LANES, layer),
            _layer_spec(SSM_WIDTH, SSM_LANES, layer, row_block=0),
            _layer_spec(SSM_WIDTH, SSM_LANES, layer, row_block=1),
            _layer_spec(SSM_LANES, SSM_WIDTH, layer, col_block=0),
            _layer_spec(SSM_LANES, SSM_WIDTH, layer, col_block=1),
            pl.BlockSpec((SSM_CHUNK, SSM_CHUNK), const),
            pl.BlockSpec((SSM_SEQS * SSM_CHUNK, SSM_HALF), const),
            pl.BlockSpec((1, rows, SSM_LANES), grp),
        ],
        out_specs=[
            pl.BlockSpec((SSM_CHUNK, width), fwd),
            pl.BlockSpec((SSM_CHUNK, width), bwd),
            pl.BlockSpec((1, rows, SSM_LANES), grp),
        ],
        out_shape=[
            jax.ShapeDtypeStruct((SCAN_T, width), F32),
            jax.ShapeDtypeStruct((SCAN_T, width), F32),
            jax.ShapeDtypeStruct((n_groups, rows, SSM_LANES), F32),
        ],
        scratch_shapes=[pltpu.VMEM((SSM_SLABS, 2 * SSM_HALF, LANE), F32), pltpu.VMEM((rows, SSM_LANES), F32),
                        pltpu.VMEM((2 * SSM_HALF, SSM_WIDTH), F32), pltpu.VMEM((SSM_HALF, SSM_WIDTH), F32)],
        compiler_params=_params(("arbitrary",)),
        name="ssm_scan",
    )(su8, su8, a_rows, b_mat, b_mat, c_mat, c_mat, j_in, j_out, hh)
    fin = hfin[:SSM_GROUPS_PROMPT].reshape(SSM_GROUPS_PROMPT, 2, SSM_SEQS, 2, SSM_GROUPS, SSM_STATE)
    fin = fin.transpose(0, 2, 1, 4, 5, 3).reshape(BATCH, 2, SSM_GROUPS, SSM_STATE, 2)
    return y_f, y_b, fin


def _ssm_matrices(a_re, a_im, log_dt, b_re, b_im, c_re, c_im):
    n_l = a_re.shape[0]
    lam = lax.complex(a_re.astype(F32), a_im.astype(F32))
    dt = jnp.exp(log_dt.astype(F32))[..., None]
    a_bar = jnp.exp(lam * dt)
    b_bar = ((a_bar - 1.0) / lam)[..., None] * lax.complex(b_re.astype(F32), b_im.astype(F32))
    a_dir = jnp.concatenate([jnp.real(a_bar).reshape(n_l, 2, SSM_COLS),
                             jnp.imag(a_bar).reshape(n_l, 2, SSM_COLS)], axis=-1)
    a_rows = jnp.repeat(a_dir, SSM_SEQS, axis=1)
    eye = jnp.eye(SSM_GROUPS, dtype=F32)
    bt = jnp.transpose(b_bar, (0, 1, 2, 4, 3))
    b_real = jnp.einsum('ldghp,ge->ldghep', jnp.real(bt), eye).reshape(n_l, 2 * SSM_WIDTH, SSM_COLS)
    b_imag = jnp.einsum('ldghp,ge->ldghep', jnp.imag(bt), eye).reshape(n_l, 2 * SSM_WIDTH, SSM_COLS)
    b_mat = jnp.concatenate([b_real, b_imag], axis=-1).astype(BF16)
    cr = jnp.transpose(c_re.astype(F32), (0, 1, 2, 4, 3))
    ci = jnp.transpose(c_im.astype(F32), (0, 1, 2, 4, 3))
    c_real = jnp.einsum('ldgph,ge->lgpdeh', cr, eye).reshape(n_l, SSM_COLS, 2 * SSM_WIDTH)
    c_imag = jnp.einsum('ldgph,ge->lgpdeh', -ci, eye).reshape(n_l, SSM_COLS, 2 * SSM_WIDTH)
    c_mat = jnp.concatenate([c_real, c_imag], axis=1).astype(BF16)
    return a_rows, b_mat, c_mat


def _merge_kernel(xp_ref, xs_ref, mod_ref, g1_ref, wg_ref, oap_ref, oas_ref, opp_ref, ops_ref, su_ref, yf_ref, yb_ref,
                  d_ref, wglu_ref, bglu_ref, wua_ref, wup_ref, wus_ref, wo_ref, o_ref):
    x = _stream_rows(xp_ref, xs_ref)
    h = _norm_mod(x, g1_ref[...], mod_ref[0, 0:1, :], mod_ref[0, 1:2, :])
    gates = _sigmoid(jnp.dot(h.astype(BF16), wg_ref[...], preferred_element_type=F32))
    y = d_ref[...] * su_ref[...] + yf_ref[...] + yb_ref[...]
    y = 0.5 * y * (1.0 + jnp.tanh(math.sqrt(2.0 / math.pi) * (y + 0.044715 * (y * y * y))))
    glu = jnp.dot(y.astype(BF16), wglu_ref[...], preferred_element_type=F32) + bglu_ref[...]
    o_ssm = y * _sigmoid(glu)
    o_attn = _stream_rows(oap_ref, oas_ref)
    o_pool = _stream_rows(opp_ref, ops_ref)
    m = (gates[:, :D_MODEL] * jnp.dot(o_attn, wua_ref[...], preferred_element_type=F32)
         + gates[:, D_MODEL:2 * D_MODEL] * jnp.dot(o_pool, wup_ref[...], preferred_element_type=F32)
         + gates[:, 2 * D_MODEL:] * jnp.dot(o_ssm.astype(BF16), wus_ref[...], preferred_element_type=F32))
    o_ref[...] = x + mod_ref[0, 2:3, :] * jnp.dot(m.astype(BF16), wo_ref[...], preferred_element_type=F32)


def _merge(layer, x_pair, mod, norm1_g, w_gates, oa_p, oa_s, op_p, op_s, ssm_u, y_f, y_b, d_skip, w_glu, b_glu,
           w_up_attn, w_up_pool, w_up_ssm, w_out):
    tm = SEQ_TILE
    row = lambda i: (i, 0)
    const = lambda i: (0, 0)
    return pl.pallas_call(
        _merge_kernel,
        grid=(N_TOK // tm,),
        in_specs=_stream_specs(D_MODEL, x_pair[2]) + [
            _mod_spec(layer, tm),
            _layer_spec(1, D_MODEL, layer),
            pl.BlockSpec((D_MODEL, GATE_WIDTH), const),
        ] + _stream_specs(ATTN_WIDTH, 0) + _stream_specs(POOL_WIDTH, 0) + [
            pl.BlockSpec((tm, SSM_WIDTH), _scan_block),
            pl.BlockSpec((tm, SSM_WIDTH), _scan_block),
            pl.BlockSpec((tm, SSM_WIDTH), _scan_block),
            _layer_spec(1, SSM_WIDTH, layer),
            _layer_spec(SSM_WIDTH, SSM_WIDTH, layer),
            _layer_spec(1, SSM_WIDTH, layer),
            _layer_spec(ATTN_WIDTH, D_MODEL, layer),
            _layer_spec(POOL_WIDTH, D_MODEL, layer),
            _layer_spec(SSM_WIDTH, D_MODEL, layer),
            _layer_spec(D_MODEL, D_MODEL, layer),
        ],
        out_specs=pl.BlockSpec((tm, D_MODEL), row),
        out_shape=jax.ShapeDtypeStruct((N_TOK, D_MODEL), F32),
        compiler_params=_params(("parallel",)),
        name="merge_branches",
    )(x_pair[0], x_pair[1], mod, norm1_g, w_gates, oa_p, oa_s, op_p, op_s, ssm_u, y_f, y_b, d_skip, w_glu, b_glu,
      w_up_attn, w_up_pool, w_up_ssm, w_out)


def _router_kernel(x_ref, mod_ref, g_ref, wrh_ref, wrl_ref, rb_ref, h_ref, gw_ref):
    h2 = _norm_mod(x_ref[...], g_ref[...], mod_ref[0, 3:4, :], mod_ref[0, 4:5, :])
    h_hi = h2.astype(BF16)
    h_ref[...] = h_hi
    h_lo = (h2 - h_hi.astype(F32)).astype(BF16)
    logits = (jnp.dot(h_hi, wrh_ref[...], preferred_element_type=F32)
              + jnp.dot(h_lo, wrh_ref[...], preferred_element_type=F32)
              + jnp.dot(h_hi, wrl_ref[...], preferred_element_type=F32))
    scores = _sigmoid(logits)
    sc_t = jnp.transpose(scores)[:N_EXPERTS, :]
    work = jnp.transpose(scores + rb_ref[...])[:N_EXPERTS, :]
    idx = lax.broadcasted_iota(jnp.int32, work.shape, 0).astype(F32)
    w_t = jnp.zeros_like(sc_t)
    for _ in range(TOP_K):
        mx = jnp.max(work, axis=0, keepdims=True)
        first = jnp.min(jnp.where(work == mx, idx, float(N_EXPERTS)), axis=0, keepdims=True)
        pick = idx == first
        w_t = jnp.where(pick, sc_t, w_t)
        work = jnp.where(pick, -jnp.inf, work)
    w_t = w_t / jnp.sum(w_t, axis=0, keepdims=True) * ROUTE_SCALE
    w_pad = jnp.concatenate([w_t, jnp.zeros((ROUTER_LANES - N_EXPERTS, w_t.shape[1]), F32)], axis=0)
    gw_ref[...] = jnp.transpose(w_pad)


def _route(layer, x, mod, norm_g, w_router, router_bias):
    tm = 1024
    row = lambda i: (i, 0)
    const = lambda i: (0, 0)
    w_hi = w_router.astype(BF16)
    w_lo = (w_router - w_hi.astype(F32)).astype(BF16)
    return pl.pallas_call(
        _router_kernel,
        grid=(N_TOK // tm,),
        in_specs=[
            pl.BlockSpec((tm, D_MODEL), row),
            _mod_spec(layer, tm),
            _layer_spec(1, D_MODEL, layer),
            pl.BlockSpec((D_MODEL, ROUTER_LANES), const),
            pl.BlockSpec((D_MODEL, ROUTER_LANES), const),
            pl.BlockSpec((1, ROUTER_LANES), const),
        ],
        out_specs=[pl.BlockSpec((tm, D_MODEL), row), pl.BlockSpec((tm, ROUTER_LANES), row)],
        out_shape=[jax.ShapeDtypeStruct((N_TOK, D_MODEL), BF16),
                   jax.ShapeDtypeStruct((N_TOK, ROUTER_LANES), F32)],
        compiler_params=_params(("parallel",)),
        name="moe_router",
    )(x, mod, norm_g, w_hi, w_lo, router_bias)


def _swiglu_mid(h, w_gate, w_up):
    a = jnp.dot(h, w_gate, preferred_element_type=F32)
    b = jnp.dot(h, w_up, preferred_element_type=F32)
    return (a * _sigmoid(a)) * b


def _experts_kernel(h_ref, gw_ref, wg_ref, wu_ref, wd_ref, sg_ref, su_ref, sd_ref, x_ref, mod_ref, fg_ref,
                    o_ref, *, final):
    j = pl.program_id(1)
    h = h_ref[...]

    @pl.when(j == 0)
    def _():
        mid = _swiglu_mid(h, sg_ref[...], su_ref[...]).astype(BF16)
        o_ref[...] = jnp.dot(mid, sd_ref[...], preferred_element_type=F32)

    gw = pltpu.roll(gw_ref[...], (ROUTER_LANES - j * EXPERT_CHUNK) & (ROUTER_LANES - 1), axis=1)
    mids = [(_swiglu_mid(h, wg_ref[e], wu_ref[e]) * gw[:, e:e + 1]).astype(BF16) for e in range(EXPERT_CHUNK)]
    mid = jnp.concatenate(mids, axis=1)
    wd = wd_ref[...].reshape(EXPERT_CHUNK * EXPERT_DIM, D_MODEL)
    o_ref[...] += jnp.dot(mid, wd, preferred_element_type=F32)

    @pl.when(j == pl.num_programs(1) - 1)
    def _():
        x = x_ref[...] + mod_ref[0, 5:6, :] * o_ref[...]
        if final:
            ms = jnp.mean(x * x, axis=-1, keepdims=True)
            x = x * lax.rsqrt(ms + EPS) * fg_ref[...]
        o_ref[...] = x


def _experts(layer, h, gate_w, wg, wu, wd, sg, su, sd, x, mod, final_g, final, row0=0, n_rows=N_TOK):
    tm = 1024
    n_chunks = N_EXPERTS // EXPERT_CHUNK
    base = row0 // tm
    row = lambda i, j: (base + i, 0)
    const = lambda i, j: (0, 0)
    chunk = lambda i, j: (j, 0, 0)
    return pl.pallas_call(
        functools.partial(_experts_kernel, final=final),
        grid=(n_rows // tm, n_chunks),
        in_specs=[
            pl.BlockSpec((tm, D_MODEL), row),
            pl.BlockSpec((tm, ROUTER_LANES), row),
            pl.BlockSpec((EXPERT_CHUNK, D_MODEL, EXPERT_DIM), chunk),
            pl.BlockSpec((EXPERT_CHUNK, D_MODEL, EXPERT_DIM), chunk),
            pl.BlockSpec((EXPERT_CHUNK, EXPERT_DIM, D_MODEL), chunk),
            _layer_spec(D_MODEL, SHARED_DIM, layer, rank=2),
            _layer_spec(D_MODEL, SHARED_DIM, layer, rank=2),
            _layer_spec(SHARED_DIM, D_MODEL, layer, rank=2),
            pl.BlockSpec((tm, D_MODEL), row),
            _mod_spec(layer, tm, rank=2, base=base),
            pl.BlockSpec((1, D_MODEL), const),
        ],
        out_specs=pl.BlockSpec((tm, D_MODEL), lambda i, j: (i, 0)),
        out_shape=jax.ShapeDtypeStruct((n_rows, D_MODEL), F32),
        compiler_params=_params(("parallel", "arbitrary")),
        name="moe_experts",
    )(h, gate_w, wg, wu, wd, sg, su, sd, x, mod, final_g)


def kernel(x_prompt, x_sample, c, cache_k, cache_v, state_ssm, c_ctx, w_mod, b_mod, norm1_g, norm2_g, w_in, q_norm_g, k_norm_g, w_up_attn, pool_w, pool_scale, w_up_pool, ssm_a_re, ssm_a_im, ssm_log_dt, ssm_b_re, ssm_b_im, ssm_c_re, ssm_c_im, ssm_d, w_glu, b_glu, w_up_ssm, w_out, w_router, router_bias, w_gate, w_up, w_down, ws_gate, ws_up, ws_down, final_norm_g):
    x_pair = (x_prompt.reshape(N_PROMPT, D_MODEL), x_sample.reshape(N_SAMPLE, D_MODEL), 0)
    cond = jnp.concatenate([c_ctx[None, :], c, jnp.zeros((COND_ROWS - N_COND, D_MODEL), F32)], axis=0)
    mod_all = _ada_all(cond, w_mod, b_mod)

    cos_t, sin_t = _rope_tables(SEQ_TILE)
    head_avg = jnp.asarray(np.kron(np.eye(LANE // HEAD_DIM, dtype=np.float32),
                                   np.full((HEAD_DIM, HEAD_DIM), 1.0 / HEAD_DIM, np.float32)), BF16)
    final_g = final_norm_g.reshape(1, D_MODEL)
    norm1 = norm1_g.reshape(DEPTH, 1, D_MODEL)
    norm2 = norm2_g.reshape(DEPTH, 1, D_MODEL)
    p_scale = pool_scale.reshape(DEPTH, 1, POOL_WIDTH)
    d_skip = ssm_d.reshape(DEPTH, 1, SSM_WIDTH)
    glu_b = b_glu.reshape(DEPTH, 1, SSM_WIDTH)
    glu_w, up_attn, up_pool, up_ssm, out_w = (w.astype(BF16) for w in (w_glu, w_up_attn, w_up_pool, w_up_ssm, w_out))
    shared = tuple(w.astype(BF16) for w in (ws_gate, ws_up, ws_down))
    a_rows, b_mat, c_mat = _ssm_matrices(ssm_a_re, ssm_a_im, ssm_log_dt, ssm_b_re, ssm_b_im, ssm_c_re, ssm_c_im)

    new_k, new_v, new_s = [], [], []
    for l in range(DEPTH):
        qk_gain = jnp.concatenate([jnp.tile(q_norm_g[l], N_HEADS), jnp.tile(k_norm_g[l], N_KV_HEADS)])[None, :]
        q, k_p, v_p, pool_u, ssm_u, kv_p, kv_s = _in_project(
            l, x_pair, mod_all, norm1, w_in[l][:, :MIX_WIDTH].astype(BF16), head_avg, qk_gain, cos_t, sin_t,
            cache_k[:, l], cache_v[:, l])
        new_k.append(k_p.reshape(BATCH, SEQ, N_KV_HEADS, HEAD_DIM))
        new_v.append(v_p.reshape(BATCH, SEQ, N_KV_HEADS, HEAD_DIM))
        oa_p = _attention(q, kv_p, BATCH, SEQ, 0)
        oa_s, wg, wu, wd = _attention(q, kv_s, DEC_BATCH, DEC_SEQ, N_PROMPT, cast=(l, [w_gate, w_up, w_down]))

        pool_bd = jax.scipy.linalg.block_diag(*[pool_w[l, g] for g in range(len(POOL_WINDOWS))]).astype(BF16)
        op_p = _pool(l, pool_u, pool_bd, p_scale, BATCH, SEQ, 0)
        op_s = _pool(l, pool_u, pool_bd, p_scale, DEC_BATCH, DEC_SEQ, N_PROMPT)

        y_f, y_b, st = _ssm_scan(l, ssm_u, state_ssm[:, l], a_rows, b_mat, c_mat)
        new_s.append(st)

        w_r = jnp.pad(w_router[l], ((0, 0), (0, ROUTER_LANES - N_EXPERTS)))
        r_b = jnp.concatenate([router_bias[l], jnp.full((ROUTER_LANES - N_EXPERTS,), -jnp.inf, F32)])[None, :]
        x = _merge(
            l, x_pair, mod_all, norm1, w_in[l][:, MIX_WIDTH:].astype(BF16), oa_p, oa_s, op_p, op_s, ssm_u,
            y_f, y_b, d_skip, glu_w, glu_b, up_attn, up_pool, up_ssm, out_w)
        h2, gate_w = _route(l, x, mod_all, norm2, w_r, r_b)
        if l < DEPTH - 1:
            x = _experts(l, h2, gate_w, wg, wu, wd, *shared, x, mod_all, final_g, final=False)
            x_pair = (x, x, N_PROMPT // SEQ_TILE)
        else:
            y_prompt = _experts(l, h2, gate_w, wg, wu, wd, *shared, x, mod_all, final_g, True, 0, N_PROMPT)
            y_sample = _experts(l, h2, gate_w, wg, wu, wd, *shared, x, mod_all, final_g, True, N_PROMPT, N_SAMPLE)

    y_prompt = y_prompt.reshape(BATCH, SEQ, D_MODEL)
    y_sample = y_sample.reshape(DEC_BATCH, DEC_SEQ, D_MODEL)
    return (y_prompt, y_sample, jnp.stack(new_k, axis=1), jnp.stack(new_v, axis=1), jnp.stack(new_s, axis=1))
```

```python
import functools
import math

import jax
import jax.numpy as jnp
import numpy as np
from jax import lax
from jax.experimental import pallas as pl
from jax.experimental.pallas import tpu as pltpu

D_MODEL = 1024
BATCH = 32
SEQ = 256
DEPTH = 2
DEC_BATCH = 4
DEC_SEQ = 4096
PAST_LEN = 256
GRID_W = 64
EPS = 1e-6
N_MOD = 6
HEAD_DIM = 64
N_HEADS = 8
N_KV_HEADS = 2
ATTN_WIDTH = N_HEADS * HEAD_DIM
KV_WIDTH = N_KV_HEADS * HEAD_DIM
ROPE_BASE = 10000.0
ROPE_PAIRS_PER_AXIS = HEAD_DIM // 4
POOL_WINDOWS = (2, 4, 8, 16)
POOL_GROUP = 64
POOL_WIDTH = len(POOL_WINDOWS) * POOL_GROUP
SSM_H = 16
SSM_GROUPS = 16
SSM_WIDTH = SSM_H * SSM_GROUPS
SSM_STATE = 64
N_EXPERTS = 64
TOP_K = 8
EXPERT_DIM = 256
SHARED_DIM = 256
ROUTE_SCALE = 2.5

N_PROMPT = BATCH * SEQ
N_SAMPLE = DEC_BATCH * DEC_SEQ
N_TOK = N_PROMPT + N_SAMPLE
N_COND = 1 + DEC_BATCH
COND_ROWS = 8
QK_WIDTH = ATTN_WIDTH + KV_WIDTH
MIX_WIDTH = QK_WIDTH + KV_WIDTH + POOL_WIDTH + SSM_WIDTH
GATE_WIDTH = 3 * D_MODEL
SSM_COLS = SSM_GROUPS * SSM_STATE
SSM_LANES = 2 * SSM_COLS
SSM_SEQS = 4
SSM_CHUNK = 128
ROUTER_LANES = 128
EXPERT_CHUNK = 8

SEQ_TILE = SEQ
SSM_ROWS = 2 * SSM_SEQS
SSM_GROUPS_PROMPT = BATCH // SSM_SEQS
SCAN_T = SSM_GROUPS_PROMPT * SEQ + DEC_SEQ
SCAN_CHUNKS = SCAN_T // SSM_CHUNK
PROMPT_CHUNKS = SSM_GROUPS_PROMPT * SEQ // SSM_CHUNK
SSM_PITCH = SSM_CHUNK + 4
SSM_HALF = SSM_SEQS * SSM_PITCH

LANE = 128
SSM_SLABS = SSM_LANES // LANE
VMEM_LIMIT = 56 * 1024 * 1024

F32 = jnp.float32
BF16 = jnp.bfloat16
HIGHEST = lax.Precision.HIGHEST


def _sigmoid(x):
    return 1.0 / (1.0 + jnp.exp(-x))


def _params(dims, vmem=VMEM_LIMIT):
    return pltpu.CompilerParams(dimension_semantics=dims, vmem_limit_bytes=vmem)


def _mod_index(i, tm):
    p = N_PROMPT // tm
    t = DEC_SEQ // tm
    return jnp.where(i < p, 0, 1 + (i - p) // t)


def _mod_spec(layer, tm, rank=1, base=0):
    if rank == 1:
        return pl.BlockSpec((None, 1, N_MOD, D_MODEL), lambda i: (layer, _mod_index(base + i, tm), 0, 0))
    return pl.BlockSpec((None, 1, N_MOD, D_MODEL), lambda i, j: (layer, _mod_index(base + i, tm), 0, 0))


def _layer_spec(rows, cols, layer, rank=1, row_block=0, col_block=0):
    if rank == 1:
        return pl.BlockSpec((None, rows, cols), lambda i: (layer, row_block, col_block))
    return pl.BlockSpec((None, rows, cols), lambda i, j: (layer, row_block, col_block))


def _scan_block(i):
    k = i - BATCH
    tiles = DEC_SEQ // SEQ_TILE
    rb = jnp.where(i < BATCH, i // SSM_SEQS, SSM_GROUPS_PROMPT + k % tiles)
    slot = jnp.where(i < BATCH, i % SSM_SEQS, k // tiles)
    return rb, slot


def _norm_mod(x, g, shift, scale):
    ms = jnp.mean(x * x, axis=-1, keepdims=True)
    return (x * lax.rsqrt(ms + EPS) * g) * (1.0 + scale) + shift


def _stream_rows(xp_ref, xs_ref):
    return jnp.where(pl.program_id(0) < N_PROMPT // SEQ_TILE, xp_ref[...], xs_ref[...])


def _stream_specs(width, s_base):
    p_tiles = N_PROMPT // SEQ_TILE
    return [pl.BlockSpec((SEQ_TILE, width), lambda i: (jnp.minimum(i, p_tiles - 1), 0)),
            pl.BlockSpec((SEQ_TILE, width), lambda i: (s_base + jnp.maximum(i - p_tiles, 0), 0))]


def _mod_kernel(cond_ref, w_ref, b_ref, o_ref):
    c = cond_ref[...]
    s = c * _sigmoid(c)
    o_ref[0] = jnp.dot(s, w_ref[0], preferred_element_type=F32, precision=HIGHEST) + b_ref[0]


def _ada_all(cond, w_mod, b_mod):
    tn = 1536
    width = N_MOD * D_MODEL
    out = pl.pallas_call(
        _mod_kernel,
        grid=(DEPTH, width // tn),
        in_specs=[
            pl.BlockSpec((COND_ROWS, D_MODEL), lambda l, j: (0, 0)),
            pl.BlockSpec((1, D_MODEL, tn), lambda l, j: (l, 0, j)),
            pl.BlockSpec((1, 1, tn), lambda l, j: (l, 0, j)),
        ],
        out_specs=pl.BlockSpec((1, COND_ROWS, tn), lambda l, j: (l, 0, j)),
        out_shape=jax.ShapeDtypeStruct((DEPTH, COND_ROWS, width), F32),
        compiler_params=_params(("parallel", "parallel")),
        name="ada_mod",
    )(cond, w_mod, b_mod.reshape(DEPTH, 1, width))
    return out.reshape(DEPTH, COND_ROWS, N_MOD, D_MODEL)


def _kv_layouts(k, v):
    kt = jnp.transpose(k).astype(BF16)
    lane = lax.broadcasted_iota(jnp.int32, v.shape, 1)
    first = lane < HEAD_DIM
    v_sw = pltpu.roll(v, HEAD_DIM, axis=1)
    one = jnp.ones_like(v)
    kts, ves, vos = [], [], []
    for g in range(N_KV_HEADS):
        kg = kt[g * HEAD_DIM:(g + 1) * HEAD_DIM, :]
        kts.append(jnp.concatenate([kg, kg], axis=0))
        lo_half = v if g == 0 else v_sw
        hi_half = v_sw if g == 0 else v
        ves.append(jnp.where(first, lo_half, one).astype(BF16))
        vos.append(jnp.where(first, one, hi_half).astype(BF16))
    return kts, ves, vos


def _inproj_kernel(xp_ref, xs_ref, mod_ref, g_ref, w_ref, bd_ref, qkg_ref, cos_ref, sin_ref, *rest):
    (q_ref, kp_ref, vp_ref, pu_ref, su_ref, ktp_ref, vep_ref, vop_ref, kts_ref, ves_ref, vos_ref) = rest[3:]
    h = _norm_mod(_stream_rows(xp_ref, xs_ref), g_ref[...], mod_ref[0, 0:1, :], mod_ref[0, 1:2, :])
    z = jnp.dot(h.astype(BF16), w_ref[...], preferred_element_type=F32)
    qk = z[:, :QK_WIDTH]
    qq = qk * qk
    hi = qq.astype(BF16)
    lo = (qq - hi.astype(F32)).astype(BF16)
    bd = bd_ref[...]
    ms = jnp.concatenate(
        [jnp.dot(hi[:, c * LANE:(c + 1) * LANE], bd, preferred_element_type=F32)
         + jnp.dot(lo[:, c * LANE:(c + 1) * LANE], bd, preferred_element_type=F32)
         for c in range(QK_WIDTH // LANE)], axis=1)
    qkn = qk * lax.rsqrt(ms + EPS) * qkg_ref[...]
    cos, sin = cos_ref[...], sin_ref[...]
    parts = []
    for c in range(QK_WIDTH // LANE):
        blk = qkn[:, c * LANE:(c + 1) * LANE]
        nxt = pltpu.roll(blk, LANE - 1, axis=1)
        prv = pltpu.roll(blk, 1, axis=1)
        lane = lax.broadcasted_iota(jnp.int32, blk.shape, 1)
        parts.append(blk * cos + jnp.where((lane & 1) == 0, nxt, prv) * sin)
    qkr = jnp.concatenate(parts, axis=1)
    q_ref[...] = (qkr[:, :ATTN_WIDTH] * (HEAD_DIM ** -0.5 * math.log2(math.e))).astype(BF16)
    k = qkr[:, ATTN_WIDTH:QK_WIDTH]
    v = z[:, QK_WIDTH:QK_WIDTH + KV_WIDTH]
    kts, ves, vos = _kv_layouts(k, v)
    is_prompt = pl.program_id(0) < N_PROMPT // SEQ_TILE

    @pl.when(is_prompt)
    def _():
        kp_ref[...] = k
        vp_ref[...] = v
        for g in range(N_KV_HEADS):
            ktp_ref[g], vep_ref[g], vop_ref[g] = kts[g], ves[g], vos[g]

    @pl.when(jnp.logical_not(is_prompt))
    def _():
        for g in range(N_KV_HEADS):
            kts_ref[g], ves_ref[g], vos_ref[g] = kts[g], ves[g], vos[g]

    pu_ref[...] = z[:, QK_WIDTH + KV_WIDTH:QK_WIDTH + KV_WIDTH + POOL_WIDTH]
    su_ref[...] = z[:, QK_WIDTH + KV_WIDTH + POOL_WIDTH:MIX_WIDTH]


def _in_project(layer, x_pair, mod, norm_g, w_mix, bd, qk_gain, cos_t, sin_t, cache_k, cache_v):
    tm = SEQ_TILE
    p_tiles = N_PROMPT // tm
    s_tiles = DEC_SEQ // tm
    n_keys = PAST_LEN + DEC_SEQ
    past_tiles = PAST_LEN // tm

    def rope_idx(i):
        return (jnp.where(i < p_tiles, 0, 1 + (i - p_tiles) % s_tiles), 0)

    kc = jnp.transpose(cache_k, (0, 2, 3, 1)).astype(BF16)
    kc = jnp.concatenate([kc, kc], axis=2).reshape(DEC_BATCH * N_KV_HEADS, LANE, PAST_LEN)
    vc = jnp.transpose(cache_v, (0, 2, 1, 3)).astype(BF16)
    ones = jnp.ones_like(vc)
    ve_c = jnp.concatenate([vc, ones], axis=3).reshape(DEC_BATCH * N_KV_HEADS, PAST_LEN, LANE)
    vo_c = jnp.concatenate([ones, vc], axis=3).reshape(DEC_BATCH * N_KV_HEADS, PAST_LEN, LANE)
    kt_s0 = jnp.pad(kc, ((0, 0), (0, 0), (0, DEC_SEQ)))
    ve_s0 = jnp.pad(ve_c, ((0, 0), (0, DEC_SEQ), (0, 0)))
    vo_s0 = jnp.pad(vo_c, ((0, 0), (0, DEC_SEQ), (0, 0)))

    p_blk = lambda i: jnp.minimum(i, p_tiles - 1)
    s_seq = lambda i: jnp.maximum(i - p_tiles, 0) // s_tiles
    s_blk = lambda i: past_tiles + jnp.maximum(i - p_tiles, 0) % s_tiles
    row = lambda i: (i, 0)
    const = lambda i: (0, 0)
    any_spec = pl.BlockSpec(memory_space=pl.ANY)
    outs = pl.pallas_call(
        _inproj_kernel,
        grid=(N_TOK // tm,),
        in_specs=_stream_specs(D_MODEL, x_pair[2]) + [
            _mod_spec(layer, tm),
            _layer_spec(1, D_MODEL, layer),
            pl.BlockSpec((D_MODEL, MIX_WIDTH), const),
            pl.BlockSpec((LANE, LANE), const),
            pl.BlockSpec((1, QK_WIDTH), const),
            pl.BlockSpec((tm, LANE), rope_idx),
            pl.BlockSpec((tm, LANE), rope_idx),
            any_spec, any_spec, any_spec,
        ],
        out_specs=[
            pl.BlockSpec((tm, ATTN_WIDTH), row),
            pl.BlockSpec((tm, KV_WIDTH), lambda i: (p_blk(i), 0)),
            pl.BlockSpec((tm, KV_WIDTH), lambda i: (p_blk(i), 0)),
            pl.BlockSpec((tm, POOL_WIDTH), row),
            pl.BlockSpec((tm, SSM_WIDTH), _scan_block),
            pl.BlockSpec((N_KV_HEADS, LANE, tm), lambda i: (p_blk(i), 0, 0)),
            pl.BlockSpec((N_KV_HEADS, tm, LANE), lambda i: (p_blk(i), 0, 0)),
            pl.BlockSpec((N_KV_HEADS, tm, LANE), lambda i: (p_blk(i), 0, 0)),
            pl.BlockSpec((N_KV_HEADS, LANE, tm), lambda i: (s_seq(i), 0, s_blk(i))),
            pl.BlockSpec((N_KV_HEADS, tm, LANE), lambda i: (s_seq(i), s_blk(i), 0)),
            pl.BlockSpec((N_KV_HEADS, tm, LANE), lambda i: (s_seq(i), s_blk(i), 0)),
        ],
        out_shape=[
            jax.ShapeDtypeStruct((N_TOK, ATTN_WIDTH), BF16),
            jax.ShapeDtypeStruct((N_PROMPT, KV_WIDTH), F32),
            jax.ShapeDtypeStruct((N_PROMPT, KV_WIDTH), F32),
            jax.ShapeDtypeStruct((N_TOK, POOL_WIDTH), F32),
            jax.ShapeDtypeStruct((SCAN_T, SSM_SEQS * SSM_WIDTH), F32),
            jax.ShapeDtypeStruct((BATCH * N_KV_HEADS, LANE, SEQ), BF16),
            jax.ShapeDtypeStruct((BATCH * N_KV_HEADS, SEQ, LANE), BF16),
            jax.ShapeDtypeStruct((BATCH * N_KV_HEADS, SEQ, LANE), BF16),
            jax.ShapeDtypeStruct((DEC_BATCH * N_KV_HEADS, LANE, n_keys), BF16),
            jax.ShapeDtypeStruct((DEC_BATCH * N_KV_HEADS, n_keys, LANE), BF16),
            jax.ShapeDtypeStruct((DEC_BATCH * N_KV_HEADS, n_keys, LANE), BF16),
        ],
        input_output_aliases={9: 8, 10: 9, 11: 10},
        compiler_params=_params(("arbitrary",)),
        name="in_project",
    )(x_pair[0], x_pair[1], mod, norm_g, w_mix, bd, qk_gain, cos_t, sin_t, kt_s0, ve_s0, vo_s0)
    return outs[0], outs[1], outs[2], outs[3], outs[4], tuple(outs[5:8]), tuple(outs[8:11])


def _rope_tables(tm):
    f32 = np.float32
    rows = DEC_SEQ // GRID_W
    row = np.repeat(np.arange(rows, dtype=f32), GRID_W)
    col = np.tile(np.arange(GRID_W, dtype=f32), rows)
    inv_freq = (f32(ROPE_BASE) ** (-np.arange(ROPE_PAIRS_PER_AXIS, dtype=f32) / f32(ROPE_PAIRS_PER_AXIS))).astype(f32)
    ang = np.concatenate([row[:, None] * inv_freq, col[:, None] * inv_freq], axis=-1).astype(f32)
    cos = np.repeat(np.cos(ang), 2, axis=-1)
    sin = np.repeat(np.sin(ang), 2, axis=-1) * np.tile(np.array([-1.0, 1.0], f32), HEAD_DIM // 2)
    cos = np.concatenate([np.ones((tm, HEAD_DIM), f32), cos], axis=0)
    sin = np.concatenate([np.zeros((tm, HEAD_DIM), f32), sin], axis=0)
    n_rep = LANE // HEAD_DIM
    return jnp.asarray(np.tile(cos, (1, n_rep)), F32), jnp.asarray(np.tile(sin, (1, n_rep)), F32)


def _attn_kernel(q_ref, kt_ref, ve_ref, vo_ref, *rest):
    n_cast = (len(rest) - 1) // 2
    o_ref = rest[n_cast]
    for src, dst in zip(rest[:n_cast], rest[n_cast + 1:]):
        dst[...] = src[...].astype(dst.dtype)
    kt = kt_ref[0]
    lane = lax.broadcasted_iota(jnp.int32, (q_ref.shape[0], LANE), 1)
    first = lane < HEAD_DIM
    slabs = []
    for pair in range(q_ref.shape[1] // LANE):
        q = q_ref[:, pair * LANE:(pair + 1) * LANE]
        halves = []
        for keep, v_ref in ((first, ve_ref), (lane >= HEAD_DIM, vo_ref)):
            qh = jnp.where(keep, q, jnp.zeros_like(q))
            s = jnp.dot(qh, kt, preferred_element_type=F32)
            m = jnp.max(s, axis=-1, keepdims=True)
            p = jnp.exp2(s - m).astype(BF16)
            a = jnp.dot(p, v_ref[0], preferred_element_type=F32)
            halves.append(a / pltpu.roll(a, HEAD_DIM, axis=1))
        slabs.append(jnp.where(first, halves[0], halves[1]))
    o_ref[...] = jnp.concatenate(slabs, axis=1).astype(o_ref.dtype)


def _attention(q, kv_ops, n_batch, n_q, row0, cast=None):
    kt, v_e, v_o = kv_ops
    n_keys = kt.shape[2]
    tq = 256
    q_tiles = n_q // tq
    base = row0 // tq
    group_w = ATTN_WIDTH // N_KV_HEADS
    kv_idx = lambda b, g, i: (b * N_KV_HEADS + g, 0, 0)
    in_specs = [
        pl.BlockSpec((tq, group_w), lambda b, g, i: (base + b * q_tiles + i, g)),
        pl.BlockSpec((1, LANE, n_keys), kv_idx),
        pl.BlockSpec((1, n_keys, LANE), kv_idx),
        pl.BlockSpec((1, n_keys, LANE), kv_idx),
    ]
    out_specs = [pl.BlockSpec((tq, group_w), lambda b, g, i: (b * q_tiles + i, g))]
    out_shape = [jax.ShapeDtypeStruct((n_batch * n_q, ATTN_WIDTH), BF16)]
    args = [q, kt, v_e, v_o]
    if cast is not None:
        layer, tensors = cast
        steps = n_batch * N_KV_HEADS * q_tiles
        step = lambda b, g, i: (b * N_KV_HEADS + g) * q_tiles + i
        for w in tensors:
            _, n_e, rows, cols = w.shape
            part = n_e * rows // steps
            assert part * steps == n_e * rows and part % 16 == 0 and rows % part == 0
            in_specs.append(pl.BlockSpec((1, part, cols), lambda b, g, i: (layer * steps + step(b, g, i), 0, 0)))
            out_specs.append(pl.BlockSpec((1, part, cols), lambda b, g, i: (step(b, g, i), 0, 0)))
            out_shape.append(jax.ShapeDtypeStruct((steps, part, cols), BF16))
            args.append(w.reshape(DEPTH * steps, part, cols))
    outs = pl.pallas_call(
        _attn_kernel,
        grid=(n_batch, N_KV_HEADS, q_tiles),
        in_specs=in_specs,
        out_specs=out_specs,
        out_shape=out_shape,
        compiler_params=_params(("parallel", "parallel", "parallel")),
        name="attention",
    )(*args)
    if cast is None:
        return outs[0]
    return [outs[0]] + [o.reshape(w.shape[1:]) for o, w in zip(outs[1:], cast[1])]


def _pool_kernel(u_ref, w_ref, sc_ref, o_ref, pad_ref, *, n_seq):
    halo = 8
    u = u_ref[...]
    zeros = jnp.zeros((halo, POOL_WIDTH), F32)
    pad_ref[0:halo, :] = zeros
    pad_ref[halo + n_seq:2 * halo + n_seq, :] = zeros
    pad_ref[halo:halo + n_seq, :] = u

    def sh(j, c):
        return pad_ref[halo + j:halo + j + n_seq, c * LANE:(c + 1) * LANE]

    t = lax.broadcasted_iota(jnp.int32, (n_seq, LANE), 0)
    first = lax.broadcasted_iota(jnp.int32, (n_seq, LANE), 1) < POOL_GROUP
    parts = []
    for c in range(POOL_WIDTH // LANE):
        h_lo, h_hi = POOL_WINDOWS[2 * c] // 2, POOL_WINDOWS[2 * c + 1] // 2
        uc = u[:, c * LANE:(c + 1) * LANE]
        small = uc
        for j in range(-h_lo, h_lo):
            if j != 0:
                small = small + sh(j, c)
        big = small
        for j in range(-h_hi, h_hi):
            if not -h_lo <= j < h_lo:
                big = big + sh(j, c)
        half = jnp.where(first, h_lo, h_hi)
        count = jnp.minimum(t + half, n_seq) - jnp.maximum(t - half, 0)
        parts.append(jnp.where(first, small, big) / count.astype(F32) - uc)
    pooled = jnp.concatenate(parts, axis=1)
    mixed = jnp.dot(pooled.astype(BF16), w_ref[...], preferred_element_type=F32)
    o_ref[...] = (mixed * sc_ref[...]).astype(o_ref.dtype)


def _pool(layer, u, w_bd, scale, n_batch, n_seq, row0):
    base = row0 // n_seq
    return pl.pallas_call(
        functools.partial(_pool_kernel, n_seq=n_seq),
        grid=(n_batch,),
        in_specs=[
            pl.BlockSpec((n_seq, POOL_WIDTH), lambda b: (base + b, 0)),
            pl.BlockSpec((POOL_WIDTH, POOL_WIDTH), lambda b: (0, 0)),
            _layer_spec(1, POOL_WIDTH, layer),
        ],
        out_specs=pl.BlockSpec((n_seq, POOL_WIDTH), lambda b: (b, 0)),
        out_shape=jax.ShapeDtypeStruct((n_batch * n_seq, POOL_WIDTH), BF16),
        scratch_shapes=[pltpu.VMEM((n_seq + 16, POOL_WIDTH), F32)],
        compiler_params=_params(("parallel",)),
        name="pool_mixer",
    )(u, w_bd, scale)


def _scan_chunk_first(i):
    per_seq = SEQ // SSM_CHUNK
    return jnp.where(i < PROMPT_CHUNKS, i % per_seq == 0, i == PROMPT_CHUNKS)


def _scan_bwd_chunk(i):
    per_seq = SEQ // SSM_CHUNK
    return jnp.where(i < PROMPT_CHUNKS, (i // per_seq) * per_seq + (per_seq - 1 - i % per_seq),
                     PROMPT_CHUNKS + SCAN_CHUNKS - 1 - i)


def _scan_group(i):
    return jnp.where(i < PROMPT_CHUNKS, i // (SEQ // SSM_CHUNK), SSM_GROUPS_PROMPT)


def _split3(x):
    hi = x.astype(BF16)
    r1 = x - hi.astype(F32)
    mid = r1.astype(BF16)
    lo = (r1 - mid.astype(F32)).astype(BF16)
    return hi, mid, lo


def _ssm_kernel(uf_ref, ub_ref, a_ref, bf_ref, bb_ref, cf_ref, cb_ref, jin_ref, jout_ref, h0_ref,
                yf_ref, yb_ref, hfin_ref, bu_ref, st_ref, lhs_ref, yy_ref):
    @pl.when(_scan_chunk_first(pl.program_id(0)))
    def _():
        st_ref[...] = h0_ref[0]

    ub_rev = jnp.dot(jin_ref[...], ub_ref[...].astype(BF16), preferred_element_type=F32)
    gap = jnp.zeros((SSM_PITCH - SSM_CHUNK, SSM_WIDTH), F32)
    for s in range(SSM_SEQS):
        cols = slice(s * SSM_WIDTH, (s + 1) * SSM_WIDTH)
        lhs_ref[s * SSM_PITCH:s * SSM_PITCH + SSM_CHUNK, :] = uf_ref[:, cols]
        lhs_ref[s * SSM_PITCH + SSM_CHUNK:(s + 1) * SSM_PITCH, :] = gap
        lhs_ref[SSM_HALF + s * SSM_PITCH:SSM_HALF + s * SSM_PITCH + SSM_CHUNK, :] = ub_rev[:, cols]
        lhs_ref[SSM_HALF + s * SSM_PITCH + SSM_CHUNK:SSM_HALF + (s + 1) * SSM_PITCH, :] = gap
    bu_f = jnp.dot(lhs_ref[:SSM_HALF, :].astype(BF16), bf_ref[...], preferred_element_type=F32)
    bu_b = jnp.dot(lhs_ref[SSM_HALF:, :].astype(BF16), bb_ref[...], preferred_element_type=F32)
    for k in range(SSM_SLABS):
        bu_ref[k, :SSM_HALF, :] = bu_f[:, k * LANE:(k + 1) * LANE]
        bu_ref[k, SSM_HALF:, :] = bu_b[:, k * LANE:(k + 1) * LANE]

    half = SSM_SLABS // 2
    a_re = [a_ref[:, k * LANE:(k + 1) * LANE] for k in range(half)]
    a_im = [a_ref[:, SSM_COLS + k * LANE:SSM_COLS + (k + 1) * LANE] for k in range(half)]

    def step(t, carry):
        rows = pl.ds(t, SSM_ROWS, stride=SSM_PITCH)
        new_re, new_im = [], []
        for k in range(half):
            h_re, h_im = carry[k], carry[half + k]
            n_re = a_re[k] * h_re - a_im[k] * h_im + bu_ref[k, rows, :]
            n_im = a_re[k] * h_im + a_im[k] * h_re + bu_ref[half + k, rows, :]
            bu_ref[k, rows, :] = n_re
            bu_ref[half + k, rows, :] = n_im
            new_re.append(n_re)
            new_im.append(n_im)
        return tuple(new_re + new_im)

    init = tuple(st_ref[:, k * LANE:(k + 1) * LANE] for k in range(SSM_SLABS))
    fin = lax.fori_loop(0, SSM_CHUNK, step, init, unroll=2)
    for k in range(SSM_SLABS):
        st_ref[:, k * LANE:(k + 1) * LANE] = fin[k]
    hfin_ref[0] = st_ref[...]

    h_f = jnp.concatenate([bu_ref[k, :SSM_HALF, :] for k in range(SSM_SLABS)], axis=1).astype(BF16)
    h_b = jnp.concatenate([bu_ref[k, SSM_HALF:, :] for k in range(SSM_SLABS)], axis=1).astype(BF16)
    yy_ref[...] = jnp.dot(h_f, cf_ref[...], preferred_element_type=F32)
    for s in range(SSM_SEQS):
        yf_ref[:, s * SSM_WIDTH:(s + 1) * SSM_WIDTH] = yy_ref[s * SSM_PITCH:s * SSM_PITCH + SSM_CHUNK, :]
    y_b = jnp.dot(h_b, cb_ref[...], preferred_element_type=F32)
    y_nat = sum(jnp.dot(jout_ref[...], piece, preferred_element_type=F32) for piece in _split3(y_b))
    for s in range(SSM_SEQS):
        yb_ref[:, s * SSM_WIDTH:(s + 1) * SSM_WIDTH] = y_nat[s * SSM_CHUNK:(s + 1) * SSM_CHUNK, :]


def _ssm_scan(layer, su8, h0_sample, a_rows, b_mat, c_mat):
    rows = SSM_ROWS
    width = SSM_SEQS * SSM_WIDTH
    n_groups = SSM_GROUPS_PROMPT + 1
    hs = h0_sample.astype(F32).reshape(SSM_SEQS, 2, SSM_COLS, 2).transpose(1, 0, 3, 2).reshape(1, rows, SSM_LANES)
    hh = jnp.concatenate([jnp.zeros((SSM_GROUPS_PROMPT, rows, SSM_LANES), F32), hs], axis=0)
    j_in = jnp.asarray(np.eye(SSM_CHUNK, dtype=np.float32)[::-1], dtype=BF16)
    sel = np.zeros((SSM_SEQS * SSM_CHUNK, SSM_HALF), np.float32)
    for s in range(SSM_SEQS):
        for t in range(SSM_CHUNK):
            sel[s * SSM_CHUNK + t, s * SSM_PITCH + SSM_CHUNK - 1 - t] = 1.0
    j_out = jnp.asarray(sel, dtype=BF16)
    fwd = lambda i: (i, 0)
    bwd = lambda i: (_scan_bwd_chunk(i), 0)
    grp = lambda i: (_scan_group(i), 0, 0)
    const = lambda i: (0, 0)
    y_f, y_b, hfin = pl.pallas_call(
        _ssm_kernel,
        grid=(SCAN_CHUNKS,),
        in_specs=[
            pl.BlockSpec((SSM_CHUNK, width), fwd),
            pl.BlockSpec((SSM_CHUNK, width), bwd),
            _layer_spec(rows, SSM_LANES, layer),
            _layer_spec(SSM_WIDTH, SSM_LANES, layer, row_block=0),
            _layer_spec(SSM_WIDTH, SSM_LANES, layer, row_block=1),
            _layer_spec(SSM_LANES, SSM_WIDTH, layer, col_block=0),
            _layer_spec(SSM_LANES, SSM_WIDTH, layer, col_block=1),
            pl.BlockSpec((SSM_CHUNK, SSM_CHUNK), const),
            pl.BlockSpec((SSM_SEQS * SSM_CHUNK, SSM_HALF), const),
            pl.BlockSpec((1, rows, SSM_LANES), grp),
        ],
        out_specs=[
            pl.BlockSpec((SSM_CHUNK, width), fwd),
            pl.BlockSpec((SSM_CHUNK, width), bwd),
            pl.BlockSpec((1, rows, SSM_LANES), grp),
        ],
        out_shape=[
            jax.ShapeDtypeStruct((SCAN_T, width), F32),
            jax.ShapeDtypeStruct((SCAN_T, width), F32),
            jax.ShapeDtypeStruct((n_groups, rows, SSM_LANES), F32),
        ],
        scratch_shapes=[pltpu.VMEM((SSM_SLABS, 2 * SSM_HALF, LANE), F32), pltpu.VMEM((rows, SSM_LANES), F32),
                        pltpu.VMEM((2 * SSM_HALF, SSM_WIDTH), F32), pltpu.VMEM((SSM_HALF, SSM_WIDTH), F32)],
        compiler_params=_params(("arbitrary",)),
        name="ssm_scan",
    )(su8, su8, a_rows, b_mat, b_mat, c_mat, c_mat, j_in, j_out, hh)
    fin = hfin[:SSM_GROUPS_PROMPT].reshape(SSM_GROUPS_PROMPT, 2, SSM_SEQS, 2, SSM_GROUPS, SSM_STATE)
    fin = fin.transpose(0, 2, 1, 4, 5, 3).reshape(BATCH, 2, SSM_GROUPS, SSM_STATE, 2)
    return y_f, y_b, fin


def _ssm_matrices(a_re, a_im, log_dt, b_re, b_im, c_re, c_im):
    n_l = a_re.shape[0]
    lam = lax.complex(a_re.astype(F32), a_im.astype(F32))
    dt = jnp.exp(log_dt.astype(F32))[..., None]
    a_bar = jnp.exp(lam * dt)
    b_bar = ((a_bar - 1.0) / lam)[..., None] * lax.complex(b_re.astype(F32), b_im.astype(F32))
    a_dir = jnp.concatenate([jnp.real(a_bar).reshape(n_l, 2, SSM_COLS),
                             jnp.imag(a_bar).reshape(n_l, 2, SSM_COLS)], axis=-1)
    a_rows = jnp.repeat(a_dir, SSM_SEQS, axis=1)
    eye = jnp.eye(SSM_GROUPS, dtype=F32)
    bt = jnp.transpose(b_bar, (0, 1, 2, 4, 3))
    b_real = jnp.einsum('ldghp,ge->ldghep', jnp.real(bt), eye).reshape(n_l, 2 * SSM_WIDTH, SSM_COLS)
    b_imag = jnp.einsum('ldghp,ge->ldghep', jnp.imag(bt), eye).reshape(n_l, 2 * SSM_WIDTH, SSM_COLS)
    b_mat = jnp.concatenate([b_real, b_imag], axis=-1).astype(BF16)
    cr = jnp.transpose(c_re.astype(F32), (0, 1, 2, 4, 3))
    ci = jnp.transpose(c_im.astype(F32), (0, 1, 2, 4, 3))
    c_real = jnp.einsum('ldgph,ge->lgpdeh', cr, eye).reshape(n_l, SSM_COLS, 2 * SSM_WIDTH)
    c_imag = jnp.einsum('ldgph,ge->lgpdeh', -ci, eye).reshape(n_l, SSM_COLS, 2 * SSM_WIDTH)
    c_mat = jnp.concatenate([c_real, c_imag], axis=1).astype(BF16)
    return a_rows, b_mat, c_mat


def _merge_kernel(xp_ref, xs_ref, mod_ref, g1_ref, wg_ref, oap_ref, oas_ref, opp_ref, ops_ref, su_ref, yf_ref, yb_ref,
                  d_ref, wglu_ref, bglu_ref, wua_ref, wup_ref, wus_ref, wo_ref, o_ref):
    x = _stream_rows(xp_ref, xs_ref)
    h = _norm_mod(x, g1_ref[...], mod_ref[0, 0:1, :], mod_ref[0, 1:2, :])
    gates = _sigmoid(jnp.dot(h.astype(BF16), wg_ref[...], preferred_element_type=F32))
    y = d_ref[...] * su_ref[...] + yf_ref[...] + yb_ref[...]
    y = 0.5 * y * (1.0 + jnp.tanh(math.sqrt(2.0 / math.pi) * (y + 0.044715 * (y * y * y))))
    glu = jnp.dot(y.astype(BF16), wglu_ref[...], preferred_element_type=F32) + bglu_ref[...]
    o_ssm = y * _sigmoid(glu)
    o_attn = _stream_rows(oap_ref, oas_ref)
    o_pool = _stream_rows(opp_ref, ops_ref)
    m = (gates[:, :D_MODEL] * jnp.dot(o_attn, wua_ref[...], preferred_element_type=F32)
         + gates[:, D_MODEL:2 * D_MODEL] * jnp.dot(o_pool, wup_ref[...], preferred_element_type=F32)
         + gates[:, 2 * D_MODEL:] * jnp.dot(o_ssm.astype(BF16), wus_ref[...], preferred_element_type=F32))
    o_ref[...] = x + mod_ref[0, 2:3, :] * jnp.dot(m.astype(BF16), wo_ref[...], preferred_element_type=F32)


def _merge(layer, x_pair, mod, norm1_g, w_gates, oa_p, oa_s, op_p, op_s, ssm_u, y_f, y_b, d_skip, w_glu, b_glu,
           w_up_attn, w_up_pool, w_up_ssm, w_out):
    tm = SEQ_TILE
    row = lambda i: (i, 0)
    const = lambda i: (0, 0)
    return pl.pallas_call(
        _merge_kernel,
        grid=(N_TOK // tm,),
        in_specs=_stream_specs(D_MODEL, x_pair[2]) + [
            _mod_spec(layer, tm),
            _layer_spec(1, D_MODEL, layer),
            pl.BlockSpec((D_MODEL, GATE_WIDTH), const),
        ] + _stream_specs(ATTN_WIDTH, 0) + _stream_specs(POOL_WIDTH, 0) + [
            pl.BlockSpec((tm, SSM_WIDTH), _scan_block),
            pl.BlockSpec((tm, SSM_WIDTH), _scan_block),
            pl.BlockSpec((tm, SSM_WIDTH), _scan_block),
            _layer_spec(1, SSM_WIDTH, layer),
            _layer_spec(SSM_WIDTH, SSM_WIDTH, layer),
            _layer_spec(1, SSM_WIDTH, layer),
            _layer_spec(ATTN_WIDTH, D_MODEL, layer),
            _layer_spec(POOL_WIDTH, D_MODEL, layer),
            _layer_spec(SSM_WIDTH, D_MODEL, layer),
            _layer_spec(D_MODEL, D_MODEL, layer),
        ],
        out_specs=pl.BlockSpec((tm, D_MODEL), row),
        out_shape=jax.ShapeDtypeStruct((N_TOK, D_MODEL), F32),
        compiler_params=_params(("parallel",)),
        name="merge_branches",
    )(x_pair[0], x_pair[1], mod, norm1_g, w_gates, oa_p, oa_s, op_p, op_s, ssm_u, y_f, y_b, d_skip, w_glu, b_glu,
      w_up_attn, w_up_pool, w_up_ssm, w_out)


def _router_kernel(x_ref, mod_ref, g_ref, wrh_ref, wrl_ref, rb_ref, h_ref, gw_ref):
    h2 = _norm_mod(x_ref[...], g_ref[...], mod_ref[0, 3:4, :], mod_ref[0, 4:5, :])
    h_hi = h2.astype(BF16)
    h_ref[...] = h_hi
    h_lo = (h2 - h_hi.astype(F32)).astype(BF16)
    logits = (jnp.dot(h_hi, wrh_ref[...], preferred_element_type=F32)
              + jnp.dot(h_lo, wrh_ref[...], preferred_element_type=F32)
              + jnp.dot(h_hi, wrl_ref[...], preferred_element_type=F32))
    scores = _sigmoid(logits)
    sc_t = jnp.transpose(scores)[:N_EXPERTS, :]
    work = jnp.transpose(scores + rb_ref[...])[:N_EXPERTS, :]
    idx = lax.broadcasted_iota(jnp.int32, work.shape, 0).astype(F32)
    w_t = jnp.zeros_like(sc_t)
    for _ in range(TOP_K):
        mx = jnp.max(work, axis=0, keepdims=True)
        first = jnp.min(jnp.where(work == mx, idx, float(N_EXPERTS)), axis=0, keepdims=True)
        pick = idx == first
        w_t = jnp.where(pick, sc_t, w_t)
        work = jnp.where(pick, -jnp.inf, work)
    w_t = w_t / jnp.sum(w_t, axis=0, keepdims=True) * ROUTE_SCALE
    w_pad = jnp.concatenate([w_t, jnp.zeros((ROUTER_LANES - N_EXPERTS, w_t.shape[1]), F32)], axis=0)
    gw_ref[...] = jnp.transpose(w_pad)


def _route(layer, x, mod, norm_g, w_router, router_bias):
    tm = 1024
    row = lambda i: (i, 0)
    const = lambda i: (0, 0)
    w_hi = w_router.astype(BF16)
    w_lo = (w_router - w_hi.astype(F32)).astype(BF16)
    return pl.pallas_call(
        _router_kernel,
        grid=(N_TOK // tm,),
        in_specs=[
            pl.BlockSpec((tm, D_MODEL), row),
            _mod_spec(layer, tm),
            _layer_spec(1, D_MODEL, layer),
            pl.BlockSpec((D_MODEL, ROUTER_LANES), const),
            pl.BlockSpec((D_MODEL, ROUTER_LANES), const),
            pl.BlockSpec((1, ROUTER_LANES), const),
        ],
        out_specs=[pl.BlockSpec((tm, D_MODEL), row), pl.BlockSpec((tm, ROUTER_LANES), row)],
        out_shape=[jax.ShapeDtypeStruct((N_TOK, D_MODEL), BF16),
                   jax.ShapeDtypeStruct((N_TOK, ROUTER_LANES), F32)],
        compiler_params=_params(("parallel",)),
        name="moe_router",
    )(x, mod, norm_g, w_hi, w_lo, router_bias)


def _swiglu_mid(h, w_gate, w_up):
    a = jnp.dot(h, w_gate, preferred_element_type=F32)
    b = jnp.dot(h, w_up, preferred_element_type=F32)
    return (a * _sigmoid(a)) * b


def _experts_kernel(h_ref, gw_ref, wg_ref, wu_ref, wd_ref, sg_ref, su_ref, sd_ref, x_ref, mod_ref, fg_ref,
                    o_ref, *, final):
    j = pl.program_id(1)
    h = h_ref[...]

    @pl.when(j == 0)
    def _():
        mid = _swiglu_mid(h, sg_ref[...], su_ref[...]).astype(BF16)
        o_ref[...] = jnp.dot(mid, sd_ref[...], preferred_element_type=F32)

    gw = pltpu.roll(gw_ref[...], (ROUTER_LANES - j * EXPERT_CHUNK) & (ROUTER_LANES - 1), axis=1)
    mids = [(_swiglu_mid(h, wg_ref[e], wu_ref[e]) * gw[:, e:e + 1]).astype(BF16) for e in range(EXPERT_CHUNK)]
    mid = jnp.concatenate(mids, axis=1)
    wd = wd_ref[...].reshape(EXPERT_CHUNK * EXPERT_DIM, D_MODEL)
    o_ref[...] += jnp.dot(mid, wd, preferred_element_type=F32)

    @pl.when(j == pl.num_programs(1) - 1)
    def _():
        x = x_ref[...] + mod_ref[0, 5:6, :] * o_ref[...]
        if final:
            ms = jnp.mean(x * x, axis=-1, keepdims=True)
            x = x * lax.rsqrt(ms + EPS) * fg_ref[...]
        o_ref[...] = x


def _experts(layer, h, gate_w, wg, wu, wd, sg, su, sd, x, mod, final_g, final, row0=0, n_rows=N_TOK):
    tm = 1024
    n_chunks = N_EXPERTS // EXPERT_CHUNK
    base = row0 // tm
    row = lambda i, j: (base + i, 0)
    const = lambda i, j: (0, 0)
    chunk = lambda i, j: (j, 0, 0)
    return pl.pallas_call(
        functools.partial(_experts_kernel, final=final),
        grid=(n_rows // tm, n_chunks),
        in_specs=[
            pl.BlockSpec((tm, D_MODEL), row),
            pl.BlockSpec((tm, ROUTER_LANES), row),
            pl.BlockSpec((EXPERT_CHUNK, D_MODEL, EXPERT_DIM), chunk),
            pl.BlockSpec((EXPERT_CHUNK, D_MODEL, EXPERT_DIM), chunk),
            pl.BlockSpec((EXPERT_CHUNK, EXPERT_DIM, D_MODEL), chunk),
            _layer_spec(D_MODEL, SHARED_DIM, layer, rank=2),
            _layer_spec(D_MODEL, SHARED_DIM, layer, rank=2),
            _layer_spec(SHARED_DIM, D_MODEL, layer, rank=2),
            pl.BlockSpec((tm, D_MODEL), row),
            _mod_spec(layer, tm, rank=2, base=base),
            pl.BlockSpec((1, D_MODEL), const),
        ],
        out_specs=pl.BlockSpec((tm, D_MODEL), lambda i, j: (i, 0)),
        out_shape=jax.ShapeDtypeStruct((n_rows, D_MODEL), F32),
        compiler_params=_params(("parallel", "arbitrary")),
        name="moe_experts",
    )(h, gate_w, wg, wu, wd, sg, su, sd, x, mod, final_g)


def kernel(x_prompt, x_sample, c, cache_k, cache_v, state_ssm, c_ctx, w_mod, b_mod, norm1_g, norm2_g, w_in, q_norm_g, k_norm_g, w_up_attn, pool_w, pool_scale, w_up_pool, ssm_a_re, ssm_a_im, ssm_log_dt, ssm_b_re, ssm_b_im, ssm_c_re, ssm_c_im, ssm_d, w_glu, b_glu, w_up_ssm, w_out, w_router, router_bias, w_gate, w_up, w_down, ws_gate, ws_up, ws_down, final_norm_g):
    x_pair = (x_prompt.reshape(N_PROMPT, D_MODEL), x_sample.reshape(N_SAMPLE, D_MODEL), 0)
    cond = jnp.concatenate([c_ctx[None, :], c, jnp.zeros((COND_ROWS - N_COND, D_MODEL), F32)], axis=0)
    mod_all = _ada_all(cond, w_mod, b_mod)

    cos_t, sin_t = _rope_tables(SEQ_TILE)
    head_avg = jnp.asarray(np.kron(np.eye(LANE // HEAD_DIM, dtype=np.float32),
                                   np.full((HEAD_DIM, HEAD_DIM), 1.0 / HEAD_DIM, np.float32)), BF16)
    final_g = final_norm_g.reshape(1, D_MODEL)
    norm1 = norm1_g.reshape(DEPTH, 1, D_MODEL)
    norm2 = norm2_g.reshape(DEPTH, 1, D_MODEL)
    p_scale = pool_scale.reshape(DEPTH, 1, POOL_WIDTH)
    d_skip = ssm_d.reshape(DEPTH, 1, SSM_WIDTH)
    glu_b = b_glu.reshape(DEPTH, 1, SSM_WIDTH)
    glu_w, up_attn, up_pool, up_ssm, out_w = (w.astype(BF16) for w in (w_glu, w_up_attn, w_up_pool, w_up_ssm, w_out))
    shared = tuple(w.astype(BF16) for w in (ws_gate, ws_up, ws_down))
    a_rows, b_mat, c_mat = _ssm_matrices(ssm_a_re, ssm_a_im, ssm_log_dt, ssm_b_re, ssm_b_im, ssm_c_re, ssm_c_im)

    new_k, new_v, new_s = [], [], []
    for l in range(DEPTH):
        qk_gain = jnp.concatenate([jnp.tile(q_norm_g[l], N_HEADS), jnp.tile(k_norm_g[l], N_KV_HEADS)])[None, :]
        q, k_p, v_p, pool_u, ssm_u, kv_p, kv_s = _in_project(
            l, x_pair, mod_all, norm1, w_in[l][:, :MIX_WIDTH].astype(BF16), head_avg, qk_gain, cos_t, sin_t,
            cache_k[:, l], cache_v[:, l])
        new_k.append(k_p.reshape(BATCH, SEQ, N_KV_HEADS, HEAD_DIM))
        new_v.append(v_p.reshape(BATCH, SEQ, N_KV_HEADS, HEAD_DIM))
        oa_p = _attention(q, kv_p, BATCH, SEQ, 0)
        oa_s, wg, wu, wd = _attention(q, kv_s, DEC_BATCH, DEC_SEQ, N_PROMPT, cast=(l, [w_gate, w_up, w_down]))

        pool_bd = jax.scipy.linalg.block_diag(*[pool_w[l, g] for g in range(len(POOL_WINDOWS))]).astype(BF16)
        op_p = _pool(l, pool_u, pool_bd, p_scale, BATCH, SEQ, 0)
        op_s = _pool(l, pool_u, pool_bd, p_scale, DEC_BATCH, DEC_SEQ, N_PROMPT)

        y_f, y_b, st = _ssm_scan(l, ssm_u, state_ssm[:, l], a_rows, b_mat, c_mat)
        new_s.append(st)

        w_r = jnp.pad(w_router[l], ((0, 0), (0, ROUTER_LANES - N_EXPERTS)))
        r_b = jnp.concatenate([router_bias[l], jnp.full((ROUTER_LANES - N_EXPERTS,), -jnp.inf, F32)])[None, :]
        x = _merge(
            l, x_pair, mod_all, norm1, w_in[l][:, MIX_WIDTH:].astype(BF16), oa_p, oa_s, op_p, op_s, ssm_u,
            y_f, y_b, d_skip, glu_w, glu_b, up_attn, up_pool, up_ssm, out_w)
        h2, gate_w = _route(l, x, mod_all, norm2, w_r, r_b)
        if l < DEPTH - 1:
            x = _experts(l, h2, gate_w, wg, wu, wd, *shared, x, mod_all, final_g, final=False)
            x_pair = (x, x, N_PROMPT // SEQ_TILE)
        else:
            y_prompt = _experts(l, h2, gate_w, wg, wu, wd, *shared, x, mod_all, final_g, True, 0, N_PROMPT)
            y_sample = _experts(l, h2, gate_w, wg, wu, wd, *shared, x, mod_all, final_g, True, N_PROMPT, N_SAMPLE)

    y_prompt = y_prompt.reshape(BATCH, SEQ, D_MODEL)
    y_sample = y_sample.reshape(DEC_BATCH, DEC_SEQ, D_MODEL)
    return (y_prompt, y_sample, jnp.stack(new_k, axis=1), jnp.stack(new_v, axis=1), jnp.stack(new_s, axis=1))
```

```python
import functools
import math

import jax
import jax.numpy as jnp
import numpy as np
from jax import lax
from jax.experimental import pallas as pl
from jax.experimental.pallas import tpu as pltpu

D_MODEL = 1024
BATCH = 32
SEQ = 256
DEPTH = 2
DEC_BATCH = 4
DEC_SEQ = 4096
PAST_LEN = 256
GRID_W = 64
EPS = 1e-6
N_MOD = 6
HEAD_DIM = 64
N_HEADS = 8
N_KV_HEADS = 2
ATTN_WIDTH = N_HEADS * HEAD_DIM
KV_WIDTH = N_KV_HEADS * HEAD_DIM
ROPE_BASE = 10000.0
ROPE_PAIRS_PER_AXIS = HEAD_DIM // 4
POOL_WINDOWS = (2, 4, 8, 16)
POOL_GROUP = 64
POOL_WIDTH = len(POOL_WINDOWS) * POOL_GROUP
SSM_H = 16
SSM_GROUPS = 16
SSM_WIDTH = SSM_H * SSM_GROUPS
SSM_STATE = 64
N_EXPERTS = 64
TOP_K = 8
EXPERT_DIM = 256
SHARED_DIM = 256
ROUTE_SCALE = 2.5

N_PROMPT = BATCH * SEQ
N_SAMPLE = DEC_BATCH * DEC_SEQ
N_TOK = N_PROMPT + N_SAMPLE
N_COND = 1 + DEC_BATCH
COND_ROWS = 8
QK_WIDTH = ATTN_WIDTH + KV_WIDTH
MIX_WIDTH = QK_WIDTH + KV_WIDTH + POOL_WIDTH + SSM_WIDTH
GATE_WIDTH = 3 * D_MODEL
SSM_COLS = SSM_GROUPS * SSM_STATE
SSM_LANES = 2 * SSM_COLS
SSM_SEQS = 4
SSM_CHUNK = 128
ROUTER_LANES = 128
EXPERT_CHUNK = 8

SEQ_TILE = SEQ
SSM_ROWS = 2 * SSM_SEQS
SSM_GROUPS_PROMPT = BATCH // SSM_SEQS
SCAN_T = SSM_GROUPS_PROMPT * SEQ + DEC_SEQ
SCAN_CHUNKS = SCAN_T // SSM_CHUNK
PROMPT_CHUNKS = SSM_GROUPS_PROMPT * SEQ // SSM_CHUNK
SSM_PITCH = SSM_CHUNK + 4
SSM_HALF = SSM_SEQS * SSM_PITCH

LANE = 128
SSM_SLABS = SSM_LANES // LANE
VMEM_LIMIT = 56 * 1024 * 1024

F32 = jnp.float32
BF16 = jnp.bfloat16
HIGHEST = lax.Precision.HIGHEST


def _sigmoid(x):
    return 1.0 / (1.0 + jnp.exp(-x))


def _params(dims, vmem=VMEM_LIMIT):
    return pltpu.CompilerParams(dimension_semantics=dims, vmem_limit_bytes=vmem)


def _mod_index(i, tm):
    p = N_PROMPT // tm
    t = DEC_SEQ // tm
    return jnp.where(i < p, 0, 1 + (i - p) // t)


def _mod_spec(layer, tm, rank=1, base=0):
    if rank == 1:
        return pl.BlockSpec((None, 1, N_MOD, D_MODEL), lambda i: (layer, _mod_index(base + i, tm), 0, 0))
    return pl.BlockSpec((None, 1, N_MOD, D_MODEL), lambda i, j: (layer, _mod_index(base + i, tm), 0, 0))


def _layer_spec(rows, cols, layer, rank=1, row_block=0, col_block=0):
    if rank == 1:
        return pl.BlockSpec((None, rows, cols), lambda i: (layer, row_block, col_block))
    return pl.BlockSpec((None, rows, cols), lambda i, j: (layer, row_block, col_block))


def _scan_block(i):
    k = i - BATCH
    tiles = DEC_SEQ // SEQ_TILE
    rb = jnp.where(i < BATCH, i // SSM_SEQS, SSM_GROUPS_PROMPT + k % tiles)
    slot = jnp.where(i < BATCH, i % SSM_SEQS, k // tiles)
    return rb, slot


def _norm_mod(x, g, shift, scale):
    ms = jnp.mean(x * x, axis=-1, keepdims=True)
    return (x * lax.rsqrt(ms + EPS) * g) * (1.0 + scale) + shift


def _stream_rows(xp_ref, xs_ref):
    return jnp.where(pl.program_id(0) < N_PROMPT // SEQ_TILE, xp_ref[...], xs_ref[...])


def _stream_specs(width, s_base):
    p_tiles = N_PROMPT // SEQ_TILE
    return [pl.BlockSpec((SEQ_TILE, width), lambda i: (jnp.minimum(i, p_tiles - 1), 0)),
            pl.BlockSpec((SEQ_TILE, width), lambda i: (s_base + jnp.maximum(i - p_tiles, 0), 0))]


def _mod_kernel(cond_ref, w_ref, b_ref, o_ref):
    c = cond_ref[...]
    s = c * _sigmoid(c)
    o_ref[0] = jnp.dot(s, w_ref[0], preferred_element_type=F32, precision=HIGHEST) + b_ref[0]


def _ada_all(cond, w_mod, b_mod):
    tn = 1536
    width = N_MOD * D_MODEL
    out = pl.pallas_call(
        _mod_kernel,
        grid=(DEPTH, width // tn),
        in_specs=[
            pl.BlockSpec((COND_ROWS, D_MODEL), lambda l, j: (0, 0)),
            pl.BlockSpec((1, D_MODEL, tn), lambda l, j: (l, 0, j)),
            pl.BlockSpec((1, 1, tn), lambda l, j: (l, 0, j)),
        ],
        out_specs=pl.BlockSpec((1, COND_ROWS, tn), lambda l, j: (l, 0, j)),
        out_shape=jax.ShapeDtypeStruct((DEPTH, COND_ROWS, width), F32),
        compiler_params=_params(("parallel", "parallel")),
        name="ada_mod",
    )(cond, w_mod, b_mod.reshape(DEPTH, 1, width))
    return out.reshape(DEPTH, COND_ROWS, N_MOD, D_MODEL)


def _kv_layouts(k, v):
    kt = jnp.transpose(k).astype(BF16)
    lane = lax.broadcasted_iota(jnp.int32, v.shape, 1)
    first = lane < HEAD_DIM
    v_sw = pltpu.roll(v, HEAD_DIM, axis=1)
    one = jnp.ones_like(v)
    kts, ves, vos = [], [], []
    for g in range(N_KV_HEADS):
        kg = kt[g * HEAD_DIM:(g + 1) * HEAD_DIM, :]
        kts.append(jnp.concatenate([kg, kg], axis=0))
        lo_half = v if g == 0 else v_sw
        hi_half = v_sw if g == 0 else v
        ves.append(jnp.where(first, lo_half, one).astype(BF16))
        vos.append(jnp.where(first, one, hi_half).astype(BF16))
    return kts, ves, vos


def _inproj_kernel(xp_ref, xs_ref, mod_ref, g_ref, w_ref, bd_ref, qkg_ref, cos_ref, sin_ref, *rest):
    (q_ref, kp_ref, vp_ref, pu_ref, su_ref, ktp_ref, vep_ref, vop_ref, kts_ref, ves_ref, vos_ref) = rest[3:]
    h = _norm_mod(_stream_rows(xp_ref, xs_ref), g_ref[...], mod_ref[0, 0:1, :], mod_ref[0, 1:2, :])
    z = jnp.dot(h.astype(BF16), w_ref[...], preferred_element_type=F32)
    qk = z[:, :QK_WIDTH]
    qq = qk * qk
    hi = qq.astype(BF16)
    lo = (qq - hi.astype(F32)).astype(BF16)
    bd = bd_ref[...]
    ms = jnp.concatenate(
        [jnp.dot(hi[:, c * LANE:(c + 1) * LANE], bd, preferred_element_type=F32)
         + jnp.dot(lo[:, c * LANE:(c + 1) * LANE], bd, preferred_element_type=F32)
         for c in range(QK_WIDTH // LANE)], axis=1)
    qkn = qk * lax.rsqrt(ms + EPS) * qkg_ref[...]
    cos, sin = cos_ref[...], sin_ref[...]
    parts = []
    for c in range(QK_WIDTH // LANE):
        blk = qkn[:, c * LANE:(c + 1) * LANE]
        nxt = pltpu.roll(blk, LANE - 1, axis=1)
        prv = pltpu.roll(blk, 1, axis=1)
        lane = lax.broadcasted_iota(jnp.int32, blk.shape, 1)
        parts.append(blk * cos + jnp.where((lane & 1) == 0, nxt, prv) * sin)
    qkr = jnp.concatenate(parts, axis=1)
    q_ref[...] = (qkr[:, :ATTN_WIDTH] * (HEAD_DIM ** -0.5 * math.log2(math.e))).astype(BF16)
    k = qkr[:, ATTN_WIDTH:QK_WIDTH]
    v = z[:, QK_WIDTH:QK_WIDTH + KV_WIDTH]
    kts, ves, vos = _kv_layouts(k, v)
    is_prompt = pl.program_id(0) < N_PROMPT // SEQ_TILE

    @pl.when(is_prompt)
    def _():
        kp_ref[...] = k
        vp_ref[...] = v
        for g in range(N_KV_HEADS):
            ktp_ref[g], vep_ref[g], vop_ref[g] = kts[g], ves[g], vos[g]

    @pl.when(jnp.logical_not(is_prompt))
    def _():
        for g in range(N_KV_HEADS):
            kts_ref[g], ves_ref[g], vos_ref[g] = kts[g], ves[g], vos[g]

    pu_ref[...] = z[:, QK_WIDTH + KV_WIDTH:QK_WIDTH + KV_WIDTH + POOL_WIDTH]
    su_ref[...] = z[:, QK_WIDTH + KV_WIDTH + POOL_WIDTH:MIX_WIDTH]


def _in_project(layer, x_pair, mod, norm_g, w_mix, bd, qk_gain, cos_t, sin_t, cache_k, cache_v):
    tm = SEQ_TILE
    p_tiles = N_PROMPT // tm
    s_tiles = DEC_SEQ // tm
    n_keys = PAST_LEN + DEC_SEQ
    past_tiles = PAST_LEN // tm

    def rope_idx(i):
        return (jnp.where(i < p_tiles, 0, 1 + (i - p_tiles) % s_tiles), 0)

    kc = jnp.transpose(cache_k, (0, 2, 3, 1)).astype(BF16)
    kc = jnp.concatenate([kc, kc], axis=2).reshape(DEC_BATCH * N_KV_HEADS, LANE, PAST_LEN)
    vc = jnp.transpose(cache_v, (0, 2, 1, 3)).astype(BF16)
    ones = jnp.ones_like(vc)
    ve_c = jnp.concatenate([vc, ones], axis=3).reshape(DEC_BATCH * N_KV_HEADS, PAST_LEN, LANE)
    vo_c = jnp.concatenate([ones, vc], axis=3).reshape(DEC_BATCH * N_KV_HEADS, PAST_LEN, LANE)
    kt_s0 = jnp.pad(kc, ((0, 0), (0, 0), (0, DEC_SEQ)))
    ve_s0 = jnp.pad(ve_c, ((0, 0), (0, DEC_SEQ), (0, 0)))
    vo_s0 = jnp.pad(vo_c, ((0, 0), (0, DEC_SEQ), (0, 0)))

    p_blk = lambda i: jnp.minimum(i, p_tiles - 1)
    s_seq = lambda i: jnp.maximum(i - p_tiles, 0) // s_tiles
    s_blk = lambda i: past_tiles + jnp.maximum(i - p_tiles, 0) % s_tiles
    row = lambda i: (i, 0)
    const = lambda i: (0, 0)
    any_spec = pl.BlockSpec(memory_space=pl.ANY)
    outs = pl.pallas_call(
        _inproj_kernel,
        grid=(N_TOK // tm,),
        in_specs=_stream_specs(D_MODEL, x_pair[2]) + [
            _mod_spec(layer, tm),
            _layer_spec(1, D_MODEL, layer),
            pl.BlockSpec((D_MODEL, MIX_WIDTH), const),
            pl.BlockSpec((LANE, LANE), const),
            pl.BlockSpec((1, QK_WIDTH), const),
            pl.BlockSpec((tm, LANE), rope_idx),
            pl.BlockSpec((tm, LANE), rope_idx),
            any_spec, any_spec, any_spec,
        ],
        out_specs=[
            pl.BlockSpec((tm, ATTN_WIDTH), row),
            pl.BlockSpec((tm, KV_WIDTH), lambda i: (p_blk(i), 0)),
            pl.BlockSpec((tm, KV_WIDTH), lambda i: (p_blk(i), 0)),
            pl.BlockSpec((tm, POOL_WIDTH), row),
            pl.BlockSpec((tm, SSM_WIDTH), _scan_block),
            pl.BlockSpec((N_KV_HEADS, LANE, tm), lambda i: (p_blk(i), 0, 0)),
            pl.BlockSpec((N_KV_HEADS, tm, LANE), lambda i: (p_blk(i), 0, 0)),
            pl.BlockSpec((N_KV_HEADS, tm, LANE), lambda i: (p_blk(i), 0, 0)),
            pl.BlockSpec((N_KV_HEADS, LANE, tm), lambda i: (s_seq(i), 0, s_blk(i))),
            pl.BlockSpec((N_KV_HEADS, tm, LANE), lambda i: (s_seq(i), s_blk(i), 0)),
            pl.BlockSpec((N_KV_HEADS, tm, LANE), lambda i: (s_seq(i), s_blk(i), 0)),
        ],
        out_shape=[
            jax.ShapeDtypeStruct((N_TOK, ATTN_WIDTH), BF16),
            jax.ShapeDtypeStruct((N_PROMPT, KV_WIDTH), F32),
            jax.ShapeDtypeStruct((N_PROMPT, KV_WIDTH), F32),
            jax.ShapeDtypeStruct((N_TOK, POOL_WIDTH), F32),
            jax.ShapeDtypeStruct((SCAN_T, SSM_SEQS * SSM_WIDTH), F32),
            jax.ShapeDtypeStruct((BATCH * N_KV_HEADS, LANE, SEQ), BF16),
            jax.ShapeDtypeStruct((BATCH * N_KV_HEADS, SEQ, LANE), BF16),
            jax.ShapeDtypeStruct((BATCH * N_KV_HEADS, SEQ, LANE), BF16),
            jax.ShapeDtypeStruct((DEC_BATCH * N_KV_HEADS, LANE, n_keys), BF16),
            jax.ShapeDtypeStruct((DEC_BATCH * N_KV_HEADS, n_keys, LANE), BF16),
            jax.ShapeDtypeStruct((DEC_BATCH * N_KV_HEADS, n_keys, LANE), BF16),
        ],
        input_output_aliases={9: 8, 10: 9, 11: 10},
        compiler_params=_params(("arbitrary",)),
        name="in_project",
    )(x_pair[0], x_pair[1], mod, norm_g, w_mix, bd, qk_gain, cos_t, sin_t, kt_s0, ve_s0, vo_s0)
    return outs[0], outs[1], outs[2], outs[3], outs[4], tuple(outs[5:8]), tuple(outs[8:11])


def _rope_tables(tm):
    f32 = np.float32
    rows = DEC_SEQ // GRID_W
    row = np.repeat(np.arange(rows, dtype=f32), GRID_W)
    col = np.tile(np.arange(GRID_W, dtype=f32), rows)
    inv_freq = (f32(ROPE_BASE) ** (-np.arange(ROPE_PAIRS_PER_AXIS, dtype=f32) / f32(ROPE_PAIRS_PER_AXIS))).astype(f32)
    ang = np.concatenate([row[:, None] * inv_freq, col[:, None] * inv_freq], axis=-1).astype(f32)
    cos = np.repeat(np.cos(ang), 2, axis=-1)
    sin = np.repeat(np.sin(ang), 2, axis=-1) * np.tile(np.array([-1.0, 1.0], f32), HEAD_DIM // 2)
    cos = np.concatenate([np.ones((tm, HEAD_DIM), f32), cos], axis=0)
    sin = np.concatenate([np.zeros((tm, HEAD_DIM), f32), sin], axis=0)
    n_rep = LANE // HEAD_DIM
    return jnp.asarray(np.tile(cos, (1, n_rep)), F32), jnp.asarray(np.tile(sin, (1, n_rep)), F32)


def _attn_kernel(q_ref, kt_ref, ve_ref, vo_ref, *rest):
    n_cast = (len(rest) - 1) // 2
    o_ref = rest[n_cast]
    for src, dst in zip(rest[:n_cast], rest[n_cast + 1:]):
        dst[...] = src[...].astype(dst.dtype)
    lane = lax.broadcasted_iota(jnp.int32, (q_ref.shape[0], LANE), 1)
    first = lane < HEAD_DIM
    slabs = []
    for pair in range(q_ref.shape[1] // LANE):
        grp = pair // (N_HEADS // N_KV_HEADS // 2)
        kt = kt_ref[grp]
        q = q_ref[:, pair * LANE:(pair + 1) * LANE]
        halves = []
        for keep, v_ref in ((first, ve_ref), (lane >= HEAD_DIM, vo_ref)):
            qh = jnp.where(keep, q, jnp.zeros_like(q))
            s = jnp.dot(qh, kt, preferred_element_type=F32)
            m = jnp.max(s, axis=-1, keepdims=True)
            p = jnp.exp2(s - m).astype(BF16)
            a = jnp.dot(p, v_ref[grp], preferred_element_type=F32)
            halves.append(a / pltpu.roll(a, HEAD_DIM, axis=1))
        slabs.append(jnp.where(first, halves[0], halves[1]))
    o_ref[...] = jnp.concatenate(slabs, axis=1).astype(o_ref.dtype)


def _attention(q, kv_ops, n_batch, n_q, row0, cast=None):
    kt, v_e, v_o = kv_ops
    n_keys = kt.shape[2]
    tq = 256
    q_tiles = n_q // tq
    base = row0 // tq
    kv_idx = lambda b, i: (b, 0, 0)
    in_specs = [
        pl.BlockSpec((tq, ATTN_WIDTH), lambda b, i: (base + b * q_tiles + i, 0)),
        pl.BlockSpec((N_KV_HEADS, LANE, n_keys), kv_idx),
        pl.BlockSpec((N_KV_HEADS, n_keys, LANE), kv_idx),
        pl.BlockSpec((N_KV_HEADS, n_keys, LANE), kv_idx),
    ]
    out_specs = [pl.BlockSpec((tq, ATTN_WIDTH), lambda b, i: (b * q_tiles + i, 0))]
    out_shape = [jax.ShapeDtypeStruct((n_batch * n_q, ATTN_WIDTH), BF16)]
    args = [q, kt, v_e, v_o]
    if cast is not None:
        layer, tensors = cast
        steps = n_batch * q_tiles
        step = lambda b, i: b * q_tiles + i
        for w in tensors:
            _, n_e, rows, cols = w.shape
            part = n_e * rows // steps
            assert part * steps == n_e * rows and part % 16 == 0 and rows % part == 0
            in_specs.append(pl.BlockSpec((1, part, cols), lambda b, i: (layer * steps + step(b, i), 0, 0)))
            out_specs.append(pl.BlockSpec((1, part, cols), lambda b, i: (step(b, i), 0, 0)))
            out_shape.append(jax.ShapeDtypeStruct((steps, part, cols), BF16))
            args.append(w.reshape(DEPTH * steps, part, cols))
    outs = pl.pallas_call(
        _attn_kernel,
        grid=(n_batch, q_tiles),
        in_specs=in_specs,
        out_specs=out_specs,
        out_shape=out_shape,
        compiler_params=_params(("parallel", "parallel")),
        name="attention",
    )(*args)
    if cast is None:
        return outs[0]
    return [outs[0]] + [o.reshape(w.shape[1:]) for o, w in zip(outs[1:], cast[1])]


def _pool_kernel(u_ref, w_ref, sc_ref, o_ref, pad_ref, *, n_seq):
    halo = 8
    u = u_ref[...]
    zeros = jnp.zeros((halo, POOL_WIDTH), F32)
    pad_ref[0:halo, :] = zeros
    pad_ref[halo + n_seq:2 * halo + n_seq, :] = zeros
    pad_ref[halo:halo + n_seq, :] = u

    def sh(j, c):
        return pad_ref[halo + j:halo + j + n_seq, c * LANE:(c + 1) * LANE]

    t = lax.broadcasted_iota(jnp.int32, (n_seq, LANE), 0)
    first = lax.broadcasted_iota(jnp.int32, (n_seq, LANE), 1) < POOL_GROUP
    parts = []
    for c in range(POOL_WIDTH // LANE):
        h_lo, h_hi = POOL_WINDOWS[2 * c] // 2, POOL_WINDOWS[2 * c + 1] // 2
        uc = u[:, c * LANE:(c + 1) * LANE]
        small = uc
        for j in range(-h_lo, h_lo):
            if j != 0:
                small = small + sh(j, c)
        big = small
        for j in range(-h_hi, h_hi):
            if not -h_lo <= j < h_lo:
                big = big + sh(j, c)
        half = jnp.where(first, h_lo, h_hi)
        count = jnp.minimum(t + half, n_seq) - jnp.maximum(t - half, 0)
        parts.append(jnp.where(first, small, big) / count.astype(F32) - uc)
    pooled = jnp.concatenate(parts, axis=1)
    mixed = jnp.dot(pooled.astype(BF16), w_ref[...], preferred_element_type=F32)
    o_ref[...] = (mixed * sc_ref[...]).astype(o_ref.dtype)


def _pool(layer, u, w_bd, scale, n_batch, n_seq, row0):
    base = row0 // n_seq
    return pl.pallas_call(
        functools.partial(_pool_kernel, n_seq=n_seq),
        grid=(n_batch,),
        in_specs=[
            pl.BlockSpec((n_seq, POOL_WIDTH), lambda b: (base + b, 0)),
            pl.BlockSpec((POOL_WIDTH, POOL_WIDTH), lambda b: (0, 0)),
            _layer_spec(1, POOL_WIDTH, layer),
        ],
        out_specs=pl.BlockSpec((n_seq, POOL_WIDTH), lambda b: (b, 0)),
        out_shape=jax.ShapeDtypeStruct((n_batch * n_seq, POOL_WIDTH), BF16),
        scratch_shapes=[pltpu.VMEM((n_seq + 16, POOL_WIDTH), F32)],
        compiler_params=_params(("parallel",)),
        name="pool_mixer",
    )(u, w_bd, scale)


def _scan_chunk_first(i):
    per_seq = SEQ // SSM_CHUNK
    return jnp.where(i < PROMPT_CHUNKS, i % per_seq == 0, i == PROMPT_CHUNKS)


def _scan_bwd_chunk(i):
    per_seq = SEQ // SSM_CHUNK
    return jnp.where(i < PROMPT_CHUNKS, (i // per_seq) * per_seq + (per_seq - 1 - i % per_seq),
                     PROMPT_CHUNKS + SCAN_CHUNKS - 1 - i)


def _scan_group(i):
    return jnp.where(i < PROMPT_CHUNKS, i // (SEQ // SSM_CHUNK), SSM_GROUPS_PROMPT)


def _split3(x):
    hi = x.astype(BF16)
    r1 = x - hi.astype(F32)
    mid = r1.astype(BF16)
    lo = (r1 - mid.astype(F32)).astype(BF16)
    return hi, mid, lo


def _ssm_kernel(uf_ref, ub_ref, a_ref, bf_ref, bb_ref, cf_ref, cb_ref, jin_ref, jout_ref, h0_ref,
                yf_ref, yb_ref, hfin_ref, bu_ref, st_ref, lhs_ref, yy_ref):
    @pl.when(_scan_chunk_first(pl.program_id(0)))
    def _():
        st_ref[...] = h0_ref[0]

    ub_rev = jnp.dot(jin_ref[...], ub_ref[...].astype(BF16), preferred_element_type=F32)
    gap = jnp.zeros((SSM_PITCH - SSM_CHUNK, SSM_WIDTH), F32)
    for s in range(SSM_SEQS):
        cols = slice(s * SSM_WIDTH, (s + 1) * SSM_WIDTH)
        lhs_ref[s * SSM_PITCH:s * SSM_PITCH + SSM_CHUNK, :] = uf_ref[:, cols]
        lhs_ref[s * SSM_PITCH + SSM_CHUNK:(s + 1) * SSM_PITCH, :] = gap
        lhs_ref[SSM_HALF + s * SSM_PITCH:SSM_HALF + s * SSM_PITCH + SSM_CHUNK, :] = ub_rev[:, cols]
        lhs_ref[SSM_HALF + s * SSM_PITCH + SSM_CHUNK:SSM_HALF + (s + 1) * SSM_PITCH, :] = gap
    bu_f = jnp.dot(lhs_ref[:SSM_HALF, :].astype(BF16), bf_ref[...], preferred_element_type=F32)
    bu_b = jnp.dot(lhs_ref[SSM_HALF:, :].astype(BF16), bb_ref[...], preferred_element_type=F32)
    for k in range(SSM_SLABS):
        bu_ref[k, :SSM_HALF, :] = bu_f[:, k * LANE:(k + 1) * LANE]
        bu_ref[k, SSM_HALF:, :] = bu_b[:, k * LANE:(k + 1) * LANE]

    half = SSM_SLABS // 2
    a_re = [a_ref[:, k * LANE:(k + 1) * LANE] for k in range(half)]
    a_im = [a_ref[:, SSM_COLS + k * LANE:SSM_COLS + (k + 1) * LANE] for k in range(half)]

    def step(t, carry):
        rows = pl.ds(t, SSM_ROWS, stride=SSM_PITCH)
        new_re, new_im = [], []
        for k in range(half):
            h_re, h_im = carry[k], carry[half + k]
            n_re = a_re[k] * h_re - a_im[k] * h_im + bu_ref[k, rows, :]
            n_im = a_re[k] * h_im + a_im[k] * h_re + bu_ref[half + k, rows, :]
            bu_ref[k, rows, :] = n_re
            bu_ref[half + k, rows, :] = n_im
            new_re.append(n_re)
            new_im.append(n_im)
        return tuple(new_re + new_im)

    init = tuple(st_ref[:, k * LANE:(k + 1) * LANE] for k in range(SSM_SLABS))
    fin = lax.fori_loop(0, SSM_CHUNK, step, init, unroll=2)
    for k in range(SSM_SLABS):
        st_ref[:, k * LANE:(k + 1) * LANE] = fin[k]
    hfin_ref[0] = st_ref[...]

    h_f = jnp.concatenate([bu_ref[k, :SSM_HALF, :] for k in range(SSM_SLABS)], axis=1).astype(BF16)
    h_b = jnp.concatenate([bu_ref[k, SSM_HALF:, :] for k in range(SSM_SLABS)], axis=1).astype(BF16)
    yy_ref[...] = jnp.dot(h_f, cf_ref[...], preferred_element_type=F32)
    for s in range(SSM_SEQS):
        yf_ref[:, s * SSM_WIDTH:(s + 1) * SSM_WIDTH] = yy_ref[s * SSM_PITCH:s * SSM_PITCH + SSM_CHUNK, :]
    y_b = jnp.dot(h_b, cb_ref[...], preferred_element_type=F32)
    y_nat = sum(jnp.dot(jout_ref[...], piece, preferred_element_type=F32) for piece in _split3(y_b))
    for s in range(SSM_SEQS):
        yb_ref[:, s * SSM_WIDTH:(s + 1) * SSM_WIDTH] = y_nat[s * SSM_CHUNK:(s + 1) * SSM_CHUNK, :]


def _ssm_scan(layer, su8, h0_sample, a_rows, b_mat, c_mat):
    rows = SSM_ROWS
    width = SSM_SEQS * SSM_WIDTH
    n_groups = SSM_GROUPS_PROMPT + 1
    hs = h0_sample.astype(F32).reshape(SSM_SEQS, 2, SSM_COLS, 2).transpose(1, 0, 3, 2).reshape(1, rows, SSM_LANES)
    hh = jnp.concatenate([jnp.zeros((SSM_GROUPS_PROMPT, rows, SSM_LANES), F32), hs], axis=0)
    j_in = jnp.asarray(np.eye(SSM_CHUNK, dtype=np.float32)[::-1], dtype=BF16)
    sel = np.zeros((SSM_SEQS * SSM_CHUNK, SSM_HALF), np.float32)
    for s in range(SSM_SEQS):
        for t in range(SSM_CHUNK):
            sel[s * SSM_CHUNK + t, s * SSM_PITCH + SSM_CHUNK - 1 - t] = 1.0
    j_out = jnp.asarray(sel, dtype=BF16)
    fwd = lambda i: (i, 0)
    bwd = lambda i: (_scan_bwd_chunk(i), 0)
    grp = lambda i: (_scan_group(i), 0, 0)
    const = lambda i: (0, 0)
    y_f, y_b, hfin = pl.pallas_call(
        _ssm_kernel,
        grid=(SCAN_CHUNKS,),
        in_specs=[
            pl.BlockSpec((SSM_CHUNK, width), fwd),
            pl.BlockSpec((SSM_CHUNK, width), bwd),
            _layer_spec(rows, SSM_LANES, layer),
            _layer_spec(SSM_WIDTH, SSM_LANES, layer, row_block=0),
            _layer_spec(SSM_WIDTH, SSM_LANES, layer, row_block=1),
            _layer_spec(SSM_LANES, SSM_WIDTH, layer, col_block=0),
            _layer_spec(SSM_LANES, SSM_WIDTH, layer, col_block=1),
            pl.BlockSpec((SSM_CHUNK, SSM_CHUNK), const),
            pl.BlockSpec((SSM_SEQS * SSM_CHUNK, SSM_HALF), const),
            pl.BlockSpec((1, rows, SSM_LANES), grp),
        ],
        out_specs=[
            pl.BlockSpec((SSM_CHUNK, width), fwd),
            pl.BlockSpec((SSM_CHUNK, width), bwd),
            pl.BlockSpec((1, rows, SSM_LANES), grp),
        ],
        out_shape=[
            jax.ShapeDtypeStruct((SCAN_T, width), F32),
            jax.ShapeDtypeStruct((SCAN_T, width), F32),
            jax.ShapeDtypeStruct((n_groups, rows, SSM_LANES), F32),
        ],
        scratch_shapes=[pltpu.VMEM((SSM_SLABS, 2 * SSM_HALF, LANE), F32), pltpu.VMEM((rows, SSM_LANES), F32),
                        pltpu.VMEM((2 * SSM_HALF, SSM_WIDTH), F32), pltpu.VMEM((SSM_HALF, SSM_WIDTH), F32)],
        compiler_params=_params(("arbitrary",)),
        name="ssm_scan",
    )(su8, su8, a_rows, b_mat, b_mat, c_mat, c_mat, j_in, j_out, hh)
    fin = hfin[:SSM_GROUPS_PROMPT].reshape(SSM_GROUPS_PROMPT, 2, SSM_SEQS, 2, SSM_GROUPS, SSM_STATE)
    fin = fin.transpose(0, 2, 1, 4, 5, 3).reshape(BATCH, 2, SSM_GROUPS, SSM_STATE, 2)
    return y_f, y_b, fin


def _ssm_matrices(a_re, a_im, log_dt, b_re, b_im, c_re, c_im):
    n_l = a_re.shape[0]
    lam = lax.complex(a_re.astype(F32), a_im.astype(F32))
    dt = jnp.exp(log_dt.astype(F32))[..., None]
    a_bar = jnp.exp(lam * dt)
    b_bar = ((a_bar - 1.0) / lam)[..., None] * lax.complex(b_re.astype(F32), b_im.astype(F32))
    a_dir = jnp.concatenate([jnp.real(a_bar).reshape(n_l, 2, SSM_COLS),
                             jnp.imag(a_bar).reshape(n_l, 2, SSM_COLS)], axis=-1)
    a_rows = jnp.repeat(a_dir, SSM_SEQS, axis=1)
    eye = jnp.eye(SSM_GROUPS, dtype=F32)
    bt = jnp.transpose(b_bar, (0, 1, 2, 4, 3))
    b_real = jnp.einsum('ldghp,ge->ldghep', jnp.real(bt), eye).reshape(n_l, 2 * SSM_WIDTH, SSM_COLS)
    b_imag = jnp.einsum('ldghp,ge->ldghep', jnp.imag(bt), eye).reshape(n_l, 2 * SSM_WIDTH, SSM_COLS)
    b_mat = jnp.concatenate([b_real, b_imag], axis=-1).astype(BF16)
    cr = jnp.transpose(c_re.astype(F32), (0, 1, 2, 4, 3))
    ci = jnp.transpose(c_im.astype(F32), (0, 1, 2, 4, 3))
    c_real = jnp.einsum('ldgph,ge->lgpdeh', cr, eye).reshape(n_l, SSM_COLS, 2 * SSM_WIDTH)
    c_imag = jnp.einsum('ldgph,ge->lgpdeh', -ci, eye).reshape(n_l, SSM_COLS, 2 * SSM_WIDTH)
    c_mat = jnp.concatenate([c_real, c_imag], axis=1).astype(BF16)
    return a_rows, b_mat, c_mat


def _merge_kernel(xp_ref, xs_ref, mod_ref, g1_ref, wg_ref, oap_ref, oas_ref, opp_ref, ops_ref, su_ref, yf_ref, yb_ref,
                  d_ref, wglu_ref, bglu_ref, wua_ref, wup_ref, wus_ref, wo_ref, o_ref):
    x = _stream_rows(xp_ref, xs_ref)
    h = _norm_mod(x, g1_ref[...], mod_ref[0, 0:1, :], mod_ref[0, 1:2, :])
    gates = _sigmoid(jnp.dot(h.astype(BF16), wg_ref[...], preferred_element_type=F32))
    y = d_ref[...] * su_ref[...] + yf_ref[...] + yb_ref[...]
    y = 0.5 * y * (1.0 + jnp.tanh(math.sqrt(2.0 / math.pi) * (y + 0.044715 * (y * y * y))))
    glu = jnp.dot(y.astype(BF16), wglu_ref[...], preferred_element_type=F32) + bglu_ref[...]
    o_ssm = y * _sigmoid(glu)
    o_attn = _stream_rows(oap_ref, oas_ref)
    o_pool = _stream_rows(opp_ref, ops_ref)
    m = (gates[:, :D_MODEL] * jnp.dot(o_attn, wua_ref[...], preferred_element_type=F32)
         + gates[:, D_MODEL:2 * D_MODEL] * jnp.dot(o_pool, wup_ref[...], preferred_element_type=F32)
         + gates[:, 2 * D_MODEL:] * jnp.dot(o_ssm.astype(BF16), wus_ref[...], preferred_element_type=F32))
    o_ref[...] = x + mod_ref[0, 2:3, :] * jnp.dot(m.astype(BF16), wo_ref[...], preferred_element_type=F32)


def _merge(layer, x_pair, mod, norm1_g, w_gates, oa_p, oa_s, op_p, op_s, ssm_u, y_f, y_b, d_skip, w_glu, b_glu,
           w_up_attn, w_up_pool, w_up_ssm, w_out):
    tm = SEQ_TILE
    row = lambda i: (i, 0)
    const = lambda i: (0, 0)
    return pl.pallas_call(
        _merge_kernel,
        grid=(N_TOK // tm,),
        in_specs=_stream_specs(D_MODEL, x_pair[2]) + [
            _mod_spec(layer, tm),
            _layer_spec(1, D_MODEL, layer),
            pl.BlockSpec((D_MODEL, GATE_WIDTH), const),
        ] + _stream_specs(ATTN_WIDTH, 0) + _stream_specs(POOL_WIDTH, 0) + [
            pl.BlockSpec((tm, SSM_WIDTH), _scan_block),
            pl.BlockSpec((tm, SSM_WIDTH), _scan_block),
            pl.BlockSpec((tm, SSM_WIDTH), _scan_block),
            _layer_spec(1, SSM_WIDTH, layer),
            _layer_spec(SSM_WIDTH, SSM_WIDTH, layer),
            _layer_spec(1, SSM_WIDTH, layer),
            _layer_spec(ATTN_WIDTH, D_MODEL, layer),
            _layer_spec(POOL_WIDTH, D_MODEL, layer),
            _layer_spec(SSM_WIDTH, D_MODEL, layer),
            _layer_spec(D_MODEL, D_MODEL, layer),
        ],
        out_specs=pl.BlockSpec((tm, D_MODEL), row),
        out_shape=jax.ShapeDtypeStruct((N_TOK, D_MODEL), F32),
        compiler_params=_params(("parallel",)),
        name="merge_branches",
    )(x_pair[0], x_pair[1], mod, norm1_g, w_gates, oa_p, oa_s, op_p, op_s, ssm_u, y_f, y_b, d_skip, w_glu, b_glu,
      w_up_attn, w_up_pool, w_up_ssm, w_out)


def _router_kernel(x_ref, mod_ref, g_ref, wrh_ref, wrl_ref, rb_ref, h_ref, gw_ref):
    h2 = _norm_mod(x_ref[...], g_ref[...], mod_ref[0, 3:4, :], mod_ref[0, 4:5, :])
    h_hi = h2.astype(BF16)
    h_ref[...] = h_hi
    h_lo = (h2 - h_hi.astype(F32)).astype(BF16)
    logits = (jnp.dot(h_hi, wrh_ref[...], preferred_element_type=F32)
              + jnp.dot(h_lo, wrh_ref[...], preferred_element_type=F32)
              + jnp.dot(h_hi, wrl_ref[...], preferred_element_type=F32))
    scores = _sigmoid(logits)
    sc_t = jnp.transpose(scores)[:N_EXPERTS, :]
    work = jnp.transpose(scores + rb_ref[...])[:N_EXPERTS, :]
    idx = lax.broadcasted_iota(jnp.int32, work.shape, 0).astype(F32)
    w_t = jnp.zeros_like(sc_t)
    for _ in range(TOP_K):
        mx = jnp.max(work, axis=0, keepdims=True)
        first = jnp.min(jnp.where(work == mx, idx, float(N_EXPERTS)), axis=0, keepdims=True)
        pick = idx == first
        w_t = jnp.where(pick, sc_t, w_t)
        work = jnp.where(pick, -jnp.inf, work)
    w_t = w_t / jnp.sum(w_t, axis=0, keepdims=True) * ROUTE_SCALE
    w_pad = jnp.concatenate([w_t, jnp.zeros((ROUTER_LANES - N_EXPERTS, w_t.shape[1]), F32)], axis=0)
    gw_ref[...] = jnp.transpose(w_pad)


def _route(layer, x, mod, norm_g, w_router, router_bias):
    tm = 1024
    row = lambda i: (i, 0)
    const = lambda i: (0, 0)
    w_hi = w_router.astype(BF16)
    w_lo = (w_router - w_hi.astype(F32)).astype(BF16)
    return pl.pallas_call(
        _router_kernel,
        grid=(N_TOK // tm,),
        in_specs=[
            pl.BlockSpec((tm, D_MODEL), row),
            _mod_spec(layer, tm),
            _layer_spec(1, D_MODEL, layer),
            pl.BlockSpec((D_MODEL, ROUTER_LANES), const),
            pl.BlockSpec((D_MODEL, ROUTER_LANES), const),
            pl.BlockSpec((1, ROUTER_LANES), const),
        ],
        out_specs=[pl.BlockSpec((tm, D_MODEL), row), pl.BlockSpec((tm, ROUTER_LANES), row)],
        out_shape=[jax.ShapeDtypeStruct((N_TOK, D_MODEL), BF16),
                   jax.ShapeDtypeStruct((N_TOK, ROUTER_LANES), F32)],
        compiler_params=_params(("parallel",)),
        name="moe_router",
    )(x, mod, norm_g, w_hi, w_lo, router_bias)


def _swiglu_mid(h, w_gate, w_up):
    a = jnp.dot(h, w_gate, preferred_element_type=F32)
    b = jnp.dot(h, w_up, preferred_element_type=F32)
    return (a * _sigmoid(a)) * b


def _experts_kernel(h_ref, gw_ref, wg_ref, wu_ref, wd_ref, sg_ref, su_ref, sd_ref, x_ref, mod_ref, fg_ref,
                    o_ref, *, final):
    j = pl.program_id(1)
    h = h_ref[...]

    @pl.when(j == 0)
    def _():
        mid = _swiglu_mid(h, sg_ref[...], su_ref[...]).astype(BF16)
        o_ref[...] = jnp.dot(mid, sd_ref[...], preferred_element_type=F32)

    gw = pltpu.roll(gw_ref[...], (ROUTER_LANES - j * EXPERT_CHUNK) & (ROUTER_LANES - 1), axis=1)
    mids = [(_swiglu_mid(h, wg_ref[e], wu_ref[e]) * gw[:, e:e + 1]).astype(BF16) for e in range(EXPERT_CHUNK)]
    mid = jnp.concatenate(mids, axis=1)
    wd = wd_ref[...].reshape(EXPERT_CHUNK * EXPERT_DIM, D_MODEL)
    o_ref[...] += jnp.dot(mid, wd, preferred_element_type=F32)

    @pl.when(j == pl.num_programs(1) - 1)
    def _():
        x = x_ref[...] + mod_ref[0, 5:6, :] * o_ref[...]
        if final:
            ms = jnp.mean(x * x, axis=-1, keepdims=True)
            x = x * lax.rsqrt(ms + EPS) * fg_ref[...]
        o_ref[...] = x


def _experts(layer, h, gate_w, wg, wu, wd, sg, su, sd, x, mod, final_g, final, row0=0, n_rows=N_TOK):
    tm = 1024
    n_chunks = N_EXPERTS // EXPERT_CHUNK
    base = row0 // tm
    row = lambda i, j: (base + i, 0)
    const = lambda i, j: (0, 0)
    chunk = lambda i, j: (j, 0, 0)
    return pl.pallas_call(
        functools.partial(_experts_kernel, final=final),
        grid=(n_rows // tm, n_chunks),
        in_specs=[
            pl.BlockSpec((tm, D_MODEL), row),
            pl.BlockSpec((tm, ROUTER_LANES), row),
            pl.BlockSpec((EXPERT_CHUNK, D_MODEL, EXPERT_DIM), chunk),
            pl.BlockSpec((EXPERT_CHUNK, D_MODEL, EXPERT_DIM), chunk),
            pl.BlockSpec((EXPERT_CHUNK, EXPERT_DIM, D_MODEL), chunk),
            _layer_spec(D_MODEL, SHARED_DIM, layer, rank=2),
            _layer_spec(D_MODEL, SHARED_DIM, layer, rank=2),
            _layer_spec(SHARED_DIM, D_MODEL, layer, rank=2),
            pl.BlockSpec((tm, D_MODEL), row),
            _mod_spec(layer, tm, rank=2, base=base),
            pl.BlockSpec((1, D_MODEL), const),
        ],
        out_specs=pl.BlockSpec((tm, D_MODEL), lambda i, j: (i, 0)),
        out_shape=jax.ShapeDtypeStruct((n_rows, D_MODEL), F32),
        compiler_params=_params(("parallel", "arbitrary")),
        name="moe_experts",
    )(h, gate_w, wg, wu, wd, sg, su, sd, x, mod, final_g)


def kernel(x_prompt, x_sample, c, cache_k, cache_v, state_ssm, c_ctx, w_mod, b_mod, norm1_g, norm2_g, w_in, q_norm_g, k_norm_g, w_up_attn, pool_w, pool_scale, w_up_pool, ssm_a_re, ssm_a_im, ssm_log_dt, ssm_b_re, ssm_b_im, ssm_c_re, ssm_c_im, ssm_d, w_glu, b_glu, w_up_ssm, w_out, w_router, router_bias, w_gate, w_up, w_down, ws_gate, ws_up, ws_down, final_norm_g):
    x_pair = (x_prompt.reshape(N_PROMPT, D_MODEL), x_sample.reshape(N_SAMPLE, D_MODEL), 0)
    cond = jnp.concatenate([c_ctx[None, :], c, jnp.zeros((COND_ROWS - N_COND, D_MODEL), F32)], axis=0)
    mod_all = _ada_all(cond, w_mod, b_mod)

    cos_t, sin_t = _rope_tables(SEQ_TILE)
    head_avg = jnp.asarray(np.kron(np.eye(LANE // HEAD_DIM, dtype=np.float32),
                                   np.full((HEAD_DIM, HEAD_DIM), 1.0 / HEAD_DIM, np.float32)), BF16)
    final_g = final_norm_g.reshape(1, D_MODEL)
    norm1 = norm1_g.reshape(DEPTH, 1, D_MODEL)
    norm2 = norm2_g.reshape(DEPTH, 1, D_MODEL)
    p_scale = pool_scale.reshape(DEPTH, 1, POOL_WIDTH)
    d_skip = ssm_d.reshape(DEPTH, 1, SSM_WIDTH)
    glu_b = b_glu.reshape(DEPTH, 1, SSM_WIDTH)
    glu_w, up_attn, up_pool, up_ssm, out_w = (w.astype(BF16) for w in (w_glu, w_up_attn, w_up_pool, w_up_ssm, w_out))
    shared = tuple(w.astype(BF16) for w in (ws_gate, ws_up, ws_down))
    a_rows, b_mat, c_mat = _ssm_matrices(ssm_a_re, ssm_a_im, ssm_log_dt, ssm_b_re, ssm_b_im, ssm_c_re, ssm_c_im)

    new_k, new_v, new_s = [], [], []
    for l in range(DEPTH):
        qk_gain = jnp.concatenate([jnp.tile(q_norm_g[l], N_HEADS), jnp.tile(k_norm_g[l], N_KV_HEADS)])[None, :]
        q, k_p, v_p, pool_u, ssm_u, kv_p, kv_s = _in_project(
            l, x_pair, mod_all, norm1, w_in[l][:, :MIX_WIDTH].astype(BF16), head_avg, qk_gain, cos_t, sin_t,
            cache_k[:, l], cache_v[:, l])
        new_k.append(k_p.reshape(BATCH, SEQ, N_KV_HEADS, HEAD_DIM))
        new_v.append(v_p.reshape(BATCH, SEQ, N_KV_HEADS, HEAD_DIM))
        oa_p = _attention(q, kv_p, BATCH, SEQ, 0)
        oa_s, wg, wu, wd = _attention(q, kv_s, DEC_BATCH, DEC_SEQ, N_PROMPT, cast=(l, [w_gate, w_up, w_down]))

        pool_bd = jax.scipy.linalg.block_diag(*[pool_w[l, g] for g in range(len(POOL_WINDOWS))]).astype(BF16)
        op_p = _pool(l, pool_u, pool_bd, p_scale, BATCH, SEQ, 0)
        op_s = _pool(l, pool_u, pool_bd, p_scale, DEC_BATCH, DEC_SEQ, N_PROMPT)

        y_f, y_b, st = _ssm_scan(l, ssm_u, state_ssm[:, l], a_rows, b_mat, c_mat)
        new_s.append(st)

        w_r = jnp.pad(w_router[l], ((0, 0), (0, ROUTER_LANES - N_EXPERTS)))
        r_b = jnp.concatenate([router_bias[l], jnp.full((ROUTER_LANES - N_EXPERTS,), -jnp.inf, F32)])[None, :]
        x = _merge(
            l, x_pair, mod_all, norm1, w_in[l][:, MIX_WIDTH:].astype(BF16), oa_p, oa_s, op_p, op_s, ssm_u,
            y_f, y_b, d_skip, glu_w, glu_b, up_attn, up_pool, up_ssm, out_w)
        h2, gate_w = _route(l, x, mod_all, norm2, w_r, r_b)
        if l < DEPTH - 1:
            x = _experts(l, h2, gate_w, wg, wu, wd, *shared, x, mod_all, final_g, final=False)
            x_pair = (x, x, N_PROMPT // SEQ_TILE)
        else:
            y_prompt = _experts(l, h2, gate_w, wg, wu, wd, *shared, x, mod_all, final_g, True, 0, N_PROMPT)
            y_sample = _experts(l, h2, gate_w, wg, wu, wd, *shared, x, mod_all, final_g, True, N_PROMPT, N_SAMPLE)

    y_prompt = y_prompt.reshape(BATCH, SEQ, D_MODEL)
    y_sample = y_sample.reshape(DEC_BATCH, DEC_SEQ, D_MODEL)
    return (y_prompt, y_sample, jnp.stack(new_k, axis=1), jnp.stack(new_v, axis=1), jnp.stack(new_s, axis=1))
```
